```python
import math
import jax, jax.numpy as jnp
from jax import lax
import numpy as np

D_MODEL = 2048
BATCH = 4
SEQ = 2048
DEPTH = 2
DEC_BATCH = 128
DEC_SEQ = 8
PAST_LEN = 16384
PAGE_SIZE = 128

D_MIX = D_MODEL
D_SSM = D_MIX // 4
D_RET = D_MIX // 2
D_POOL = D_MIX - D_SSM - D_RET
SSM_GROUP = 16
SSM_GROUPS = D_SSM // SSM_GROUP
SSM_STATE = 64
RET_HEADS = 8
RET_HD = D_RET // RET_HEADS
RET_CHUNK = 128
POOL_WINDOWS = (2, 4, 8, 16)
POOL_GROUPS = len(POOL_WINDOWS)
POOL_GD = D_POOL // POOL_GROUPS
POOL_BUF = max(POOL_WINDOWS) - 1
D_FF = ((8 * D_MODEL // 3 + 127) // 128) * 128
CONV_W = 3
D_IN = D_SSM + 4 * D_RET + D_POOL
ROPE_BASE = 10000.0
EPS = 1e-6

kernel_name = 'hybrid_s5_retention_pool_adaln_decoder_step'


def rmsnorm(x, g):
    xf = x.astype(jnp.float32)
    xf = xf * lax.rsqrt(jnp.mean(xf * xf, axis=-1, keepdims=True) + EPS)
    return xf.astype(x.dtype) * g


def head_norm(o):
    mu = jnp.mean(o, axis=-1, keepdims=True)
    var = jnp.mean(jnp.square(o - mu), axis=-1, keepdims=True)
    return (o - mu) * lax.rsqrt(var + EPS)


def rotary(x, pos):
    half = x.shape[-1] // 2
    inv = ROPE_BASE ** (-jnp.arange(half, dtype=jnp.float32) / half)
    ang = pos.astype(jnp.float32)[:, None] * inv[None, :]
    cos = jnp.cos(ang)[None, :, None, :]
    sin = jnp.sin(ang)[None, :, None, :]
    xf = x.astype(jnp.float32)
    x1, x2 = xf[..., :half], xf[..., half:]
    return jnp.concatenate([x1 * cos - x2 * sin, x1 * sin + x2 * cos], axis=-1)


def _complex_affine_combine(e1, e2):
    a1r, a1i, b1r, b1i = e1
    a2r, a2i, b2r, b2i = e2
    ar = a2r * a1r - a2i * a1i
    ai = a2r * a1i + a2i * a1r
    br = a2r * b1r - a2i * b1i + b2r
    bi = a2r * b1i + a2i * b1r + b2i
    return (ar, ai, br, bi)


def s5_ssm(u, h0_re, h0_im, a_re, a_im, log_dt, b_re, b_im, c_re, c_im, d_skip):
    f32 = jnp.float32
    bsz, L, _ = u.shape
    uf = u.astype(f32)
    ug = uf.reshape(bsz, L, SSM_GROUPS, SSM_GROUP)
    lam_re, lam_im = a_re.astype(f32), a_im.astype(f32)
    dt = jnp.exp(log_dt.astype(f32))[:, None]
    mag = jnp.exp(lam_re * dt)
    ab_re, ab_im = mag * jnp.cos(lam_im * dt), mag * jnp.sin(lam_im * dt)
    den = lam_re * lam_re + lam_im * lam_im
    f_re = ((ab_re - 1.0) * lam_re + ab_im * lam_im) / den
    f_im = (ab_im * lam_re - (ab_re - 1.0) * lam_im) / den
    br, bi = b_re.astype(f32), b_im.astype(f32)
    bb_re = f_re[..., None] * br - f_im[..., None] * bi
    bb_im = f_re[..., None] * bi + f_im[..., None] * br
    drive_re = jnp.einsum('blgh,gph->blgp', ug, bb_re)
    drive_im = jnp.einsum('blgh,gph->blgp', ug, bb_im)
    a_seq_re = jnp.broadcast_to(ab_re, drive_re.shape)
    a_seq_im = jnp.broadcast_to(ab_im, drive_im.shape)
    pr, pi, hr, hi = lax.associative_scan(
        _complex_affine_combine, (a_seq_re, a_seq_im, drive_re, drive_im), axis=1)
    h0r = h0_re.astype(f32)[:, None]
    h0i = h0_im.astype(f32)[:, None]
    hr = hr + pr * h0r - pi * h0i
    hi = hi + pr * h0i + pi * h0r
    y = (jnp.einsum('blgp,ghp->blgh', hr, c_re.astype(f32))
         - jnp.einsum('blgp,ghp->blgh', hi, c_im.astype(f32)))
    y = y.reshape(bsz, L, D_SSM) + d_skip.astype(f32) * uf
    return y.astype(u.dtype), hr[:, -1], hi[:, -1]


def retention_chunkwise(q, k, v, s0):
    f32 = jnp.float32
    bsz, L, H, D = q.shape
    C = math.gcd(L, RET_CHUNK)
    n = L // C
    lg = jnp.log1p(-jnp.exp2(-5.0 - jnp.arange(H, dtype=f32)))
    idx = jnp.arange(C, dtype=f32)
    diff = idx[:, None] - idx[None, :]
    mask = jnp.where(diff[None] >= 0.0,
                     jnp.exp(jnp.maximum(diff, 0.0)[None] * lg[:, None, None]), 0.0)
    q_dec = jnp.exp((idx + 1.0)[:, None] * lg[None, :])
    k_dec = jnp.exp((C - 1.0 - idx)[:, None] * lg[None, :])
    c_dec = jnp.exp(C * lg)

    def blocks(t):
        return t.astype(f32).reshape(bsz, n, C, H, D).transpose(1, 0, 2, 3, 4)

    qb, kb, vb = blocks(q), blocks(k) * (D ** -0.5), blocks(v)

    def step(S, inp):
        qc, kc, vc = inp
        sc = jnp.einsum('bihd,bjhd->bhij', qc, kc) * mask[None]
        o = (jnp.einsum('bhij,bjhe->bihe', sc, vc)
             + jnp.einsum('bihd,bhde->bihe', qc * q_dec[None, :, :, None], S))
        S = (S * c_dec[None, :, None, None]
             + jnp.einsum('bjhd,bjhe->bhde', kc * k_dec[None, :, :, None], vc))
        return S, o

    S, ob = lax.scan(step, s0.astype(f32), (qb, kb, vb))
    return ob.transpose(1, 0, 2, 3, 4).reshape(bsz, L, H, D), S


def pool_mixer(u, buf, pos, w_pool, pool_scale):
    f32 = jnp.float32
    bsz, L, _ = u.shape
    full = jnp.concatenate([buf.astype(u.dtype), u], axis=1)
    ff = full.astype(f32)
    cs = jnp.concatenate([jnp.zeros((bsz, 1, D_POOL), f32), jnp.cumsum(ff, axis=1)], axis=1)
    end = cs[:, POOL_BUF + 1:]
    parts = []
    for gi, w in enumerate(POOL_WINDOWS):
        lo, hi = gi * POOL_GD, (gi + 1) * POOL_GD
        start = cs[:, POOL_BUF + 1 - w: POOL_BUF + 1 - w + L, lo:hi]
        cnt = jnp.minimum(pos + 1, w).astype(f32)[None, :, None]
        parts.append((end[..., lo:hi] - start) / cnt)
    pooled = jnp.concatenate(parts, axis=-1) - ff[:, POOL_BUF:]
    mixed = jnp.einsum('blgc,gcd->blgd',
                       pooled.reshape(bsz, L, POOL_GROUPS, POOL_GD).astype(u.dtype), w_pool)
    y = mixed.reshape(bsz, L, D_POOL) * pool_scale
    return y, full[:, -POOL_BUF:]


def conv_ffn(h, buf, w_up, conv_w, conv_b, w_down):
    L = h.shape[1]
    a, b = jnp.split(h @ w_up, 2, axis=-1)
    full = jnp.concatenate([buf.astype(a.dtype), a], axis=1)
    conv = conv_b + full[:, 0:L] * conv_w[0]
    for j in range(1, CONV_W):
        conv = conv + full[:, j:j + L] * conv_w[j]
    return (jax.nn.silu(conv) * b) @ w_down, full[:, -(CONV_W - 1):]


def hybrid_layer(x, c, pos0, h0_re, h0_im, s_ret, pool_buf, conv_buf,
                 w_ada, b_ada, norm1_g, w_in, a_re, a_im, log_dt, b_re, b_im, c_re, c_im, d_skip,
                 w_glu, b_glu, w_pool, pool_scale, w_out, norm2_g, w_up, conv_w, conv_b, w_down):
    bsz, L, _ = x.shape
    pos = pos0 + jnp.arange(L, dtype=jnp.int32)
    mod = jax.nn.silu(c) @ w_ada + b_ada
    sh1, sc1, g1, sh2, sc2, g2 = [m[:, None, :] for m in jnp.split(mod, 6, axis=-1)]

    h = rmsnorm(x, norm1_g) * (1.0 + sc1) + sh1
    proj = h @ w_in
    u_a, q, k, v, g, u_c = jnp.split(
        proj, [D_SSM, D_SSM + D_RET, D_SSM + 2 * D_RET, D_SSM + 3 * D_RET, D_SSM + 4 * D_RET], axis=-1)

    y_a, hT_re, hT_im = s5_ssm(u_a, h0_re, h0_im, a_re, a_im, log_dt, b_re, b_im, c_re, c_im, d_skip)
    y_a = jax.nn.gelu(y_a)
    y_a = y_a * jax.nn.sigmoid(y_a @ w_glu + b_glu)

    qr = rotary(q.reshape(bsz, L, RET_HEADS, RET_HD), pos)
    kr = rotary(k.reshape(bsz, L, RET_HEADS, RET_HD), pos)
    o, s_new = retention_chunkwise(qr, kr, v.reshape(bsz, L, RET_HEADS, RET_HD), s_ret)
    y_b = (head_norm(o).reshape(bsz, L, D_RET) * jax.nn.silu(g.astype(jnp.float32))).astype(x.dtype)

    y_c, pool_new = pool_mixer(u_c, pool_buf, pos, w_pool, pool_scale)

    x = x + g1 * (jnp.concatenate([y_a, y_b, y_c], axis=-1) @ w_out)

    h2 = rmsnorm(x, norm2_g) * (1.0 + sc2) + sh2
    f, conv_new = conv_ffn(h2, conv_buf, w_up, conv_w, conv_b, w_down)
    x = x + g2 * f
    return x, (hT_re, hT_im, s_new, pool_new, conv_new)


def setup_inputs(seed: int = 0) -> dict:
    key = jax.random.key(seed)
    ks = list(jax.random.split(key, 40))
    f32 = jnp.float32

    def nrm(shape, scale):
        return jax.random.normal(ks.pop(), shape, f32) * scale

    x_prompt = nrm((BATCH, SEQ, D_MODEL), 1.0)
    x_sample = nrm((DEC_BATCH, DEC_SEQ, D_MODEL), 1.0)
    state_ssm_re = nrm((DEPTH, DEC_BATCH, SSM_GROUPS, SSM_STATE), 0.5)
    state_ssm_im = nrm((DEPTH, DEC_BATCH, SSM_GROUPS, SSM_STATE), 0.5)
    state_ret = nrm((DEPTH, DEC_BATCH, RET_HEADS, RET_HD, RET_HD), 0.5)
    state_pool = nrm((DEPTH, DEC_BATCH, POOL_BUF, D_POOL), 1.0)
    state_ffn_conv = nrm((DEPTH, DEC_BATCH, CONV_W - 1, D_FF), 1.0)
    c_prompt = nrm((BATCH, D_MODEL), 1.0)
    c_sample = nrm((DEC_BATCH, D_MODEL), 1.0)
    w_ada = nrm((DEPTH, D_MODEL, 6 * D_MODEL), 0.5 * D_MODEL ** -0.5)
    b_ada = nrm((DEPTH, 6 * D_MODEL), 0.02)
    norm1_g = 1.0 + nrm((DEPTH, D_MODEL), 0.02)
    w_in = nrm((DEPTH, D_MODEL, D_IN), D_MODEL ** -0.5)
    ssm_a_re = -0.5 + nrm((DEPTH, SSM_GROUPS, SSM_STATE), 0.01)
    ssm_a_im = math.pi * jnp.arange(SSM_STATE, dtype=f32) + nrm((DEPTH, SSM_GROUPS, SSM_STATE), 0.01)
    ssm_log_dt = jax.random.uniform(ks.pop(), (DEPTH, SSM_GROUPS), f32, math.log(1e-3), math.log(1e-1))
    ssm_b_re = nrm((DEPTH, SSM_GROUPS, SSM_STATE, SSM_GROUP), (2 * SSM_GROUP) ** -0.5)
    ssm_b_im = nrm((DEPTH, SSM_GROUPS, SSM_STATE, SSM_GROUP), (2 * SSM_GROUP) ** -0.5)
    ssm_c_re = nrm((DEPTH, SSM_GROUPS, SSM_GROUP, SSM_STATE), SSM_STATE ** -0.5)
    ssm_c_im = nrm((DEPTH, SSM_GROUPS, SSM_GROUP, SSM_STATE), SSM_STATE ** -0.5)
    ssm_d = nrm((DEPTH, D_SSM), 1.0)
    ssm_w_glu = nrm((DEPTH, D_SSM, D_SSM), D_SSM ** -0.5)
    ssm_b_glu = nrm((DEPTH, D_SSM), 0.02)
    pool_w = nrm((DEPTH, POOL_GROUPS, POOL_GD, POOL_GD), POOL_GD ** -0.5)
    pool_scale = 1.0 + nrm((DEPTH, D_POOL), 0.02)
    w_out = nrm((DEPTH, D_MIX, D_MODEL), D_MIX ** -0.5)
    norm2_g = 1.0 + nrm((DEPTH, D_MODEL), 0.02)
    ffn_w_up = nrm((DEPTH, D_MODEL, 2 * D_FF), D_MODEL ** -0.5)
    ffn_conv_w = nrm((DEPTH, CONV_W, D_FF), CONV_W ** -0.5)
    ffn_conv_b = nrm((DEPTH, D_FF), 0.02)
    ffn_w_down = nrm((DEPTH, D_FF, D_MODEL), D_FF ** -0.5)
    final_norm_g = 1.0 + nrm((D_MODEL,), 0.02)
    return {
        'x_prompt': x_prompt, 'x_sample': x_sample,
        'state_ssm_re': state_ssm_re, 'state_ssm_im': state_ssm_im, 'state_ret': state_ret,
        'state_pool': state_pool, 'state_ffn_conv': state_ffn_conv,
        'c_prompt': c_prompt, 'c_sample': c_sample,
        'w_ada': w_ada, 'b_ada': b_ada, 'norm1_g': norm1_g, 'w_in': w_in,
        'ssm_a_re': ssm_a_re, 'ssm_a_im': ssm_a_im, 'ssm_log_dt': ssm_log_dt,
        'ssm_b_re': ssm_b_re, 'ssm_b_im': ssm_b_im, 'ssm_c_re': ssm_c_re, 'ssm_c_im': ssm_c_im,
        'ssm_d': ssm_d, 'ssm_w_glu': ssm_w_glu, 'ssm_b_glu': ssm_b_glu,
        'pool_w': pool_w, 'pool_scale': pool_scale, 'w_out': w_out, 'norm2_g': norm2_g,
        'ffn_w_up': ffn_w_up, 'ffn_conv_w': ffn_conv_w, 'ffn_conv_b': ffn_conv_b, 'ffn_w_down': ffn_w_down,
        'final_norm_g': final_norm_g,
    }


def reference(x_prompt, x_sample, state_ssm_re, state_ssm_im, state_ret, state_pool, state_ffn_conv,
              c_prompt, c_sample, w_ada, b_ada, norm1_g, w_in, ssm_a_re, ssm_a_im, ssm_log_dt,
              ssm_b_re, ssm_b_im, ssm_c_re, ssm_c_im, ssm_d, ssm_w_glu, ssm_b_glu, pool_w, pool_scale,
              w_out, norm2_g, ffn_w_up, ffn_conv_w, ffn_conv_b, ffn_w_down, final_norm_g):

    def run(x, c, pos0, st_re, st_im, st_ret, st_pool, st_conv):
        new = ([], [], [], [], [])
        for l in range(DEPTH):
            x, ns = hybrid_layer(
                x, c, pos0, st_re[l], st_im[l], st_ret[l], st_pool[l], st_conv[l],
                w_ada[l], b_ada[l], norm1_g[l], w_in[l], ssm_a_re[l], ssm_a_im[l], ssm_log_dt[l],
                ssm_b_re[l], ssm_b_im[l], ssm_c_re[l], ssm_c_im[l], ssm_d[l], ssm_w_glu[l], ssm_b_glu[l],
                pool_w[l], pool_scale[l], w_out[l], norm2_g[l], ffn_w_up[l], ffn_conv_w[l],
                ffn_conv_b[l], ffn_w_down[l])
            for lst, s in zip(new, ns):
                lst.append(s)
        return rmsnorm(x, final_norm_g), [jnp.stack(s) for s in new]

    bp = x_prompt.shape[0]
    dt = x_prompt.dtype
    y_prompt, (p_re, p_im, p_ret, p_pool, p_conv) = run(
        x_prompt, c_prompt, 0,
        jnp.zeros((DEPTH, bp, SSM_GROUPS, SSM_STATE), jnp.float32),
        jnp.zeros((DEPTH, bp, SSM_GROUPS, SSM_STATE), jnp.float32),
        jnp.zeros((DEPTH, bp, RET_HEADS, RET_HD, RET_HD), jnp.float32),
        jnp.zeros((DEPTH, bp, POOL_BUF, D_POOL), dt),
        jnp.zeros((DEPTH, bp, CONV_W - 1, D_FF), dt))
    y_sample, (s_re, s_im, s_ret, s_pool, s_conv) = run(
        x_sample, c_sample, PAST_LEN, state_ssm_re, state_ssm_im, state_ret, state_pool, state_ffn_conv)
    return (y_prompt, y_sample, p_re, p_im, p_ret, p_pool, p_conv, s_re, s_im, s_ret, s_pool, s_conv)
```

```python
import functools
import math

import jax
import jax.numpy as jnp
from jax import lax
from jax.experimental import pallas as pl
from jax.experimental.pallas import tpu as pltpu

F32 = jnp.float32
BF16 = jnp.bfloat16

D_MODEL = 2048
DEPTH = 2
PAST_LEN = 16384
D_SSM = 512
D_RET = 1024
D_POOL = 512
SSM_GROUP = 16
SSM_GROUPS = 32
SSM_STATE = 64
N_STATE = SSM_GROUPS * SSM_STATE
RET_HEADS = 8
RET_HD = 128
RET_CHUNK = 128
POOL_WINDOWS = (2, 4, 8, 16)
POOL_GD = 128
POOL_BUF = 15
D_FF = 5504
FF_CHUNKS = D_FF // 128
CONV_W = 3
D_IN = D_SSM + 4 * D_RET + D_POOL
ROPE_BASE = 10000.0
EPS = 1e-6

SUBLANES = 8
SCAN_LANES = 512
VMEM_LIMIT = 56 * 1024 * 1024

Q_OFF = D_SSM
K_OFF = D_SSM + D_RET
V_OFF = D_SSM + 2 * D_RET
G_OFF = D_SSM + 3 * D_RET
C_OFF = D_SSM + 4 * D_RET


def _params(*sem):
    return pltpu.CompilerParams(dimension_semantics=sem, vmem_limit_bytes=VMEM_LIMIT)


def _rms(x):
    return x * lax.rsqrt(jnp.mean(x * x, axis=-1, keepdims=True) + EPS)


def _ada_kernel(c_ref, w_ref, b_ref, o_ref):
    a = jax.nn.silu(c_ref[...]).astype(BF16)
    o_ref[0] = jnp.dot(a, w_ref[0].astype(BF16), preferred_element_type=F32) + b_ref[0]


def _ada_mod(c_all, w_ada, b_ada):
    nseq = c_all.shape[0]
    tn = 1024
    return pl.pallas_call(
        _ada_kernel,
        grid=(DEPTH, 6 * D_MODEL // tn),
        in_specs=[
            pl.BlockSpec((nseq, D_MODEL), lambda l, j: (0, 0)),
            pl.BlockSpec((1, D_MODEL, tn), lambda l, j: (l, 0, j)),
            pl.BlockSpec((1, 1, tn), lambda l, j: (l, 0, j)),
        ],
        out_specs=pl.BlockSpec((1, nseq, tn), lambda l, j: (l, 0, j)),
        out_shape=jax.ShapeDtypeStruct((DEPTH, nseq, 6 * D_MODEL), F32),
        compiler_params=_params("arbitrary", "arbitrary"),
        name="ada_mod",
    )(c_all, w_ada, b_ada.reshape(DEPTH, 1, 6 * D_MODEL))


class _Rows:
    def __init__(self, n_seq, seq_len, tm):
        if seq_len >= tm:
            assert seq_len % tm == 0
            self.bb, self.tl = 1, tm
            self.tiles_per_seq = seq_len // tm
        else:
            assert tm % seq_len == 0 and seq_len == SUBLANES
            self.bb, self.tl = tm // seq_len, seq_len
            self.tiles_per_seq = 1
        self.tm = tm
        self.n_tiles = n_seq * seq_len // tm

    def seq_block(self, i):
        return i // self.tiles_per_seq


def _in_proj_kernel(x_ref, sh_ref, sc_ref, g_ref, w_ref, o_ref, h_scr, *, bb, tl):
    @pl.when(pl.program_id(1) == 0)
    def _():
        x = x_ref[...].reshape(bb, tl, D_MODEL)
        h = _rms(x) * g_ref[...] * (1.0 + sc_ref[...]) + sh_ref[...]
        h_scr[...] = h.reshape(bb * tl, D_MODEL).astype(BF16)

    o_ref[...] = jnp.dot(h_scr[...], w_ref[...].astype(BF16), preferred_element_type=F32)


def _in_proj(x, sh, sc, g, w, rows):
    m = x.shape[0]
    tn = 512
    mod_spec = pl.BlockSpec((rows.bb, 1, D_MODEL), lambda i, j: (rows.seq_block(i), 0, 0))
    return pl.pallas_call(
        functools.partial(_in_proj_kernel, bb=rows.bb, tl=rows.tl),
        grid=(rows.n_tiles, D_IN // tn),
        in_specs=[
            pl.BlockSpec((rows.tm, D_MODEL), lambda i, j: (i, 0)),
            mod_spec,
            mod_spec,
            pl.BlockSpec((1, D_MODEL), lambda i, j: (0, 0)),
            pl.BlockSpec((D_MODEL, tn), lambda i, j: (0, j)),
        ],
        out_specs=pl.BlockSpec((rows.tm, tn), lambda i, j: (i, j)),
        out_shape=jax.ShapeDtypeStruct((m, D_IN), F32),
        scratch_shapes=[pltpu.VMEM((rows.tm, D_MODEL), BF16)],
        compiler_params=_params("arbitrary", "arbitrary"),
        name="in_proj",
    )(x, sh, sc, g.reshape(1, D_MODEL), w)


def _ssm_drive(u, bmat_ref, dr_scr, n_rows):
    half_u = D_SSM // 2
    half_n = N_STATE // 2
    ub = u.astype(BF16)
    for hf in range(2):
        d = jnp.dot(ub[:, hf * half_u:(hf + 1) * half_u], bmat_ref[hf], preferred_element_type=F32)
        dr_scr[SUBLANES:SUBLANES + n_rows, hf * half_n:(hf + 1) * half_n] = d[:, :half_n]
        dr_scr[SUBLANES:SUBLANES + n_rows, N_STATE + hf * half_n:N_STATE + (hf + 1) * half_n] = d[:, half_n:]


def _ssm_scan(dr_scr, pw_ref, n_blocks, carry_fn, block_end_fn=None):
    def body(b, carry):
        r = pl.multiple_of(b * SUBLANES + SUBLANES, SUBLANES)
        for c in range(N_STATE // SCAN_LANES):
            lo = c * SCAN_LANES
            re_cols = slice(lo, lo + SCAN_LANES)
            im_cols = slice(N_STATE + lo, N_STATE + lo + SCAN_LANES)
            d_re = dr_scr[pl.ds(r, SUBLANES), re_cols]
            d_im = dr_scr[pl.ds(r, SUBLANES), im_cols]
            for si, s in enumerate((1, 2, 4)):
                p_re = pw_ref[0, si, :, re_cols]
                p_im = pw_ref[1, si, :, re_cols]
                r_re = pltpu.roll(d_re, s, 0)
                r_im = pltpu.roll(d_im, s, 0)
                d_re, d_im = (d_re + (p_re * r_re - p_im * r_im),
                              d_im + (p_re * r_im + p_im * r_re))
            c_re, c_im = carry_fn(b, lo)
            a_re = pw_ref[0, 3, :, re_cols]
            a_im = pw_ref[1, 3, :, re_cols]
            h_re = d_re + (a_re * c_re - a_im * c_im)
            h_im = d_im + (a_re * c_im + a_im * c_re)
            dr_scr[pl.ds(r, SUBLANES), re_cols] = h_re
            dr_scr[pl.ds(r, SUBLANES), im_cols] = h_im
            if block_end_fn is not None:
                block_end_fn(b, lo, h_re[SUBLANES - 1:SUBLANES], h_im[SUBLANES - 1:SUBLANES])
        return carry

    lax.fori_loop(0, n_blocks, body, 0)


def _ssm_out(u, dr_scr, cmat_ref, dskip_ref, wglu_ref, bglu_ref, n_rows):
    half_u = D_SSM // 2
    half_n = N_STATE // 2
    parts = []
    for hf in range(2):
        h_re = dr_scr[SUBLANES:SUBLANES + n_rows, hf * half_n:(hf + 1) * half_n].astype(BF16)
        h_im = dr_scr[SUBLANES:SUBLANES + n_rows, N_STATE + hf * half_n:N_STATE + (hf + 1) * half_n].astype(BF16)
        parts.append(jnp.dot(h_re, cmat_ref[hf, :half_n], preferred_element_type=F32)
                     + jnp.dot(h_im, cmat_ref[hf, half_n:], preferred_element_type=F32))
    y = jnp.concatenate(parts, axis=-1) + dskip_ref[...] * u
    ya = jax.nn.gelu(y)
    gate = jnp.dot(ya.astype(BF16), wglu_ref[...], preferred_element_type=F32) + bglu_ref[...]
    return ya * jax.nn.sigmoid(gate)


def _rotary(x, cosv, sinv):
    return x * cosv + pltpu.roll(x, RET_HD // 2, 1) * sinv


def _head_norm_gate(o, g):
    mu = jnp.mean(o, axis=-1, keepdims=True)
    var = jnp.mean(jnp.square(o - mu), axis=-1, keepdims=True)
    return (o - mu) * lax.rsqrt(var + EPS) * jax.nn.silu(g)


def _pool_counts(pos):
    return [jnp.minimum(pos + 1, w).astype(F32) for w in POOL_WINDOWS]


def _mix_prompt_kernel(proj_ref, cos_ref, sin_ref, bmat_ref, cmat_ref, pw_ref, dskip_ref, wglu_ref,
                       bglu_ref, mask_ref, qdec_ref, kdec_ref, cdec_ref, wpool_ref, pscale_ref,
                       y_ref, hre_ref, him_ref, sret_ref, pbuf_ref,
                       dr_scr, pool_scr, *, tl):
    t = pl.program_id(1)

    @pl.when(t == 0)
    def _():
        dr_scr[0:SUBLANES, :] = jnp.zeros((SUBLANES, 2 * N_STATE), F32)
        sret_ref[...] = jnp.zeros(sret_ref.shape, F32)
        pool_scr[0:16, :] = jnp.zeros((16, D_POOL), F32)

    u = proj_ref[:, 0:D_SSM]
    _ssm_drive(u, bmat_ref, dr_scr, tl)

    def carry_fn(b, lo):
        row = b * SUBLANES + SUBLANES - 1
        return (dr_scr[pl.ds(row, 1), lo:lo + SCAN_LANES],
                dr_scr[pl.ds(row, 1), N_STATE + lo:N_STATE + lo + SCAN_LANES])

    _ssm_scan(dr_scr, pw_ref, tl // SUBLANES, carry_fn)
    y_ref[:, 0:D_SSM] = _ssm_out(u, dr_scr, cmat_ref, dskip_ref, wglu_ref, bglu_ref, tl).astype(BF16)
    last = dr_scr[tl:tl + SUBLANES, :]
    dr_scr[0:SUBLANES, :] = last
    hre_ref[0] = last[SUBLANES - 1:SUBLANES, 0:N_STATE]
    him_ref[0] = last[SUBLANES - 1:SUBLANES, N_STATE:2 * N_STATE]

    scale = RET_HD ** -0.5

    def chunk_body(ci, carry):
        r0 = pl.multiple_of(ci * RET_CHUNK, RET_CHUNK)
        rows = pl.ds(r0, RET_CHUNK)
        cosv = cos_ref[rows, :]
        sinv = sin_ref[rows, :]
        for h in range(RET_HEADS):
            hs = slice(h * RET_HD, (h + 1) * RET_HD)
            q = proj_ref[rows, Q_OFF + h * RET_HD:Q_OFF + (h + 1) * RET_HD]
            k = proj_ref[rows, K_OFF + h * RET_HD:K_OFF + (h + 1) * RET_HD]
            v = proj_ref[rows, V_OFF + h * RET_HD:V_OFF + (h + 1) * RET_HD].astype(BF16)
            g = proj_ref[rows, G_OFF + h * RET_HD:G_OFF + (h + 1) * RET_HD]
            qr = _rotary(q, cosv, sinv)
            kr = _rotary(k, cosv, sinv) * scale
            s_old = sret_ref[0, h]
            sc = lax.dot_general(qr.astype(BF16), kr.astype(BF16), (((1,), (1,)), ((), ())),
                                 preferred_element_type=F32) * mask_ref[h]
            o = (jnp.dot(sc.astype(BF16), v, preferred_element_type=F32)
                 + jnp.dot((qr * qdec_ref[:, hs]).astype(BF16), s_old.astype(BF16),
                           preferred_element_type=F32))
            sret_ref[0, h] = (s_old * cdec_ref[h:h + 1, :]
                              + lax.dot_general((kr * kdec_ref[:, hs]).astype(BF16), v,
                                                (((0,), (0,)), ((), ())), preferred_element_type=F32))
            y_ref[rows, D_SSM + h * RET_HD:D_SSM + (h + 1) * RET_HD] = _head_norm_gate(o, g).astype(BF16)
        return carry

    lax.fori_loop(0, tl // RET_CHUNK, chunk_body, 0)

    uc = proj_ref[:, C_OFF:C_OFF + D_POOL]
    pool_scr[16:16 + tl, :] = uc
    pos = t * tl + lax.broadcasted_iota(jnp.int32, (tl, POOL_GD), 0)
    counts = _pool_counts(pos)
    parts = []
    for gi, w in enumerate(POOL_WINDOWS):
        lanes = slice(gi * POOL_GD, (gi + 1) * POOL_GD)
        acc = pool_scr[16:16 + tl, lanes]
        for j in range(1, w):
            acc = acc + pool_scr[16 - j:16 - j + tl, lanes]
        parts.append(acc / counts[gi] - uc[:, lanes])
    pooled = jnp.concatenate(parts, axis=-1).astype(BF16)
    yc = jnp.dot(pooled, wpool_ref[...], preferred_element_type=F32) * pscale_ref[...]
    y_ref[:, D_SSM + D_RET:] = yc.astype(BF16)
    pbuf_ref[0] = pool_scr[tl + 1:tl + 16, :]
    pool_scr[0:16, :] = pool_scr[tl:tl + 16, :]


def _mix_prompt(proj, n_seq, seq_len, consts, tl):
    (cos_t, sin_t, bmat, cmat, pw, dskip, wglu, bglu, mask, qdec, kdec, cdec, wpool, pscale) = consts
    nt = seq_len // tl
    row_map = lambda b, t: (b * nt + t, 0)

    def const_spec(a):
        nd = a.ndim
        return pl.BlockSpec(a.shape, lambda b, t: (0,) * nd)

    in_specs = [
        pl.BlockSpec((tl, D_IN), row_map),
        pl.BlockSpec((tl, RET_HD), lambda b, t: (t, 0)),
        pl.BlockSpec((tl, RET_HD), lambda b, t: (t, 0)),
    ] + [const_spec(a) for a in (bmat, cmat, pw, dskip, wglu, bglu, mask, qdec, kdec, cdec, wpool, pscale)]
    out_shape = (
        jax.ShapeDtypeStruct((n_seq * seq_len, D_MODEL), BF16),
        jax.ShapeDtypeStruct((n_seq, 1, N_STATE), F32),
        jax.ShapeDtypeStruct((n_seq, 1, N_STATE), F32),
        jax.ShapeDtypeStruct((n_seq, RET_HEADS, RET_HD, RET_HD), F32),
        jax.ShapeDtypeStruct((n_seq, POOL_BUF, D_POOL), F32),
    )
    out_specs = (
        pl.BlockSpec((tl, D_MODEL), row_map),
        pl.BlockSpec((1, 1, N_STATE), lambda b, t: (b, 0, 0)),
        pl.BlockSpec((1, 1, N_STATE), lambda b, t: (b, 0, 0)),
        pl.BlockSpec((1, RET_HEADS, RET_HD, RET_HD), lambda b, t: (b, 0, 0, 0)),
        pl.BlockSpec((1, POOL_BUF, D_POOL), lambda b, t: (b, 0, 0)),
    )
    return pl.pallas_call(
        functools.partial(_mix_prompt_kernel, tl=tl),
        grid=(n_seq, nt),
        in_specs=in_specs,
        out_specs=out_specs,
        out_shape=out_shape,
        scratch_shapes=[
            pltpu.VMEM((SUBLANES + tl, 2 * N_STATE), F32),
            pltpu.VMEM((16 + tl, D_POOL), F32),
        ],
        compiler_params=_params("arbitrary", "arbitrary"),
        name="mix_prompt",
    )(proj, cos_t, sin_t, bmat, cmat, pw, dskip, wglu, bglu, mask, qdec, kdec, cdec, wpool, pscale)


def _mix_sample_kernel(proj_ref, cos_ref, sin_ref, bmat_ref, cmat_ref, pw_ref, dskip_ref, wglu_ref,
                       bglu_ref, mask_ref, qdec_ref, kdec_ref, cdec_ref, wpool_ref, pscale_ref,
                       h0re_ref, h0im_ref, s0_ref, pool0_ref,
                       y_ref, hre_ref, him_ref, sret_ref, pbuf_ref,
                       dr_scr, o_scr, qd_scr, kd_scr, pool_scr, *, bs, seq_len, pos0):
    n_rows = bs * seq_len

    u = proj_ref[:, 0:D_SSM]
    _ssm_drive(u, bmat_ref, dr_scr, n_rows)

    def carry_fn(b, lo):
        return (h0re_ref[pl.ds(b, 1), lo:lo + SCAN_LANES], h0im_ref[pl.ds(b, 1), lo:lo + SCAN_LANES])

    def block_end_fn(b, lo, h_re, h_im):
        hre_ref[pl.ds(b, 1), lo:lo + SCAN_LANES] = h_re
        him_ref[pl.ds(b, 1), lo:lo + SCAN_LANES] = h_im

    _ssm_scan(dr_scr, pw_ref, bs, carry_fn, block_end_fn)
    y_ref[:, 0:D_SSM] = _ssm_out(u, dr_scr, cmat_ref, dskip_ref, wglu_ref, bglu_ref, n_rows).astype(BF16)

    scale = RET_HD ** -0.5
    cosv = cos_ref[...]
    sinv = sin_ref[...]
    for h in range(RET_HEADS):
        hs = slice(h * RET_HD, (h + 1) * RET_HD)
        q = proj_ref[:, Q_OFF + h * RET_HD:Q_OFF + (h + 1) * RET_HD]
        k = proj_ref[:, K_OFF + h * RET_HD:K_OFF + (h + 1) * RET_HD]
        v = proj_ref[:, V_OFF + h * RET_HD:V_OFF + (h + 1) * RET_HD].astype(BF16)
        qr = _rotary(q, cosv, sinv)
        kr = _rotary(k, cosv, sinv) * scale
        sc = lax.dot_general(qr.astype(BF16), kr.astype(BF16), (((1,), (1,)), ((), ())),
                             preferred_element_type=F32) * mask_ref[h]
        o_scr[:, hs] = jnp.dot(sc.astype(BF16), v, preferred_element_type=F32)
        qd_scr[:, hs] = qr * qdec_ref[:, hs]
        kd_scr[:, hs] = kr * kdec_ref[:, hs]

    row_seq = lax.broadcasted_iota(jnp.int32, (n_rows, 1), 0) // seq_len

    def seq_body(s, carry):
        own = row_seq == s
        for h in range(RET_HEADS):
            hs = slice(h * RET_HD, (h + 1) * RET_HD)
            s_old = s0_ref[s, h]
            qd = jnp.where(own, qd_scr[:, hs], 0.0).astype(BF16)
            kd = jnp.where(own, kd_scr[:, hs], 0.0).astype(BF16)
            v = proj_ref[:, V_OFF + h * RET_HD:V_OFF + (h + 1) * RET_HD].astype(BF16)
            o_scr[:, hs] += jnp.dot(qd, s_old.astype(BF16), preferred_element_type=F32)
            sret_ref[s, h] = (s_old * cdec_ref[h:h + 1, :]
                              + lax.dot_general(kd, v, (((0,), (0,)), ((), ())),
                                                preferred_element_type=F32))
        return carry

    lax.fori_loop(0, bs, seq_body, 0)
    for h in range(RET_HEADS):
        hs = slice(h * RET_HD, (h + 1) * RET_HD)
        g = proj_ref[:, G_OFF + h * RET_HD:G_OFF + (h + 1) * RET_HD]
        y_ref[:, D_SSM + h * RET_HD:D_SSM + (h + 1) * RET_HD] = _head_norm_gate(o_scr[:, hs], g).astype(BF16)

    uc = proj_ref[:, C_OFF:C_OFF + D_POOL].reshape(bs, seq_len, D_POOL)
    pool_scr[:, 1:16, :] = pool0_ref[...]
    pool_scr[:, 16:16 + seq_len, :] = uc
    pos = pos0 + lax.broadcasted_iota(jnp.int32, (1, seq_len, POOL_GD), 1)
    counts = _pool_counts(pos)
    parts = []
    for gi, w in enumerate(POOL_WINDOWS):
        lanes = slice(gi * POOL_GD, (gi + 1) * POOL_GD)
        acc = pool_scr[:, 16:16 + seq_len, lanes]
        for j in range(1, w):
            acc = acc + pool_scr[:, 16 - j:16 - j + seq_len, lanes]
        parts.append(acc / counts[gi] - uc[:, :, lanes])
    pooled = jnp.concatenate(parts, axis=-1).reshape(n_rows, D_POOL).astype(BF16)
    yc = jnp.dot(pooled, wpool_ref[...], preferred_element_type=F32) * pscale_ref[...]
    y_ref[:, D_SSM + D_RET:] = yc.astype(BF16)
    pbuf_ref[...] = pool_scr[:, seq_len + 1:seq_len + 16, :]


def _mix_sample(proj, h0re, h0im, s0, pool0, n_seq, seq_len, consts, bs):
    (cos_t, sin_t, bmat, cmat, pw, dskip, wglu, bglu, mask, qdec, kdec, cdec, wpool, pscale) = consts
    n_rows = bs * seq_len

    def const_spec(a):
        nd = a.ndim
        return pl.BlockSpec(a.shape, lambda i: (0,) * nd)

    in_specs = [pl.BlockSpec((n_rows, D_IN), lambda i: (i, 0))] + [
        const_spec(a) for a in (cos_t, sin_t, bmat, cmat, pw, dskip, wglu, bglu, mask, qdec, kdec, cdec,
                                wpool, pscale)
    ] + [
        pl.BlockSpec((bs, N_STATE), lambda i: (i, 0)),
        pl.BlockSpec((bs, N_STATE), lambda i: (i, 0)),
        pl.BlockSpec((bs, RET_HEADS, RET_HD, RET_HD), lambda i: (i, 0, 0, 0)),
        pl.BlockSpec((bs, POOL_BUF, D_POOL), lambda i: (i, 0, 0)),
    ]
    out_shape = (
        jax.ShapeDtypeStruct((n_seq * seq_len, D_MODEL), BF16),
        jax.ShapeDtypeStruct((n_seq, N_STATE), F32),
        jax.ShapeDtypeStruct((n_seq, N_STATE), F32),
        jax.ShapeDtypeStruct((n_seq, RET_HEADS, RET_HD, RET_HD), F32),
        jax.ShapeDtypeStruct((n_seq, POOL_BUF, D_POOL), F32),
    )
    out_specs = (
        pl.BlockSpec((n_rows, D_MODEL), lambda i: (i, 0)),
        pl.BlockSpec((bs, N_STATE), lambda i: (i, 0)),
        pl.BlockSpec((bs, N_STATE), lambda i: (i, 0)),
        pl.BlockSpec((bs, RET_HEADS, RET_HD, RET_HD), lambda i: (i, 0, 0, 0)),
        pl.BlockSpec((bs, POOL_BUF, D_POOL), lambda i: (i, 0, 0)),
    )
    return pl.pallas_call(
        functools.partial(_mix_sample_kernel, bs=bs, seq_len=seq_len, pos0=PAST_LEN),
        grid=(n_seq // bs,),
        in_specs=in_specs,
        out_specs=out_specs,
        out_shape=out_shape,
        scratch_shapes=[
            pltpu.VMEM((SUBLANES + n_rows, 2 * N_STATE), F32),
            pltpu.VMEM((n_rows, D_RET), F32),
            pltpu.VMEM((n_rows, D_RET), F32),
            pltpu.VMEM((n_rows, D_RET), F32),
            pltpu.VMEM((bs, 16 + seq_len, D_POOL), F32),
        ],
        compiler_params=_params("arbitrary"),
        name="mix_sample",
    )(proj, cos_t, sin_t, bmat, cmat, pw, dskip, wglu, bglu, mask, qdec, kdec, cdec, wpool, pscale,
      h0re, h0im, s0, pool0)


def _out_proj_kernel(y_ref, w_ref, x_ref, g1_ref, sh_ref, sc_ref, g_ref, xo_ref, h_ref, acc_scr, *, bb, tl, nk):
    k = pl.program_id(1)

    @pl.when(k == 0)
    def _():
        acc_scr[...] = jnp.zeros(acc_scr.shape, F32)

    acc_scr[...] += jnp.dot(y_ref[...], w_ref[...].astype(BF16), preferred_element_type=F32)

    @pl.when(k == nk - 1)
    def _():
        xn = x_ref[...].reshape(bb, tl, D_MODEL) + g1_ref[...] * acc_scr[...].reshape(bb, tl, D_MODEL)
        xo_ref[...] = xn.reshape(bb * tl, D_MODEL)
        h = _rms(xn) * g_ref[...] * (1.0 + sc_ref[...]) + sh_ref[...]
        h_ref[...] = h.reshape(bb * tl, D_MODEL).astype(BF16)


def _out_proj(y, w, x, g1, sh, sc, g, rows):
    m = x.shape[0]
    tk = 512
    nk = D_MODEL // tk
    mod_spec = pl.BlockSpec((rows.bb, 1, D_MODEL), lambda i, k: (rows.seq_block(i), 0, 0))
    row_spec = pl.BlockSpec((rows.tm, D_MODEL), lambda i, k: (i, 0))
    return pl.pallas_call(
        functools.partial(_out_proj_kernel, bb=rows.bb, tl=rows.tl, nk=nk),
        grid=(rows.n_tiles, nk),
        in_specs=[
            pl.BlockSpec((rows.tm, tk), lambda i, k: (i, k)),
            pl.BlockSpec((tk, D_MODEL), lambda i, k: (k, 0)),
            row_spec, mod_spec, mod_spec, mod_spec,
            pl.BlockSpec((1, D_MODEL), lambda i, k: (0, 0)),
        ],
        out_specs=(row_spec, row_spec),
        out_shape=(jax.ShapeDtypeStruct((m, D_MODEL), F32), jax.ShapeDtypeStruct((m, D_MODEL), BF16)),
        scratch_shapes=[pltpu.VMEM((rows.tm, D_MODEL), F32)],
        compiler_params=_params("arbitrary", "arbitrary"),
        name="out_proj",
    )(y, w, x, g1, sh, sc, g.reshape(1, D_MODEL))


def _ffn_up_kernel(*refs, bb, tl, tiles_per_seq, has_state):
    if has_state:
        h_ref, w_ref, cw_ref, cb_ref, st_ref, act_ref, ns_ref, ext_scr = refs
    else:
        h_ref, w_ref, cw_ref, cb_ref, act_ref, ns_ref, ext_scr, carry_scr = refs
    i = pl.program_id(0)
    j = pl.program_id(1)
    ab = jnp.dot(h_ref[...], w_ref[...].astype(BF16), preferred_element_type=F32)
    a = ab[:, :128].reshape(bb, tl, 128)
    b = ab[:, 128:].reshape(bb, tl, 128)
    if has_state:
        ext_scr[:, 6:8, :] = st_ref[...]
    else:
        @pl.when(i % tiles_per_seq == 0)
        def _():
            carry_scr[j] = jnp.zeros((SUBLANES, 128), F32)

        ext_scr[0, 0:SUBLANES, :] = carry_scr[j]
    ext_scr[:, SUBLANES:SUBLANES + tl, :] = a
    cw = cw_ref[...]
    conv = cb_ref[...] + ext_scr[:, 6:6 + tl, :] * cw[0:1]
    conv = conv + ext_scr[:, 7:7 + tl, :] * cw[1:2]
    conv = conv + a * cw[2:3]
    act_ref[...] = (jax.nn.silu(conv) * b).reshape(bb * tl, 128).astype(BF16)
    ns_ref[...] = ext_scr[:, SUBLANES + tl - 2:SUBLANES + tl, :]
    if not has_state:
        carry_scr[j] = ext_scr[0, tl:tl + SUBLANES, :]


def _ffn_up(h, w_r, conv_w, conv_b, state, n_seq, rows):
    m = h.shape[0]
    has_state = state is not None
    in_specs = [
        pl.BlockSpec((rows.tm, D_MODEL), lambda i, j: (i, 0)),
        pl.BlockSpec((D_MODEL, 256), lambda i, j: (0, j)),
        pl.BlockSpec((CONV_W, 128), lambda i, j: (0, j)),
        pl.BlockSpec((1, 128), lambda i, j: (0, j)),
    ]
    args = [h, w_r, conv_w, conv_b.reshape(1, D_FF)]
    scratch = [pltpu.VMEM((rows.bb, SUBLANES + rows.tl, 128), F32)]
    if has_state:
        in_specs.append(pl.BlockSpec((rows.bb, CONV_W - 1, 128), lambda i, j: (i, 0, j)))
        args.append(state)
    else:
        scratch.append(pltpu.VMEM((FF_CHUNKS, SUBLANES, 128), F32))
    act, tails = pl.pallas_call(
        functools.partial(_ffn_up_kernel, bb=rows.bb, tl=rows.tl, tiles_per_seq=rows.tiles_per_seq,
                          has_state=has_state),
        grid=(rows.n_tiles, FF_CHUNKS),
        in_specs=in_specs,
        out_specs=(
            pl.BlockSpec((rows.tm, 128), lambda i, j: (i, j)),
            pl.BlockSpec((rows.bb, CONV_W - 1, 128), lambda i, j: (i, 0, j)),
        ),
        out_shape=(jax.ShapeDtypeStruct((m, D_FF), BF16),
                   jax.ShapeDtypeStruct((rows.n_tiles * rows.bb, CONV_W - 1, D_FF), F32)),
        scratch_shapes=scratch,
        compiler_params=_params("arbitrary", "arbitrary"),
        name="ffn_up",
    )(*args)
    return act, tails[rows.tiles_per_seq - 1::rows.tiles_per_seq]


def _ffn_down_kernel(act_ref, w_ref, x_ref, g2_ref, fg_ref, o_ref, *, bb, tl, tn, nj, final):
    j = pl.program_id(1)
    f = jnp.dot(act_ref[...], w_ref[...].astype(BF16), preferred_element_type=F32)
    xn = x_ref[...].reshape(bb, tl, tn) + g2_ref[...] * f.reshape(bb, tl, tn)
    o_ref[:, pl.ds(pl.multiple_of(j * tn, tn), tn)] = xn.reshape(bb * tl, tn)
    if final:
        @pl.when(j == nj - 1)
        def _():
            o_ref[...] = _rms(o_ref[...]) * fg_ref[...]


def _ffn_down(act, w, x, g2, fg, rows, final):
    m = x.shape[0]
    tn = 256
    nj = D_MODEL // tn
    return pl.pallas_call(
        functools.partial(_ffn_down_kernel, bb=rows.bb, tl=rows.tl, tn=tn, nj=nj, final=final),
        grid=(rows.n_tiles, nj),
        in_specs=[
            pl.BlockSpec((rows.tm, D_FF), lambda i, j: (i, 0)),
            pl.BlockSpec((D_FF, tn), lambda i, j: (0, j)),
            pl.BlockSpec((rows.tm, tn), lambda i, j: (i, j)),
            pl.BlockSpec((rows.bb, 1, tn), lambda i, j: (rows.seq_block(i), 0, j)),
            pl.BlockSpec((1, D_MODEL), lambda i, j: (0, 0)),
        ],
        out_specs=pl.BlockSpec((rows.tm, D_MODEL), lambda i, j: (i, 0)),
        out_shape=jax.ShapeDtypeStruct((m, D_MODEL), F32),
        compiler_params=_params("arbitrary", "arbitrary"),
        name="ffn_down",
    )(act, w, x, g2, fg.reshape(1, D_MODEL))


def _cmul(ar, ai, br, bi):
    return ar * br - ai * bi, ar * bi + ai * br


def _ssm_consts(a_re, a_im, log_dt, b_re, b_im, c_re, c_im):
    lam_re, lam_im = a_re.astype(F32), a_im.astype(F32)
    dt = jnp.exp(log_dt.astype(F32))[:, None]
    mag = jnp.exp(lam_re * dt)
    ab_re, ab_im = mag * jnp.cos(lam_im * dt), mag * jnp.sin(lam_im * dt)
    den = lam_re * lam_re + lam_im * lam_im
    f_re = ((ab_re - 1.0) * lam_re + ab_im * lam_im) / den
    f_im = (ab_im * lam_re - (ab_re - 1.0) * lam_im) / den
    br, bi = b_re.astype(F32), b_im.astype(F32)
    bb_re = f_re[..., None] * br - f_im[..., None] * bi
    bb_im = f_re[..., None] * bi + f_im[..., None] * br

    gh = SSM_GROUPS // 2
    eye = jnp.eye(gh, dtype=F32)

    def drive_half(m):
        return jnp.einsum("gph,gk->ghkp", m, eye).reshape(gh * SSM_GROUP, gh * SSM_STATE)

    def read_half(m):
        return jnp.einsum("ghp,gk->gpkh", m, eye).reshape(gh * SSM_STATE, gh * SSM_GROUP)

    bmat = jnp.stack([
        jnp.concatenate([drive_half(bb_re[s]), drive_half(bb_im[s])], axis=1)
        for s in (slice(0, gh), slice(gh, 2 * gh))]).astype(BF16)
    cre, cim = c_re.astype(F32), c_im.astype(F32)
    cmat = jnp.stack([
        jnp.concatenate([read_half(cre[s]), -read_half(cim[s])], axis=0)
        for s in (slice(0, gh), slice(gh, 2 * gh))]).astype(BF16)

    p1 = (ab_re.reshape(N_STATE), ab_im.reshape(N_STATE))
    pows = [p1]
    for _ in range(SUBLANES - 1):
        pows.append(_cmul(*pows[-1], *p1))
    row = jnp.arange(SUBLANES)[:, None]
    planes_re, planes_im = [], []
    for s in (1, 2, 4):
        keep = row >= s
        planes_re.append(jnp.where(keep, pows[s - 1][0][None, :], 0.0))
        planes_im.append(jnp.where(keep, pows[s - 1][1][None, :], 0.0))
    planes_re.append(jnp.stack([p[0] for p in pows]))
    planes_im.append(jnp.stack([p[1] for p in pows]))
    pw = jnp.stack([jnp.stack(planes_re), jnp.stack(planes_im)])
    return bmat, cmat, pw


def _rotary_tables(pos):
    half = RET_HD // 2
    inv = ROPE_BASE ** (-jnp.arange(half, dtype=F32) / half)
    ang = pos.astype(F32)[:, None] * inv[None, :]
    cos, sin = jnp.cos(ang), jnp.sin(ang)
    return jnp.concatenate([cos, cos], axis=-1), jnp.concatenate([-sin, sin], axis=-1)


def _retention_consts(chunk, n_seq):
    lg = jnp.log1p(-jnp.exp2(-5.0 - jnp.arange(RET_HEADS, dtype=F32)))
    r = jnp.arange(chunk * n_seq)
    ti = (r % chunk).astype(F32)
    seq = r // chunk
    diff = ti[:, None] - ti[None, :]
    keep = (seq[:, None] == seq[None, :]) & (diff >= 0.0)
    mask = jnp.where(keep[None], jnp.exp(jnp.maximum(diff, 0.0)[None] * lg[:, None, None]), 0.0)
    q_dec = jnp.exp((ti + 1.0)[:, None] * lg[None, :])
    k_dec = jnp.exp((chunk - 1.0 - ti)[:, None] * lg[None, :])
    c_dec = jnp.exp(chunk * lg)
    expand = lambda d: jnp.repeat(d, RET_HD, axis=1)
    return mask, expand(q_dec), expand(k_dec), jnp.broadcast_to(c_dec[:, None], (RET_HEADS, RET_HD))


def _pool_weight(w_pool):
    eye = jnp.eye(len(POOL_WINDOWS), dtype=F32)
    return jnp.einsum("gcd,gk->gckd", w_pool, eye).reshape(D_POOL, D_POOL).astype(BF16)


def _interleave_up(w_up):
    return w_up.reshape(D_MODEL, 2, FF_CHUNKS, 128).transpose(0, 2, 1, 3).reshape(D_MODEL, 2 * D_FF)


def kernel(x_prompt, x_sample, state_ssm_re, state_ssm_im, state_ret, state_pool, state_ffn_conv, c_prompt, c_sample, w_ada, b_ada, norm1_g, w_in, ssm_a_re, ssm_a_im, ssm_log_dt, ssm_b_re, ssm_b_im, ssm_c_re, ssm_c_im, ssm_d, ssm_w_glu, ssm_b_glu, pool_w, pool_scale, w_out, norm2_g, ffn_w_up, ffn_conv_w, ffn_conv_b, ffn_w_down, final_norm_g):
    bp, lp, _ = x_prompt.shape
    bs, ls, _ = x_sample.shape
    tl_mix = 256
    seq_mix = 8

    n_all = bp + bs
    n_pad = -(-n_all // SUBLANES) * SUBLANES
    c_all = jnp.concatenate([c_prompt, c_sample, jnp.zeros((n_pad - n_all, D_MODEL), F32)], axis=0)
    mod = _ada_mod(c_all, w_ada, b_ada).reshape(DEPTH, n_pad, 6, 1, D_MODEL)

    cos_p, sin_p = _rotary_tables(jnp.arange(lp, dtype=jnp.int32))
    cos_s, sin_s = _rotary_tables(PAST_LEN + jnp.arange(ls, dtype=jnp.int32))
    cos_s, sin_s = jnp.tile(cos_s, (seq_mix, 1)), jnp.tile(sin_s, (seq_mix, 1))
    ret_p = _retention_consts(math.gcd(lp, RET_CHUNK), 1)
    ret_s = _retention_consts(math.gcd(ls, RET_CHUNK), seq_mix)
    assert math.gcd(lp, RET_CHUNK) == RET_CHUNK and math.gcd(ls, RET_CHUNK) == ls == SUBLANES

    rows_p = _Rows(bp, lp, 512)
    rows_s = _Rows(bs, ls, 512)

    xp = x_prompt.reshape(bp * lp, D_MODEL)
    xs = x_sample.reshape(bs * ls, D_MODEL)
    new_p = ([], [], [], [], [])
    new_s = ([], [], [], [], [])
    for l in range(DEPTH):
        bmat, cmat, pw = _ssm_consts(ssm_a_re[l], ssm_a_im[l], ssm_log_dt[l], ssm_b_re[l], ssm_b_im[l],
                                     ssm_c_re[l], ssm_c_im[l])
        shared = (bmat, cmat, pw, ssm_d[l].reshape(1, D_SSM), ssm_w_glu[l].astype(BF16),
                  ssm_b_glu[l].reshape(1, D_SSM))
        pool_c = (_pool_weight(pool_w[l]), pool_scale[l].reshape(1, D_POOL))
        w_up_r = _interleave_up(ffn_w_up[l])
        last = l == DEPTH - 1

        def layer(x, mods, rows, mix, conv_state, n_seq):
            sh1, sc1, g1, sh2, sc2, g2 = mods
            proj = _in_proj(x, sh1, sc1, norm1_g[l], w_in[l], rows)
            y, st_re, st_im, st_ret, st_pool = mix(proj)
            x, h2 = _out_proj(y, w_out[l], x, g1, sh2, sc2, norm2_g[l], rows)
            act, st_conv = _ffn_up(h2, w_up_r, ffn_conv_w[l], ffn_conv_b[l], conv_state, n_seq, rows)
            x = _ffn_down(act, ffn_w_down[l], x, g2, final_norm_g, rows, last)
            return x, (st_re, st_im, st_ret, st_pool, st_conv)

        mods_p = [mod[l, :bp, i] for i in range(6)]
        mods_s = [mod[l, bp:n_all, i] for i in range(6)]
        mix_p = lambda proj: _mix_prompt(proj, bp, lp, (cos_p, sin_p) + shared + ret_p + pool_c, tl_mix)
        mix_s = lambda proj: _mix_sample(
            proj, state_ssm_re[l].reshape(bs, N_STATE), state_ssm_im[l].reshape(bs, N_STATE),
            state_ret[l], state_pool[l], bs, ls, (cos_s, sin_s) + shared + ret_s + pool_c, seq_mix)
        xp, st_p = layer(xp, mods_p, rows_p, mix_p, None, bp)
        xs, st_s = layer(xs, mods_s, rows_s, mix_s, state_ffn_conv[l], bs)
        for lst, s in zip(new_p, st_p):
            lst.append(s)
        for lst, s in zip(new_s, st_s):
            lst.append(s)

    def pack(new, n_seq):
        st_re, st_im, st_ret, st_pool, st_conv = [jnp.stack(s) for s in new]
        shape = (DEPTH, n_seq, SSM_GROUPS, SSM_STATE)
        return st_re.reshape(shape), st_im.reshape(shape), st_ret, st_pool, st_conv

    return ((xp.reshape(bp, lp, D_MODEL), xs.reshape(bs, ls, D_MODEL)) + pack(new_p, bp) + pack(new_s, bs))
```

```python
import functools
import math

import jax
import jax.numpy as jnp
from jax import lax
from jax.experimental import pallas as pl
from jax.experimental.pallas import tpu as pltpu

F32 = jnp.float32
BF16 = jnp.bfloat16

D_MODEL = 2048
DEPTH = 2
PAST_LEN = 16384
D_SSM = 512
D_RET = 1024
D_POOL = 512
SSM_GROUP = 16
SSM_GROUPS = 32
SSM_STATE = 64
N_STATE = SSM_GROUPS * SSM_STATE
RET_HEADS = 8
RET_HD = 128
RET_CHUNK = 128
POOL_WINDOWS = (2, 4, 8, 16)
POOL_GD = 128
POOL_BUF = 15
D_FF = 5504
FF_TILE = 512
CONV_W = 3
D_IN = D_SSM + 4 * D_RET + D_POOL
ROPE_BASE = 10000.0
EPS = 1e-6

SUBLANES = 8
SCAN_LANES = 512
VMEM_LIMIT = 56 * 1024 * 1024

Q_OFF = D_SSM
K_OFF = D_SSM + D_RET
V_OFF = D_SSM + 2 * D_RET
G_OFF = D_SSM + 3 * D_RET
C_OFF = D_SSM + 4 * D_RET


def _params(*sem):
    return pltpu.CompilerParams(dimension_semantics=sem, vmem_limit_bytes=VMEM_LIMIT)


def _rms(x):
    return x * lax.rsqrt(jnp.mean(x * x, axis=-1, keepdims=True) + EPS)


def _ada_kernel(c_ref, w_ref, b_ref, o_ref):
    a = jax.nn.silu(c_ref[...]).astype(BF16)
    o_ref[0] = jnp.dot(a, w_ref[0].astype(BF16), preferred_element_type=F32) + b_ref[0]


def _ada_mod(c_all, w_ada, b_ada):
    nseq = c_all.shape[0]
    tn = 1024
    return pl.pallas_call(
        _ada_kernel,
        grid=(DEPTH, 6 * D_MODEL // tn),
        in_specs=[
            pl.BlockSpec((nseq, D_MODEL), lambda l, j: (0, 0)),
            pl.BlockSpec((1, D_MODEL, tn), lambda l, j: (l, 0, j)),
            pl.BlockSpec((1, 1, tn), lambda l, j: (l, 0, j)),
        ],
        out_specs=pl.BlockSpec((1, nseq, tn), lambda l, j: (l, 0, j)),
        out_shape=jax.ShapeDtypeStruct((DEPTH, nseq, 6 * D_MODEL), F32),
        compiler_params=_params("arbitrary", "arbitrary"),
        name="ada_mod",
    )(c_all, w_ada, b_ada.reshape(DEPTH, 1, 6 * D_MODEL))


class _Rows:
    def __init__(self, n_seq, seq_len, tm):
        if seq_len >= tm:
            assert seq_len % tm == 0
            self.bb, self.tl = 1, tm
            self.tiles_per_seq = seq_len // tm
        else:
            assert tm % seq_len == 0 and seq_len == SUBLANES
            self.bb, self.tl = tm // seq_len, seq_len
            self.tiles_per_seq = 1
        self.tm = tm
        self.n_tiles = n_seq * seq_len // tm

    def seq_block(self, i):
        return i // self.tiles_per_seq


def _norm_kernel(*refs, bb, tl, modulated):
    if modulated:
        x_ref, g_ref, sh_ref, sc_ref, o_ref = refs
    else:
        x_ref, g_ref, o_ref = refs
    h = _rms(x_ref[...].reshape(bb, tl, D_MODEL)) * g_ref[...]
    if modulated:
        h = h * (1.0 + sc_ref[...]) + sh_ref[...]
    o_ref[...] = h.reshape(bb * tl, D_MODEL).astype(o_ref.dtype)


def _norm(x, g, layer, mods, rows, out_dtype):
    m = x.shape[0]
    row_spec = pl.BlockSpec((rows.tm, D_MODEL), lambda i: (i, 0))
    in_specs = [row_spec, pl.BlockSpec((None, 1, D_MODEL), lambda i: (layer, 0, 0))]
    args = [x, g]
    if mods is not None:
        mod_spec = pl.BlockSpec((rows.bb, 1, D_MODEL), lambda i: (rows.seq_block(i), 0, 0))
        in_specs += [mod_spec, mod_spec]
        args += list(mods)
    return pl.pallas_call(
        functools.partial(_norm_kernel, bb=rows.bb, tl=rows.tl, modulated=mods is not None),
        grid=(rows.n_tiles,),
        in_specs=in_specs,
        out_specs=row_spec,
        out_shape=jax.ShapeDtypeStruct((m, D_MODEL), out_dtype),
        compiler_params=_params("arbitrary"),
        name="norm",
    )(*args)


def _matmul_kernel(*refs, bb, tl, residual):
    if residual:
        a_ref, w_ref, x_ref, g_ref, o_ref, w_scr = refs
    else:
        a_ref, w_ref, o_ref, w_scr = refs

    @pl.when(pl.program_id(1) == 0)
    def _():
        w_scr[...] = w_ref[...].astype(BF16)

    f = jnp.dot(a_ref[...], w_scr[...], preferred_element_type=F32)
    if residual:
        tn = f.shape[1]
        xn = x_ref[...].reshape(bb, tl, tn) + g_ref[...] * f.reshape(bb, tl, tn)
        o_ref[...] = xn.reshape(bb * tl, tn)
    else:
        o_ref[...] = f


def _matmul(a, w, layer, rows, tn, name, residual=None, single_buffer_w=False):
    m, k = a.shape
    n = w.shape[2]
    w_mode = dict(pipeline_mode=pl.Buffered(1)) if single_buffer_w else {}
    in_specs = [
        pl.BlockSpec((rows.tm, k), lambda j, i: (i, 0)),
        pl.BlockSpec((None, k, tn), lambda j, i: (layer, 0, j), **w_mode),
    ]
    args = [a, w]
    if residual is not None:
        in_specs += [
            pl.BlockSpec((rows.tm, tn), lambda j, i: (i, j)),
            pl.BlockSpec((rows.bb, 1, tn), lambda j, i: (rows.seq_block(i), 0, j)),
        ]
        args += list(residual)
    return pl.pallas_call(
        functools.partial(_matmul_kernel, bb=rows.bb, tl=rows.tl, residual=residual is not None),
        grid=(n // tn, rows.n_tiles),
        in_specs=in_specs,
        out_specs=pl.BlockSpec((rows.tm, tn), lambda j, i: (i, j)),
        out_shape=jax.ShapeDtypeStruct((m, n), F32),
        scratch_shapes=[pltpu.VMEM((k, tn), BF16)],
        compiler_params=_params("arbitrary", "arbitrary"),
        name=name,
    )(*args)


def _ssm_drive(u, bmat_ref, dr_scr, n_rows):
    half_u = D_SSM // 2
    half_n = N_STATE // 2
    ub = u.astype(BF16)
    for hf in range(2):
        d = jnp.dot(ub[:, hf * half_u:(hf + 1) * half_u], bmat_ref[hf], preferred_element_type=F32)
        dr_scr[SUBLANES:SUBLANES + n_rows, hf * half_n:(hf + 1) * half_n] = d[:, :half_n]
        dr_scr[SUBLANES:SUBLANES + n_rows, N_STATE + hf * half_n:N_STATE + (hf + 1) * half_n] = d[:, half_n:]


def _ssm_scan(dr_scr, pw_ref, n_blocks, carry_fn, block_end_fn=None):
    def body(b, carry):
        r = pl.multiple_of(b * SUBLANES + SUBLANES, SUBLANES)
        for c in range(N_STATE // SCAN_LANES):
            lo = c * SCAN_LANES
            re_cols = slice(lo, lo + SCAN_LANES)
            im_cols = slice(N_STATE + lo, N_STATE + lo + SCAN_LANES)
            d_re = dr_scr[pl.ds(r, SUBLANES), re_cols]
            d_im = dr_scr[pl.ds(r, SUBLANES), im_cols]
            for si, s in enumerate((1, 2, 4)):
                p_re = pw_ref[0, si, :, re_cols]
                p_im = pw_ref[1, si, :, re_cols]
                r_re = pltpu.roll(d_re, s, 0)
                r_im = pltpu.roll(d_im, s, 0)
                d_re, d_im = (d_re + (p_re * r_re - p_im * r_im),
                              d_im + (p_re * r_im + p_im * r_re))
            c_re, c_im = carry_fn(b, lo)
            a_re = pw_ref[0, 3, :, re_cols]
            a_im = pw_ref[1, 3, :, re_cols]
            h_re = d_re + (a_re * c_re - a_im * c_im)
            h_im = d_im + (a_re * c_im + a_im * c_re)
            dr_scr[pl.ds(r, SUBLANES), re_cols] = h_re
            dr_scr[pl.ds(r, SUBLANES), im_cols] = h_im
            if block_end_fn is not None:
                block_end_fn(b, lo, h_re[SUBLANES - 1:SUBLANES], h_im[SUBLANES - 1:SUBLANES])
        return carry

    lax.fori_loop(0, n_blocks, body, 0)


def _ssm_out(u, dr_scr, cmat_ref, dskip_ref, wglu_ref, bglu_ref, n_rows):
    half_u = D_SSM // 2
    half_n = N_STATE // 2
    parts = []
    for hf in range(2):
        h_re = dr_scr[SUBLANES:SUBLANES + n_rows, hf * half_n:(hf + 1) * half_n].astype(BF16)
        h_im = dr_scr[SUBLANES:SUBLANES + n_rows, N_STATE + hf * half_n:N_STATE + (hf + 1) * half_n].astype(BF16)
        parts.append(jnp.dot(h_re, cmat_ref[hf, :half_n], preferred_element_type=F32)
                     + jnp.dot(h_im, cmat_ref[hf, half_n:], preferred_element_type=F32))
    y = jnp.concatenate(parts, axis=-1) + dskip_ref[...] * u
    ya = jax.nn.gelu(y)
    gate = jnp.dot(ya.astype(BF16), wglu_ref[...], preferred_element_type=F32) + bglu_ref[...]
    return ya * jax.nn.sigmoid(gate)


def _rotary(x, cosv, sinv):
    return x * cosv + pltpu.roll(x, RET_HD // 2, 1) * sinv


def _head_norm_gate(o, g):
    mu = jnp.mean(o, axis=-1, keepdims=True)
    var = jnp.mean(jnp.square(o - mu), axis=-1, keepdims=True)
    return (o - mu) * lax.rsqrt(var + EPS) * jax.nn.silu(g)


def _pool_counts(pos):
    return [jnp.minimum(pos + 1, w).astype(F32) for w in POOL_WINDOWS]


def _mix_prompt_kernel(proj_ref, cos_ref, sin_ref, bmat_ref, cmat_ref, pw_ref, dskip_ref, wglu_ref,
                       bglu_ref, mask_ref, qdec_ref, kdec_ref, cdec_ref, wpool_ref, pscale_ref,
                       y_ref, hre_ref, him_ref, sret_ref, pbuf_ref,
                       dr_scr, pool_scr, *, tl):
    t = pl.program_id(1)

    @pl.when(t == 0)
    def _():
        dr_scr[0:SUBLANES, :] = jnp.zeros((SUBLANES, 2 * N_STATE), F32)
        sret_ref[...] = jnp.zeros(sret_ref.shape, F32)
        pool_scr[0:16, :] = jnp.zeros((16, D_POOL), F32)

    u = proj_ref[:, 0:D_SSM]
    _ssm_drive(u, bmat_ref, dr_scr, tl)

    def carry_fn(b, lo):
        row = b * SUBLANES + SUBLANES - 1
        return (dr_scr[pl.ds(row, 1), lo:lo + SCAN_LANES],
                dr_scr[pl.ds(row, 1), N_STATE + lo:N_STATE + lo + SCAN_LANES])

    _ssm_scan(dr_scr, pw_ref, tl // SUBLANES, carry_fn)
    y_ref[:, 0:D_SSM] = _ssm_out(u, dr_scr, cmat_ref, dskip_ref, wglu_ref, bglu_ref, tl).astype(BF16)
    last = dr_scr[tl:tl + SUBLANES, :]
    dr_scr[0:SUBLANES, :] = last
    hre_ref[0] = last[SUBLANES - 1:SUBLANES, 0:N_STATE]
    him_ref[0] = last[SUBLANES - 1:SUBLANES, N_STATE:2 * N_STATE]

    scale = RET_HD ** -0.5

    def chunk_body(ci, carry):
        r0 = pl.multiple_of(ci * RET_CHUNK, RET_CHUNK)
        rows = pl.ds(r0, RET_CHUNK)
        cosv = cos_ref[rows, :]
        sinv = sin_ref[rows, :]
        for h in range(RET_HEADS):
            hs = slice(h * RET_HD, (h + 1) * RET_HD)
            q = proj_ref[rows, Q_OFF + h * RET_HD:Q_OFF + (h + 1) * RET_HD]
            k = proj_ref[rows, K_OFF + h * RET_HD:K_OFF + (h + 1) * RET_HD]
            v = proj_ref[rows, V_OFF + h * RET_HD:V_OFF + (h + 1) * RET_HD].astype(BF16)
            g = proj_ref[rows, G_OFF + h * RET_HD:G_OFF + (h + 1) * RET_HD]
            qr = _rotary(q, cosv, sinv)
            kr = _rotary(k, cosv, sinv) * scale
            s_old = sret_ref[0, h]
            sc = lax.dot_general(qr.astype(BF16), kr.astype(BF16), (((1,), (1,)), ((), ())),
                                 preferred_element_type=F32) * mask_ref[h]
            o = (jnp.dot(sc.astype(BF16), v, preferred_element_type=F32)
                 + jnp.dot((qr * qdec_ref[:, hs]).astype(BF16), s_old.astype(BF16),
                           preferred_element_type=F32))
            sret_ref[0, h] = (s_old * cdec_ref[h:h + 1, :]
                              + lax.dot_general((kr * kdec_ref[:, hs]).astype(BF16), v,
                                                (((0,), (0,)), ((), ())), preferred_element_type=F32))
            y_ref[rows, D_SSM + h * RET_HD:D_SSM + (h + 1) * RET_HD] = _head_norm_gate(o, g).astype(BF16)
        return carry

    lax.fori_loop(0, tl // RET_CHUNK, chunk_body, 0)

    uc = proj_ref[:, C_OFF:C_OFF + D_POOL]
    pool_scr[16:16 + tl, :] = uc
    pos = t * tl + lax.broadcasted_iota(jnp.int32, (tl, POOL_GD), 0)
    counts = _pool_counts(pos)
    parts = []
    for gi, w in enumerate(POOL_WINDOWS):
        lanes = slice(gi * POOL_GD, (gi + 1) * POOL_GD)
        acc = pool_scr[16:16 + tl, lanes]
        for j in range(1, w):
            acc = acc + pool_scr[16 - j:16 - j + tl, lanes]
        parts.append(acc / counts[gi] - uc[:, lanes])
    pooled = jnp.concatenate(parts, axis=-1).astype(BF16)
    yc = jnp.dot(pooled, wpool_ref[...], preferred_element_type=F32) * pscale_ref[...]
    y_ref[:, D_SSM + D_RET:] = yc.astype(BF16)
    pbuf_ref[0] = pool_scr[tl + 1:tl + 16, :]
    pool_scr[0:16, :] = pool_scr[tl:tl + 16, :]


def _mix_prompt(proj, n_seq, seq_len, consts, tl):
    (cos_t, sin_t, bmat, cmat, pw, dskip, wglu, bglu, mask, qdec, kdec, cdec, wpool, pscale) = consts
    nt = seq_len // tl
    row_map = lambda b, t: (b * nt + t, 0)

    def const_spec(a):
        nd = a.ndim
        return pl.BlockSpec(a.shape, lambda b, t: (0,) * nd)

    in_specs = [
        pl.BlockSpec((tl, D_IN), row_map),
        pl.BlockSpec((tl, RET_HD), lambda b, t: (t, 0)),
        pl.BlockSpec((tl, RET_HD), lambda b, t: (t, 0)),
    ] + [const_spec(a) for a in (bmat, cmat, pw, dskip, wglu, bglu, mask, qdec, kdec, cdec, wpool, pscale)]
    out_shape = (
        jax.ShapeDtypeStruct((n_seq * seq_len, D_MODEL), BF16),
        jax.ShapeDtypeStruct((n_seq, 1, N_STATE), F32),
        jax.ShapeDtypeStruct((n_seq, 1, N_STATE), F32),
        jax.ShapeDtypeStruct((n_seq, RET_HEADS, RET_HD, RET_HD), F32),
        jax.ShapeDtypeStruct((n_seq, POOL_BUF, D_POOL), F32),
    )
    out_specs = (
        pl.BlockSpec((tl, D_MODEL), row_map),
        pl.BlockSpec((1, 1, N_STATE), lambda b, t: (b, 0, 0)),
        pl.BlockSpec((1, 1, N_STATE), lambda b, t: (b, 0, 0)),
        pl.BlockSpec((1, RET_HEADS, RET_HD, RET_HD), lambda b, t: (b, 0, 0, 0)),
        pl.BlockSpec((1, POOL_BUF, D_POOL), lambda b, t: (b, 0, 0)),
    )
    return pl.pallas_call(
        functools.partial(_mix_prompt_kernel, tl=tl),
        grid=(n_seq, nt),
        in_specs=in_specs,
        out_specs=out_specs,
        out_shape=out_shape,
        scratch_shapes=[
            pltpu.VMEM((SUBLANES + tl, 2 * N_STATE), F32),
            pltpu.VMEM((16 + tl, D_POOL), F32),
        ],
        compiler_params=_params("arbitrary", "arbitrary"),
        name="mix_prompt",
    )(proj, cos_t, sin_t, bmat, cmat, pw, dskip, wglu, bglu, mask, qdec, kdec, cdec, wpool, pscale)


def _mix_sample_kernel(proj_ref, cos_ref, sin_ref, bmat_ref, cmat_ref, pw_ref, dskip_ref, wglu_ref,
                       bglu_ref, mask_ref, qdec_ref, kdec_ref, cdec_ref, wpool_ref, pscale_ref,
                       h0re_ref, h0im_ref, s0_ref, pool0_ref,
                       y_ref, hre_ref, him_ref, sret_ref, pbuf_ref,
                       dr_scr, o_scr, qd_scr, kd_scr, pool_scr, *, bs, seq_len, pos0):
    n_rows = bs * seq_len

    u = proj_ref[:, 0:D_SSM]
    _ssm_drive(u, bmat_ref, dr_scr, n_rows)

    def carry_fn(b, lo):
        return (h0re_ref[pl.ds(b, 1), lo:lo + SCAN_LANES], h0im_ref[pl.ds(b, 1), lo:lo + SCAN_LANES])

    def block_end_fn(b, lo, h_re, h_im):
        hre_ref[pl.ds(b, 1), lo:lo + SCAN_LANES] = h_re
        him_ref[pl.ds(b, 1), lo:lo + SCAN_LANES] = h_im

    _ssm_scan(dr_scr, pw_ref, bs, carry_fn, block_end_fn)
    y_ref[:, 0:D_SSM] = _ssm_out(u, dr_scr, cmat_ref, dskip_ref, wglu_ref, bglu_ref, n_rows).astype(BF16)

    scale = RET_HD ** -0.5
    cosv = cos_ref[...]
    sinv = sin_ref[...]
    for h in range(RET_HEADS):
        hs = slice(h * RET_HD, (h + 1) * RET_HD)
        q = proj_ref[:, Q_OFF + h * RET_HD:Q_OFF + (h + 1) * RET_HD]
        k = proj_ref[:, K_OFF + h * RET_HD:K_OFF + (h + 1) * RET_HD]
        v = proj_ref[:, V_OFF + h * RET_HD:V_OFF + (h + 1) * RET_HD].astype(BF16)
        qr = _rotary(q, cosv, sinv)
        kr = _rotary(k, cosv, sinv) * scale
        sc = lax.dot_general(qr.astype(BF16), kr.astype(BF16), (((1,), (1,)), ((), ())),
                             preferred_element_type=F32) * mask_ref[h]
        o_scr[:, hs] = jnp.dot(sc.astype(BF16), v, preferred_element_type=F32)
        qd_scr[:, hs] = qr * qdec_ref[:, hs]
        kd_scr[:, hs] = kr * kdec_ref[:, hs]

    row_seq = lax.broadcasted_iota(jnp.int32, (n_rows, 1), 0) // seq_len

    def seq_body(s, carry):
        own = row_seq == s
        for h in range(RET_HEADS):
            hs = slice(h * RET_HD, (h + 1) * RET_HD)
            s_old = s0_ref[s, h]
            qd = jnp.where(own, qd_scr[:, hs], 0.0).astype(BF16)
            kd = jnp.where(own, kd_scr[:, hs], 0.0).astype(BF16)
            v = proj_ref[:, V_OFF + h * RET_HD:V_OFF + (h + 1) * RET_HD].astype(BF16)
            o_scr[:, hs] += jnp.dot(qd, s_old.astype(BF16), preferred_element_type=F32)
            sret_ref[s, h] = (s_old * cdec_ref[h:h + 1, :]
                              + lax.dot_general(kd, v, (((0,), (0,)), ((), ())),
                                                preferred_element_type=F32))
        return carry

    lax.fori_loop(0, bs, seq_body, 0)
    for h in range(RET_HEADS):
        hs = slice(h * RET_HD, (h + 1) * RET_HD)
        g = proj_ref[:, G_OFF + h * RET_HD:G_OFF + (h + 1) * RET_HD]
        y_ref[:, D_SSM + h * RET_HD:D_SSM + (h + 1) * RET_HD] = _head_norm_gate(o_scr[:, hs], g).astype(BF16)

    uc = proj_ref[:, C_OFF:C_OFF + D_POOL].reshape(bs, seq_len, D_POOL)
    pool_scr[:, 1:16, :] = pool0_ref[...]
    pool_scr[:, 16:16 + seq_len, :] = uc
    pos = pos0 + lax.broadcasted_iota(jnp.int32, (1, seq_len, POOL_GD), 1)
    counts = _pool_counts(pos)
    parts = []
    for gi, w in enumerate(POOL_WINDOWS):
        lanes = slice(gi * POOL_GD, (gi + 1) * POOL_GD)
        acc = pool_scr[:, 16:16 + seq_len, lanes]
        for j in range(1, w):
            acc = acc + pool_scr[:, 16 - j:16 - j + seq_len, lanes]
        parts.append(acc / counts[gi] - uc[:, :, lanes])
    pooled = jnp.concatenate(parts, axis=-1).reshape(n_rows, D_POOL).astype(BF16)
    yc = jnp.dot(pooled, wpool_ref[...], preferred_element_type=F32) * pscale_ref[...]
    y_ref[:, D_SSM + D_RET:] = yc.astype(BF16)
    pbuf_ref[...] = pool_scr[:, seq_len + 1:seq_len + 16, :]


def _mix_sample(proj, states, layer, prev_out, n_seq, seq_len, consts, bs):
    n_rows = bs * seq_len

    def const_spec(a):
        nd = a.ndim
        return pl.BlockSpec(a.shape, lambda i: (0,) * nd)

    state_specs = [
        pl.BlockSpec((None, bs, N_STATE), lambda i: (layer, i, 0)),
        pl.BlockSpec((None, bs, N_STATE), lambda i: (layer, i, 0)),
        pl.BlockSpec((None, bs, RET_HEADS, RET_HD, RET_HD), lambda i: (layer, i, 0, 0, 0)),
        pl.BlockSpec((None, bs, POOL_BUF, D_POOL), lambda i: (layer, i, 0, 0)),
    ]
    in_specs = [pl.BlockSpec((n_rows, D_IN), lambda i: (i, 0))] + [const_spec(a) for a in consts] + state_specs
    args = [proj, *consts, *states]
    n_in = len(args)
    aliases = {}
    if prev_out is not None:
        in_specs += [pl.BlockSpec(memory_space=pl.ANY)] * len(prev_out)
        aliases = {n_in + k: 1 + k for k in range(len(prev_out))}
        args += list(prev_out)
    n_args = len(args)
    out_shape = (
        jax.ShapeDtypeStruct((n_seq * seq_len, D_MODEL), BF16),
        jax.ShapeDtypeStruct((DEPTH, n_seq, N_STATE), F32),
        jax.ShapeDtypeStruct((DEPTH, n_seq, N_STATE), F32),
        jax.ShapeDtypeStruct((DEPTH, n_seq, RET_HEADS, RET_HD, RET_HD), F32),
        jax.ShapeDtypeStruct((DEPTH, n_seq, POOL_BUF, D_POOL), F32),
    )
    out_specs = (pl.BlockSpec((n_rows, D_MODEL), lambda i: (i, 0)),) + tuple(state_specs)

    def body(*refs):
        _mix_sample_kernel(*refs[:n_in], *refs[n_args:], bs=bs, seq_len=seq_len, pos0=PAST_LEN)

    return pl.pallas_call(
        body,
        grid=(n_seq // bs,),
        in_specs=in_specs,
        out_specs=out_specs,
        out_shape=out_shape,
        scratch_shapes=[
            pltpu.VMEM((SUBLANES + n_rows, 2 * N_STATE), F32),
            pltpu.VMEM((n_rows, D_RET), F32),
            pltpu.VMEM((n_rows, D_RET), F32),
            pltpu.VMEM((n_rows, D_RET), F32),
            pltpu.VMEM((bs, 16 + seq_len, D_POOL), F32),
        ],
        input_output_aliases=aliases,
        compiler_params=_params("arbitrary"),
        name="mix_sample",
    )(*args)


def _out_proj_kernel(y_ref, w_ref, x_ref, g1_ref, sh_ref, sc_ref, g_ref, xo_ref, h_ref, w_scr, *, bb, tl):
    @pl.when(pl.program_id(0) == 0)
    def _():
        w_scr[...] = w_ref[...].astype(BF16)

    f = jnp.dot(y_ref[...], w_scr[...], preferred_element_type=F32)
    xn = x_ref[...].reshape(bb, tl, D_MODEL) + g1_ref[...] * f.reshape(bb, tl, D_MODEL)
    xo_ref[...] = xn.reshape(bb * tl, D_MODEL)
    h = _rms(xn) * g_ref[...] * (1.0 + sc_ref[...]) + sh_ref[...]
    h_ref[...] = h.reshape(bb * tl, D_MODEL).astype(BF16)


def _out_proj(y, w, layer, x, g1, sh, sc, g, rows):
    m = x.shape[0]
    mod_spec = pl.BlockSpec((rows.bb, 1, D_MODEL), lambda i: (rows.seq_block(i), 0, 0))
    row_spec = pl.BlockSpec((rows.tm, D_MODEL), lambda i: (i, 0))
    return pl.pallas_call(
        functools.partial(_out_proj_kernel, bb=rows.bb, tl=rows.tl),
        grid=(rows.n_tiles,),
        in_specs=[
            row_spec,
            pl.BlockSpec((None, D_MODEL, D_MODEL), lambda i: (layer, 0, 0), pipeline_mode=pl.Buffered(1)),
            row_spec, mod_spec, mod_spec, mod_spec,
            pl.BlockSpec((None, 1, D_MODEL), lambda i: (layer, 0, 0)),
        ],
        out_specs=(row_spec, row_spec),
        out_shape=(jax.ShapeDtypeStruct((m, D_MODEL), F32), jax.ShapeDtypeStruct((m, D_MODEL), BF16)),
        scratch_shapes=[pltpu.VMEM((D_MODEL, D_MODEL), BF16)],
        compiler_params=_params("arbitrary"),
        name="out_proj",
    )(y, w, x, g1, sh, sc, g)


def _ffn_up_kernel(*refs, bb, tl, tiles_per_seq, has_state):
    if has_state:
        h_ref, wa_ref, wb_ref, cw_ref, cb_ref, st_ref, act_ref, ns_ref, w_scr, ext_scr = refs
    else:
        h_ref, wa_ref, wb_ref, cw_ref, cb_ref, act_ref, ns_ref, w_scr, ext_scr, carry_scr = refs
    i = pl.program_id(1)

    @pl.when(i == 0)
    def _():
        w_scr[:, :FF_TILE] = wa_ref[...].astype(BF16)
        w_scr[:, FF_TILE:] = wb_ref[...].astype(BF16)

    ab = jnp.dot(h_ref[...], w_scr[...], preferred_element_type=F32)
    a = ab[:, :FF_TILE].reshape(bb, tl, FF_TILE)
    b = ab[:, FF_TILE:].reshape(bb, tl, FF_TILE)
    if has_state:
        ext_scr[:, 6:8, :] = st_ref[...]
    else:
        @pl.when(i % tiles_per_seq == 0)
        def _():
            carry_scr[...] = jnp.zeros((SUBLANES, FF_TILE), F32)

        ext_scr[0, 0:SUBLANES, :] = carry_scr[...]
    ext_scr[:, SUBLANES:SUBLANES + tl, :] = a
    cw = cw_ref[...]
    conv = cb_ref[...] + ext_scr[:, 6:6 + tl, :] * cw[0:1]
    conv = conv + ext_scr[:, 7:7 + tl, :] * cw[1:2]
    conv = conv + a * cw[2:3]
    act_ref[...] = (jax.nn.silu(conv) * b).reshape(bb * tl, FF_TILE).astype(BF16)
    ns_ref[...] = ext_scr[:, SUBLANES + tl - 2:SUBLANES + tl, :]
    if not has_state:
        carry_scr[...] = ext_scr[0, tl:tl + SUBLANES, :]


def _ffn_up(h, w_up, w_val, conv_w, conv_b, layer, state, prev_tails, rows):
    m = h.shape[0]
    has_state = state is not None
    in_specs = [
        pl.BlockSpec((rows.tm, D_MODEL), lambda c, i: (i, 0)),
        pl.BlockSpec((None, D_MODEL, FF_TILE), lambda c, i: (layer, 0, c)),
        pl.BlockSpec((None, D_MODEL, FF_TILE), lambda c, i: (layer, 0, c)),
        pl.BlockSpec((None, CONV_W, FF_TILE), lambda c, i: (layer, 0, c)),
        pl.BlockSpec((None, 1, FF_TILE), lambda c, i: (layer, 0, c)),
    ]
    args = [h, w_up, w_val, conv_w, conv_b]
    scratch = [pltpu.VMEM((D_MODEL, 2 * FF_TILE), BF16),
               pltpu.VMEM((rows.bb, SUBLANES + rows.tl, FF_TILE), F32)]
    if has_state:
        in_specs.append(pl.BlockSpec((None, rows.bb, CONV_W - 1, FF_TILE), lambda c, i: (layer, i, 0, c)))
        args.append(state)
    else:
        scratch.append(pltpu.VMEM((SUBLANES, FF_TILE), F32))
    n_in = len(args)
    aliases = {}
    if prev_tails is not None:
        aliases = {n_in: 1}
        in_specs.append(pl.BlockSpec(memory_space=pl.ANY))
        args.append(prev_tails)
    n_args = len(args)

    def body(*refs):
        _ffn_up_kernel(*refs[:n_in], *refs[n_args:], bb=rows.bb, tl=rows.tl,
                       tiles_per_seq=rows.tiles_per_seq, has_state=has_state)

    return pl.pallas_call(
        body,
        grid=(pl.cdiv(D_FF, FF_TILE), rows.n_tiles),
        in_specs=in_specs,
        out_specs=(
            pl.BlockSpec((rows.tm, FF_TILE), lambda c, i: (i, c)),
            pl.BlockSpec((None, rows.bb, CONV_W - 1, FF_TILE), lambda c, i: (layer, i, 0, c)),
        ),
        out_shape=(jax.ShapeDtypeStruct((m, D_FF), BF16),
                   jax.ShapeDtypeStruct((DEPTH, rows.n_tiles * rows.bb, CONV_W - 1, D_FF), F32)),
        scratch_shapes=scratch,
        input_output_aliases=aliases,
        compiler_params=_params("arbitrary", "arbitrary"),
        name="ffn_up",
    )(*args)


def _cmul(ar, ai, br, bi):
    return ar * br - ai * bi, ar * bi + ai * br


def _ssm_consts(a_re, a_im, log_dt, b_re, b_im, c_re, c_im):
    lam_re, lam_im = a_re.astype(F32), a_im.astype(F32)
    dt = jnp.exp(log_dt.astype(F32))[:, None]
    mag = jnp.exp(lam_re * dt)
    ab_re, ab_im = mag * jnp.cos(lam_im * dt), mag * jnp.sin(lam_im * dt)
    den = lam_re * lam_re + lam_im * lam_im
    f_re = ((ab_re - 1.0) * lam_re + ab_im * lam_im) / den
    f_im = (ab_im * lam_re - (ab_re - 1.0) * lam_im) / den
    br, bi = b_re.astype(F32), b_im.astype(F32)
    bb_re = f_re[..., None] * br - f_im[..., None] * bi
    bb_im = f_re[..., None] * bi + f_im[..., None] * br

    gh = SSM_GROUPS // 2
    eye = jnp.eye(gh, dtype=F32)

    def drive_half(m):
        return jnp.einsum("gph,gk->ghkp", m, eye).reshape(gh * SSM_GROUP, gh * SSM_STATE)

    def read_half(m):
        return jnp.einsum("ghp,gk->gpkh", m, eye).reshape(gh * SSM_STATE, gh * SSM_GROUP)

    bmat = jnp.stack([
        jnp.concatenate([drive_half(bb_re[s]), drive_half(bb_im[s])], axis=1)
        for s in (slice(0, gh), slice(gh, 2 * gh))]).astype(BF16)
    cre, cim = c_re.astype(F32), c_im.astype(F32)
    cmat = jnp.stack([
        jnp.concatenate([read_half(cre[s]), -read_half(cim[s])], axis=0)
        for s in (slice(0, gh), slice(gh, 2 * gh))]).astype(BF16)

    p1 = (ab_re.reshape(N_STATE), ab_im.reshape(N_STATE))
    pows = [p1]
    for _ in range(SUBLANES - 1):
        pows.append(_cmul(*pows[-1], *p1))
    row = jnp.arange(SUBLANES)[:, None]
    planes_re, planes_im = [], []
    for s in (1, 2, 4):
        keep = row >= s
        planes_re.append(jnp.where(keep, pows[s - 1][0][None, :], 0.0))
        planes_im.append(jnp.where(keep, pows[s - 1][1][None, :], 0.0))
    planes_re.append(jnp.stack([p[0] for p in pows]))
    planes_im.append(jnp.stack([p[1] for p in pows]))
    pw = jnp.stack([jnp.stack(planes_re), jnp.stack(planes_im)])
    return bmat, cmat, pw


def _rotary_tables(pos):
    half = RET_HD // 2
    inv = ROPE_BASE ** (-jnp.arange(half, dtype=F32) / half)
    ang = pos.astype(F32)[:, None] * inv[None, :]
    cos, sin = jnp.cos(ang), jnp.sin(ang)
    return jnp.concatenate([cos, cos], axis=-1), jnp.concatenate([-sin, sin], axis=-1)


def _retention_consts(chunk, n_seq):
    lg = jnp.log1p(-jnp.exp2(-5.0 - jnp.arange(RET_HEADS, dtype=F32)))
    r = jnp.arange(chunk * n_seq)
    ti = (r % chunk).astype(F32)
    seq = r // chunk
    diff = ti[:, None] - ti[None, :]
    keep = (seq[:, None] == seq[None, :]) & (diff >= 0.0)
    mask = jnp.where(keep[None], jnp.exp(jnp.maximum(diff, 0.0)[None] * lg[:, None, None]), 0.0)
    q_dec = jnp.exp((ti + 1.0)[:, None] * lg[None, :])
    k_dec = jnp.exp((chunk - 1.0 - ti)[:, None] * lg[None, :])
    c_dec = jnp.exp(chunk * lg)
    expand = lambda d: jnp.repeat(d, RET_HD, axis=1)
    return mask, expand(q_dec), expand(k_dec), jnp.broadcast_to(c_dec[:, None], (RET_HEADS, RET_HD))


def _pool_weight(w_pool):
    eye = jnp.eye(len(POOL_WINDOWS), dtype=F32)
    return jnp.einsum("gcd,gk->gckd", w_pool, eye).reshape(D_POOL, D_POOL).astype(BF16)


def kernel(x_prompt, x_sample, state_ssm_re, state_ssm_im, state_ret, state_pool, state_ffn_conv, c_prompt, c_sample, w_ada, b_ada, norm1_g, w_in, ssm_a_re, ssm_a_im, ssm_log_dt, ssm_b_re, ssm_b_im, ssm_c_re, ssm_c_im, ssm_d, ssm_w_glu, ssm_b_glu, pool_w, pool_scale, w_out, norm2_g, ffn_w_up, ffn_conv_w, ffn_conv_b, ffn_w_down, final_norm_g):
    bp, lp, _ = x_prompt.shape
    bs, ls, _ = x_sample.shape
    tl_mix = 256
    seq_mix = 8

    n_all = bp + bs
    n_pad = -(-n_all // SUBLANES) * SUBLANES
    c_all = jnp.concatenate([c_prompt, c_sample, jnp.zeros((n_pad - n_all, D_MODEL), F32)], axis=0)
    mod = _ada_mod(c_all, w_ada, b_ada).reshape(DEPTH, n_pad, 6, 1, D_MODEL)

    cos_p, sin_p = _rotary_tables(jnp.arange(lp, dtype=jnp.int32))
    cos_s, sin_s = _rotary_tables(PAST_LEN + jnp.arange(ls, dtype=jnp.int32))
    cos_s, sin_s = jnp.tile(cos_s, (seq_mix, 1)), jnp.tile(sin_s, (seq_mix, 1))
    assert math.gcd(lp, RET_CHUNK) == RET_CHUNK and math.gcd(ls, RET_CHUNK) == ls == SUBLANES
    ret_p = _retention_consts(RET_CHUNK, 1)
    ret_s = _retention_consts(ls, seq_mix)

    big_p, big_s = _Rows(bp, lp, 1024), _Rows(bs, ls, 1024)
    mid_p, mid_s = _Rows(bp, lp, 512), _Rows(bs, ls, 512)

    norm1 = norm1_g.reshape(DEPTH, 1, D_MODEL)
    norm2 = norm2_g.reshape(DEPTH, 1, D_MODEL)
    final_g = final_norm_g.reshape(1, 1, D_MODEL)
    conv_b = ffn_conv_b.reshape(DEPTH, 1, D_FF)
    w_val = ffn_w_up[:, :, D_FF:]
    sample_states = (state_ssm_re.reshape(DEPTH, bs, N_STATE), state_ssm_im.reshape(DEPTH, bs, N_STATE),
                     state_ret, state_pool)

    xp = x_prompt.reshape(bp * lp, D_MODEL)
    xs = x_sample.reshape(bs * ls, D_MODEL)
    new_p = ([], [], [], [], [])
    s_mix = None
    s_conv = None
    for l in range(DEPTH):
        bmat, cmat, pw = _ssm_consts(ssm_a_re[l], ssm_a_im[l], ssm_log_dt[l], ssm_b_re[l], ssm_b_im[l],
                                     ssm_c_re[l], ssm_c_im[l])
        shared = (bmat, cmat, pw, ssm_d[l].reshape(1, D_SSM), ssm_w_glu[l].astype(BF16),
                  ssm_b_glu[l].reshape(1, D_SSM))
        pool_c = (_pool_weight(pool_w[l]), pool_scale[l].reshape(1, D_POOL))
        sh1p, sc1p, g1p, sh2p, sc2p, g2p = [mod[l, :bp, i] for i in range(6)]
        sh1s, sc1s, g1s, sh2s, sc2s, g2s = [mod[l, bp:n_all, i] for i in range(6)]

        h = _norm(xp, norm1, l, (sh1p, sc1p), mid_p, BF16)
        proj = _matmul(h, w_in, l, big_p, 1024, "in_proj")
        y, st_re, st_im, st_ret, st_pool = _mix_prompt(
            proj, bp, lp, (cos_p, sin_p) + shared + ret_p + pool_c, tl_mix)
        xp, h2 = _out_proj(y, w_out, l, xp, g1p, sh2p, sc2p, norm2, mid_p)
        act, tails = _ffn_up(h2, ffn_w_up, w_val, ffn_conv_w, conv_b, l, None, None, big_p)
        xp = _matmul(act, ffn_w_down, l, big_p, 512, "ffn_down", residual=(xp, g2p), single_buffer_w=True)
        st_conv = tails[l, big_p.tiles_per_seq - 1::big_p.tiles_per_seq]
        for lst, st in zip(new_p, (st_re, st_im, st_ret, st_pool, st_conv)):
            lst.append(st)

        h = _norm(xs, norm1, l, (sh1s, sc1s), mid_s, BF16)
        proj = _matmul(h, w_in, l, big_s, 1024, "in_proj")
        y, *s_mix = _mix_sample(proj, sample_states, l, s_mix, bs, ls,
                                (cos_s, sin_s) + shared + ret_s + pool_c, seq_mix)
        xs, h2 = _out_proj(y, w_out, l, xs, g1s, sh2s, sc2s, norm2, mid_s)
        act, s_conv = _ffn_up(h2, ffn_w_up, w_val, ffn_conv_w, conv_b, l, state_ffn_conv, s_conv, big_s)
        xs = _matmul(act, ffn_w_down, l, big_s, 512, "ffn_down", residual=(xs, g2s), single_buffer_w=True)

    yp = _norm(xp, final_g, 0, None, mid_p, F32)
    ys = _norm(xs, final_g, 0, None, mid_s, F32)

    p_re, p_im, p_ret, p_pool, p_conv = [jnp.stack(st) for st in new_p]
    s_re, s_im, s_ret, s_pool = s_mix
    shape_p = (DEPTH, bp, SSM_GROUPS, SSM_STATE)
    shape_s = (DEPTH, bs, SSM_GROUPS, SSM_STATE)
    return (yp.reshape(bp, lp, D_MODEL), ys.reshape(bs, ls, D_MODEL),
            p_re.reshape(shape_p), p_im.reshape(shape_p), p_ret, p_pool, p_conv,
            s_re.reshape(shape_s), s_im.reshape(shape_s), s_ret, s_pool, s_conv)
```

```python
import functools
import math

import jax
import jax.numpy as jnp
from jax import lax
from jax.experimental import pallas as pl
from jax.experimental.pallas import tpu as pltpu

F32 = jnp.float32
BF16 = jnp.bfloat16

D_MODEL = 2048
DEPTH = 2
PAST_LEN = 16384
D_SSM = 512
D_RET = 1024
D_POOL = 512
SSM_GROUP = 16
SSM_GROUPS = 32
SSM_STATE = 64
N_STATE = SSM_GROUPS * SSM_STATE
RET_HEADS = 8
RET_HD = 128
RET_CHUNK = 128
POOL_WINDOWS = (2, 4, 8, 16)
POOL_GD = 128
POOL_BUF = 15
D_FF = 5504
FF_TILE = 512
FF_LAST = D_FF % FF_TILE
CONV_W = 3
D_IN = D_SSM + 4 * D_RET + D_POOL
ROPE_BASE = 10000.0
EPS = 1e-6

SUBLANES = 8
LANES = 128
ROW_TILE = 1024
ROW_TILE_FULL = 512
FF_SUB_ROWS = 512
IN_PROJ_COLS = 1024
FF_DOWN_COLS = 512
ADA_COLS = 1024
MOD_SHIFT1, MOD_SCALE1, MOD_GATE1, MOD_SHIFT2, MOD_SCALE2, MOD_GATE2 = range(6)
SCAN_LANES = 512
VMEM_LIMIT = 56 * 1024 * 1024

Q_OFF = D_SSM
K_OFF = D_SSM + D_RET
V_OFF = D_SSM + 2 * D_RET
G_OFF = D_SSM + 3 * D_RET
C_OFF = D_SSM + 4 * D_RET


def _params(*sem):
    return pltpu.CompilerParams(dimension_semantics=sem, vmem_limit_bytes=VMEM_LIMIT)


def _rms(x):
    return x * lax.rsqrt(jnp.mean(x * x, axis=-1, keepdims=True) + EPS)


def _ada_kernel(c_ref, w_ref, b_ref, o_ref):
    a = jax.nn.silu(c_ref[...]).astype(BF16)
    res = jnp.dot(a, w_ref[...].astype(BF16), preferred_element_type=F32) + b_ref[...]
    for r in range(res.shape[0]):
        o_ref[r] = res[r:r + 1, :]


def _ada_mod(c_all, w_ada, b_ada):
    nseq = c_all.shape[0]
    per_vec = D_MODEL // ADA_COLS
    return pl.pallas_call(
        _ada_kernel,
        grid=(DEPTH, 6 * per_vec),
        in_specs=[
            pl.BlockSpec((nseq, D_MODEL), lambda l, j: (0, 0)),
            pl.BlockSpec((None, D_MODEL, ADA_COLS), lambda l, j: (l, 0, j)),
            pl.BlockSpec((None, 1, ADA_COLS), lambda l, j: (l, 0, j)),
        ],
        out_specs=pl.BlockSpec((None, None, nseq, 1, ADA_COLS), lambda l, j: (l, j // per_vec, 0, 0, j % per_vec)),
        out_shape=jax.ShapeDtypeStruct((DEPTH, 6, nseq, 1, D_MODEL), F32),
        compiler_params=_params("arbitrary", "arbitrary"),
        name="ada_mod",
    )(c_all, w_ada, b_ada.reshape(DEPTH, 1, 6 * D_MODEL))


class _Rows:
    def __init__(self, n_seq, seq_len, tm, first_seq):
        if seq_len >= tm:
            assert seq_len % tm == 0
            self.bb, self.tl = 1, tm
            self.tiles_per_seq = seq_len // tm
        else:
            assert tm % seq_len == 0 and seq_len == SUBLANES
            self.bb, self.tl = tm // seq_len, seq_len
            self.tiles_per_seq = 1
        assert first_seq % self.bb == 0
        self.tm = tm
        self.n_tiles = n_seq * seq_len // tm
        self.first_block = first_seq // self.bb

    def mod_block(self, i):
        return self.first_block + i // self.tiles_per_seq

    def mod_spec(self, layer, vec, row_tile_of, cols=D_MODEL, col_block_of=lambda *ids: 0):
        return pl.BlockSpec(
            (None, None, self.bb, 1, cols),
            lambda *ids: (layer, vec, self.mod_block(row_tile_of(*ids)), 0, col_block_of(*ids)))


def _norm_kernel(*refs, bb, tl, modulated):
    if modulated:
        x_ref, g_ref, sh_ref, sc_ref, o_ref = refs
    else:
        x_ref, g_ref, o_ref = refs
    h = _rms(x_ref[...].reshape(bb, tl, D_MODEL)) * g_ref[...]
    if modulated:
        h = h * (1.0 + sc_ref[...]) + sh_ref[...]
    o_ref[...] = h.reshape(bb * tl, D_MODEL).astype(o_ref.dtype)


def _norm(x, g, layer, mods, rows, out_dtype):
    m = x.shape[0]
    row_spec = pl.BlockSpec((rows.tm, D_MODEL), lambda i: (i, 0))
    in_specs = [row_spec, pl.BlockSpec((None, 1, D_MODEL), lambda i: (layer, 0, 0))]
    args = [x, g]
    if mods is not None:
        mod, shift_vec, scale_vec = mods
        in_specs += [rows.mod_spec(layer, shift_vec, lambda i: i), rows.mod_spec(layer, scale_vec, lambda i: i)]
        args += [mod, mod]
    return pl.pallas_call(
        functools.partial(_norm_kernel, bb=rows.bb, tl=rows.tl, modulated=mods is not None),
        grid=(rows.n_tiles,),
        in_specs=in_specs,
        out_specs=row_spec,
        out_shape=jax.ShapeDtypeStruct((m, D_MODEL), out_dtype),
        compiler_params=_params("arbitrary"),
        name="norm",
    )(*args)


def _matmul_kernel(*refs, bb, tl, residual):
    if residual:
        a_ref, w_ref, x_ref, g_ref, o_ref, w_scr = refs
    else:
        a_ref, w_ref, o_ref, w_scr = refs

    @pl.when(pl.program_id(1) == 0)
    def _():
        w_scr[...] = w_ref[...].astype(BF16)

    f = jnp.dot(a_ref[...], w_scr[...], preferred_element_type=F32)
    if residual:
        tn = f.shape[1]
        xn = x_ref[...].reshape(bb, tl, tn) + g_ref[...] * f.reshape(bb, tl, tn)
        o_ref[...] = xn.reshape(bb * tl, tn)
    else:
        o_ref[...] = f


def _matmul(a, w, layer, rows, tn, name, residual=None, single_buffer_w=False):
    m, k = a.shape
    n = w.shape[2]
    w_mode = dict(pipeline_mode=pl.Buffered(1)) if single_buffer_w else {}
    in_specs = [
        pl.BlockSpec((rows.tm, k), lambda j, i: (i, 0)),
        pl.BlockSpec((None, k, tn), lambda j, i: (layer, 0, j), **w_mode),
    ]
    args = [a, w]
    if residual is not None:
        x, mod, gate_vec = residual
        in_specs += [
            pl.BlockSpec((rows.tm, tn), lambda j, i: (i, j)),
            rows.mod_spec(layer, gate_vec, lambda j, i: i, tn, lambda j, i: j),
        ]
        args += [x, mod]
    return pl.pallas_call(
        functools.partial(_matmul_kernel, bb=rows.bb, tl=rows.tl, residual=residual is not None),
        grid=(n // tn, rows.n_tiles),
        in_specs=in_specs,
        out_specs=pl.BlockSpec((rows.tm, tn), lambda j, i: (i, j)),
        out_shape=jax.ShapeDtypeStruct((m, n), F32),
        scratch_shapes=[pltpu.VMEM((k, tn), BF16)],
        compiler_params=_params("arbitrary", "arbitrary"),
        name=name,
    )(*args)


def _ssm_drive(u, bmat_ref, dr_scr, n_rows):
    half_u = D_SSM // 2
    half_n = N_STATE // 2
    ub = u.astype(BF16)
    for hf in range(2):
        d = jnp.dot(ub[:, hf * half_u:(hf + 1) * half_u], bmat_ref[hf], preferred_element_type=F32)
        dr_scr[SUBLANES:SUBLANES + n_rows, hf * half_n:(hf + 1) * half_n] = d[:, :half_n]
        dr_scr[SUBLANES:SUBLANES + n_rows, N_STATE + hf * half_n:N_STATE + (hf + 1) * half_n] = d[:, half_n:]


def _ssm_scan(dr_scr, pw_ref, n_blocks, carry_fn, block_end_fn=None):
    def body(b, carry):
        r = pl.multiple_of(b * SUBLANES + SUBLANES, SUBLANES)
        for c in range(N_STATE // SCAN_LANES):
            lo = c * SCAN_LANES
            re_cols = slice(lo, lo + SCAN_LANES)
            im_cols = slice(N_STATE + lo, N_STATE + lo + SCAN_LANES)
            d_re = dr_scr[pl.ds(r, SUBLANES), re_cols]
            d_im = dr_scr[pl.ds(r, SUBLANES), im_cols]
            for si, s in enumerate((1, 2, 4)):
                p_re = pw_ref[0, si, :, re_cols]
                p_im = pw_ref[1, si, :, re_cols]
                r_re = pltpu.roll(d_re, s, 0)
                r_im = pltpu.roll(d_im, s, 0)
                d_re, d_im = (d_re + (p_re * r_re - p_im * r_im),
                              d_im + (p_re * r_im + p_im * r_re))
            c_re, c_im = carry_fn(b, lo)
            a_re = pw_ref[0, 3, :, re_cols]
            a_im = pw_ref[1, 3, :, re_cols]
            h_re = d_re + (a_re * c_re - a_im * c_im)
            h_im = d_im + (a_re * c_im + a_im * c_re)
            dr_scr[pl.ds(r, SUBLANES), re_cols] = h_re
            dr_scr[pl.ds(r, SUBLANES), im_cols] = h_im
            if block_end_fn is not None:
                block_end_fn(b, lo, h_re[SUBLANES - 1:SUBLANES], h_im[SUBLANES - 1:SUBLANES])
        return carry

    lax.fori_loop(0, n_blocks, body, 0)


def _ssm_out(u, dr_scr, cmat_ref, dskip_ref, wglu_ref, bglu_ref, n_rows):
    half_u = D_SSM // 2
    half_n = N_STATE // 2
    parts = []
    for hf in range(2):
        h_re = dr_scr[SUBLANES:SUBLANES + n_rows, hf * half_n:(hf + 1) * half_n].astype(BF16)
        h_im = dr_scr[SUBLANES:SUBLANES + n_rows, N_STATE + hf * half_n:N_STATE + (hf + 1) * half_n].astype(BF16)
        parts.append(jnp.dot(h_re, cmat_ref[hf, :half_n], preferred_element_type=F32)
                     + jnp.dot(h_im, cmat_ref[hf, half_n:], preferred_element_type=F32))
    y = jnp.concatenate(parts, axis=-1) + dskip_ref[...] * u
    ya = jax.nn.gelu(y)
    gate = jnp.dot(ya.astype(BF16), wglu_ref[...], preferred_element_type=F32) + bglu_ref[...]
    return ya * jax.nn.sigmoid(gate)


def _rotary(x, cosv, sinv):
    return x * cosv + pltpu.roll(x, RET_HD // 2, 1) * sinv


def _head_norm_gate(o, g):
    mu = jnp.mean(o, axis=-1, keepdims=True)
    var = jnp.mean(jnp.square(o - mu), axis=-1, keepdims=True)
    return (o - mu) * lax.rsqrt(var + EPS) * jax.nn.silu(g)


def _pool_counts(pos):
    return [jnp.minimum(pos + 1, w).astype(F32) for w in POOL_WINDOWS]


def _mix_prompt_kernel(proj_ref, cos_ref, sin_ref, bmat_ref, cmat_ref, pw_ref, dskip_ref, wglu_ref,
                       bglu_ref, mask_ref, qdec_ref, kdec_ref, cdec_ref, wpool_ref, pscale_ref,
                       y_ref, hre_ref, him_ref, sret_ref, pbuf_ref,
                       dr_scr, pool_scr, *, tl):
    t = pl.program_id(1)

    @pl.when(t == 0)
    def _():
        dr_scr[0:SUBLANES, :] = jnp.zeros((SUBLANES, 2 * N_STATE), F32)
        sret_ref[...] = jnp.zeros(sret_ref.shape, F32)
        pool_scr[0:16, :] = jnp.zeros((16, D_POOL), F32)

    u = proj_ref[:, 0:D_SSM]
    _ssm_drive(u, bmat_ref, dr_scr, tl)

    def carry_fn(b, lo):
        row = b * SUBLANES + SUBLANES - 1
        return (dr_scr[pl.ds(row, 1), lo:lo + SCAN_LANES],
                dr_scr[pl.ds(row, 1), N_STATE + lo:N_STATE + lo + SCAN_LANES])

    _ssm_scan(dr_scr, pw_ref, tl // SUBLANES, carry_fn)
    y_ref[:, 0:D_SSM] = _ssm_out(u, dr_scr, cmat_ref, dskip_ref, wglu_ref, bglu_ref, tl).astype(BF16)
    last = dr_scr[tl:tl + SUBLANES, :]
    dr_scr[0:SUBLANES, :] = last
    hre_ref[0] = last[SUBLANES - 1:SUBLANES, 0:N_STATE]
    him_ref[0] = last[SUBLANES - 1:SUBLANES, N_STATE:2 * N_STATE]

    scale = RET_HD ** -0.5

    def chunk_body(ci, carry):
        r0 = pl.multiple_of(ci * RET_CHUNK, RET_CHUNK)
        rows = pl.ds(r0, RET_CHUNK)
        cosv = cos_ref[rows, :]
        sinv = sin_ref[rows, :]
        for h in range(RET_HEADS):
            hs = slice(h * RET_HD, (h + 1) * RET_HD)
            q = proj_ref[rows, Q_OFF + h * RET_HD:Q_OFF + (h + 1) * RET_HD]
            k = proj_ref[rows, K_OFF + h * RET_HD:K_OFF + (h + 1) * RET_HD]
            v = proj_ref[rows, V_OFF + h * RET_HD:V_OFF + (h + 1) * RET_HD].astype(BF16)
            g = proj_ref[rows, G_OFF + h * RET_HD:G_OFF + (h + 1) * RET_HD]
            qr = _rotary(q, cosv, sinv)
            kr = _rotary(k, cosv, sinv) * scale
            s_old = sret_ref[0, h]
            sc = lax.dot_general(qr.astype(BF16), kr.astype(BF16), (((1,), (1,)), ((), ())),
                                 preferred_element_type=F32) * mask_ref[h]
            o = (jnp.dot(sc.astype(BF16), v, preferred_element_type=F32)
                 + jnp.dot((qr * qdec_ref[:, hs]).astype(BF16), s_old.astype(BF16),
                           preferred_element_type=F32))
            sret_ref[0, h] = (s_old * cdec_ref[h:h + 1, :]
                              + lax.dot_general((kr * kdec_ref[:, hs]).astype(BF16), v,
                                                (((0,), (0,)), ((), ())), preferred_element_type=F32))
            y_ref[rows, D_SSM + h * RET_HD:D_SSM + (h + 1) * RET_HD] = _head_norm_gate(o, g).astype(BF16)
        return carry

    lax.fori_loop(0, tl // RET_CHUNK, chunk_body, 0)

    uc = proj_ref[:, C_OFF:C_OFF + D_POOL]
    pool_scr[16:16 + tl, :] = uc
    pos = t * tl + lax.broadcasted_iota(jnp.int32, (tl, POOL_GD), 0)
    counts = _pool_counts(pos)
    parts = []
    for gi, w in enumerate(POOL_WINDOWS):
        lanes = slice(gi * POOL_GD, (gi + 1) * POOL_GD)
        acc = pool_scr[16:16 + tl, lanes]
        for j in range(1, w):
            acc = acc + pool_scr[16 - j:16 - j + tl, lanes]
        parts.append(acc / counts[gi] - uc[:, lanes])
    pooled = jnp.concatenate(parts, axis=-1).astype(BF16)
    yc = jnp.dot(pooled, wpool_ref[...], preferred_element_type=F32) * pscale_ref[...]
    y_ref[:, D_SSM + D_RET:] = yc.astype(BF16)
    pbuf_ref[0] = pool_scr[tl + 1:tl + 16, :]
    pool_scr[0:16, :] = pool_scr[tl:tl + 16, :]


def _mix_prompt(proj, n_seq, seq_len, consts, tl):
    (cos_t, sin_t, bmat, cmat, pw, dskip, wglu, bglu, mask, qdec, kdec, cdec, wpool, pscale) = consts
    nt = seq_len // tl
    row_map = lambda b, t: (b * nt + t, 0)

    def const_spec(a):
        nd = a.ndim
        return pl.BlockSpec(a.shape, lambda b, t: (0,) * nd)

    in_specs = [
        pl.BlockSpec((tl, D_IN), row_map),
        pl.BlockSpec((tl, RET_HD), lambda b, t: (t, 0)),
        pl.BlockSpec((tl, RET_HD), lambda b, t: (t, 0)),
    ] + [const_spec(a) for a in (bmat, cmat, pw, dskip, wglu, bglu, mask, qdec, kdec, cdec, wpool, pscale)]
    out_shape = (
        jax.ShapeDtypeStruct((n_seq * seq_len, D_MODEL), BF16),
        jax.ShapeDtypeStruct((n_seq, 1, N_STATE), F32),
        jax.ShapeDtypeStruct((n_seq, 1, N_STATE), F32),
        jax.ShapeDtypeStruct((n_seq, RET_HEADS, RET_HD, RET_HD), F32),
        jax.ShapeDtypeStruct((n_seq, POOL_BUF, D_POOL), F32),
    )
    out_specs = (
        pl.BlockSpec((tl, D_MODEL), row_map),
        pl.BlockSpec((1, 1, N_STATE), lambda b, t: (b, 0, 0)),
        pl.BlockSpec((1, 1, N_STATE), lambda b, t: (b, 0, 0)),
        pl.BlockSpec((1, RET_HEADS, RET_HD, RET_HD), lambda b, t: (b, 0, 0, 0)),
        pl.BlockSpec((1, POOL_BUF, D_POOL), lambda b, t: (b, 0, 0)),
    )
    return pl.pallas_call(
        functools.partial(_mix_prompt_kernel, tl=tl),
        grid=(n_seq, nt),
        in_specs=in_specs,
        out_specs=out_specs,
        out_shape=out_shape,
        scratch_shapes=[
            pltpu.VMEM((SUBLANES + tl, 2 * N_STATE), F32),
            pltpu.VMEM((16 + tl, D_POOL), F32),
        ],
        compiler_params=_params("arbitrary", "arbitrary"),
        name="mix_prompt",
    )(proj, cos_t, sin_t, bmat, cmat, pw, dskip, wglu, bglu, mask, qdec, kdec, cdec, wpool, pscale)


def _mix_sample_kernel(proj_ref, cos_ref, sin_ref, bmat_ref, cmat_ref, pw_ref, dskip_ref, wglu_ref,
                       bglu_ref, mask_ref, qdec_ref, kdec_ref, cdec_ref, wpool_ref, pscale_ref,
                       h0re_ref, h0im_ref, s0_ref, pool0_ref,
                       y_ref, hre_ref, him_ref, sret_ref, pbuf_ref,
                       dr_scr, o_scr, qd_scr, kd_scr, pool_scr, *, bs, seq_len, pos0):
    n_rows = bs * seq_len

    u = proj_ref[:, 0:D_SSM]
    _ssm_drive(u, bmat_ref, dr_scr, n_rows)

    def carry_fn(b, lo):
        return (h0re_ref[pl.ds(b, 1), lo:lo + SCAN_LANES], h0im_ref[pl.ds(b, 1), lo:lo + SCAN_LANES])

    def block_end_fn(b, lo, h_re, h_im):
        hre_ref[pl.ds(b, 1), lo:lo + SCAN_LANES] = h_re
        him_ref[pl.ds(b, 1), lo:lo + SCAN_LANES] = h_im

    _ssm_scan(dr_scr, pw_ref, bs, carry_fn, block_end_fn)
    y_ref[:, 0:D_SSM] = _ssm_out(u, dr_scr, cmat_ref, dskip_ref, wglu_ref, bglu_ref, n_rows).astype(BF16)

    scale = RET_HD ** -0.5
    cosv = cos_ref[...]
    sinv = sin_ref[...]
    for h in range(RET_HEADS):
        hs = slice(h * RET_HD, (h + 1) * RET_HD)
        q = proj_ref[:, Q_OFF + h * RET_HD:Q_OFF + (h + 1) * RET_HD]
        k = proj_ref[:, K_OFF + h * RET_HD:K_OFF + (h + 1) * RET_HD]
        v = proj_ref[:, V_OFF + h * RET_HD:V_OFF + (h + 1) * RET_HD].astype(BF16)
        qr = _rotary(q, cosv, sinv)
        kr = _rotary(k, cosv, sinv) * scale
        sc = lax.dot_general(qr.astype(BF16), kr.astype(BF16), (((1,), (1,)), ((), ())),
                             preferred_element_type=F32) * mask_ref[h]
        o_scr[:, hs] = jnp.dot(sc.astype(BF16), v, preferred_element_type=F32)
        qd_scr[:, hs] = qr * qdec_ref[:, hs]
        kd_scr[:, hs] = kr * kdec_ref[:, hs]

    row_seq = lax.broadcasted_iota(jnp.int32, (n_rows, 1), 0) // seq_len

    def seq_body(s, carry):
        own = row_seq == s
        for h in range(RET_HEADS):
            hs = slice(h * RET_HD, (h + 1) * RET_HD)
            s_old = s0_ref[s, h]
            qd = jnp.where(own, qd_scr[:, hs], 0.0).astype(BF16)
            kd = jnp.where(own, kd_scr[:, hs], 0.0).astype(BF16)
            v = proj_ref[:, V_OFF + h * RET_HD:V_OFF + (h + 1) * RET_HD].astype(BF16)
            o_scr[:, hs] += jnp.dot(qd, s_old.astype(BF16), preferred_element_type=F32)
            sret_ref[s, h] = (s_old * cdec_ref[h:h + 1, :]
                              + lax.dot_general(kd, v, (((0,), (0,)), ((), ())),
                                                preferred_element_type=F32))
        return carry

    lax.fori_loop(0, bs, seq_body, 0)
    for h in range(RET_HEADS):
        hs = slice(h * RET_HD, (h + 1) * RET_HD)
        g = proj_ref[:, G_OFF + h * RET_HD:G_OFF + (h + 1) * RET_HD]
        y_ref[:, D_SSM + h * RET_HD:D_SSM + (h + 1) * RET_HD] = _head_norm_gate(o_scr[:, hs], g).astype(BF16)

    uc = proj_ref[:, C_OFF:C_OFF + D_POOL].reshape(bs, seq_len, D_POOL)
    pool_scr[:, 1:16, :] = pool0_ref[...]
    pool_scr[:, 16:16 + seq_len, :] = uc
    pos = pos0 + lax.broadcasted_iota(jnp.int32, (1, seq_len, POOL_GD), 1)
    counts = _pool_counts(pos)
    parts = []
    for gi, w in enumerate(POOL_WINDOWS):
        lanes = slice(gi * POOL_GD, (gi + 1) * POOL_GD)
        acc = pool_scr[:, 16:16 + seq_len, lanes]
        for j in range(1, w):
            acc = acc + pool_scr[:, 16 - j:16 - j + seq_len, lanes]
        parts.append(acc / counts[gi] - uc[:, :, lanes])
    pooled = jnp.concatenate(parts, axis=-1).reshape(n_rows, D_POOL).astype(BF16)
    yc = jnp.dot(pooled, wpool_ref[...], preferred_element_type=F32) * pscale_ref[...]
    y_ref[:, D_SSM + D_RET:] = yc.astype(BF16)
    pbuf_ref[...] = pool_scr[:, seq_len + 1:seq_len + 16, :]


def _mix_sample(proj, states, layer, prev_out, n_seq, seq_len, consts, bs):
    n_rows = bs * seq_len

    def const_spec(a):
        nd = a.ndim
        return pl.BlockSpec(a.shape, lambda i: (0,) * nd)

    state_specs = [
        pl.BlockSpec((None, bs, N_STATE), lambda i: (layer, i, 0)),
        pl.BlockSpec((None, bs, N_STATE), lambda i: (layer, i, 0)),
        pl.BlockSpec((None, bs, RET_HEADS, RET_HD, RET_HD), lambda i: (layer, i, 0, 0, 0)),
        pl.BlockSpec((None, bs, POOL_BUF, D_POOL), lambda i: (layer, i, 0, 0)),
    ]
    in_specs = [pl.BlockSpec((n_rows, D_IN), lambda i: (i, 0))] + [const_spec(a) for a in consts] + state_specs
    args = [proj, *consts, *states]
    n_in = len(args)
    aliases = {}
    if prev_out is not None:
        in_specs += [pl.BlockSpec(memory_space=pl.ANY)] * len(prev_out)
        aliases = {n_in + k: 1 + k for k in range(len(prev_out))}
        args += list(prev_out)
    n_args = len(args)
    out_shape = (
        jax.ShapeDtypeStruct((n_seq * seq_len, D_MODEL), BF16),
        jax.ShapeDtypeStruct((DEPTH, n_seq, N_STATE), F32),
        jax.ShapeDtypeStruct((DEPTH, n_seq, N_STATE), F32),
        jax.ShapeDtypeStruct((DEPTH, n_seq, RET_HEADS, RET_HD, RET_HD), F32),
        jax.ShapeDtypeStruct((DEPTH, n_seq, POOL_BUF, D_POOL), F32),
    )
    out_specs = (pl.BlockSpec((n_rows, D_MODEL), lambda i: (i, 0)),) + tuple(state_specs)

    def body(*refs):
        _mix_sample_kernel(*refs[:n_in], *refs[n_args:], bs=bs, seq_len=seq_len, pos0=PAST_LEN)

    return pl.pallas_call(
        body,
        grid=(n_seq // bs,),
        in_specs=in_specs,
        out_specs=out_specs,
        out_shape=out_shape,
        scratch_shapes=[
            pltpu.VMEM((SUBLANES + n_rows, 2 * N_STATE), F32),
            pltpu.VMEM((n_rows, D_RET), F32),
            pltpu.VMEM((n_rows, D_RET), F32),
            pltpu.VMEM((n_rows, D_RET), F32),
            pltpu.VMEM((bs, 16 + seq_len, D_POOL), F32),
        ],
        input_output_aliases=aliases,
        compiler_params=_params("arbitrary"),
        name="mix_sample",
    )(*args)


def _out_proj_kernel(y_ref, w_ref, x_ref, g1_ref, sh_ref, sc_ref, g_ref, xo_ref, h_ref, w_scr, *, bb, tl):
    @pl.when(pl.program_id(0) == 0)
    def _():
        w_scr[...] = w_ref[...].astype(BF16)

    f = jnp.dot(y_ref[...], w_scr[...], preferred_element_type=F32)
    xn = x_ref[...].reshape(bb, tl, D_MODEL) + g1_ref[...] * f.reshape(bb, tl, D_MODEL)
    xo_ref[...] = xn.reshape(bb * tl, D_MODEL)
    h = _rms(xn) * g_ref[...] * (1.0 + sc_ref[...]) + sh_ref[...]
    h_ref[...] = h.reshape(bb * tl, D_MODEL).astype(BF16)


def _out_proj(y, w, layer, x, mod, g, rows):
    m = x.shape[0]
    row_spec = pl.BlockSpec((rows.tm, D_MODEL), lambda i: (i, 0))
    return pl.pallas_call(
        functools.partial(_out_proj_kernel, bb=rows.bb, tl=rows.tl),
        grid=(rows.n_tiles,),
        in_specs=[
            row_spec,
            pl.BlockSpec((None, D_MODEL, D_MODEL), lambda i: (layer, 0, 0), pipeline_mode=pl.Buffered(1)),
            row_spec,
            rows.mod_spec(layer, MOD_GATE1, lambda i: i),
            rows.mod_spec(layer, MOD_SHIFT2, lambda i: i),
            rows.mod_spec(layer, MOD_SCALE2, lambda i: i),
            pl.BlockSpec((None, 1, D_MODEL), lambda i: (layer, 0, 0)),
        ],
        out_specs=(row_spec, row_spec),
        out_shape=(jax.ShapeDtypeStruct((m, D_MODEL), F32), jax.ShapeDtypeStruct((m, D_MODEL), BF16)),
        scratch_shapes=[pltpu.VMEM((D_MODEL, D_MODEL), BF16)],
        compiler_params=_params("arbitrary"),
        name="out_proj",
    )(y, w, x, mod, mod, mod, g)


def _conv_gate(a, b, am1, am2, cw, cb):
    conv = cb + am2 * cw[0:1]
    conv = conv + am1 * cw[1:2]
    conv = conv + a * cw[2:3]
    return jax.nn.silu(conv) * b


def _ffn_up_kernel(*refs, bb, tl, tiles_per_seq, has_state, n_chunks):
    if has_state:
        h_ref, wa_ref, wb_ref, cw_ref, cb_ref, st_ref, act_ref, ns_ref, w_scr = refs
    else:
        h_ref, wa_ref, wb_ref, cw_ref, cb_ref, act_ref, ns_ref, w_scr, carry_scr = refs
    c = pl.program_id(0)
    i = pl.program_id(1)

    @pl.when(i == 0)
    def _():
        w_scr[:, :FF_TILE] = wa_ref[...].astype(BF16)

    @pl.when((i == 0) & (c < n_chunks - 1))
    def _():
        w_scr[:, FF_TILE:] = wb_ref[0].astype(BF16)

    @pl.when((i == 0) & (c == n_chunks - 1))
    def _():
        w_scr[:, FF_TILE:FF_TILE + FF_LAST] = wb_ref[0, :, FF_TILE - FF_LAST:].astype(BF16)

    cw = cw_ref[...]
    cb = cb_ref[...]
    n_sub = bb * tl // FF_SUB_ROWS
    if has_state:
        sb = FF_SUB_ROWS // tl
        tok = lax.broadcasted_iota(jnp.int32, (sb, tl, FF_TILE), 1)
        for r in range(n_sub):
            rows = slice(r * FF_SUB_ROWS, (r + 1) * FF_SUB_ROWS)
            sq = slice(r * sb, (r + 1) * sb)
            ab = jnp.dot(h_ref[rows, :], w_scr[...], preferred_element_type=F32)
            a = ab[:, :FF_TILE].reshape(sb, tl, FF_TILE)
            b = ab[:, FF_TILE:].reshape(sb, tl, FF_TILE)
            p0 = st_ref[sq, 0:1, :]
            p1 = st_ref[sq, 1:2, :]
            am1 = jnp.where(tok == 0, p1, pltpu.roll(a, 1, 1))
            am2 = jnp.where(tok == 0, p0, jnp.where(tok == 1, p1, pltpu.roll(a, 2, 1)))
            act = _conv_gate(a, b, am1, am2, cw, cb)
            act_ref[rows, :] = act.reshape(FF_SUB_ROWS, FF_TILE).astype(BF16)
            ns_ref[sq] = a[:, tl - 2:tl, :]
    else:
        @pl.when(i % tiles_per_seq == 0)
        def _():
            carry_scr[...] = jnp.zeros((SUBLANES, FF_TILE), F32)

        tok = lax.broadcasted_iota(jnp.int32, (SUBLANES, FF_TILE), 0)
        for r in range(n_sub):
            rows = slice(r * FF_SUB_ROWS, (r + 1) * FF_SUB_ROWS)
            ab = jnp.dot(h_ref[rows, :], w_scr[...], preferred_element_type=F32)
            a = ab[:, :FF_TILE]
            b = ab[:, FF_TILE:]
            p0 = carry_scr[SUBLANES - 2:SUBLANES - 1, :]
            p1 = carry_scr[SUBLANES - 1:SUBLANES, :]
            r1 = pltpu.roll(a, 1, 0)
            r2 = pltpu.roll(a, 2, 0)
            head1 = jnp.where(tok == 0, p1, r1[:SUBLANES])
            head2 = jnp.where(tok == 0, p0, jnp.where(tok == 1, p1, r2[:SUBLANES]))
            am1 = jnp.concatenate([head1, r1[SUBLANES:]], axis=0)
            am2 = jnp.concatenate([head2, r2[SUBLANES:]], axis=0)
            act_ref[rows, :] = _conv_gate(a, b, am1, am2, cw, cb).astype(BF16)
            carry_scr[...] = a[FF_SUB_ROWS - SUBLANES:, :]
        ns_ref[0] = carry_scr[SUBLANES - 2:SUBLANES, :]


def _ffn_up(h, w_up, conv_w, conv_b, layer, state, prev_tails, rows):
    m = h.shape[0]
    has_state = state is not None
    n_chunks = pl.cdiv(D_FF, FF_TILE)
    in_specs = [
        pl.BlockSpec((rows.tm, D_MODEL), lambda c, i: (i, 0)),
        pl.BlockSpec((None, D_MODEL, FF_TILE), lambda c, i: (layer, 0, c)),
        pl.BlockSpec((pl.Element(1), pl.Element(D_MODEL), pl.Element(FF_TILE)),
                     lambda c, i: (layer, 0, LANES * jnp.minimum((D_FF + c * FF_TILE) // LANES,
                                                                 (2 * D_FF - FF_TILE) // LANES))),
        pl.BlockSpec((None, CONV_W, FF_TILE), lambda c, i: (layer, 0, c)),
        pl.BlockSpec((None, 1, FF_TILE), lambda c, i: (layer, 0, c)),
    ]
    args = [h, w_up, w_up, conv_w, conv_b]
    scratch = [pltpu.VMEM((D_MODEL, 2 * FF_TILE), BF16)]
    if has_state:
        in_specs.append(pl.BlockSpec((None, rows.bb, CONV_W - 1, FF_TILE), lambda c, i: (layer, i, 0, c)))
        args.append(state)
    else:
        scratch.append(pltpu.VMEM((SUBLANES, FF_TILE), F32))
    n_in = len(args)
    aliases = {}
    if prev_tails is not None:
        aliases = {n_in: 1}
        in_specs.append(pl.BlockSpec(memory_space=pl.ANY))
        args.append(prev_tails)
    n_args = len(args)

    def body(*refs):
        _ffn_up_kernel(*refs[:n_in], *refs[n_args:], bb=rows.bb, tl=rows.tl,
                       tiles_per_seq=rows.tiles_per_seq, has_state=has_state, n_chunks=n_chunks)

    return pl.pallas_call(
        body,
        grid=(n_chunks, rows.n_tiles),
        in_specs=in_specs,
        out_specs=(
            pl.BlockSpec((rows.tm, FF_TILE), lambda c, i: (i, c)),
            pl.BlockSpec((None, rows.bb, CONV_W - 1, FF_TILE), lambda c, i: (layer, i, 0, c)),
        ),
        out_shape=(jax.ShapeDtypeStruct((m, D_FF), BF16),
                   jax.ShapeDtypeStruct((DEPTH, rows.n_tiles * rows.bb, CONV_W - 1, D_FF), F32)),
        scratch_shapes=scratch,
        input_output_aliases=aliases,
        compiler_params=_params("arbitrary", "arbitrary"),
        name="ffn_up",
    )(*args)


def _cmul(ar, ai, br, bi):
    return ar * br - ai * bi, ar * bi + ai * br


def _ssm_consts(a_re, a_im, log_dt, b_re, b_im, c_re, c_im):
    lam_re, lam_im = a_re.astype(F32), a_im.astype(F32)
    dt = jnp.exp(log_dt.astype(F32))[:, None]
    mag = jnp.exp(lam_re * dt)
    ab_re, ab_im = mag * jnp.cos(lam_im * dt), mag * jnp.sin(lam_im * dt)
    den = lam_re * lam_re + lam_im * lam_im
    f_re = ((ab_re - 1.0) * lam_re + ab_im * lam_im) / den
    f_im = (ab_im * lam_re - (ab_re - 1.0) * lam_im) / den
    br, bi = b_re.astype(F32), b_im.astype(F32)
    bb_re = f_re[..., None] * br - f_im[..., None] * bi
    bb_im = f_re[..., None] * bi + f_im[..., None] * br

    gh = SSM_GROUPS // 2
    eye = jnp.eye(gh, dtype=F32)

    def drive_half(m):
        return jnp.einsum("gph,gk->ghkp", m, eye).reshape(gh * SSM_GROUP, gh * SSM_STATE)

    def read_half(m):
        return jnp.einsum("ghp,gk->gpkh", m, eye).reshape(gh * SSM_STATE, gh * SSM_GROUP)

    bmat = jnp.stack([
        jnp.concatenate([drive_half(bb_re[s]), drive_half(bb_im[s])], axis=1)
        for s in (slice(0, gh), slice(gh, 2 * gh))]).astype(BF16)
    cre, cim = c_re.astype(F32), c_im.astype(F32)
    cmat = jnp.stack([
        jnp.concatenate([read_half(cre[s]), -read_half(cim[s])], axis=0)
        for s in (slice(0, gh), slice(gh, 2 * gh))]).astype(BF16)

    p1 = (ab_re.reshape(N_STATE), ab_im.reshape(N_STATE))
    pows = [p1]
    for _ in range(SUBLANES - 1):
        pows.append(_cmul(*pows[-1], *p1))
    row = jnp.arange(SUBLANES)[:, None]
    planes_re, planes_im = [], []
    for s in (1, 2, 4):
        keep = row >= s
        planes_re.append(jnp.where(keep, pows[s - 1][0][None, :], 0.0))
        planes_im.append(jnp.where(keep, pows[s - 1][1][None, :], 0.0))
    planes_re.append(jnp.stack([p[0] for p in pows]))
    planes_im.append(jnp.stack([p[1] for p in pows]))
    pw = jnp.stack([jnp.stack(planes_re), jnp.stack(planes_im)])
    return bmat, cmat, pw


def _rotary_tables(pos):
    half = RET_HD // 2
    inv = ROPE_BASE ** (-jnp.arange(half, dtype=F32) / half)
    ang = pos.astype(F32)[:, None] * inv[None, :]
    cos, sin = jnp.cos(ang), jnp.sin(ang)
    return jnp.concatenate([cos, cos], axis=-1), jnp.concatenate([-sin, sin], axis=-1)


def _retention_consts(chunk, n_seq):
    lg = jnp.log1p(-jnp.exp2(-5.0 - jnp.arange(RET_HEADS, dtype=F32)))
    r = jnp.arange(chunk * n_seq)
    ti = (r % chunk).astype(F32)
    seq = r // chunk
    diff = ti[:, None] - ti[None, :]
    keep = (seq[:, None] == seq[None, :]) & (diff >= 0.0)
    mask = jnp.where(keep[None], jnp.exp(jnp.maximum(diff, 0.0)[None] * lg[:, None, None]), 0.0)
    q_dec = jnp.exp((ti + 1.0)[:, None] * lg[None, :])
    k_dec = jnp.exp((chunk - 1.0 - ti)[:, None] * lg[None, :])
    c_dec = jnp.exp(chunk * lg)
    expand = lambda d: jnp.repeat(d, RET_HD, axis=1)
    return mask, expand(q_dec), expand(k_dec), jnp.broadcast_to(c_dec[:, None], (RET_HEADS, RET_HD))


def _pool_weight(w_pool):
    eye = jnp.eye(len(POOL_WINDOWS), dtype=F32)
    return jnp.einsum("gcd,gk->gckd", w_pool, eye).reshape(D_POOL, D_POOL).astype(BF16)


def kernel(x_prompt, x_sample, state_ssm_re, state_ssm_im, state_ret, state_pool, state_ffn_conv, c_prompt, c_sample, w_ada, b_ada, norm1_g, w_in, ssm_a_re, ssm_a_im, ssm_log_dt, ssm_b_re, ssm_b_im, ssm_c_re, ssm_c_im, ssm_d, ssm_w_glu, ssm_b_glu, pool_w, pool_scale, w_out, norm2_g, ffn_w_up, ffn_conv_w, ffn_conv_b, ffn_w_down, final_norm_g):
    bp, lp, _ = x_prompt.shape
    bs, ls, _ = x_sample.shape
    tl_mix = 256
    seq_mix = 8

    n_all = bp + bs
    n_pad = -(-n_all // SUBLANES) * SUBLANES
    c_all = jnp.concatenate([c_sample, c_prompt, jnp.zeros((n_pad - n_all, D_MODEL), F32)], axis=0)
    mod = _ada_mod(c_all, w_ada, b_ada)

    cos_p, sin_p = _rotary_tables(jnp.arange(lp, dtype=jnp.int32))
    cos_s, sin_s = _rotary_tables(PAST_LEN + jnp.arange(ls, dtype=jnp.int32))
    cos_s, sin_s = jnp.tile(cos_s, (seq_mix, 1)), jnp.tile(sin_s, (seq_mix, 1))
    assert math.gcd(lp, RET_CHUNK) == RET_CHUNK and math.gcd(ls, RET_CHUNK) == ls == SUBLANES
    ret_p = _retention_consts(RET_CHUNK, 1)
    ret_s = _retention_consts(ls, seq_mix)

    big_p, big_s = _Rows(bp, lp, ROW_TILE, bs), _Rows(bs, ls, ROW_TILE, 0)
    mid_p, mid_s = _Rows(bp, lp, ROW_TILE_FULL, bs), _Rows(bs, ls, ROW_TILE_FULL, 0)

    norm1 = norm1_g.reshape(DEPTH, 1, D_MODEL)
    norm2 = norm2_g.reshape(DEPTH, 1, D_MODEL)
    final_g = final_norm_g.reshape(1, 1, D_MODEL)
    conv_b = ffn_conv_b.reshape(DEPTH, 1, D_FF)
    sample_states = (state_ssm_re.reshape(DEPTH, bs, N_STATE), state_ssm_im.reshape(DEPTH, bs, N_STATE),
                     state_ret, state_pool)

    xp = x_prompt.reshape(bp * lp, D_MODEL)
    xs = x_sample.reshape(bs * ls, D_MODEL)
    new_p = ([], [], [], [], [])
    s_mix = None
    s_conv = None
    for l in range(DEPTH):
        bmat, cmat, pw = _ssm_consts(ssm_a_re[l], ssm_a_im[l], ssm_log_dt[l], ssm_b_re[l], ssm_b_im[l],
                                     ssm_c_re[l], ssm_c_im[l])
        shared = (bmat, cmat, pw, ssm_d[l].reshape(1, D_SSM), ssm_w_glu[l].astype(BF16),
                  ssm_b_glu[l].reshape(1, D_SSM))
        pool_c = (_pool_weight(pool_w[l]), pool_scale[l].reshape(1, D_POOL))
        pre = (mod, MOD_SHIFT1, MOD_SCALE1)

        h = _norm(xp, norm1, l, pre, mid_p, BF16)
        proj = _matmul(h, w_in, l, big_p, IN_PROJ_COLS, "in_proj")
        y, st_re, st_im, st_ret, st_pool = _mix_prompt(
            proj, bp, lp, (cos_p, sin_p) + shared + ret_p + pool_c, tl_mix)
        xp, h2 = _out_proj(y, w_out, l, xp, mod, norm2, mid_p)
        act, tails = _ffn_up(h2, ffn_w_up, ffn_conv_w, conv_b, l, None, None, big_p)
        xp = _matmul(act, ffn_w_down, l, big_p, FF_DOWN_COLS, "ffn_down", residual=(xp, mod, MOD_GATE2),
                     single_buffer_w=True)
        st_conv = tails[l, big_p.tiles_per_seq - 1::big_p.tiles_per_seq]
        for lst, st in zip(new_p, (st_re, st_im, st_ret, st_pool, st_conv)):
            lst.append(st)

        h = _norm(xs, norm1, l, pre, mid_s, BF16)
        proj = _matmul(h, w_in, l, big_s, IN_PROJ_COLS, "in_proj")
        y, *s_mix = _mix_sample(proj, sample_states, l, s_mix, bs, ls,
                                (cos_s, sin_s) + shared + ret_s + pool_c, seq_mix)
        xs, h2 = _out_proj(y, w_out, l, xs, mod, norm2, mid_s)
        act, s_conv = _ffn_up(h2, ffn_w_up, ffn_conv_w, conv_b, l, state_ffn_conv, s_conv, big_s)
        xs = _matmul(act, ffn_w_down, l, big_s, FF_DOWN_COLS, "ffn_down", residual=(xs, mod, MOD_GATE2),
                     single_buffer_w=True)

    yp = _norm(xp, final_g, 0, None, mid_p, F32)
    ys = _norm(xs, final_g, 0, None, mid_s, F32)

    p_re, p_im, p_ret, p_pool, p_conv = [jnp.stack(st) for st in new_p]
    s_re, s_im, s_ret, s_pool = s_mix
    shape_p = (DEPTH, bp, SSM_GROUPS, SSM_STATE)
    shape_s = (DEPTH, bs, SSM_GROUPS, SSM_STATE)
    return (yp.reshape(bp, lp, D_MODEL), ys.reshape(bs, ls, D_MODEL),
            p_re.reshape(shape_p), p_im.reshape(shape_p), p_ret, p_pool, p_conv,
            s_re.reshape(shape_s), s_im.reshape(shape_s), s_ret, s_pool, s_conv)
```

```python
import functools
import math

import jax
import jax.numpy as jnp
import numpy as np
from jax import lax
from jax.experimental import pallas as pl
from jax.experimental.pallas import tpu as pltpu

F32 = jnp.float32
BF16 = jnp.bfloat16

D_MODEL = 2048
DEPTH = 2
PAST_LEN = 16384
D_SSM = 512
D_RET = 1024
D_POOL = 512
SSM_GROUP = 16
SSM_GROUPS = 32
SSM_STATE = 64
N_STATE = SSM_GROUPS * SSM_STATE
RET_HEADS = 8
RET_HD = 128
RET_CHUNK = 128
POOL_WINDOWS = (2, 4, 8, 16)
POOL_GD = 128
POOL_BUF = 15
D_FF = 5504
FF_TILE = 512
FF_LAST = D_FF % FF_TILE
CONV_W = 3
D_IN = D_SSM + 4 * D_RET + D_POOL
ROPE_BASE = 10000.0
EPS = 1e-6

SUBLANES = 8
LANES = 128
ROW_TILE = 1024
ROW_TILE_FULL = 512
FF_SUB_ROWS = 512
IN_PROJ_COLS = 1024
FF_DOWN_COLS = 512
ADA_COLS = 1024
MOD_SHIFT1, MOD_SCALE1, MOD_GATE1, MOD_SHIFT2, MOD_SCALE2, MOD_GATE2 = range(6)
SCAN_LANES = 512
VMEM_LIMIT = 56 * 1024 * 1024

Q_OFF = D_SSM
K_OFF = D_SSM + D_RET
V_OFF = D_SSM + 2 * D_RET
G_OFF = D_SSM + 3 * D_RET
C_OFF = D_SSM + 4 * D_RET


def _params(*sem):
    return pltpu.CompilerParams(dimension_semantics=sem, vmem_limit_bytes=VMEM_LIMIT)


def _rms(x):
    return x * lax.rsqrt(jnp.mean(x * x, axis=-1, keepdims=True) + EPS)


def _ada_kernel(c_ref, w_ref, b_ref, o_ref):
    a = jax.nn.silu(c_ref[...]).astype(BF16)
    res = jnp.dot(a, w_ref[...].astype(BF16), preferred_element_type=F32) + b_ref[...]
    for r in range(res.shape[0]):
        o_ref[r] = res[r:r + 1, :]


def _ada_mod(c_all, w_ada, b_ada):
    nseq = c_all.shape[0]
    per_vec = D_MODEL // ADA_COLS
    return pl.pallas_call(
        _ada_kernel,
        grid=(DEPTH, 6 * per_vec),
        in_specs=[
            pl.BlockSpec((nseq, D_MODEL), lambda l, j: (0, 0)),
            pl.BlockSpec((None, D_MODEL, ADA_COLS), lambda l, j: (l, 0, j)),
            pl.BlockSpec((None, 1, ADA_COLS), lambda l, j: (l, 0, j)),
        ],
        out_specs=pl.BlockSpec((None, None, nseq, 1, ADA_COLS), lambda l, j: (l, j // per_vec, 0, 0, j % per_vec)),
        out_shape=jax.ShapeDtypeStruct((DEPTH, 6, nseq, 1, D_MODEL), F32),
        compiler_params=_params("arbitrary", "arbitrary"),
        name="ada_mod",
    )(c_all, w_ada, b_ada.reshape(DEPTH, 1, 6 * D_MODEL))


class _Rows:
    def __init__(self, n_seq, seq_len, tm, first_seq):
        if seq_len >= tm:
            assert seq_len % tm == 0
            self.bb, self.tl = 1, tm
            self.tiles_per_seq = seq_len // tm
        else:
            assert tm % seq_len == 0 and seq_len == SUBLANES
            self.bb, self.tl = tm // seq_len, seq_len
            self.tiles_per_seq = 1
        assert first_seq % self.bb == 0
        self.tm = tm
        self.n_tiles = n_seq * seq_len // tm
        self.first_block = first_seq // self.bb

    def mod_block(self, i):
        return self.first_block + i // self.tiles_per_seq

    def mod_spec(self, layer, vec, row_tile_of, cols=D_MODEL, col_block_of=lambda *ids: 0):
        return pl.BlockSpec(
            (None, None, self.bb, 1, cols),
            lambda *ids: (layer, vec, self.mod_block(row_tile_of(*ids)), 0, col_block_of(*ids)))


def _norm_kernel(*refs, bb, tl, modulated):
    if modulated:
        x_ref, g_ref, sh_ref, sc_ref, o_ref = refs
    else:
        x_ref, g_ref, o_ref = refs
    h = _rms(x_ref[...].reshape(bb, tl, D_MODEL)) * g_ref[...]
    if modulated:
        h = h * (1.0 + sc_ref[...]) + sh_ref[...]
    o_ref[...] = h.reshape(bb * tl, D_MODEL).astype(o_ref.dtype)


def _norm(x, g, layer, mods, rows, out_dtype):
    m = x.shape[0]
    row_spec = pl.BlockSpec((rows.tm, D_MODEL), lambda i: (i, 0))
    in_specs = [row_spec, pl.BlockSpec((None, 1, D_MODEL), lambda i: (layer, 0, 0))]
    args = [x, g]
    if mods is not None:
        mod, shift_vec, scale_vec = mods
        in_specs += [rows.mod_spec(layer, shift_vec, lambda i: i), rows.mod_spec(layer, scale_vec, lambda i: i)]
        args += [mod, mod]
    return pl.pallas_call(
        functools.partial(_norm_kernel, bb=rows.bb, tl=rows.tl, modulated=mods is not None),
        grid=(rows.n_tiles,),
        in_specs=in_specs,
        out_specs=row_spec,
        out_shape=jax.ShapeDtypeStruct((m, D_MODEL), out_dtype),
        compiler_params=_params("arbitrary"),
        name="norm",
    )(*args)


def _matmul_kernel(*refs, bb, tl, residual):
    if residual:
        a_ref, w_ref, x_ref, g_ref, o_ref, w_scr = refs
    else:
        a_ref, w_ref, o_ref, w_scr = refs

    @pl.when(pl.program_id(1) == 0)
    def _():
        w_scr[...] = w_ref[...].astype(BF16)

    f = jnp.dot(a_ref[...], w_scr[...], preferred_element_type=F32)
    if residual:
        tn = f.shape[1]
        xn = x_ref[...].reshape(bb, tl, tn) + g_ref[...] * f.reshape(bb, tl, tn)
        o_ref[...] = xn.reshape(bb * tl, tn)
    else:
        o_ref[...] = f


def _matmul(a, w, layer, rows, tn, name, residual=None, single_buffer_w=False):
    m, k = a.shape
    n = w.shape[2]
    w_mode = dict(pipeline_mode=pl.Buffered(1)) if single_buffer_w else {}
    in_specs = [
        pl.BlockSpec((rows.tm, k), lambda j, i: (i, 0)),
        pl.BlockSpec((None, k, tn), lambda j, i: (layer, 0, j), **w_mode),
    ]
    args = [a, w]
    if residual is not None:
        x, mod, gate_vec = residual
        in_specs += [
            pl.BlockSpec((rows.tm, tn), lambda j, i: (i, j)),
            rows.mod_spec(layer, gate_vec, lambda j, i: i, tn, lambda j, i: j),
        ]
        args += [x, mod]
    return pl.pallas_call(
        functools.partial(_matmul_kernel, bb=rows.bb, tl=rows.tl, residual=residual is not None),
        grid=(n // tn, rows.n_tiles),
        in_specs=in_specs,
        out_specs=pl.BlockSpec((rows.tm, tn), lambda j, i: (i, j)),
        out_shape=jax.ShapeDtypeStruct((m, n), F32),
        scratch_shapes=[pltpu.VMEM((k, tn), BF16)],
        compiler_params=_params("arbitrary", "arbitrary"),
        name=name,
    )(*args)


def _ssm_drive(ub, bmat_ref, dr_scr, row0, n_rows):
    half_u = D_SSM // 2
    half_n = N_STATE // 2
    for hf in range(2):
        d = jnp.dot(ub[:, hf * half_u:(hf + 1) * half_u], bmat_ref[hf], preferred_element_type=F32)
        dr_scr[row0:row0 + n_rows, hf * half_n:(hf + 1) * half_n] = d[:, :half_n]
        dr_scr[row0:row0 + n_rows, N_STATE + hf * half_n:N_STATE + (hf + 1) * half_n] = d[:, half_n:]


def _cmul_add(a_re, a_im, x_re, x_im, y_re, y_im):
    return y_re + (a_re * x_re - a_im * x_im), y_im + (a_re * x_im + a_im * x_re)


def _scan_tile(dr_scr, abar_ref, pws_ref, carry_scr, n_steps):
    row = lax.broadcasted_iota(jnp.int32, (SUBLANES, SCAN_LANES), 0)
    for c in range(N_STATE // SCAN_LANES):
        lo = c * SCAN_LANES
        re_cols = slice(lo, lo + SCAN_LANES)
        im_cols = slice(N_STATE + lo, N_STATE + lo + SCAN_LANES)
        a_re = jnp.broadcast_to(abar_ref[0, :, re_cols], (SUBLANES, SCAN_LANES))
        a_im = jnp.broadcast_to(abar_ref[1, :, re_cols], (SUBLANES, SCAN_LANES))

        def local_step(t, h):
            r = pl.ds(pl.multiple_of(t * SUBLANES, SUBLANES), SUBLANES)
            h_re, h_im = _cmul_add(a_re, a_im, h[0], h[1], dr_scr[r, re_cols], dr_scr[r, im_cols])
            dr_scr[r, re_cols] = h_re
            dr_scr[r, im_cols] = h_im
            return h_re, h_im

        zero = jnp.zeros((SUBLANES, SCAN_LANES), F32)
        g_re, g_im = lax.fori_loop(0, n_steps, local_step, (zero, zero), unroll=2)
        for si, s in enumerate((1, 2, 4)):
            g_re, g_im = _cmul_add(pws_ref[0, si, :, re_cols], pws_ref[1, si, :, re_cols],
                                   pltpu.roll(g_re, s, 0), pltpu.roll(g_im, s, 0), g_re, g_im)
        c_re = carry_scr[SUBLANES - 1:SUBLANES, re_cols]
        c_im = carry_scr[SUBLANES - 1:SUBLANES, im_cols]
        e_re, e_im = _cmul_add(pws_ref[0, 3, :, re_cols], pws_ref[1, 3, :, re_cols], c_re, c_im, g_re, g_im)
        carry_scr[:, re_cols] = e_re
        carry_scr[:, im_cols] = e_im
        in_re = jnp.where(row == 0, c_re, pltpu.roll(e_re, 1, 0))
        in_im = jnp.where(row == 0, c_im, pltpu.roll(e_im, 1, 0))

        def fix_step(t, w):
            w_re, w_im = _cmul(a_re, a_im, w[0], w[1])
            r = pl.ds(pl.multiple_of(t * SUBLANES, SUBLANES), SUBLANES)
            dr_scr[r, re_cols] = dr_scr[r, re_cols] + w_re
            dr_scr[r, im_cols] = dr_scr[r, im_cols] + w_im
            return w_re, w_im

        lax.fori_loop(0, n_steps, fix_step, (in_re, in_im), unroll=2)


def _scan_block(dr_scr, pw_ref, b, carry_fn, block_end_fn=None):
    r = pl.multiple_of(b * SUBLANES + SUBLANES, SUBLANES)
    for c in range(N_STATE // SCAN_LANES):
        lo = c * SCAN_LANES
        re_cols = slice(lo, lo + SCAN_LANES)
        im_cols = slice(N_STATE + lo, N_STATE + lo + SCAN_LANES)
        d_re = dr_scr[pl.ds(r, SUBLANES), re_cols]
        d_im = dr_scr[pl.ds(r, SUBLANES), im_cols]
        for si, s in enumerate((1, 2, 4)):
            p_re = pw_ref[0, si, :, re_cols]
            p_im = pw_ref[1, si, :, re_cols]
            r_re = pltpu.roll(d_re, s, 0)
            r_im = pltpu.roll(d_im, s, 0)
            d_re, d_im = (d_re + (p_re * r_re - p_im * r_im),
                          d_im + (p_re * r_im + p_im * r_re))
        c_re, c_im = carry_fn(b, lo)
        a_re = pw_ref[0, 3, :, re_cols]
        a_im = pw_ref[1, 3, :, re_cols]
        h_re = d_re + (a_re * c_re - a_im * c_im)
        h_im = d_im + (a_re * c_im + a_im * c_re)
        dr_scr[pl.ds(r, SUBLANES), re_cols] = h_re
        dr_scr[pl.ds(r, SUBLANES), im_cols] = h_im
        if block_end_fn is not None:
            block_end_fn(b, lo, h_re[SUBLANES - 1:SUBLANES], h_im[SUBLANES - 1:SUBLANES])


def _ssm_scan(dr_scr, pw_ref, n_blocks, carry_fn, block_end_fn=None):
    def body(b, carry):
        _scan_block(dr_scr, pw_ref, b, carry_fn, block_end_fn)
        return carry

    lax.fori_loop(0, n_blocks, body, 0)


def _ssm_readout(dr_scr, cmat_ref, row0, n_rows):
    half_n = N_STATE // 2
    parts = []
    for hf in range(2):
        h_re = dr_scr[row0:row0 + n_rows, hf * half_n:(hf + 1) * half_n].astype(BF16)
        h_im = dr_scr[row0:row0 + n_rows, N_STATE + hf * half_n:N_STATE + (hf + 1) * half_n].astype(BF16)
        parts.append(jnp.dot(h_re, cmat_ref[hf, :half_n], preferred_element_type=F32)
                     + jnp.dot(h_im, cmat_ref[hf, half_n:], preferred_element_type=F32))
    return jnp.concatenate(parts, axis=-1)


def _ssm_gate(y, wglu_ref, bglu_ref):
    ya = jax.nn.gelu(y)
    gate = jnp.dot(ya.astype(BF16), wglu_ref[...], preferred_element_type=F32) + bglu_ref[...]
    return ya * jax.nn.sigmoid(gate)


def _unpermute_rows(perm_t_ref, x):
    hi = x.astype(BF16)
    rest = x - hi.astype(F32)
    mid = rest.astype(BF16)
    lo = (rest - mid.astype(F32)).astype(BF16)
    p = perm_t_ref[...]
    return (jnp.dot(p, hi, preferred_element_type=F32) + jnp.dot(p, mid, preferred_element_type=F32)
            + jnp.dot(p, lo, preferred_element_type=F32))


def _rotary(x, cosv, sinv):
    return x * cosv + pltpu.roll(x, RET_HD // 2, 1) * sinv


def _head_norm_gate(o, g):
    mu = jnp.mean(o, axis=-1, keepdims=True)
    var = jnp.mean(jnp.square(o - mu), axis=-1, keepdims=True)
    return (o - mu) * lax.rsqrt(var + EPS) * jax.nn.silu(g)


def _pool_counts(pos):
    return [jnp.minimum(pos + 1, w).astype(F32) for w in POOL_WINDOWS]


def _mix_prompt_kernel(proj_ref, cos_ref, sin_ref, perm_ref, permt_ref, abar_ref, pws_ref,
                       bmat_ref, cmat_ref, dskip_ref, wglu_ref, bglu_ref,
                       mask_ref, qdec_ref, kdec_ref, cdec_ref, wpool_ref, pscale_ref,
                       y_ref, hre_ref, him_ref, sret_ref, pbuf_ref,
                       dr_scr, carry_scr, pool_scr, *, tl):
    t = pl.program_id(1)

    @pl.when(t == 0)
    def _():
        carry_scr[...] = jnp.zeros((SUBLANES, 2 * N_STATE), F32)
        sret_ref[...] = jnp.zeros(sret_ref.shape, F32)
        pool_scr[0:16, :] = jnp.zeros((16, D_POOL), F32)

    u = proj_ref[:, 0:D_SSM]
    up = jnp.dot(perm_ref[...], u.astype(BF16), preferred_element_type=F32).astype(BF16)
    _ssm_drive(up, bmat_ref, dr_scr, 0, tl)
    _scan_tile(dr_scr, abar_ref, pws_ref, carry_scr, tl // SUBLANES)
    y = _unpermute_rows(permt_ref, _ssm_readout(dr_scr, cmat_ref, 0, tl)) + dskip_ref[...] * u
    y_ref[:, 0:D_SSM] = _ssm_gate(y, wglu_ref, bglu_ref).astype(BF16)
    hre_ref[0] = carry_scr[SUBLANES - 1:SUBLANES, 0:N_STATE]
    him_ref[0] = carry_scr[SUBLANES - 1:SUBLANES, N_STATE:2 * N_STATE]

    scale = RET_HD ** -0.5

    def chunk_body(ci, carry):
        rows = pl.ds(pl.multiple_of(ci * RET_CHUNK, RET_CHUNK), RET_CHUNK)
        cosv = cos_ref[rows, :]
        sinv = sin_ref[rows, :]
        for h in range(RET_HEADS):
            hs = slice(h * RET_HD, (h + 1) * RET_HD)
            q = proj_ref[rows, Q_OFF + h * RET_HD:Q_OFF + (h + 1) * RET_HD]
            k = proj_ref[rows, K_OFF + h * RET_HD:K_OFF + (h + 1) * RET_HD]
            v = proj_ref[rows, V_OFF + h * RET_HD:V_OFF + (h + 1) * RET_HD].astype(BF16)
            g = proj_ref[rows, G_OFF + h * RET_HD:G_OFF + (h + 1) * RET_HD]
            qr = _rotary(q, cosv, sinv)
            kr = _rotary(k, cosv, sinv) * scale
            s_old = sret_ref[0, h]
            sc = lax.dot_general(qr.astype(BF16), kr.astype(BF16), (((1,), (1,)), ((), ())),
                                 preferred_element_type=F32) * mask_ref[h]
            o = (jnp.dot(sc.astype(BF16), v, preferred_element_type=F32)
                 + jnp.dot((qr * qdec_ref[:, hs]).astype(BF16), s_old.astype(BF16),
                           preferred_element_type=F32))
            sret_ref[0, h] = (s_old * cdec_ref[h:h + 1, :]
                              + lax.dot_general((kr * kdec_ref[:, hs]).astype(BF16), v,
                                                (((0,), (0,)), ((), ())), preferred_element_type=F32))
            y_ref[rows, D_SSM + h * RET_HD:D_SSM + (h + 1) * RET_HD] = _head_norm_gate(o, g).astype(BF16)
        return carry

    lax.fori_loop(0, tl // RET_CHUNK, chunk_body, 0)

    uc = proj_ref[:, C_OFF:C_OFF + D_POOL]
    pool_scr[16:16 + tl, :] = uc
    pos = t * tl + lax.broadcasted_iota(jnp.int32, (tl, POOL_GD), 0)
    counts = _pool_counts(pos)
    parts = []
    for gi, w in enumerate(POOL_WINDOWS):
        lanes = slice(gi * POOL_GD, (gi + 1) * POOL_GD)
        acc = pool_scr[16:16 + tl, lanes]
        for j in range(1, w):
            acc = acc + pool_scr[16 - j:16 - j + tl, lanes]
        parts.append(acc / counts[gi] - uc[:, lanes])
    pooled = jnp.concatenate(parts, axis=-1).astype(BF16)
    yc = jnp.dot(pooled, wpool_ref[...], preferred_element_type=F32) * pscale_ref[...]
    y_ref[:, D_SSM + D_RET:] = yc.astype(BF16)
    pbuf_ref[0] = pool_scr[tl + 1:tl + 16, :]
    pool_scr[0:16, :] = pool_scr[tl:tl + 16, :]


def _mix_prompt(proj, n_seq, seq_len, consts, tl):
    cos_t, sin_t = consts[:2]
    whole = consts[2:]
    nt = seq_len // tl
    row_map = lambda b, t: (b * nt + t, 0)

    def const_spec(a):
        nd = a.ndim
        return pl.BlockSpec(a.shape, lambda b, t: (0,) * nd)

    in_specs = [
        pl.BlockSpec((tl, D_IN), row_map),
        pl.BlockSpec((tl, RET_HD), lambda b, t: (t, 0)),
        pl.BlockSpec((tl, RET_HD), lambda b, t: (t, 0)),
    ] + [const_spec(a) for a in whole]
    out_shape = (
        jax.ShapeDtypeStruct((n_seq * seq_len, D_MODEL), BF16),
        jax.ShapeDtypeStruct((n_seq, 1, N_STATE), F32),
        jax.ShapeDtypeStruct((n_seq, 1, N_STATE), F32),
        jax.ShapeDtypeStruct((n_seq, RET_HEADS, RET_HD, RET_HD), F32),
        jax.ShapeDtypeStruct((n_seq, POOL_BUF, D_POOL), F32),
    )
    out_specs = (
        pl.BlockSpec((tl, D_MODEL), row_map),
        pl.BlockSpec((1, 1, N_STATE), lambda b, t: (b, 0, 0)),
        pl.BlockSpec((1, 1, N_STATE), lambda b, t: (b, 0, 0)),
        pl.BlockSpec((1, RET_HEADS, RET_HD, RET_HD), lambda b, t: (b, 0, 0, 0)),
        pl.BlockSpec((1, POOL_BUF, D_POOL), lambda b, t: (b, 0, 0)),
    )
    return pl.pallas_call(
        functools.partial(_mix_prompt_kernel, tl=tl),
        grid=(n_seq, nt),
        in_specs=in_specs,
        out_specs=out_specs,
        out_shape=out_shape,
        scratch_shapes=[
            pltpu.VMEM((tl, 2 * N_STATE), F32),
            pltpu.VMEM((SUBLANES, 2 * N_STATE), F32),
            pltpu.VMEM((16 + tl, D_POOL), F32),
        ],
        compiler_params=_params("arbitrary", "arbitrary"),
        name="mix_prompt",
    )(proj, cos_t, sin_t, *whole)


def _mix_sample_kernel(proj_ref, cos_ref, sin_ref, pw_ref, bmat_ref, cmat_ref, dskip_ref, wglu_ref,
                       bglu_ref, mask_ref, qdec_ref, kdec_ref, cdec_ref, wpool_ref, pscale_ref,
                       h0re_ref, h0im_ref, s0_ref, pool0_ref,
                       y_ref, hre_ref, him_ref, sret_ref, pbuf_ref,
                       dr_scr, o_scr, qd_scr, kd_scr, pool_scr, *, bs, seq_len, pos0):
    n_rows = bs * seq_len

    u = proj_ref[:, 0:D_SSM]
    _ssm_drive(u.astype(BF16), bmat_ref, dr_scr, SUBLANES, n_rows)

    def carry_fn(b, lo):
        return (h0re_ref[pl.ds(b, 1), lo:lo + SCAN_LANES], h0im_ref[pl.ds(b, 1), lo:lo + SCAN_LANES])

    def block_end_fn(b, lo, h_re, h_im):
        hre_ref[pl.ds(b, 1), lo:lo + SCAN_LANES] = h_re
        him_ref[pl.ds(b, 1), lo:lo + SCAN_LANES] = h_im

    _ssm_scan(dr_scr, pw_ref, bs, carry_fn, block_end_fn)
    y = _ssm_readout(dr_scr, cmat_ref, SUBLANES, n_rows) + dskip_ref[...] * u
    y_ref[:, 0:D_SSM] = _ssm_gate(y, wglu_ref, bglu_ref).astype(BF16)

    scale = RET_HD ** -0.5
    cosv = cos_ref[...]
    sinv = sin_ref[...]
    for h in range(RET_HEADS):
        hs = slice(h * RET_HD, (h + 1) * RET_HD)
        q = proj_ref[:, Q_OFF + h * RET_HD:Q_OFF + (h + 1) * RET_HD]
        k = proj_ref[:, K_OFF + h * RET_HD:K_OFF + (h + 1) * RET_HD]
        v = proj_ref[:, V_OFF + h * RET_HD:V_OFF + (h + 1) * RET_HD].astype(BF16)
        qr = _rotary(q, cosv, sinv)
        kr = _rotary(k, cosv, sinv) * scale
        sc = lax.dot_general(qr.astype(BF16), kr.astype(BF16), (((1,), (1,)), ((), ())),
                             preferred_element_type=F32) * mask_ref[h]
        o_scr[:, hs] = jnp.dot(sc.astype(BF16), v, preferred_element_type=F32)
        qd_scr[:, hs] = qr * qdec_ref[:, hs]
        kd_scr[:, hs] = kr * kdec_ref[:, hs]

    row_seq = lax.broadcasted_iota(jnp.int32, (n_rows, 1), 0) // seq_len

    def seq_body(s, carry):
        own = row_seq == s
        for h in range(RET_HEADS):
            hs = slice(h * RET_HD, (h + 1) * RET_HD)
            s_old = s0_ref[s, h]
            qd = jnp.where(own, qd_scr[:, hs], 0.0).astype(BF16)
            kd = jnp.where(own, kd_scr[:, hs], 0.0).astype(BF16)
            v = proj_ref[:, V_OFF + h * RET_HD:V_OFF + (h + 1) * RET_HD].astype(BF16)
            o_scr[:, hs] += jnp.dot(qd, s_old.astype(BF16), preferred_element_type=F32)
            sret_ref[s, h] = (s_old * cdec_ref[h:h + 1, :]
                              + lax.dot_general(kd, v, (((0,), (0,)), ((), ())),
                                                preferred_element_type=F32))
        return carry

    lax.fori_loop(0, bs, seq_body, 0)
    for h in range(RET_HEADS):
        hs = slice(h * RET_HD, (h + 1) * RET_HD)
        g = proj_ref[:, G_OFF + h * RET_HD:G_OFF + (h + 1) * RET_HD]
        y_ref[:, D_SSM + h * RET_HD:D_SSM + (h + 1) * RET_HD] = _head_norm_gate(o_scr[:, hs], g).astype(BF16)

    uc = proj_ref[:, C_OFF:C_OFF + D_POOL].reshape(bs, seq_len, D_POOL)
    pool_scr[:, 1:16, :] = pool0_ref[...]
    pool_scr[:, 16:16 + seq_len, :] = uc
    pos = pos0 + lax.broadcasted_iota(jnp.int32, (1, seq_len, POOL_GD), 1)
    counts = _pool_counts(pos)
    parts = []
    for gi, w in enumerate(POOL_WINDOWS):
        lanes = slice(gi * POOL_GD, (gi + 1) * POOL_GD)
        acc = pool_scr[:, 16:16 + seq_len, lanes]
        for j in range(1, w):
            acc = acc + pool_scr[:, 16 - j:16 - j + seq_len, lanes]
        parts.append(acc / counts[gi] - uc[:, :, lanes])
    pooled = jnp.concatenate(parts, axis=-1).reshape(n_rows, D_POOL).astype(BF16)
    yc = jnp.dot(pooled, wpool_ref[...], preferred_element_type=F32) * pscale_ref[...]
    y_ref[:, D_SSM + D_RET:] = yc.astype(BF16)
    pbuf_ref[...] = pool_scr[:, seq_len + 1:seq_len + 16, :]


def _mix_sample(proj, states, layer, prev_out, n_seq, seq_len, consts, bs):
    n_rows = bs * seq_len

    def const_spec(a):
        nd = a.ndim
        return pl.BlockSpec(a.shape, lambda i: (0,) * nd)

    state_specs = [
        pl.BlockSpec((None, bs, N_STATE), lambda i: (layer, i, 0)),
        pl.BlockSpec((None, bs, N_STATE), lambda i: (layer, i, 0)),
        pl.BlockSpec((None, bs, RET_HEADS, RET_HD, RET_HD), lambda i: (layer, i, 0, 0, 0)),
        pl.BlockSpec((None, bs, POOL_BUF, D_POOL), lambda i: (layer, i, 0, 0)),
    ]
    in_specs = [pl.BlockSpec((n_rows, D_IN), lambda i: (i, 0))] + [const_spec(a) for a in consts] + state_specs
    args = [proj, *consts, *states]
    n_in = len(args)
    aliases = {}
    if prev_out is not None:
        in_specs += [pl.BlockSpec(memory_space=pl.ANY)] * len(prev_out)
        aliases = {n_in + k: 1 + k for k in range(len(prev_out))}
        args += list(prev_out)
    n_args = len(args)
    out_shape = (
        jax.ShapeDtypeStruct((n_seq * seq_len, D_MODEL), BF16),
        jax.ShapeDtypeStruct((DEPTH, n_seq, N_STATE), F32),
        jax.ShapeDtypeStruct((DEPTH, n_seq, N_STATE), F32),
        jax.ShapeDtypeStruct((DEPTH, n_seq, RET_HEADS, RET_HD, RET_HD), F32),
        jax.ShapeDtypeStruct((DEPTH, n_seq, POOL_BUF, D_POOL), F32),
    )
    out_specs = (pl.BlockSpec((n_rows, D_MODEL), lambda i: (i, 0)),) + tuple(state_specs)

    def body(*refs):
        _mix_sample_kernel(*refs[:n_in], *refs[n_args:], bs=bs, seq_len=seq_len, pos0=PAST_LEN)

    return pl.pallas_call(
        body,
        grid=(n_seq // bs,),
        in_specs=in_specs,
        out_specs=out_specs,
        out_shape=out_shape,
        scratch_shapes=[
            pltpu.VMEM((SUBLANES + n_rows, 2 * N_STATE), F32),
            pltpu.VMEM((n_rows, D_RET), F32),
            pltpu.VMEM((n_rows, D_RET), F32),
            pltpu.VMEM((n_rows, D_RET), F32),
            pltpu.VMEM((bs, 16 + seq_len, D_POOL), F32),
        ],
        input_output_aliases=aliases,
        compiler_params=_params("arbitrary"),
        name="mix_sample",
    )(*args)


def _out_proj_kernel(y_ref, w_ref, x_ref, g1_ref, sh_ref, sc_ref, g_ref, xo_ref, h_ref, w_scr, *, bb, tl):
    @pl.when(pl.program_id(0) == 0)
    def _():
        w_scr[...] = w_ref[...].astype(BF16)

    f = jnp.dot(y_ref[...], w_scr[...], preferred_element_type=F32)
    xn = x_ref[...].reshape(bb, tl, D_MODEL) + g1_ref[...] * f.reshape(bb, tl, D_MODEL)
    xo_ref[...] = xn.reshape(bb * tl, D_MODEL)
    h = _rms(xn) * g_ref[...] * (1.0 + sc_ref[...]) + sh_ref[...]
    h_ref[...] = h.reshape(bb * tl, D_MODEL).astype(BF16)


def _out_proj(y, w, layer, x, mod, g, rows):
    m = x.shape[0]
    row_spec = pl.BlockSpec((rows.tm, D_MODEL), lambda i: (i, 0))
    return pl.pallas_call(
        functools.partial(_out_proj_kernel, bb=rows.bb, tl=rows.tl),
        grid=(rows.n_tiles,),
        in_specs=[
            row_spec,
            pl.BlockSpec((None, D_MODEL, D_MODEL), lambda i: (layer, 0, 0), pipeline_mode=pl.Buffered(1)),
            row_spec,
            rows.mod_spec(layer, MOD_GATE1, lambda i: i),
            rows.mod_spec(layer, MOD_SHIFT2, lambda i: i),
            rows.mod_spec(layer, MOD_SCALE2, lambda i: i),
            pl.BlockSpec((None, 1, D_MODEL), lambda i: (layer, 0, 0)),
        ],
        out_specs=(row_spec, row_spec),
        out_shape=(jax.ShapeDtypeStruct((m, D_MODEL), F32), jax.ShapeDtypeStruct((m, D_MODEL), BF16)),
        scratch_shapes=[pltpu.VMEM((D_MODEL, D_MODEL), BF16)],
        compiler_params=_params("arbitrary"),
        name="out_proj",
    )(y, w, x, mod, mod, mod, g)


def _conv_gate(a, b, am1, am2, cw, cb):
    conv = cb + am2 * cw[0:1]
    conv = conv + am1 * cw[1:2]
    conv = conv + a * cw[2:3]
    return jax.nn.silu(conv) * b


def _ffn_up_kernel(*refs, bb, tl, tiles_per_seq, has_state, n_chunks):
    if has_state:
        h_ref, wa_ref, wb_ref, cw_ref, cb_ref, st_ref, act_ref, ns_ref, w_scr = refs
    else:
        h_ref, wa_ref, wb_ref, cw_ref, cb_ref, act_ref, ns_ref, w_scr, carry_scr = refs
    c = pl.program_id(0)
    i = pl.program_id(1)

    @pl.when(i == 0)
    def _():
        w_scr[:, :FF_TILE] = wa_ref[...].astype(BF16)

    @pl.when((i == 0) & (c < n_chunks - 1))
    def _():
        w_scr[:, FF_TILE:] = wb_ref[0].astype(BF16)

    @pl.when((i == 0) & (c == n_chunks - 1))
    def _():
        w_scr[:, FF_TILE:FF_TILE + FF_LAST] = wb_ref[0, :, FF_TILE - FF_LAST:].astype(BF16)

    cw = cw_ref[...]
    cb = cb_ref[...]
    n_sub = bb * tl // FF_SUB_ROWS
    if has_state:
        sb = FF_SUB_ROWS // tl
        tok = lax.broadcasted_iota(jnp.int32, (sb, tl, FF_TILE), 1)
        for r in range(n_sub):
            rows = slice(r * FF_SUB_ROWS, (r + 1) * FF_SUB_ROWS)
            sq = slice(r * sb, (r + 1) * sb)
            ab = jnp.dot(h_ref[rows, :], w_scr[...], preferred_element_type=F32)
            a = ab[:, :FF_TILE].reshape(sb, tl, FF_TILE)
            b = ab[:, FF_TILE:].reshape(sb, tl, FF_TILE)
            p0 = st_ref[sq, 0:1, :]
            p1 = st_ref[sq, 1:2, :]
            am1 = jnp.where(tok == 0, p1, pltpu.roll(a, 1, 1))
            am2 = jnp.where(tok == 0, p0, jnp.where(tok == 1, p1, pltpu.roll(a, 2, 1)))
            act = _conv_gate(a, b, am1, am2, cw, cb)
            act_ref[rows, :] = act.reshape(FF_SUB_ROWS, FF_TILE).astype(BF16)
            ns_ref[sq] = a[:, tl - 2:tl, :]
    else:
        @pl.when(i % tiles_per_seq == 0)
        def _():
            carry_scr[...] = jnp.zeros((SUBLANES, FF_TILE), F32)

        tok = lax.broadcasted_iota(jnp.int32, (SUBLANES, FF_TILE), 0)
        for r in range(n_sub):
            rows = slice(r * FF_SUB_ROWS, (r + 1) * FF_SUB_ROWS)
            ab = jnp.dot(h_ref[rows, :], w_scr[...], preferred_element_type=F32)
            a = ab[:, :FF_TILE]
            b = ab[:, FF_TILE:]
            p0 = carry_scr[SUBLANES - 2:SUBLANES - 1, :]
            p1 = carry_scr[SUBLANES - 1:SUBLANES, :]
            r1 = pltpu.roll(a, 1, 0)
            r2 = pltpu.roll(a, 2, 0)
            head1 = jnp.where(tok == 0, p1, r1[:SUBLANES])
            head2 = jnp.where(tok == 0, p0, jnp.where(tok == 1, p1, r2[:SUBLANES]))
            am1 = jnp.concatenate([head1, r1[SUBLANES:]], axis=0)
            am2 = jnp.concatenate([head2, r2[SUBLANES:]], axis=0)
            act_ref[rows, :] = _conv_gate(a, b, am1, am2, cw, cb).astype(BF16)
            carry_scr[...] = a[FF_SUB_ROWS - SUBLANES:, :]
        ns_ref[0] = carry_scr[SUBLANES - 2:SUBLANES, :]


def _ffn_up(h, w_up, conv_w, conv_b, layer, state, prev_tails, rows):
    m = h.shape[0]
    has_state = state is not None
    n_chunks = pl.cdiv(D_FF, FF_TILE)
    in_specs = [
        pl.BlockSpec((rows.tm, D_MODEL), lambda c, i: (i, 0)),
        pl.BlockSpec((None, D_MODEL, FF_TILE), lambda c, i: (layer, 0, c)),
        pl.BlockSpec((pl.Element(1), pl.Element(D_MODEL), pl.Element(FF_TILE)),
                     lambda c, i: (layer, 0, LANES * jnp.minimum((D_FF + c * FF_TILE) // LANES,
                                                                 (2 * D_FF - FF_TILE) // LANES))),
        pl.BlockSpec((None, CONV_W, FF_TILE), lambda c, i: (layer, 0, c)),
        pl.BlockSpec((None, 1, FF_TILE), lambda c, i: (layer, 0, c)),
    ]
    args = [h, w_up, w_up, conv_w, conv_b]
    scratch = [pltpu.VMEM((D_MODEL, 2 * FF_TILE), BF16)]
    if has_state:
        in_specs.append(pl.BlockSpec((None, rows.bb, CONV_W - 1, FF_TILE), lambda c, i: (layer, i, 0, c)))
        args.append(state)
    else:
        scratch.append(pltpu.VMEM((SUBLANES, FF_TILE), F32))
    n_in = len(args)
    aliases = {}
    if prev_tails is not None:
        aliases = {n_in: 1}
        in_specs.append(pl.BlockSpec(memory_space=pl.ANY))
        args.append(prev_tails)
    n_args = len(args)

    def body(*refs):
        _ffn_up_kernel(*refs[:n_in], *refs[n_args:], bb=rows.bb, tl=rows.tl,
                       tiles_per_seq=rows.tiles_per_seq, has_state=has_state, n_chunks=n_chunks)

    return pl.pallas_call(
        body,
        grid=(n_chunks, rows.n_tiles),
        in_specs=in_specs,
        out_specs=(
            pl.BlockSpec((rows.tm, FF_TILE), lambda c, i: (i, c)),
            pl.BlockSpec((None, rows.bb, CONV_W - 1, FF_TILE), lambda c, i: (layer, i, 0, c)),
        ),
        out_shape=(jax.ShapeDtypeStruct((m, D_FF), BF16),
                   jax.ShapeDtypeStruct((DEPTH, rows.n_tiles * rows.bb, CONV_W - 1, D_FF), F32)),
        scratch_shapes=scratch,
        input_output_aliases=aliases,
        compiler_params=_params("arbitrary", "arbitrary"),
        name="ffn_up",
    )(*args)


def _cmul(ar, ai, br, bi):
    return ar * br - ai * bi, ar * bi + ai * br


def _ssm_consts(a_re, a_im, log_dt, b_re, b_im, c_re, c_im, scan_steps):
    lam_re, lam_im = a_re.astype(F32), a_im.astype(F32)
    dt = jnp.exp(log_dt.astype(F32))[:, None]
    mag = jnp.exp(lam_re * dt)
    ab_re, ab_im = mag * jnp.cos(lam_im * dt), mag * jnp.sin(lam_im * dt)
    den = lam_re * lam_re + lam_im * lam_im
    f_re = ((ab_re - 1.0) * lam_re + ab_im * lam_im) / den
    f_im = (ab_im * lam_re - (ab_re - 1.0) * lam_im) / den
    br, bi = b_re.astype(F32), b_im.astype(F32)
    bb_re = f_re[..., None] * br - f_im[..., None] * bi
    bb_im = f_re[..., None] * bi + f_im[..., None] * br

    gh = SSM_GROUPS // 2
    eye = jnp.eye(gh, dtype=F32)

    def drive_half(m):
        return jnp.einsum("gph,gk->ghkp", m, eye).reshape(gh * SSM_GROUP, gh * SSM_STATE)

    def read_half(m):
        return jnp.einsum("ghp,gk->gpkh", m, eye).reshape(gh * SSM_STATE, gh * SSM_GROUP)

    bmat = jnp.stack([
        jnp.concatenate([drive_half(bb_re[s]), drive_half(bb_im[s])], axis=1)
        for s in (slice(0, gh), slice(gh, 2 * gh))]).astype(BF16)
    cre, cim = c_re.astype(F32), c_im.astype(F32)
    cmat = jnp.stack([
        jnp.concatenate([read_half(cre[s]), -read_half(cim[s])], axis=0)
        for s in (slice(0, gh), slice(gh, 2 * gh))]).astype(BF16)

    step_re, step_im = _power_table(ab_re.reshape(1, N_STATE), ab_im.reshape(1, N_STATE), scan_steps)
    sub_re, sub_im = _power_table(step_re[-1:], step_im[-1:], SUBLANES)
    abar = jnp.stack([step_re[:1], step_im[:1]])
    pw_block = _scan_planes(step_re[:SUBLANES], step_im[:SUBLANES])
    pw_sub = _scan_planes(sub_re, sub_im)
    return bmat, cmat, abar, pw_block, pw_sub


def _power_table(re, im, n):
    while re.shape[0] < n:
        top_re, top_im = re[-1:], im[-1:]
        more_re, more_im = _cmul(re, im, top_re, top_im)
        re, im = jnp.concatenate([re, more_re]), jnp.concatenate([im, more_im])
    return re[:n], im[:n]


def _scan_planes(re, im):
    row = jnp.arange(SUBLANES)[:, None]
    planes_re = [jnp.where(row >= s, re[s - 1][None, :], 0.0) for s in (1, 2, 4)] + [re]
    planes_im = [jnp.where(row >= s, im[s - 1][None, :], 0.0) for s in (1, 2, 4)] + [im]
    return jnp.stack([jnp.stack(planes_re), jnp.stack(planes_im)])


def _time_on_sublanes(n_rows):
    steps = n_rows // SUBLANES
    p = np.zeros((n_rows, n_rows), np.float32)
    r = np.arange(n_rows)
    p[r, (r % SUBLANES) * steps + r // SUBLANES] = 1.0
    return jnp.asarray(p, BF16), jnp.asarray(p.T, BF16)


def _rotary_tables(pos):
    half = RET_HD // 2
    inv = ROPE_BASE ** (-np.arange(half, dtype=np.float64) / half)
    ang = np.asarray(pos, np.float64)[:, None] * inv[None, :]
    cos, sin = np.cos(ang), np.sin(ang)
    return (np.concatenate([cos, cos], axis=-1).astype(np.float32),
            np.concatenate([-sin, sin], axis=-1).astype(np.float32))


def _retention_consts(chunk, n_seq):
    lg = np.log1p(-np.exp2(-5.0 - np.arange(RET_HEADS, dtype=np.float64)))
    r = np.arange(chunk * n_seq)
    ti = (r % chunk).astype(np.float64)
    seq = r // chunk
    diff = ti[:, None] - ti[None, :]
    keep = (seq[:, None] == seq[None, :]) & (diff >= 0.0)
    mask = np.where(keep[None], np.exp(np.maximum(diff, 0.0)[None] * lg[:, None, None]), 0.0)
    q_dec = np.exp((ti + 1.0)[:, None] * lg[None, :])
    k_dec = np.exp((chunk - 1.0 - ti)[:, None] * lg[None, :])
    c_dec = np.exp(chunk * lg)
    expand = lambda d: np.repeat(d, RET_HD, axis=1).astype(np.float32)
    return (mask.astype(np.float32), expand(q_dec), expand(k_dec),
            np.broadcast_to(c_dec[:, None], (RET_HEADS, RET_HD)).astype(np.float32))


def _pool_weight(w_pool):
    eye = jnp.eye(len(POOL_WINDOWS), dtype=F32)
    return jnp.einsum("gcd,gk->gckd", w_pool, eye).reshape(D_POOL, D_POOL).astype(BF16)


def kernel(x_prompt, x_sample, state_ssm_re, state_ssm_im, state_ret, state_pool, state_ffn_conv, c_prompt, c_sample, w_ada, b_ada, norm1_g, w_in, ssm_a_re, ssm_a_im, ssm_log_dt, ssm_b_re, ssm_b_im, ssm_c_re, ssm_c_im, ssm_d, ssm_w_glu, ssm_b_glu, pool_w, pool_scale, w_out, norm2_g, ffn_w_up, ffn_conv_w, ffn_conv_b, ffn_w_down, final_norm_g):
    bp, lp, _ = x_prompt.shape
    bs, ls, _ = x_sample.shape
    tl_mix = 256
    seq_mix = 8

    n_all = bp + bs
    n_pad = -(-n_all // SUBLANES) * SUBLANES
    c_all = jnp.concatenate([c_sample, c_prompt, jnp.zeros((n_pad - n_all, D_MODEL), F32)], axis=0)
    mod = _ada_mod(c_all, w_ada, b_ada)

    cos_p, sin_p = _rotary_tables(np.arange(lp))
    cos_s, sin_s = _rotary_tables(PAST_LEN + np.arange(ls))
    cos_s, sin_s = np.tile(cos_s, (seq_mix, 1)), np.tile(sin_s, (seq_mix, 1))
    assert math.gcd(lp, RET_CHUNK) == RET_CHUNK and math.gcd(ls, RET_CHUNK) == ls == SUBLANES
    ret_p = _retention_consts(RET_CHUNK, 1)
    ret_s = _retention_consts(ls, seq_mix)
    perm = _time_on_sublanes(tl_mix)

    big_p, big_s = _Rows(bp, lp, ROW_TILE, bs), _Rows(bs, ls, ROW_TILE, 0)
    mid_p, mid_s = _Rows(bp, lp, ROW_TILE_FULL, bs), _Rows(bs, ls, ROW_TILE_FULL, 0)

    norm1 = norm1_g.reshape(DEPTH, 1, D_MODEL)
    norm2 = norm2_g.reshape(DEPTH, 1, D_MODEL)
    final_g = final_norm_g.reshape(1, 1, D_MODEL)
    conv_b = ffn_conv_b.reshape(DEPTH, 1, D_FF)
    sample_states = (state_ssm_re.reshape(DEPTH, bs, N_STATE), state_ssm_im.reshape(DEPTH, bs, N_STATE),
                     state_ret, state_pool)

    xp = x_prompt.reshape(bp * lp, D_MODEL)
    xs = x_sample.reshape(bs * ls, D_MODEL)
    new_p = ([], [], [], [], [])
    s_mix = None
    s_conv = None
    for l in range(DEPTH):
        bmat, cmat, abar, pw_block, pw_sub = _ssm_consts(
            ssm_a_re[l], ssm_a_im[l], ssm_log_dt[l], ssm_b_re[l], ssm_b_im[l], ssm_c_re[l], ssm_c_im[l],
            tl_mix // SUBLANES)
        shared = (bmat, cmat, ssm_d[l].reshape(1, D_SSM), ssm_w_glu[l].astype(BF16),
                  ssm_b_glu[l].reshape(1, D_SSM))
        pool_c = (_pool_weight(pool_w[l]), pool_scale[l].reshape(1, D_POOL))
        pre = (mod, MOD_SHIFT1, MOD_SCALE1)

        h = _norm(xp, norm1, l, pre, mid_p, BF16)
        proj = _matmul(h, w_in, l, big_p, IN_PROJ_COLS, "in_proj")
        y, st_re, st_im, st_ret, st_pool = _mix_prompt(
            proj, bp, lp, (cos_p, sin_p) + perm + (abar, pw_sub) + shared + ret_p + pool_c, tl_mix)
        xp, h2 = _out_proj(y, w_out, l, xp, mod, norm2, mid_p)
        act, tails = _ffn_up(h2, ffn_w_up, ffn_conv_w, conv_b, l, None, None, big_p)
        xp = _matmul(act, ffn_w_down, l, big_p, FF_DOWN_COLS, "ffn_down", residual=(xp, mod, MOD_GATE2),
                     single_buffer_w=True)
        st_conv = tails[l, big_p.tiles_per_seq - 1::big_p.tiles_per_seq]
        for lst, st in zip(new_p, (st_re, st_im, st_ret, st_pool, st_conv)):
            lst.append(st)

        h = _norm(xs, norm1, l, pre, mid_s, BF16)
        proj = _matmul(h, w_in, l, big_s, IN_PROJ_COLS, "in_proj")
        y, *s_mix = _mix_sample(proj, sample_states, l, s_mix, bs, ls,
                                (cos_s, sin_s, pw_block) + shared + ret_s + pool_c, seq_mix)
        xs, h2 = _out_proj(y, w_out, l, xs, mod, norm2, mid_s)
        act, s_conv = _ffn_up(h2, ffn_w_up, ffn_conv_w, conv_b, l, state_ffn_conv, s_conv, big_s)
        xs = _matmul(act, ffn_w_down, l, big_s, FF_DOWN_COLS, "ffn_down", residual=(xs, mod, MOD_GATE2),
                     single_buffer_w=True)

    yp = _norm(xp, final_g, 0, None, mid_p, F32)
    ys = _norm(xs, final_g, 0, None, mid_s, F32)

    p_re, p_im, p_ret, p_pool, p_conv = [jnp.stack(st) for st in new_p]
    s_re, s_im, s_ret, s_pool = s_mix
    shape_p = (DEPTH, bp, SSM_GROUPS, SSM_STATE)
    shape_s = (DEPTH, bs, SSM_GROUPS, SSM_STATE)
    return (yp.reshape(bp, lp, D_MODEL), ys.reshape(bs, ls, D_MODEL),
            p_re.reshape(shape_p), p_im.reshape(shape_p), p_ret, p_pool, p_conv,
            s_re.reshape(shape_s), s_im.reshape(shape_s), s_ret, s_pool, s_conv)
```

```python
import functools
import math

import jax
import jax.numpy as jnp
import numpy as np
from jax import lax
from jax.experimental import pallas as pl
from jax.experimental.pallas import tpu as pltpu

F32 = jnp.float32
BF16 = jnp.bfloat16

D_MODEL = 2048
DEPTH = 2
PAST_LEN = 16384
D_SSM = 512
D_RET = 1024
D_POOL = 512
SSM_GROUP = 16
SSM_GROUPS = 32
SSM_STATE = 64
N_STATE = SSM_GROUPS * SSM_STATE
RET_HEADS = 8
RET_HD = 128
RET_CHUNK = 128
POOL_WINDOWS = (2, 4, 8, 16)
POOL_GD = 128
POOL_BUF = 15
D_FF = 5504
FF_TILE = 512
FF_LAST = D_FF % FF_TILE
CONV_W = 3
D_IN = D_SSM + 4 * D_RET + D_POOL
ROPE_BASE = 10000.0
EPS = 1e-6

SUBLANES = 8
LANES = 128
ROW_TILE = 1024
ROW_TILE_FULL = 512
FF_SUB_ROWS = 512
IN_PROJ_COLS = 1024
FF_DOWN_COLS = 512
ADA_COLS = 1024
MOD_SHIFT1, MOD_SCALE1, MOD_GATE1, MOD_SHIFT2, MOD_SCALE2, MOD_GATE2 = range(6)
SCAN_LANES = 512
VMEM_LIMIT = 56 * 1024 * 1024

Q_OFF = D_SSM
K_OFF = D_SSM + D_RET
V_OFF = D_SSM + 2 * D_RET
G_OFF = D_SSM + 3 * D_RET
C_OFF = D_SSM + 4 * D_RET


def _params(*sem):
    return pltpu.CompilerParams(dimension_semantics=sem, vmem_limit_bytes=VMEM_LIMIT)


def _rms(x):
    return x * lax.rsqrt(jnp.mean(x * x, axis=-1, keepdims=True) + EPS)


def _ada_kernel(c_ref, w_ref, b_ref, o_ref):
    a = jax.nn.silu(c_ref[...]).astype(BF16)
    res = jnp.dot(a, w_ref[...].astype(BF16), preferred_element_type=F32) + b_ref[...]
    for r in range(res.shape[0]):
        o_ref[r] = res[r:r + 1, :]


def _ada_mod(c_all, w_ada, b_ada):
    nseq = c_all.shape[0]
    per_vec = D_MODEL // ADA_COLS
    return pl.pallas_call(
        _ada_kernel,
        grid=(DEPTH, 6 * per_vec),
        in_specs=[
            pl.BlockSpec((nseq, D_MODEL), lambda l, j: (0, 0)),
            pl.BlockSpec((None, D_MODEL, ADA_COLS), lambda l, j: (l, 0, j)),
            pl.BlockSpec((None, 1, ADA_COLS), lambda l, j: (l, 0, j)),
        ],
        out_specs=pl.BlockSpec((None, None, nseq, 1, ADA_COLS), lambda l, j: (l, j // per_vec, 0, 0, j % per_vec)),
        out_shape=jax.ShapeDtypeStruct((DEPTH, 6, nseq, 1, D_MODEL), F32),
        compiler_params=_params("arbitrary", "arbitrary"),
        name="ada_mod",
    )(c_all, w_ada, b_ada.reshape(DEPTH, 1, 6 * D_MODEL))


class _Rows:
    def __init__(self, n_seq, seq_len, tm, first_seq, first_row):
        if seq_len >= tm:
            assert seq_len % tm == 0
            self.bb, self.tl = 1, tm
            self.tiles_per_seq = seq_len // tm
        else:
            assert tm % seq_len == 0 and seq_len == SUBLANES
            self.bb, self.tl = tm // seq_len, seq_len
            self.tiles_per_seq = 1
        assert first_seq % self.bb == 0 and first_row % tm == 0
        self.tm = tm
        self.n_tiles = n_seq * seq_len // tm
        self.first_block = first_seq // self.bb
        self.first_tile = first_row // tm

    def mod_block(self, i):
        return self.first_block + i // self.tiles_per_seq

    def mod_spec(self, layer, vec, row_tile_of, cols=D_MODEL, col_block_of=lambda *ids: 0):
        return pl.BlockSpec(
            (None, None, self.bb, 1, cols),
            lambda *ids: (layer, vec, self.mod_block(row_tile_of(*ids)), 0, col_block_of(*ids)))


def _call_into(kernel_fn, prev, *, grid, in_specs, args, out_specs, out_shape, scratch_shapes=(), sem, name):
    n_in = len(args)
    in_specs, args = list(in_specs), list(args)
    aliases = {}
    for k, arr in sorted((prev or {}).items()):
        aliases[len(args)] = k
        in_specs.append(pl.BlockSpec(memory_space=pl.ANY))
        args.append(arr)
    n_args = len(args)

    def body(*refs):
        kernel_fn(*refs[:n_in], *refs[n_args:])

    return pl.pallas_call(
        body, grid=grid, in_specs=in_specs, out_specs=out_specs, out_shape=out_shape,
        scratch_shapes=list(scratch_shapes), input_output_aliases=aliases,
        compiler_params=_params(*sem), name=name,
    )(*args)


def _norm_kernel(*refs, bb, tl, modulated):
    if modulated:
        x_ref, g_ref, sh_ref, sc_ref, o_ref = refs
    else:
        x_ref, g_ref, o_ref = refs
    h = _rms(x_ref[...].reshape(bb, tl, D_MODEL)) * g_ref[...]
    if modulated:
        h = h * (1.0 + sc_ref[...]) + sh_ref[...]
    o_ref[...] = h.reshape(bb * tl, D_MODEL).astype(o_ref.dtype)


def _norm(x, x_tile0, g, layer, mods, rows, out_dtype, out_rows=None, prev=None):
    in_specs = [pl.BlockSpec((rows.tm, D_MODEL), lambda i: (x_tile0 + i, 0)),
                pl.BlockSpec((None, 1, D_MODEL), lambda i: (layer, 0, 0))]
    args = [x, g]
    if mods is not None:
        mod, shift_vec, scale_vec = mods
        in_specs += [rows.mod_spec(layer, shift_vec, lambda i: i), rows.mod_spec(layer, scale_vec, lambda i: i)]
        args += [mod, mod]
    out_tile0 = 0 if out_rows is None else rows.first_tile
    return _call_into(
        functools.partial(_norm_kernel, bb=rows.bb, tl=rows.tl, modulated=mods is not None),
        None if prev is None else {0: prev},
        grid=(rows.n_tiles,),
        in_specs=in_specs,
        args=args,
        out_specs=pl.BlockSpec((rows.tm, D_MODEL), lambda i: (out_tile0 + i, 0)),
        out_shape=jax.ShapeDtypeStruct((out_rows or rows.n_tiles * rows.tm, D_MODEL), out_dtype),
        sem=("arbitrary",),
        name="norm",
    )


def _matmul_kernel(*refs, groups, residual):
    if residual:
        a_ref, w_ref, x_ref, *gate_refs, o_ref, w_scr = refs
    else:
        a_ref, w_ref, o_ref, w_scr = refs
    i = pl.program_id(1)

    @pl.when(i == 0)
    def _():
        w_scr[...] = w_ref[...].astype(BF16)

    f = jnp.dot(a_ref[...], w_scr[...], preferred_element_type=F32)
    if not residual:
        o_ref[...] = f
        return
    tn = f.shape[1]
    for rows, g_ref in zip(groups, gate_refs):
        @pl.when((i >= rows.first_tile) & (i < rows.first_tile + rows.n_tiles))
        def _(rows=rows, g_ref=g_ref):
            xn = x_ref[...].reshape(rows.bb, rows.tl, tn) + g_ref[...] * f.reshape(rows.bb, rows.tl, tn)
            o_ref[...] = xn.reshape(rows.tm, tn)


def _matmul(a, w, layer, groups, tn, name, residual=None, single_buffer_w=False):
    m, k = a.shape
    n = w.shape[2]
    tm = groups[0].tm
    n_tiles = sum(rows.n_tiles for rows in groups)
    assert m == n_tiles * tm and all(rows.tm == tm for rows in groups)
    w_mode = dict(pipeline_mode=pl.Buffered(1)) if single_buffer_w else {}
    in_specs = [
        pl.BlockSpec((tm, k), lambda j, i: (i, 0)),
        pl.BlockSpec((None, k, tn), lambda j, i: (layer, 0, j), **w_mode),
    ]
    args = [a, w]
    if residual is not None:
        x, mod, gate_vec = residual
        in_specs.append(pl.BlockSpec((tm, tn), lambda j, i: (i, j)))
        args.append(x)
        for rows in groups:
            in_specs.append(rows.mod_spec(
                layer, gate_vec,
                lambda j, i, rows=rows: jnp.clip(i - rows.first_tile, 0, rows.n_tiles - 1),
                tn, lambda j, i: j))
            args.append(mod)
    return pl.pallas_call(
        functools.partial(_matmul_kernel, groups=groups, residual=residual is not None),
        grid=(n // tn, n_tiles),
        in_specs=in_specs,
        out_specs=pl.BlockSpec((tm, tn), lambda j, i: (i, j)),
        out_shape=jax.ShapeDtypeStruct((m, n), F32),
        scratch_shapes=[pltpu.VMEM((k, tn), BF16)],
        compiler_params=_params("arbitrary", "arbitrary"),
        name=name,
    )(*args)


def _ssm_drive(ub, bmat_ref, dr_scr, row0, n_rows):
    half_u = D_SSM // 2
    half_n = N_STATE // 2
    for hf in range(2):
        d = jnp.dot(ub[:, hf * half_u:(hf + 1) * half_u], bmat_ref[hf], preferred_element_type=F32)
        dr_scr[row0:row0 + n_rows, hf * half_n:(hf + 1) * half_n] = d[:, :half_n]
        dr_scr[row0:row0 + n_rows, N_STATE + hf * half_n:N_STATE + (hf + 1) * half_n] = d[:, half_n:]


def _cmul_add(a_re, a_im, x_re, x_im, y_re, y_im):
    return y_re + (a_re * x_re - a_im * x_im), y_im + (a_re * x_im + a_im * x_re)


def _scan_tile(dr_scr, abar_ref, pws_ref, carry_scr, n_steps):
    row = lax.broadcasted_iota(jnp.int32, (SUBLANES, SCAN_LANES), 0)
    for c in range(N_STATE // SCAN_LANES):
        lo = c * SCAN_LANES
        re_cols = slice(lo, lo + SCAN_LANES)
        im_cols = slice(N_STATE + lo, N_STATE + lo + SCAN_LANES)
        a_re = jnp.broadcast_to(abar_ref[0, :, re_cols], (SUBLANES, SCAN_LANES))
        a_im = jnp.broadcast_to(abar_ref[1, :, re_cols], (SUBLANES, SCAN_LANES))

        def local_step(t, h):
            r = pl.ds(pl.multiple_of(t * SUBLANES, SUBLANES), SUBLANES)
            h_re, h_im = _cmul_add(a_re, a_im, h[0], h[1], dr_scr[r, re_cols], dr_scr[r, im_cols])
            dr_scr[r, re_cols] = h_re
            dr_scr[r, im_cols] = h_im
            return h_re, h_im

        zero = jnp.zeros((SUBLANES, SCAN_LANES), F32)
        g_re, g_im = lax.fori_loop(0, n_steps, local_step, (zero, zero), unroll=2)
        for si, s in enumerate((1, 2, 4)):
            g_re, g_im = _cmul_add(pws_ref[0, si, :, re_cols], pws_ref[1, si, :, re_cols],
                                   pltpu.roll(g_re, s, 0), pltpu.roll(g_im, s, 0), g_re, g_im)
        c_re = carry_scr[SUBLANES - 1:SUBLANES, re_cols]
        c_im = carry_scr[SUBLANES - 1:SUBLANES, im_cols]
        e_re, e_im = _cmul_add(pws_ref[0, 3, :, re_cols], pws_ref[1, 3, :, re_cols], c_re, c_im, g_re, g_im)
        carry_scr[:, re_cols] = e_re
        carry_scr[:, im_cols] = e_im
        in_re = jnp.where(row == 0, c_re, pltpu.roll(e_re, 1, 0))
        in_im = jnp.where(row == 0, c_im, pltpu.roll(e_im, 1, 0))

        def fix_step(t, w):
            w_re, w_im = _cmul(a_re, a_im, w[0], w[1])
            r = pl.ds(pl.multiple_of(t * SUBLANES, SUBLANES), SUBLANES)
            dr_scr[r, re_cols] = dr_scr[r, re_cols] + w_re
            dr_scr[r, im_cols] = dr_scr[r, im_cols] + w_im
            return w_re, w_im

        lax.fori_loop(0, n_steps, fix_step, (in_re, in_im), unroll=2)


def _scan_block(dr_scr, pw_ref, b, carry_fn, block_end_fn=None):
    r = pl.multiple_of(b * SUBLANES + SUBLANES, SUBLANES)
    for c in range(N_STATE // SCAN_LANES):
        lo = c * SCAN_LANES
        re_cols = slice(lo, lo + SCAN_LANES)
        im_cols = slice(N_STATE + lo, N_STATE + lo + SCAN_LANES)
        d_re = dr_scr[pl.ds(r, SUBLANES), re_cols]
        d_im = dr_scr[pl.ds(r, SUBLANES), im_cols]
        for si, s in enumerate((1, 2, 4)):
            p_re = pw_ref[0, si, :, re_cols]
            p_im = pw_ref[1, si, :, re_cols]
            r_re = pltpu.roll(d_re, s, 0)
            r_im = pltpu.roll(d_im, s, 0)
            d_re, d_im = (d_re + (p_re * r_re - p_im * r_im),
                          d_im + (p_re * r_im + p_im * r_re))
        c_re, c_im = carry_fn(b, lo)
        a_re = pw_ref[0, 3, :, re_cols]
        a_im = pw_ref[1, 3, :, re_cols]
        h_re = d_re + (a_re * c_re - a_im * c_im)
        h_im = d_im + (a_re * c_im + a_im * c_re)
        dr_scr[pl.ds(r, SUBLANES), re_cols] = h_re
        dr_scr[pl.ds(r, SUBLANES), im_cols] = h_im
        if block_end_fn is not None:
            block_end_fn(b, lo, h_re[SUBLANES - 1:SUBLANES], h_im[SUBLANES - 1:SUBLANES])


def _ssm_scan(dr_scr, pw_ref, n_blocks, carry_fn, block_end_fn=None):
    def body(b, carry):
        _scan_block(dr_scr, pw_ref, b, carry_fn, block_end_fn)
        return carry

    lax.fori_loop(0, n_blocks, body, 0)


def _ssm_readout(dr_scr, cmat_ref, row0, n_rows):
    half_n = N_STATE // 2
    parts = []
    for hf in range(2):
        h_re = dr_scr[row0:row0 + n_rows, hf * half_n:(hf + 1) * half_n].astype(BF16)
        h_im = dr_scr[row0:row0 + n_rows, N_STATE + hf * half_n:N_STATE + (hf + 1) * half_n].astype(BF16)
        parts.append(jnp.dot(h_re, cmat_ref[hf, :half_n], preferred_element_type=F32)
                     + jnp.dot(h_im, cmat_ref[hf, half_n:], preferred_element_type=F32))
    return jnp.concatenate(parts, axis=-1)


def _ssm_gate(y, wglu_ref, bglu_ref):
    ya = jax.nn.gelu(y)
    gate = jnp.dot(ya.astype(BF16), wglu_ref[...], preferred_element_type=F32) + bglu_ref[...]
    return ya * jax.nn.sigmoid(gate)


def _unpermute_rows(perm_t_ref, x):
    hi = x.astype(BF16)
    rest = x - hi.astype(F32)
    mid = rest.astype(BF16)
    lo = (rest - mid.astype(F32)).astype(BF16)
    p = perm_t_ref[...]
    return (jnp.dot(p, hi, preferred_element_type=F32) + jnp.dot(p, mid, preferred_element_type=F32)
            + jnp.dot(p, lo, preferred_element_type=F32))


def _rotary(x, cosv, sinv):
    return x * cosv + pltpu.roll(x, RET_HD // 2, 1) * sinv


def _head_norm_gate(o, g):
    mu = jnp.mean(o, axis=-1, keepdims=True)
    var = jnp.mean(jnp.square(o - mu), axis=-1, keepdims=True)
    return (o - mu) * lax.rsqrt(var + EPS) * jax.nn.silu(g)


def _pool_counts(pos):
    return [jnp.minimum(pos + 1, w).astype(F32) for w in POOL_WINDOWS]


def _mix_prompt_kernel(proj_ref, cos_ref, sin_ref, perm_ref, permt_ref, abar_ref, pws_ref,
                       bmat_ref, cmat_ref, dskip_ref, wglu_ref, bglu_ref,
                       mask_ref, qdec_ref, kdec_ref, cdec_ref, wpool_ref, pscale_ref,
                       y_ref, hre_ref, him_ref, sret_ref, pbuf_ref,
                       dr_scr, carry_scr, pool_scr, *, tl):
    t = pl.program_id(1)

    @pl.when(t == 0)
    def _():
        carry_scr[...] = jnp.zeros((SUBLANES, 2 * N_STATE), F32)
        sret_ref[...] = jnp.zeros(sret_ref.shape, F32)
        pool_scr[0:16, :] = jnp.zeros((16, D_POOL), F32)

    u = proj_ref[:, 0:D_SSM]
    up = jnp.dot(perm_ref[...], u.astype(BF16), preferred_element_type=F32).astype(BF16)
    _ssm_drive(up, bmat_ref, dr_scr, 0, tl)
    _scan_tile(dr_scr, abar_ref, pws_ref, carry_scr, tl // SUBLANES)
    y = _unpermute_rows(permt_ref, _ssm_readout(dr_scr, cmat_ref, 0, tl)) + dskip_ref[...] * u
    y_ref[:, 0:D_SSM] = _ssm_gate(y, wglu_ref, bglu_ref).astype(BF16)
    hre_ref[0] = carry_scr[SUBLANES - 1:SUBLANES, 0:N_STATE]
    him_ref[0] = carry_scr[SUBLANES - 1:SUBLANES, N_STATE:2 * N_STATE]

    scale = RET_HD ** -0.5

    def chunk_body(ci, carry):
        rows = pl.ds(pl.multiple_of(ci * RET_CHUNK, RET_CHUNK), RET_CHUNK)
        cosv = cos_ref[rows, :]
        sinv = sin_ref[rows, :]
        for h in range(RET_HEADS):
            hs = slice(h * RET_HD, (h + 1) * RET_HD)
            q = proj_ref[rows, Q_OFF + h * RET_HD:Q_OFF + (h + 1) * RET_HD]
            k = proj_ref[rows, K_OFF + h * RET_HD:K_OFF + (h + 1) * RET_HD]
            v = proj_ref[rows, V_OFF + h * RET_HD:V_OFF + (h + 1) * RET_HD].astype(BF16)
            g = proj_ref[rows, G_OFF + h * RET_HD:G_OFF + (h + 1) * RET_HD]
            qr = _rotary(q, cosv, sinv)
            kr = _rotary(k, cosv, sinv) * scale
            s_old = sret_ref[0, h]
            sc = lax.dot_general(qr.astype(BF16), kr.astype(BF16), (((1,), (1,)), ((), ())),
                                 preferred_element_type=F32) * mask_ref[h]
            o = (jnp.dot(sc.astype(BF16), v, preferred_element_type=F32)
                 + jnp.dot((qr * qdec_ref[:, hs]).astype(BF16), s_old.astype(BF16),
                           preferred_element_type=F32))
            sret_ref[0, h] = (s_old * cdec_ref[h:h + 1, :]
                              + lax.dot_general((kr * kdec_ref[:, hs]).astype(BF16), v,
                                                (((0,), (0,)), ((), ())), preferred_element_type=F32))
            y_ref[rows, D_SSM + h * RET_HD:D_SSM + (h + 1) * RET_HD] = _head_norm_gate(o, g).astype(BF16)
        return carry

    lax.fori_loop(0, tl // RET_CHUNK, chunk_body, 0)

    uc = proj_ref[:, C_OFF:C_OFF + D_POOL]
    pool_scr[16:16 + tl, :] = uc
    pos = t * tl + lax.broadcasted_iota(jnp.int32, (tl, POOL_GD), 0)
    counts = _pool_counts(pos)
    parts = []
    for gi, w in enumerate(POOL_WINDOWS):
        lanes = slice(gi * POOL_GD, (gi + 1) * POOL_GD)
        acc = pool_scr[16:16 + tl, lanes]
        for j in range(1, w):
            acc = acc + pool_scr[16 - j:16 - j + tl, lanes]
        parts.append(acc / counts[gi] - uc[:, lanes])
    pooled = jnp.concatenate(parts, axis=-1).astype(BF16)
    yc = jnp.dot(pooled, wpool_ref[...], preferred_element_type=F32) * pscale_ref[...]
    y_ref[:, D_SSM + D_RET:] = yc.astype(BF16)
    pbuf_ref[0] = pool_scr[tl + 1:tl + 16, :]
    pool_scr[0:16, :] = pool_scr[tl:tl + 16, :]


def _mix_prompt(proj, n_seq, seq_len, consts, tl):
    cos_t, sin_t = consts[:2]
    whole = consts[2:]
    nt = seq_len // tl
    row_map = lambda b, t: (b * nt + t, 0)

    def const_spec(a):
        nd = a.ndim
        return pl.BlockSpec(a.shape, lambda b, t: (0,) * nd)

    in_specs = [
        pl.BlockSpec((tl, D_IN), row_map),
        pl.BlockSpec((tl, RET_HD), lambda b, t: (t, 0)),
        pl.BlockSpec((tl, RET_HD), lambda b, t: (t, 0)),
    ] + [const_spec(a) for a in whole]
    out_shape = (
        jax.ShapeDtypeStruct((n_seq * seq_len, D_MODEL), BF16),
        jax.ShapeDtypeStruct((n_seq, 1, N_STATE), F32),
        jax.ShapeDtypeStruct((n_seq, 1, N_STATE), F32),
        jax.ShapeDtypeStruct((n_seq, RET_HEADS, RET_HD, RET_HD), F32),
        jax.ShapeDtypeStruct((n_seq, POOL_BUF, D_POOL), F32),
    )
    out_specs = (
        pl.BlockSpec((tl, D_MODEL), row_map),
        pl.BlockSpec((1, 1, N_STATE), lambda b, t: (b, 0, 0)),
        pl.BlockSpec((1, 1, N_STATE), lambda b, t: (b, 0, 0)),
        pl.BlockSpec((1, RET_HEADS, RET_HD, RET_HD), lambda b, t: (b, 0, 0, 0)),
        pl.BlockSpec((1, POOL_BUF, D_POOL), lambda b, t: (b, 0, 0)),
    )
    return pl.pallas_call(
        functools.partial(_mix_prompt_kernel, tl=tl),
        grid=(n_seq, nt),
        in_specs=in_specs,
        out_specs=out_specs,
        out_shape=out_shape,
        scratch_shapes=[
            pltpu.VMEM((tl, 2 * N_STATE), F32),
            pltpu.VMEM((SUBLANES, 2 * N_STATE), F32),
            pltpu.VMEM((16 + tl, D_POOL), F32),
        ],
        compiler_params=_params("arbitrary", "arbitrary"),
        name="mix_prompt",
    )(proj, cos_t, sin_t, *whole)


def _mix_sample_kernel(proj_ref, cos_ref, sin_ref, pw_ref, bmat_ref, cmat_ref, dskip_ref, wglu_ref,
                       bglu_ref, mask_ref, qdec_ref, kdec_ref, cdec_ref, wpool_ref, pscale_ref,
                       h0re_ref, h0im_ref, s0_ref, pool0_ref,
                       y_ref, hre_ref, him_ref, sret_ref, pbuf_ref,
                       dr_scr, o_scr, qd_scr, kd_scr, pool_scr, *, bs, seq_len, pos0):
    n_rows = bs * seq_len

    u = proj_ref[:, 0:D_SSM]
    _ssm_drive(u.astype(BF16), bmat_ref, dr_scr, SUBLANES, n_rows)

    def carry_fn(b, lo):
        return (h0re_ref[pl.ds(b, 1), lo:lo + SCAN_LANES], h0im_ref[pl.ds(b, 1), lo:lo + SCAN_LANES])

    def block_end_fn(b, lo, h_re, h_im):
        hre_ref[pl.ds(b, 1), lo:lo + SCAN_LANES] = h_re
        him_ref[pl.ds(b, 1), lo:lo + SCAN_LANES] = h_im

    _ssm_scan(dr_scr, pw_ref, bs, carry_fn, block_end_fn)
    y = _ssm_readout(dr_scr, cmat_ref, SUBLANES, n_rows) + dskip_ref[...] * u
    y_ref[:, 0:D_SSM] = _ssm_gate(y, wglu_ref, bglu_ref).astype(BF16)

    scale = RET_HD ** -0.5
    cosv = cos_ref[...]
    sinv = sin_ref[...]
    for h in range(RET_HEADS):
        hs = slice(h * RET_HD, (h + 1) * RET_HD)
        q = proj_ref[:, Q_OFF + h * RET_HD:Q_OFF + (h + 1) * RET_HD]
        k = proj_ref[:, K_OFF + h * RET_HD:K_OFF + (h + 1) * RET_HD]
        v = proj_ref[:, V_OFF + h * RET_HD:V_OFF + (h + 1) * RET_HD].astype(BF16)
        qr = _rotary(q, cosv, sinv)
        kr = _rotary(k, cosv, sinv) * scale
        sc = lax.dot_general(qr.astype(BF16), kr.astype(BF16), (((1,), (1,)), ((), ())),
                             preferred_element_type=F32) * mask_ref[h]
        o_scr[:, hs] = jnp.dot(sc.astype(BF16), v, preferred_element_type=F32)
        qd_scr[:, hs] = qr * qdec_ref[:, hs]
        kd_scr[:, hs] = kr * kdec_ref[:, hs]

    row_seq = lax.broadcasted_iota(jnp.int32, (n_rows, 1), 0) // seq_len

    def seq_body(s, carry):
        own = row_seq == s
        for h in range(RET_HEADS):
            hs = slice(h * RET_HD, (h + 1) * RET_HD)
            s_old = s0_ref[s, h]
            qd = jnp.where(own, qd_scr[:, hs], 0.0).astype(BF16)
            kd = jnp.where(own, kd_scr[:, hs], 0.0).astype(BF16)
            v = proj_ref[:, V_OFF + h * RET_HD:V_OFF + (h + 1) * RET_HD].astype(BF16)
            o_scr[:, hs] += jnp.dot(qd, s_old.astype(BF16), preferred_element_type=F32)
            sret_ref[s, h] = (s_old * cdec_ref[h:h + 1, :]
                              + lax.dot_general(kd, v, (((0,), (0,)), ((), ())),
                                                preferred_element_type=F32))
        return carry

    lax.fori_loop(0, bs, seq_body, 0)
    for h in range(RET_HEADS):
        hs = slice(h * RET_HD, (h + 1) * RET_HD)
        g = proj_ref[:, G_OFF + h * RET_HD:G_OFF + (h + 1) * RET_HD]
        y_ref[:, D_SSM + h * RET_HD:D_SSM + (h + 1) * RET_HD] = _head_norm_gate(o_scr[:, hs], g).astype(BF16)

    uc = proj_ref[:, C_OFF:C_OFF + D_POOL].reshape(bs, seq_len, D_POOL)
    pool_scr[:, 1:16, :] = pool0_ref[...]
    pool_scr[:, 16:16 + seq_len, :] = uc
    pos = pos0 + lax.broadcasted_iota(jnp.int32, (1, seq_len, POOL_GD), 1)
    counts = _pool_counts(pos)
    parts = []
    for gi, w in enumerate(POOL_WINDOWS):
        lanes = slice(gi * POOL_GD, (gi + 1) * POOL_GD)
        acc = pool_scr[:, 16:16 + seq_len, lanes]
        for j in range(1, w):
            acc = acc + pool_scr[:, 16 - j:16 - j + seq_len, lanes]
        parts.append(acc / counts[gi] - uc[:, :, lanes])
    pooled = jnp.concatenate(parts, axis=-1).reshape(n_rows, D_POOL).astype(BF16)
    yc = jnp.dot(pooled, wpool_ref[...], preferred_element_type=F32) * pscale_ref[...]
    y_ref[:, D_SSM + D_RET:] = yc.astype(BF16)
    pbuf_ref[...] = pool_scr[:, seq_len + 1:seq_len + 16, :]


def _mix_sample(proj, proj_row0, states, layer, prev_out, n_seq, seq_len, consts, bs):
    n_rows = bs * seq_len

    def const_spec(a):
        nd = a.ndim
        return pl.BlockSpec(a.shape, lambda i: (0,) * nd)

    state_specs = [
        pl.BlockSpec((None, bs, N_STATE), lambda i: (layer, i, 0)),
        pl.BlockSpec((None, bs, N_STATE), lambda i: (layer, i, 0)),
        pl.BlockSpec((None, bs, RET_HEADS, RET_HD, RET_HD), lambda i: (layer, i, 0, 0, 0)),
        pl.BlockSpec((None, bs, POOL_BUF, D_POOL), lambda i: (layer, i, 0, 0)),
    ]
    assert proj_row0 % n_rows == 0
    in_specs = ([pl.BlockSpec((n_rows, D_IN), lambda i: (proj_row0 // n_rows + i, 0))]
                + [const_spec(a) for a in consts] + state_specs)
    args = [proj, *consts, *states]
    n_in = len(args)
    aliases = {}
    if prev_out is not None:
        in_specs += [pl.BlockSpec(memory_space=pl.ANY)] * len(prev_out)
        aliases = {n_in + k: 1 + k for k in range(len(prev_out))}
        args += list(prev_out)
    n_args = len(args)
    out_shape = (
        jax.ShapeDtypeStruct((n_seq * seq_len, D_MODEL), BF16),
        jax.ShapeDtypeStruct((DEPTH, n_seq, N_STATE), F32),
        jax.ShapeDtypeStruct((DEPTH, n_seq, N_STATE), F32),
        jax.ShapeDtypeStruct((DEPTH, n_seq, RET_HEADS, RET_HD, RET_HD), F32),
        jax.ShapeDtypeStruct((DEPTH, n_seq, POOL_BUF, D_POOL), F32),
    )
    out_specs = (pl.BlockSpec((n_rows, D_MODEL), lambda i: (i, 0)),) + tuple(state_specs)

    def body(*refs):
        _mix_sample_kernel(*refs[:n_in], *refs[n_args:], bs=bs, seq_len=seq_len, pos0=PAST_LEN)

    return pl.pallas_call(
        body,
        grid=(n_seq // bs,),
        in_specs=in_specs,
        out_specs=out_specs,
        out_shape=out_shape,
        scratch_shapes=[
            pltpu.VMEM((SUBLANES + n_rows, 2 * N_STATE), F32),
            pltpu.VMEM((n_rows, D_RET), F32),
            pltpu.VMEM((n_rows, D_RET), F32),
            pltpu.VMEM((n_rows, D_RET), F32),
            pltpu.VMEM((bs, 16 + seq_len, D_POOL), F32),
        ],
        input_output_aliases=aliases,
        compiler_params=_params("arbitrary"),
        name="mix_sample",
    )(*args)


def _out_proj_kernel(y_ref, w_ref, x_ref, g1_ref, sh_ref, sc_ref, g_ref, xo_ref, h_ref, w_scr, *, bb, tl):
    @pl.when(pl.program_id(0) == 0)
    def _():
        w_scr[...] = w_ref[...].astype(BF16)

    f = jnp.dot(y_ref[...], w_scr[...], preferred_element_type=F32)
    xn = x_ref[...].reshape(bb, tl, D_MODEL) + g1_ref[...] * f.reshape(bb, tl, D_MODEL)
    xo_ref[...] = xn.reshape(bb * tl, D_MODEL)
    h = _rms(xn) * g_ref[...] * (1.0 + sc_ref[...]) + sh_ref[...]
    h_ref[...] = h.reshape(bb * tl, D_MODEL).astype(BF16)


def _out_proj(y, w, layer, x, x_tile0, mod, g, rows, out_rows, prev):
    out_spec = pl.BlockSpec((rows.tm, D_MODEL), lambda i: (rows.first_tile + i, 0))
    return _call_into(
        functools.partial(_out_proj_kernel, bb=rows.bb, tl=rows.tl),
        None if prev is None else {0: prev[0], 1: prev[1]},
        grid=(rows.n_tiles,),
        in_specs=[
            pl.BlockSpec((rows.tm, D_MODEL), lambda i: (i, 0)),
            pl.BlockSpec((None, D_MODEL, D_MODEL), lambda i: (layer, 0, 0), pipeline_mode=pl.Buffered(1)),
            pl.BlockSpec((rows.tm, D_MODEL), lambda i: (x_tile0 + i, 0)),
            rows.mod_spec(layer, MOD_GATE1, lambda i: i),
            rows.mod_spec(layer, MOD_SHIFT2, lambda i: i),
            rows.mod_spec(layer, MOD_SCALE2, lambda i: i),
            pl.BlockSpec((None, 1, D_MODEL), lambda i: (layer, 0, 0)),
        ],
        args=[y, w, x, mod, mod, mod, g],
        out_specs=(out_spec, out_spec),
        out_shape=(jax.ShapeDtypeStruct((out_rows, D_MODEL), F32),
                   jax.ShapeDtypeStruct((out_rows, D_MODEL), BF16)),
        scratch_shapes=[pltpu.VMEM((D_MODEL, D_MODEL), BF16)],
        sem=("arbitrary",),
        name="out_proj",
    )


def _conv_gate(a, b, am1, am2, cw, cb):
    conv = cb + am2 * cw[0:1]
    conv = conv + am1 * cw[1:2]
    conv = conv + a * cw[2:3]
    return jax.nn.silu(conv) * b


def _ffn_up_kernel(h_ref, wa_ref, wb_ref, cw_ref, cb_ref, st_ref, act_ref, tail_ref, ns_ref, w_scr, carry_scr,
                   *, prompt, sample, n_chunks):
    c = pl.program_id(0)
    i = pl.program_id(1)

    @pl.when(i == 0)
    def _():
        w_scr[:, :FF_TILE] = wa_ref[...].astype(BF16)

    @pl.when((i == 0) & (c < n_chunks - 1))
    def _():
        w_scr[:, FF_TILE:] = wb_ref[0].astype(BF16)

    @pl.when((i == 0) & (c == n_chunks - 1))
    def _():
        w_scr[:, FF_TILE:FF_TILE + FF_LAST] = wb_ref[0, :, FF_TILE - FF_LAST:].astype(BF16)

    cw = cw_ref[...]
    cb = cb_ref[...]
    n_sub = prompt.tm // FF_SUB_ROWS

    @pl.when(i >= prompt.n_tiles)
    def _():
        tl = sample.tl
        sb = FF_SUB_ROWS // tl
        tok = lax.broadcasted_iota(jnp.int32, (sb, tl, FF_TILE), 1)
        for r in range(n_sub):
            rows = slice(r * FF_SUB_ROWS, (r + 1) * FF_SUB_ROWS)
            sq = slice(r * sb, (r + 1) * sb)
            ab = jnp.dot(h_ref[rows, :], w_scr[...], preferred_element_type=F32)
            a = ab[:, :FF_TILE].reshape(sb, tl, FF_TILE)
            b = ab[:, FF_TILE:].reshape(sb, tl, FF_TILE)
            p0 = st_ref[sq, 0:1, :]
            p1 = st_ref[sq, 1:2, :]
            am1 = jnp.where(tok == 0, p1, pltpu.roll(a, 1, 1))
            am2 = jnp.where(tok == 0, p0, jnp.where(tok == 1, p1, pltpu.roll(a, 2, 1)))
            act = _conv_gate(a, b, am1, am2, cw, cb)
            act_ref[rows, :] = act.reshape(FF_SUB_ROWS, FF_TILE).astype(BF16)
            ns_ref[sq] = a[:, tl - 2:tl, :]

    @pl.when(i < prompt.n_tiles)
    def _():
        @pl.when(i % prompt.tiles_per_seq == 0)
        def _():
            carry_scr[...] = jnp.zeros((SUBLANES, FF_TILE), F32)

        tok = lax.broadcasted_iota(jnp.int32, (SUBLANES, FF_TILE), 0)
        for r in range(n_sub):
            rows = slice(r * FF_SUB_ROWS, (r + 1) * FF_SUB_ROWS)
            ab = jnp.dot(h_ref[rows, :], w_scr[...], preferred_element_type=F32)
            a = ab[:, :FF_TILE]
            b = ab[:, FF_TILE:]
            p0 = carry_scr[SUBLANES - 2:SUBLANES - 1, :]
            p1 = carry_scr[SUBLANES - 1:SUBLANES, :]
            r1 = pltpu.roll(a, 1, 0)
            r2 = pltpu.roll(a, 2, 0)
            head1 = jnp.where(tok == 0, p1, r1[:SUBLANES])
            head2 = jnp.where(tok == 0, p0, jnp.where(tok == 1, p1, r2[:SUBLANES]))
            am1 = jnp.concatenate([head1, r1[SUBLANES:]], axis=0)
            am2 = jnp.concatenate([head2, r2[SUBLANES:]], axis=0)
            act_ref[rows, :] = _conv_gate(a, b, am1, am2, cw, cb).astype(BF16)
            carry_scr[...] = a[FF_SUB_ROWS - SUBLANES:, :]
        tail_ref[0] = carry_scr[SUBLANES - 2:SUBLANES, :]


def _ffn_up(h, w_up, conv_w, conv_b, layer, state, prev_state, prompt, sample):
    m = h.shape[0]
    assert prompt.tm == sample.tm and sample.n_tiles == 1 and sample.first_tile == prompt.n_tiles
    tm = prompt.tm
    n_tiles = prompt.n_tiles + sample.n_tiles
    n_seq = sample.bb
    n_chunks = pl.cdiv(D_FF, FF_TILE)
    in_specs = [
        pl.BlockSpec((tm, D_MODEL), lambda c, i: (i, 0)),
        pl.BlockSpec((None, D_MODEL, FF_TILE), lambda c, i: (layer, 0, c)),
        pl.BlockSpec((pl.Element(1), pl.Element(D_MODEL), pl.Element(FF_TILE)),
                     lambda c, i: (layer, 0, LANES * jnp.minimum((D_FF + c * FF_TILE) // LANES,
                                                                 (2 * D_FF - FF_TILE) // LANES))),
        pl.BlockSpec((None, CONV_W, FF_TILE), lambda c, i: (layer, 0, c)),
        pl.BlockSpec((None, 1, FF_TILE), lambda c, i: (layer, 0, c)),
        pl.BlockSpec((None, n_seq, CONV_W - 1, FF_TILE), lambda c, i: (layer, 0, 0, c)),
    ]
    return _call_into(
        functools.partial(_ffn_up_kernel, prompt=prompt, sample=sample, n_chunks=n_chunks),
        None if prev_state is None else {2: prev_state},
        grid=(n_chunks, n_tiles),
        in_specs=in_specs,
        args=[h, w_up, w_up, conv_w, conv_b, state],
        out_specs=(
            pl.BlockSpec((tm, FF_TILE), lambda c, i: (i, c)),
            pl.BlockSpec((1, CONV_W - 1, FF_TILE), lambda c, i: (jnp.minimum(i, prompt.n_tiles - 1), 0, c)),
            pl.BlockSpec((None, n_seq, CONV_W - 1, FF_TILE), lambda c, i: (layer, 0, 0, c)),
        ),
        out_shape=(jax.ShapeDtypeStruct((m, D_FF), BF16),
                   jax.ShapeDtypeStruct((prompt.n_tiles, CONV_W - 1, D_FF), F32),
                   jax.ShapeDtypeStruct((DEPTH, n_seq, CONV_W - 1, D_FF), F32)),
        scratch_shapes=[pltpu.VMEM((D_MODEL, 2 * FF_TILE), BF16), pltpu.VMEM((SUBLANES, FF_TILE), F32)],
        sem=("arbitrary", "arbitrary"),
        name="ffn_up",
    )


def _cmul(ar, ai, br, bi):
    return ar * br - ai * bi, ar * bi + ai * br


def _ssm_consts(a_re, a_im, log_dt, b_re, b_im, c_re, c_im, scan_steps):
    lam_re, lam_im = a_re.astype(F32), a_im.astype(F32)
    dt = jnp.exp(log_dt.astype(F32))[:, None]
    mag = jnp.exp(lam_re * dt)
    ab_re, ab_im = mag * jnp.cos(lam_im * dt), mag * jnp.sin(lam_im * dt)
    den = lam_re * lam_re + lam_im * lam_im
    f_re = ((ab_re - 1.0) * lam_re + ab_im * lam_im) / den
    f_im = (ab_im * lam_re - (ab_re - 1.0) * lam_im) / den
    br, bi = b_re.astype(F32), b_im.astype(F32)
    bb_re = f_re[..., None] * br - f_im[..., None] * bi
    bb_im = f_re[..., None] * bi + f_im[..., None] * br

    gh = SSM_GROUPS // 2
    eye = jnp.eye(gh, dtype=F32)

    def drive_half(m):
        return jnp.einsum("gph,gk->ghkp", m, eye).reshape(gh * SSM_GROUP, gh * SSM_STATE)

    def read_half(m):
        return jnp.einsum("ghp,gk->gpkh", m, eye).reshape(gh * SSM_STATE, gh * SSM_GROUP)

    bmat = jnp.stack([
        jnp.concatenate([drive_half(bb_re[s]), drive_half(bb_im[s])], axis=1)
        for s in (slice(0, gh), slice(gh, 2 * gh))]).astype(BF16)
    cre, cim = c_re.astype(F32), c_im.astype(F32)
    cmat = jnp.stack([
        jnp.concatenate([read_half(cre[s]), -read_half(cim[s])], axis=0)
        for s in (slice(0, gh), slice(gh, 2 * gh))]).astype(BF16)

    step_re, step_im = _power_table(ab_re.reshape(1, N_STATE), ab_im.reshape(1, N_STATE), scan_steps)
    sub_re, sub_im = _power_table(step_re[-1:], step_im[-1:], SUBLANES)
    abar = jnp.stack([step_re[:1], step_im[:1]])
    pw_block = _scan_planes(step_re[:SUBLANES], step_im[:SUBLANES])
    pw_sub = _scan_planes(sub_re, sub_im)
    return bmat, cmat, abar, pw_block, pw_sub


def _power_table(re, im, n):
    while re.shape[0] < n:
        top_re, top_im = re[-1:], im[-1:]
        more_re, more_im = _cmul(re, im, top_re, top_im)
        re, im = jnp.concatenate([re, more_re]), jnp.concatenate([im, more_im])
    return re[:n], im[:n]


def _scan_planes(re, im):
    row = jnp.arange(SUBLANES)[:, None]
    planes_re = [jnp.where(row >= s, re[s - 1][None, :], 0.0) for s in (1, 2, 4)] + [re]
    planes_im = [jnp.where(row >= s, im[s - 1][None, :], 0.0) for s in (1, 2, 4)] + [im]
    return jnp.stack([jnp.stack(planes_re), jnp.stack(planes_im)])


def _time_on_sublanes(n_rows):
    steps = n_rows // SUBLANES
    p = np.zeros((n_rows, n_rows), np.float32)
    r = np.arange(n_rows)
    p[r, (r % SUBLANES) * steps + r // SUBLANES] = 1.0
    return jnp.asarray(p, BF16), jnp.asarray(p.T, BF16)


def _rotary_tables(pos):
    half = RET_HD // 2
    inv = ROPE_BASE ** (-np.arange(half, dtype=np.float64) / half)
    ang = np.asarray(pos, np.float64)[:, None] * inv[None, :]
    cos, sin = np.cos(ang), np.sin(ang)
    return (np.concatenate([cos, cos], axis=-1).astype(np.float32),
            np.concatenate([-sin, sin], axis=-1).astype(np.float32))


def _retention_consts(chunk, n_seq):
    lg = np.log1p(-np.exp2(-5.0 - np.arange(RET_HEADS, dtype=np.float64)))
    r = np.arange(chunk * n_seq)
    ti = (r % chunk).astype(np.float64)
    seq = r // chunk
    diff = ti[:, None] - ti[None, :]
    keep = (seq[:, None] == seq[None, :]) & (diff >= 0.0)
    mask = np.where(keep[None], np.exp(np.maximum(diff, 0.0)[None] * lg[:, None, None]), 0.0)
    q_dec = np.exp((ti + 1.0)[:, None] * lg[None, :])
    k_dec = np.exp((chunk - 1.0 - ti)[:, None] * lg[None, :])
    c_dec = np.exp(chunk * lg)
    expand = lambda d: np.repeat(d, RET_HD, axis=1).astype(np.float32)
    return (mask.astype(np.float32), expand(q_dec), expand(k_dec),
            np.broadcast_to(c_dec[:, None], (RET_HEADS, RET_HD)).astype(np.float32))


def _pool_weight(w_pool):
    eye = jnp.eye(len(POOL_WINDOWS), dtype=F32)
    return jnp.einsum("gcd,gk->gckd", w_pool, eye).reshape(D_POOL, D_POOL).astype(BF16)


def kernel(x_prompt, x_sample, state_ssm_re, state_ssm_im, state_ret, state_pool, state_ffn_conv, c_prompt, c_sample, w_ada, b_ada, norm1_g, w_in, ssm_a_re, ssm_a_im, ssm_log_dt, ssm_b_re, ssm_b_im, ssm_c_re, ssm_c_im, ssm_d, ssm_w_glu, ssm_b_glu, pool_w, pool_scale, w_out, norm2_g, ffn_w_up, ffn_conv_w, ffn_conv_b, ffn_w_down, final_norm_g):
    bp, lp, _ = x_prompt.shape
    bs, ls, _ = x_sample.shape
    tl_mix = 512
    seq_mix = 8

    n_all = bp + bs
    n_pad = -(-n_all // SUBLANES) * SUBLANES
    c_all = jnp.concatenate([c_sample, c_prompt, jnp.zeros((n_pad - n_all, D_MODEL), F32)], axis=0)
    mod = _ada_mod(c_all, w_ada, b_ada)

    cos_p, sin_p = _rotary_tables(np.arange(lp))
    cos_s, sin_s = _rotary_tables(PAST_LEN + np.arange(ls))
    cos_s, sin_s = np.tile(cos_s, (seq_mix, 1)), np.tile(sin_s, (seq_mix, 1))
    assert math.gcd(lp, RET_CHUNK) == RET_CHUNK and math.gcd(ls, RET_CHUNK) == ls == SUBLANES
    ret_p = _retention_consts(RET_CHUNK, 1)
    ret_s = _retention_consts(ls, seq_mix)
    perm = _time_on_sublanes(tl_mix)

    rows_total = bp * lp + bs * ls
    groups = (_Rows(bp, lp, ROW_TILE, bs, 0), _Rows(bs, ls, ROW_TILE, 0, bp * lp))
    groups_full = (_Rows(bp, lp, ROW_TILE_FULL, bs, 0), _Rows(bs, ls, ROW_TILE_FULL, 0, bp * lp))
    big_p, big_s = groups

    norm1 = norm1_g.reshape(DEPTH, 1, D_MODEL)
    norm2 = norm2_g.reshape(DEPTH, 1, D_MODEL)
    final_g = final_norm_g.reshape(1, 1, D_MODEL)
    conv_b = ffn_conv_b.reshape(DEPTH, 1, D_FF)
    sample_states = (state_ssm_re.reshape(DEPTH, bs, N_STATE), state_ssm_im.reshape(DEPTH, bs, N_STATE),
                     state_ret, state_pool)

    x_first = (x_prompt.reshape(bp * lp, D_MODEL), x_sample.reshape(bs * ls, D_MODEL))
    x_all = None

    def residual_rows(k, rows):
        return (x_first[k], 0) if x_all is None else (x_all, rows.first_tile)

    new_p = ([], [], [], [], [])
    s_mix = None
    s_conv = None
    for l in range(DEPTH):
        bmat, cmat, abar, pw_block, pw_sub = _ssm_consts(
            ssm_a_re[l], ssm_a_im[l], ssm_log_dt[l], ssm_b_re[l], ssm_b_im[l], ssm_c_re[l], ssm_c_im[l],
            tl_mix // SUBLANES)
        shared = (bmat, cmat, ssm_d[l].reshape(1, D_SSM), ssm_w_glu[l].astype(BF16),
                  ssm_b_glu[l].reshape(1, D_SSM))
        pool_c = (_pool_weight(pool_w[l]), pool_scale[l].reshape(1, D_POOL))
        pre = (mod, MOD_SHIFT1, MOD_SCALE1)

        h = None
        for k, rows in enumerate(groups):
            h = _norm(*residual_rows(k, rows), norm1, l, pre, rows, BF16, rows_total, h)
        proj = _matmul(h, w_in, l, groups, IN_PROJ_COLS, "in_proj")
        y_p, st_re, st_im, st_ret, st_pool = _mix_prompt(
            proj, bp, lp, (cos_p, sin_p) + perm + (abar, pw_sub) + shared + ret_p + pool_c, tl_mix)
        y_s, *s_mix = _mix_sample(proj, bp * lp, sample_states, l, s_mix, bs, ls,
                                  (cos_s, sin_s, pw_block) + shared + ret_s + pool_c, seq_mix)
        xh = None
        for k, (rows, y) in enumerate(zip(groups_full, (y_p, y_s))):
            xh = _out_proj(y, w_out, l, *residual_rows(k, rows), mod, norm2, rows, rows_total, xh)
        xn, h2 = xh
        act, tails, s_conv = _ffn_up(h2, ffn_w_up, ffn_conv_w, conv_b, l, state_ffn_conv, s_conv, big_p, big_s)
        x_all = _matmul(act, ffn_w_down, l, groups, FF_DOWN_COLS, "ffn_down", residual=(xn, mod, MOD_GATE2),
                        single_buffer_w=True)
        st_conv = tails[big_p.tiles_per_seq - 1::big_p.tiles_per_seq]
        for lst, st in zip(new_p, (st_re, st_im, st_ret, st_pool, st_conv)):
            lst.append(st)

    yp = _norm(x_all, big_p.first_tile, final_g, 0, None, big_p, F32)
    ys = _norm(x_all, big_s.first_tile, final_g, 0, None, big_s, F32)

    p_re, p_im, p_ret, p_pool, p_conv = [jnp.stack(st) for st in new_p]
    s_re, s_im, s_ret, s_pool = s_mix
    shape_p = (DEPTH, bp, SSM_GROUPS, SSM_STATE)
    shape_s = (DEPTH, bs, SSM_GROUPS, SSM_STATE)
    return (yp.reshape(bp, lp, D_MODEL), ys.reshape(bs, ls, D_MODEL),
            p_re.reshape(shape_p), p_im.reshape(shape_p), p_ret, p_pool, p_conv,
            s_re.reshape(shape_s), s_im.reshape(shape_s), s_ret, s_pool, s_conv)
```

```python
import functools
import math

import jax
import jax.numpy as jnp
import numpy as np
from jax import lax
from jax.experimental import pallas as pl
from jax.experimental.pallas import tpu as pltpu

F32 = jnp.float32
BF16 = jnp.bfloat16

D_MODEL = 2048
DEPTH = 2
PAST_LEN = 16384
D_SSM = 512
D_RET = 1024
D_POOL = 512
SSM_GROUP = 16
SSM_GROUPS = 32
SSM_STATE = 64
N_STATE = SSM_GROUPS * SSM_STATE
RET_HEADS = 8
RET_HD = 128
RET_CHUNK = 128
POOL_WINDOWS = (2, 4, 8, 16)
POOL_GD = 128
POOL_BUF = 15
D_FF = 5504
FF_TILE = 512
FF_LAST = D_FF % FF_TILE
CONV_W = 3
D_IN = D_SSM + 4 * D_RET + D_POOL
ROPE_BASE = 10000.0
EPS = 1e-6

SUBLANES = 8
LANES = 128
ROW_TILE = 1024
ROW_TILE_FULL = 512
FF_SUB_ROWS = 256
IN_PROJ_COLS = 1024
FF_DOWN_COLS = 512
ADA_COLS = 1024
MOD_SHIFT1, MOD_SCALE1, MOD_GATE1, MOD_SHIFT2, MOD_SCALE2, MOD_GATE2 = range(6)
SCAN_LANES = 512
VMEM_LIMIT = 56 * 1024 * 1024

Q_OFF = D_SSM
K_OFF = D_SSM + D_RET
V_OFF = D_SSM + 2 * D_RET
G_OFF = D_SSM + 3 * D_RET
C_OFF = D_SSM + 4 * D_RET


def _params(*sem):
    return pltpu.CompilerParams(dimension_semantics=sem, vmem_limit_bytes=VMEM_LIMIT)


def _rms(x):
    return x * lax.rsqrt(jnp.mean(x * x, axis=-1, keepdims=True) + EPS)


def _ada_kernel(c_ref, w_ref, b_ref, o_ref):
    a = jax.nn.silu(c_ref[...]).astype(BF16)
    res = jnp.dot(a, w_ref[...].astype(BF16), preferred_element_type=F32) + b_ref[...]
    for r in range(res.shape[0]):
        o_ref[r] = res[r:r + 1, :]


def _ada_mod(c_all, w_ada, b_ada):
    nseq = c_all.shape[0]
    per_vec = D_MODEL // ADA_COLS
    return pl.pallas_call(
        _ada_kernel,
        grid=(DEPTH, 6 * per_vec),
        in_specs=[
            pl.BlockSpec((nseq, D_MODEL), lambda l, j: (0, 0)),
            pl.BlockSpec((None, D_MODEL, ADA_COLS), lambda l, j: (l, 0, j)),
            pl.BlockSpec((None, 1, ADA_COLS), lambda l, j: (l, 0, j)),
        ],
        out_specs=pl.BlockSpec((None, None, nseq, 1, ADA_COLS), lambda l, j: (l, j // per_vec, 0, 0, j % per_vec)),
        out_shape=jax.ShapeDtypeStruct((DEPTH, 6, nseq, 1, D_MODEL), F32),
        compiler_params=_params("arbitrary", "arbitrary"),
        name="ada_mod",
    )(c_all, w_ada, b_ada.reshape(DEPTH, 1, 6 * D_MODEL))


class _Rows:
    def __init__(self, n_seq, seq_len, tm, first_seq, first_row):
        if seq_len >= tm:
            assert seq_len % tm == 0
            self.bb, self.tl = 1, tm
            self.tiles_per_seq = seq_len // tm
        else:
            assert tm % seq_len == 0 and seq_len == SUBLANES
            self.bb, self.tl = tm // seq_len, seq_len
            self.tiles_per_seq = 1
        assert first_seq % self.bb == 0 and first_row % tm == 0
        self.tm = tm
        self.n_tiles = n_seq * seq_len // tm
        self.first_block = first_seq // self.bb
        self.first_tile = first_row // tm

    def mod_block(self, i):
        return self.first_block + i // self.tiles_per_seq

    def mod_spec(self, layer, vec, row_tile_of, cols=D_MODEL, col_block_of=lambda *ids: 0):
        return pl.BlockSpec(
            (None, None, self.bb, 1, cols),
            lambda *ids: (layer, vec, self.mod_block(row_tile_of(*ids)), 0, col_block_of(*ids)))


def _call_into(kernel_fn, prev, *, grid, in_specs, args, out_specs, out_shape, scratch_shapes=(), sem, name):
    n_in = len(args)
    in_specs, args = list(in_specs), list(args)
    aliases = {}
    for k, arr in sorted((prev or {}).items()):
        aliases[len(args)] = k
        in_specs.append(pl.BlockSpec(memory_space=pl.ANY))
        args.append(arr)
    n_args = len(args)

    def body(*refs):
        kernel_fn(*refs[:n_in], *refs[n_args:])

    return pl.pallas_call(
        body, grid=grid, in_specs=in_specs, out_specs=out_specs, out_shape=out_shape,
        scratch_shapes=list(scratch_shapes), input_output_aliases=aliases,
        compiler_params=_params(*sem), name=name,
    )(*args)


def _norm_kernel(*refs, bb, tl, modulated):
    if modulated:
        x_ref, g_ref, sh_ref, sc_ref, o_ref = refs
    else:
        x_ref, g_ref, o_ref = refs
    h = _rms(x_ref[...].reshape(bb, tl, D_MODEL)) * g_ref[...]
    if modulated:
        h = h * (1.0 + sc_ref[...]) + sh_ref[...]
    o_ref[...] = h.reshape(bb * tl, D_MODEL).astype(o_ref.dtype)


def _norm(x, x_tile0, g, layer, mods, rows, out_dtype, out_rows=None, prev=None):
    in_specs = [pl.BlockSpec((rows.tm, D_MODEL), lambda i: (x_tile0 + i, 0)),
                pl.BlockSpec((None, 1, D_MODEL), lambda i: (layer, 0, 0))]
    args = [x, g]
    if mods is not None:
        mod, shift_vec, scale_vec = mods
        in_specs += [rows.mod_spec(layer, shift_vec, lambda i: i), rows.mod_spec(layer, scale_vec, lambda i: i)]
        args += [mod, mod]
    out_tile0 = 0 if out_rows is None else rows.first_tile
    return _call_into(
        functools.partial(_norm_kernel, bb=rows.bb, tl=rows.tl, modulated=mods is not None),
        None if prev is None else {0: prev},
        grid=(rows.n_tiles,),
        in_specs=in_specs,
        args=args,
        out_specs=pl.BlockSpec((rows.tm, D_MODEL), lambda i: (out_tile0 + i, 0)),
        out_shape=jax.ShapeDtypeStruct((out_rows or rows.n_tiles * rows.tm, D_MODEL), out_dtype),
        sem=("arbitrary",),
        name="norm",
    )


def _matmul_kernel(*refs, groups, residual):
    if residual:
        a_ref, w_ref, x_ref, *gate_refs, o_ref, w_scr = refs
    else:
        a_ref, w_ref, o_ref, w_scr = refs
    i = pl.program_id(1)

    @pl.when(i == 0)
    def _():
        w_scr[...] = w_ref[...].astype(BF16)

    f = jnp.dot(a_ref[...], w_scr[...], preferred_element_type=F32)
    if not residual:
        o_ref[...] = f
        return
    tn = f.shape[1]
    for rows, g_ref in zip(groups, gate_refs):
        @pl.when((i >= rows.first_tile) & (i < rows.first_tile + rows.n_tiles))
        def _(rows=rows, g_ref=g_ref):
            xn = x_ref[...].reshape(rows.bb, rows.tl, tn) + g_ref[...] * f.reshape(rows.bb, rows.tl, tn)
            o_ref[...] = xn.reshape(rows.tm, tn)


def _matmul(a, w, layer, groups, tn, name, residual=None, single_buffer_w=False):
    m, k = a.shape
    n = w.shape[2]
    tm = groups[0].tm
    n_tiles = sum(rows.n_tiles for rows in groups)
    assert m == n_tiles * tm and all(rows.tm == tm for rows in groups)
    w_mode = dict(pipeline_mode=pl.Buffered(1)) if single_buffer_w else {}
    in_specs = [
        pl.BlockSpec((tm, k), lambda j, i: (i, 0)),
        pl.BlockSpec((None, k, tn), lambda j, i: (layer, 0, j), **w_mode),
    ]
    args = [a, w]
    if residual is not None:
        x, mod, gate_vec = residual
        in_specs.append(pl.BlockSpec((tm, tn), lambda j, i: (i, j)))
        args.append(x)
        for rows in groups:
            in_specs.append(rows.mod_spec(
                layer, gate_vec,
                lambda j, i, rows=rows: jnp.clip(i - rows.first_tile, 0, rows.n_tiles - 1),
                tn, lambda j, i: j))
            args.append(mod)
    return pl.pallas_call(
        functools.partial(_matmul_kernel, groups=groups, residual=residual is not None),
        grid=(n // tn, n_tiles),
        in_specs=in_specs,
        out_specs=pl.BlockSpec((tm, tn), lambda j, i: (i, j)),
        out_shape=jax.ShapeDtypeStruct((m, n), F32),
        scratch_shapes=[pltpu.VMEM((k, tn), BF16)],
        compiler_params=_params("arbitrary", "arbitrary"),
        name=name,
    )(*args)


def _ssm_drive(ub, bmat_ref, dr_scr, row0, n_rows):
    half_u = D_SSM // 2
    half_n = N_STATE // 2
    for hf in range(2):
        d = jnp.dot(ub[:, hf * half_u:(hf + 1) * half_u], bmat_ref[hf], preferred_element_type=F32)
        dr_scr[row0:row0 + n_rows, hf * half_n:(hf + 1) * half_n] = d[:, :half_n]
        dr_scr[row0:row0 + n_rows, N_STATE + hf * half_n:N_STATE + (hf + 1) * half_n] = d[:, half_n:]


def _cmul_add(a_re, a_im, x_re, x_im, y_re, y_im):
    return y_re + (a_re * x_re - a_im * x_im), y_im + (a_re * x_im + a_im * x_re)


def _scan_tile(dr_scr, abar_ref, pws_ref, carry_scr, n_steps):
    row = lax.broadcasted_iota(jnp.int32, (SUBLANES, SCAN_LANES), 0)
    for c in range(N_STATE // SCAN_LANES):
        lo = c * SCAN_LANES
        re_cols = slice(lo, lo + SCAN_LANES)
        im_cols = slice(N_STATE + lo, N_STATE + lo + SCAN_LANES)
        a_re = jnp.broadcast_to(abar_ref[0, :, re_cols], (SUBLANES, SCAN_LANES))
        a_im = jnp.broadcast_to(abar_ref[1, :, re_cols], (SUBLANES, SCAN_LANES))

        def local_step(t, h):
            r = pl.ds(pl.multiple_of(t * SUBLANES, SUBLANES), SUBLANES)
            h_re, h_im = _cmul_add(a_re, a_im, h[0], h[1], dr_scr[r, re_cols], dr_scr[r, im_cols])
            dr_scr[r, re_cols] = h_re
            dr_scr[r, im_cols] = h_im
            return h_re, h_im

        zero = jnp.zeros((SUBLANES, SCAN_LANES), F32)
        g_re, g_im = lax.fori_loop(0, n_steps, local_step, (zero, zero), unroll=2)
        for si, s in enumerate((1, 2, 4)):
            g_re, g_im = _cmul_add(pws_ref[0, si, :, re_cols], pws_ref[1, si, :, re_cols],
                                   pltpu.roll(g_re, s, 0), pltpu.roll(g_im, s, 0), g_re, g_im)
        c_re = carry_scr[SUBLANES - 1:SUBLANES, re_cols]
        c_im = carry_scr[SUBLANES - 1:SUBLANES, im_cols]
        e_re, e_im = _cmul_add(pws_ref[0, 3, :, re_cols], pws_ref[1, 3, :, re_cols], c_re, c_im, g_re, g_im)
        carry_scr[:, re_cols] = e_re
        carry_scr[:, im_cols] = e_im
        in_re = jnp.where(row == 0, c_re, pltpu.roll(e_re, 1, 0))
        in_im = jnp.where(row == 0, c_im, pltpu.roll(e_im, 1, 0))

        def fix_step(t, w):
            w_re, w_im = _cmul(a_re, a_im, w[0], w[1])
            r = pl.ds(pl.multiple_of(t * SUBLANES, SUBLANES), SUBLANES)
            dr_scr[r, re_cols] = dr_scr[r, re_cols] + w_re
            dr_scr[r, im_cols] = dr_scr[r, im_cols] + w_im
            return w_re, w_im

        lax.fori_loop(0, n_steps, fix_step, (in_re, in_im), unroll=2)


def _scan_block(dr_scr, pw_ref, b, carry_fn, block_end_fn=None):
    r = pl.multiple_of(b * SUBLANES + SUBLANES, SUBLANES)
    for c in range(N_STATE // SCAN_LANES):
        lo = c * SCAN_LANES
        re_cols = slice(lo, lo + SCAN_LANES)
        im_cols = slice(N_STATE + lo, N_STATE + lo + SCAN_LANES)
        d_re = dr_scr[pl.ds(r, SUBLANES), re_cols]
        d_im = dr_scr[pl.ds(r, SUBLANES), im_cols]
        for si, s in enumerate((1, 2, 4)):
            p_re = pw_ref[0, si, :, re_cols]
            p_im = pw_ref[1, si, :, re_cols]
            r_re = pltpu.roll(d_re, s, 0)
            r_im = pltpu.roll(d_im, s, 0)
            d_re, d_im = (d_re + (p_re * r_re - p_im * r_im),
                          d_im + (p_re * r_im + p_im * r_re))
        c_re, c_im = carry_fn(b, lo)
        a_re = pw_ref[0, 3, :, re_cols]
        a_im = pw_ref[1, 3, :, re_cols]
        h_re = d_re + (a_re * c_re - a_im * c_im)
        h_im = d_im + (a_re * c_im + a_im * c_re)
        dr_scr[pl.ds(r, SUBLANES), re_cols] = h_re
        dr_scr[pl.ds(r, SUBLANES), im_cols] = h_im
        if block_end_fn is not None:
            block_end_fn(b, lo, h_re[SUBLANES - 1:SUBLANES], h_im[SUBLANES - 1:SUBLANES])


def _ssm_scan(dr_scr, pw_ref, n_blocks, carry_fn, block_end_fn=None):
    def body(b, carry):
        _scan_block(dr_scr, pw_ref, b, carry_fn, block_end_fn)
        return carry

    lax.fori_loop(0, n_blocks, body, 0)


def _ssm_readout(dr_scr, cmat_ref, row0, n_rows):
    half_n = N_STATE // 2
    parts = []
    for hf in range(2):
        h_re = dr_scr[row0:row0 + n_rows, hf * half_n:(hf + 1) * half_n].astype(BF16)
        h_im = dr_scr[row0:row0 + n_rows, N_STATE + hf * half_n:N_STATE + (hf + 1) * half_n].astype(BF16)
        parts.append(jnp.dot(h_re, cmat_ref[hf, :half_n], preferred_element_type=F32)
                     + jnp.dot(h_im, cmat_ref[hf, half_n:], preferred_element_type=F32))
    return jnp.concatenate(parts, axis=-1)


def _ssm_gate(y, wglu_ref, bglu_ref):
    ya = jax.nn.gelu(y)
    gate = jnp.dot(ya.astype(BF16), wglu_ref[...], preferred_element_type=F32) + bglu_ref[...]
    return ya * jax.nn.sigmoid(gate)


def _unpermute_rows(perm_t_ref, x):
    hi = x.astype(BF16)
    rest = x - hi.astype(F32)
    mid = rest.astype(BF16)
    lo = (rest - mid.astype(F32)).astype(BF16)
    p = perm_t_ref[...]
    return (jnp.dot(p, hi, preferred_element_type=F32) + jnp.dot(p, mid, preferred_element_type=F32)
            + jnp.dot(p, lo, preferred_element_type=F32))


def _rotary(x, cosv, sinv):
    return x * cosv + pltpu.roll(x, RET_HD // 2, 1) * sinv


def _head_norm_gate(o, g):
    mu = jnp.mean(o, axis=-1, keepdims=True)
    var = jnp.mean(jnp.square(o - mu), axis=-1, keepdims=True)
    return (o - mu) * lax.rsqrt(var + EPS) * jax.nn.silu(g)


def _pool_counts(pos):
    return [jnp.minimum(pos + 1, w).astype(F32) for w in POOL_WINDOWS]


def _mix_prompt_kernel(proj_ref, cos_ref, sin_ref, perm_ref, permt_ref, abar_ref, pws_ref,
                       bmat_ref, cmat_ref, dskip_ref, wglu_ref, bglu_ref,
                       mask_ref, qdec_ref, kdec_ref, cdec_ref, wpool_ref, pscale_ref,
                       y_ref, hre_ref, him_ref, sret_ref, pbuf_ref,
                       dr_scr, carry_scr, pool_scr, *, tl):
    t = pl.program_id(1)

    @pl.when(t == 0)
    def _():
        carry_scr[...] = jnp.zeros((SUBLANES, 2 * N_STATE), F32)
        sret_ref[...] = jnp.zeros(sret_ref.shape, F32)
        pool_scr[0:16, :] = jnp.zeros((16, D_POOL), F32)

    u = proj_ref[:, 0:D_SSM]
    up = jnp.dot(perm_ref[...], u.astype(BF16), preferred_element_type=F32).astype(BF16)
    _ssm_drive(up, bmat_ref, dr_scr, 0, tl)
    _scan_tile(dr_scr, abar_ref, pws_ref, carry_scr, tl // SUBLANES)
    y = _unpermute_rows(permt_ref, _ssm_readout(dr_scr, cmat_ref, 0, tl)) + dskip_ref[...] * u
    y_ref[:, 0:D_SSM] = _ssm_gate(y, wglu_ref, bglu_ref).astype(BF16)
    hre_ref[0] = carry_scr[SUBLANES - 1:SUBLANES, 0:N_STATE]
    him_ref[0] = carry_scr[SUBLANES - 1:SUBLANES, N_STATE:2 * N_STATE]

    scale = RET_HD ** -0.5

    def chunk_body(ci, carry):
        rows = pl.ds(pl.multiple_of(ci * RET_CHUNK, RET_CHUNK), RET_CHUNK)
        cosv = cos_ref[rows, :]
        sinv = sin_ref[rows, :]
        for h in range(RET_HEADS):
            hs = slice(h * RET_HD, (h + 1) * RET_HD)
            q = proj_ref[rows, Q_OFF + h * RET_HD:Q_OFF + (h + 1) * RET_HD]
            k = proj_ref[rows, K_OFF + h * RET_HD:K_OFF + (h + 1) * RET_HD]
            v = proj_ref[rows, V_OFF + h * RET_HD:V_OFF + (h + 1) * RET_HD].astype(BF16)
            g = proj_ref[rows, G_OFF + h * RET_HD:G_OFF + (h + 1) * RET_HD]
            qr = _rotary(q, cosv, sinv)
            kr = _rotary(k, cosv, sinv) * scale
            s_old = sret_ref[0, h]
            sc = lax.dot_general(qr.astype(BF16), kr.astype(BF16), (((1,), (1,)), ((), ())),
                                 preferred_element_type=F32) * mask_ref[h]
            o = (jnp.dot(sc.astype(BF16), v, preferred_element_type=F32)
                 + jnp.dot((qr * qdec_ref[:, hs]).astype(BF16), s_old.astype(BF16),
                           preferred_element_type=F32))
            sret_ref[0, h] = (s_old * cdec_ref[h:h + 1, :]
                              + lax.dot_general((kr * kdec_ref[:, hs]).astype(BF16), v,
                                                (((0,), (0,)), ((), ())), preferred_element_type=F32))
            y_ref[rows, D_SSM + h * RET_HD:D_SSM + (h + 1) * RET_HD] = _head_norm_gate(o, g).astype(BF16)
        return carry

    lax.fori_loop(0, tl // RET_CHUNK, chunk_body, 0)

    uc = proj_ref[:, C_OFF:C_OFF + D_POOL]
    pool_scr[16:16 + tl, :] = uc
    pos = t * tl + lax.broadcasted_iota(jnp.int32, (tl, POOL_GD), 0)
    counts = _pool_counts(pos)
    parts = []
    for gi, w in enumerate(POOL_WINDOWS):
        lanes = slice(gi * POOL_GD, (gi + 1) * POOL_GD)
        acc = pool_scr[16:16 + tl, lanes]
        for j in range(1, w):
            acc = acc + pool_scr[16 - j:16 - j + tl, lanes]
        parts.append(acc / counts[gi] - uc[:, lanes])
    pooled = jnp.concatenate(parts, axis=-1).astype(BF16)
    yc = jnp.dot(pooled, wpool_ref[...], preferred_element_type=F32) * pscale_ref[...]
    y_ref[:, D_SSM + D_RET:] = yc.astype(BF16)
    pbuf_ref[0] = pool_scr[tl + 1:tl + 16, :]
    pool_scr[0:16, :] = pool_scr[tl:tl + 16, :]


def _mix_prompt(proj, n_seq, seq_len, consts, tl):
    cos_t, sin_t = consts[:2]
    whole = consts[2:]
    nt = seq_len // tl
    row_map = lambda b, t: (b * nt + t, 0)

    def const_spec(a):
        nd = a.ndim
        return pl.BlockSpec(a.shape, lambda b, t: (0,) * nd)

    in_specs = [
        pl.BlockSpec((tl, D_IN), row_map),
        pl.BlockSpec((tl, RET_HD), lambda b, t: (t, 0)),
        pl.BlockSpec((tl, RET_HD), lambda b, t: (t, 0)),
    ] + [const_spec(a) for a in whole]
    out_shape = (
        jax.ShapeDtypeStruct((n_seq * seq_len, D_MODEL), BF16),
        jax.ShapeDtypeStruct((n_seq, 1, N_STATE), F32),
        jax.ShapeDtypeStruct((n_seq, 1, N_STATE), F32),
        jax.ShapeDtypeStruct((n_seq, RET_HEADS, RET_HD, RET_HD), F32),
        jax.ShapeDtypeStruct((n_seq, POOL_BUF, D_POOL), F32),
    )
    out_specs = (
        pl.BlockSpec((tl, D_MODEL), row_map),
        pl.BlockSpec((1, 1, N_STATE), lambda b, t: (b, 0, 0)),
        pl.BlockSpec((1, 1, N_STATE), lambda b, t: (b, 0, 0)),
        pl.BlockSpec((1, RET_HEADS, RET_HD, RET_HD), lambda b, t: (b, 0, 0, 0)),
        pl.BlockSpec((1, POOL_BUF, D_POOL), lambda b, t: (b, 0, 0)),
    )
    return pl.pallas_call(
        functools.partial(_mix_prompt_kernel, tl=tl),
        grid=(n_seq, nt),
        in_specs=in_specs,
        out_specs=out_specs,
        out_shape=out_shape,
        scratch_shapes=[
            pltpu.VMEM((tl, 2 * N_STATE), F32),
            pltpu.VMEM((SUBLANES, 2 * N_STATE), F32),
            pltpu.VMEM((16 + tl, D_POOL), F32),
        ],
        compiler_params=_params("arbitrary", "arbitrary"),
        name="mix_prompt",
    )(proj, cos_t, sin_t, *whole)


def _mix_sample_kernel(proj_ref, cos_ref, sin_ref, pw_ref, bmat_ref, cmat_ref, dskip_ref, wglu_ref,
                       bglu_ref, mask_ref, qdec_ref, kdec_ref, cdec_ref, wpool_ref, pscale_ref,
                       h0re_ref, h0im_ref, s0_ref, pool0_ref,
                       y_ref, hre_ref, him_ref, sret_ref, pbuf_ref,
                       dr_scr, pool_scr, *, bs, seq_len, pos0):
    n_rows = bs * seq_len

    u = proj_ref[:, 0:D_SSM]
    _ssm_drive(u.astype(BF16), bmat_ref, dr_scr, SUBLANES, n_rows)

    def carry_fn(b, lo):
        return (h0re_ref[pl.ds(b, 1), lo:lo + SCAN_LANES], h0im_ref[pl.ds(b, 1), lo:lo + SCAN_LANES])

    def block_end_fn(b, lo, h_re, h_im):
        hre_ref[pl.ds(b, 1), lo:lo + SCAN_LANES] = h_re
        him_ref[pl.ds(b, 1), lo:lo + SCAN_LANES] = h_im

    _ssm_scan(dr_scr, pw_ref, bs, carry_fn, block_end_fn)
    y = _ssm_readout(dr_scr, cmat_ref, SUBLANES, n_rows) + dskip_ref[...] * u
    y_ref[:, 0:D_SSM] = _ssm_gate(y, wglu_ref, bglu_ref).astype(BF16)

    scale = RET_HD ** -0.5
    cosv = cos_ref[...]
    sinv = sin_ref[...]
    own = (lax.broadcasted_iota(jnp.int32, (n_rows, bs * RET_HD), 0) // seq_len
           == lax.broadcasted_iota(jnp.int32, (n_rows, bs * RET_HD), 1) // RET_HD)
    for h in range(RET_HEADS):
        hs = slice(h * RET_HD, (h + 1) * RET_HD)
        q = proj_ref[:, Q_OFF + h * RET_HD:Q_OFF + (h + 1) * RET_HD]
        k = proj_ref[:, K_OFF + h * RET_HD:K_OFF + (h + 1) * RET_HD]
        v = proj_ref[:, V_OFF + h * RET_HD:V_OFF + (h + 1) * RET_HD].astype(BF16)
        g = proj_ref[:, G_OFF + h * RET_HD:G_OFF + (h + 1) * RET_HD]
        qr = _rotary(q, cosv, sinv)
        kr = _rotary(k, cosv, sinv) * scale
        sc = lax.dot_general(qr.astype(BF16), kr.astype(BF16), (((1,), (1,)), ((), ())),
                             preferred_element_type=F32) * mask_ref[h]
        qd = jnp.where(own, jnp.concatenate([qr * qdec_ref[:, hs]] * bs, axis=1), 0.0).astype(BF16)
        kd = jnp.where(own, jnp.concatenate([kr * kdec_ref[:, hs]] * bs, axis=1), 0.0).astype(BF16)
        s_old = s0_ref[:, h].reshape(bs * RET_HD, RET_HD)
        o = (jnp.dot(sc.astype(BF16), v, preferred_element_type=F32)
             + jnp.dot(qd, s_old.astype(BF16), preferred_element_type=F32))
        s_new = (s_old * cdec_ref[h:h + 1, :]
                 + lax.dot_general(kd, v, (((0,), (0,)), ((), ())), preferred_element_type=F32))
        sret_ref[:, h] = s_new.reshape(bs, RET_HD, RET_HD)
        y_ref[:, D_SSM + h * RET_HD:D_SSM + (h + 1) * RET_HD] = _head_norm_gate(o, g).astype(BF16)

    uc = proj_ref[:, C_OFF:C_OFF + D_POOL].reshape(bs, seq_len, D_POOL)
    pool_scr[:, 1:16, :] = pool0_ref[...]
    pool_scr[:, 16:16 + seq_len, :] = uc
    pos = pos0 + lax.broadcasted_iota(jnp.int32, (1, seq_len, POOL_GD), 1)
    counts = _pool_counts(pos)
    parts = []
    for gi, w in enumerate(POOL_WINDOWS):
        lanes = slice(gi * POOL_GD, (gi + 1) * POOL_GD)
        acc = pool_scr[:, 16:16 + seq_len, lanes]
        for j in range(1, w):
            acc = acc + pool_scr[:, 16 - j:16 - j + seq_len, lanes]
        parts.append(acc / counts[gi] - uc[:, :, lanes])
    pooled = jnp.concatenate(parts, axis=-1).reshape(n_rows, D_POOL).astype(BF16)
    yc = jnp.dot(pooled, wpool_ref[...], preferred_element_type=F32) * pscale_ref[...]
    y_ref[:, D_SSM + D_RET:] = yc.astype(BF16)
    pbuf_ref[...] = pool_scr[:, seq_len + 1:seq_len + 16, :]


def _mix_sample(proj, proj_row0, states, layer, prev_out, n_seq, seq_len, consts, bs):
    n_rows = bs * seq_len

    def const_spec(a):
        nd = a.ndim
        return pl.BlockSpec(a.shape, lambda i: (0,) * nd)

    state_specs = [
        pl.BlockSpec((None, bs, N_STATE), lambda i: (layer, i, 0)),
        pl.BlockSpec((None, bs, N_STATE), lambda i: (layer, i, 0)),
        pl.BlockSpec((None, bs, RET_HEADS, RET_HD, RET_HD), lambda i: (layer, i, 0, 0, 0)),
        pl.BlockSpec((None, bs, POOL_BUF, D_POOL), lambda i: (layer, i, 0, 0)),
    ]
    assert proj_row0 % n_rows == 0
    in_specs = ([pl.BlockSpec((n_rows, D_IN), lambda i: (proj_row0 // n_rows + i, 0))]
                + [const_spec(a) for a in consts] + state_specs)
    args = [proj, *consts, *states]
    n_in = len(args)
    aliases = {}
    if prev_out is not None:
        in_specs += [pl.BlockSpec(memory_space=pl.ANY)] * len(prev_out)
        aliases = {n_in + k: 1 + k for k in range(len(prev_out))}
        args += list(prev_out)
    n_args = len(args)
    out_shape = (
        jax.ShapeDtypeStruct((n_seq * seq_len, D_MODEL), BF16),
        jax.ShapeDtypeStruct((DEPTH, n_seq, N_STATE), F32),
        jax.ShapeDtypeStruct((DEPTH, n_seq, N_STATE), F32),
        jax.ShapeDtypeStruct((DEPTH, n_seq, RET_HEADS, RET_HD, RET_HD), F32),
        jax.ShapeDtypeStruct((DEPTH, n_seq, POOL_BUF, D_POOL), F32),
    )
    out_specs = (pl.BlockSpec((n_rows, D_MODEL), lambda i: (i, 0)),) + tuple(state_specs)

    def body(*refs):
        _mix_sample_kernel(*refs[:n_in], *refs[n_args:], bs=bs, seq_len=seq_len, pos0=PAST_LEN)

    return pl.pallas_call(
        body,
        grid=(n_seq // bs,),
        in_specs=in_specs,
        out_specs=out_specs,
        out_shape=out_shape,
        scratch_shapes=[
            pltpu.VMEM((SUBLANES + n_rows, 2 * N_STATE), F32),
            pltpu.VMEM((bs, 16 + seq_len, D_POOL), F32),
        ],
        input_output_aliases=aliases,
        compiler_params=_params("arbitrary"),
        name="mix_sample",
    )(*args)


def _out_proj_kernel(y_ref, w_ref, x_ref, g1_ref, sh_ref, sc_ref, g_ref, xo_ref, h_ref, w_scr, *, bb, tl):
    @pl.when(pl.program_id(0) == 0)
    def _():
        w_scr[...] = w_ref[...].astype(BF16)

    f = jnp.dot(y_ref[...], w_scr[...], preferred_element_type=F32)
    xn = x_ref[...].reshape(bb, tl, D_MODEL) + g1_ref[...] * f.reshape(bb, tl, D_MODEL)
    xo_ref[...] = xn.reshape(bb * tl, D_MODEL)
    h = _rms(xn) * g_ref[...] * (1.0 + sc_ref[...]) + sh_ref[...]
    h_ref[...] = h.reshape(bb * tl, D_MODEL).astype(BF16)


def _out_proj(y, w, layer, x, x_tile0, mod, g, rows, out_rows, prev):
    out_spec = pl.BlockSpec((rows.tm, D_MODEL), lambda i: (rows.first_tile + i, 0))
    return _call_into(
        functools.partial(_out_proj_kernel, bb=rows.bb, tl=rows.tl),
        None if prev is None else {0: prev[0], 1: prev[1]},
        grid=(rows.n_tiles,),
        in_specs=[
            pl.BlockSpec((rows.tm, D_MODEL), lambda i: (i, 0)),
            pl.BlockSpec((None, D_MODEL, D_MODEL), lambda i: (layer, 0, 0), pipeline_mode=pl.Buffered(1)),
            pl.BlockSpec((rows.tm, D_MODEL), lambda i: (x_tile0 + i, 0)),
            rows.mod_spec(layer, MOD_GATE1, lambda i: i),
            rows.mod_spec(layer, MOD_SHIFT2, lambda i: i),
            rows.mod_spec(layer, MOD_SCALE2, lambda i: i),
            pl.BlockSpec((None, 1, D_MODEL), lambda i: (layer, 0, 0)),
        ],
        args=[y, w, x, mod, mod, mod, g],
        out_specs=(out_spec, out_spec),
        out_shape=(jax.ShapeDtypeStruct((out_rows, D_MODEL), F32),
                   jax.ShapeDtypeStruct((out_rows, D_MODEL), BF16)),
        scratch_shapes=[pltpu.VMEM((D_MODEL, D_MODEL), BF16)],
        sem=("arbitrary",),
        name="out_proj",
    )


def _conv_gate(a, b, am1, am2, cw, cb):
    conv = cb + am2 * cw[0:1]
    conv = conv + am1 * cw[1:2]
    conv = conv + a * cw[2:3]
    return jax.nn.silu(conv) * b


def _ffn_up_kernel(h_ref, wa_ref, wb_ref, cw_ref, cb_ref, st_ref, act_ref, tail_ref, ns_ref, w_scr, carry_scr,
                   *, prompt, sample, n_chunks):
    c = pl.program_id(0)
    i = pl.program_id(1)

    @pl.when(i == 0)
    def _():
        w_scr[:, :FF_TILE] = wa_ref[...].astype(BF16)

    @pl.when((i == 0) & (c < n_chunks - 1))
    def _():
        w_scr[:, FF_TILE:] = wb_ref[0].astype(BF16)

    @pl.when((i == 0) & (c == n_chunks - 1))
    def _():
        w_scr[:, FF_TILE:FF_TILE + FF_LAST] = wb_ref[0, :, FF_TILE - FF_LAST:].astype(BF16)

    cw = cw_ref[...]
    cb = cb_ref[...]
    n_sub = prompt.tm // FF_SUB_ROWS

    @pl.when(i >= prompt.n_tiles)
    def _():
        tl = sample.tl
        sb = FF_SUB_ROWS // tl
        tok = lax.broadcasted_iota(jnp.int32, (sb, tl, FF_TILE), 1)
        for r in range(n_sub):
            rows = slice(r * FF_SUB_ROWS, (r + 1) * FF_SUB_ROWS)
            sq = slice(r * sb, (r + 1) * sb)
            ab = jnp.dot(h_ref[rows, :], w_scr[...], preferred_element_type=F32)
            a = ab[:, :FF_TILE].reshape(sb, tl, FF_TILE)
            b = ab[:, FF_TILE:].reshape(sb, tl, FF_TILE)
            p0 = st_ref[sq, 0:1, :]
            p1 = st_ref[sq, 1:2, :]
            am1 = jnp.where(tok == 0, p1, pltpu.roll(a, 1, 1))
            am2 = jnp.where(tok == 0, p0, jnp.where(tok == 1, p1, pltpu.roll(a, 2, 1)))
            act = _conv_gate(a, b, am1, am2, cw, cb)
            act_ref[rows, :] = act.reshape(FF_SUB_ROWS, FF_TILE).astype(BF16)
            ns_ref[sq] = a[:, tl - 2:tl, :]

    @pl.when(i < prompt.n_tiles)
    def _():
        @pl.when(i % prompt.tiles_per_seq == 0)
        def _():
            carry_scr[...] = jnp.zeros((SUBLANES, FF_TILE), F32)

        tok = lax.broadcasted_iota(jnp.int32, (SUBLANES, FF_TILE), 0)
        for r in range(n_sub):
            rows = slice(r * FF_SUB_ROWS, (r + 1) * FF_SUB_ROWS)
            ab = jnp.dot(h_ref[rows, :], w_scr[...], preferred_element_type=F32)
            a = ab[:, :FF_TILE]
            b = ab[:, FF_TILE:]
            p0 = carry_scr[SUBLANES - 2:SUBLANES - 1, :]
            p1 = carry_scr[SUBLANES - 1:SUBLANES, :]
            r1 = pltpu.roll(a, 1, 0)
            r2 = pltpu.roll(a, 2, 0)
            head1 = jnp.where(tok == 0, p1, r1[:SUBLANES])
            head2 = jnp.where(tok == 0, p0, jnp.where(tok == 1, p1, r2[:SUBLANES]))
            am1 = jnp.concatenate([head1, r1[SUBLANES:]], axis=0)
            am2 = jnp.concatenate([head2, r2[SUBLANES:]], axis=0)
            act_ref[rows, :] = _conv_gate(a, b, am1, am2, cw, cb).astype(BF16)
            carry_scr[...] = a[FF_SUB_ROWS - SUBLANES:, :]
        tail_ref[0] = carry_scr[SUBLANES - 2:SUBLANES, :]


def _ffn_up(h, w_up, conv_w, conv_b, layer, state, prev_state, prompt, sample):
    m = h.shape[0]
    assert prompt.tm == sample.tm and sample.n_tiles == 1 and sample.first_tile == prompt.n_tiles
    tm = prompt.tm
    n_tiles = prompt.n_tiles + sample.n_tiles
    n_seq = sample.bb
    n_chunks = pl.cdiv(D_FF, FF_TILE)
    in_specs = [
        pl.BlockSpec((tm, D_MODEL), lambda c, i: (i, 0)),
        pl.BlockSpec((None, D_MODEL, FF_TILE), lambda c, i: (layer, 0, c)),
        pl.BlockSpec((pl.Element(1), pl.Element(D_MODEL), pl.Element(FF_TILE)),
                     lambda c, i: (layer, 0, LANES * jnp.minimum((D_FF + c * FF_TILE) // LANES,
                                                                 (2 * D_FF - FF_TILE) // LANES))),
        pl.BlockSpec((None, CONV_W, FF_TILE), lambda c, i: (layer, 0, c)),
        pl.BlockSpec((None, 1, FF_TILE), lambda c, i: (layer, 0, c)),
        pl.BlockSpec((None, n_seq, CONV_W - 1, FF_TILE), lambda c, i: (layer, 0, 0, c)),
    ]
    return _call_into(
        functools.partial(_ffn_up_kernel, prompt=prompt, sample=sample, n_chunks=n_chunks),
        None if prev_state is None else {2: prev_state},
        grid=(n_chunks, n_tiles),
        in_specs=in_specs,
        args=[h, w_up, w_up, conv_w, conv_b, state],
        out_specs=(
            pl.BlockSpec((tm, FF_TILE), lambda c, i: (i, c)),
            pl.BlockSpec((1, CONV_W - 1, FF_TILE), lambda c, i: (jnp.minimum(i, prompt.n_tiles - 1), 0, c)),
            pl.BlockSpec((None, n_seq, CONV_W - 1, FF_TILE), lambda c, i: (layer, 0, 0, c)),
        ),
        out_shape=(jax.ShapeDtypeStruct((m, D_FF), BF16),
                   jax.ShapeDtypeStruct((prompt.n_tiles, CONV_W - 1, D_FF), F32),
                   jax.ShapeDtypeStruct((DEPTH, n_seq, CONV_W - 1, D_FF), F32)),
        scratch_shapes=[pltpu.VMEM((D_MODEL, 2 * FF_TILE), BF16), pltpu.VMEM((SUBLANES, FF_TILE), F32)],
        sem=("arbitrary", "arbitrary"),
        name="ffn_up",
    )


def _cmul(ar, ai, br, bi):
    return ar * br - ai * bi, ar * bi + ai * br


def _ssm_consts(a_re, a_im, log_dt, b_re, b_im, c_re, c_im, scan_steps):
    lam_re, lam_im = a_re.astype(F32), a_im.astype(F32)
    dt = jnp.exp(log_dt.astype(F32))[:, None]
    mag = jnp.exp(lam_re * dt)
    ab_re, ab_im = mag * jnp.cos(lam_im * dt), mag * jnp.sin(lam_im * dt)
    den = lam_re * lam_re + lam_im * lam_im
    f_re = ((ab_re - 1.0) * lam_re + ab_im * lam_im) / den
    f_im = (ab_im * lam_re - (ab_re - 1.0) * lam_im) / den
    br, bi = b_re.astype(F32), b_im.astype(F32)
    bb_re = f_re[..., None] * br - f_im[..., None] * bi
    bb_im = f_re[..., None] * bi + f_im[..., None] * br

    gh = SSM_GROUPS // 2
    eye = jnp.eye(gh, dtype=F32)

    def drive_half(m):
        return jnp.einsum("gph,gk->ghkp", m, eye).reshape(gh * SSM_GROUP, gh * SSM_STATE)

    def read_half(m):
        return jnp.einsum("ghp,gk->gpkh", m, eye).reshape(gh * SSM_STATE, gh * SSM_GROUP)

    bmat = jnp.stack([
        jnp.concatenate([drive_half(bb_re[s]), drive_half(bb_im[s])], axis=1)
        for s in (slice(0, gh), slice(gh, 2 * gh))]).astype(BF16)
    cre, cim = c_re.astype(F32), c_im.astype(F32)
    cmat = jnp.stack([
        jnp.concatenate([read_half(cre[s]), -read_half(cim[s])], axis=0)
        for s in (slice(0, gh), slice(gh, 2 * gh))]).astype(BF16)

    step_re, step_im = _power_table(ab_re.reshape(1, N_STATE), ab_im.reshape(1, N_STATE), scan_steps)
    sub_re, sub_im = _power_table(step_re[-1:], step_im[-1:], SUBLANES)
    abar = jnp.stack([step_re[:1], step_im[:1]])
    pw_block = _scan_planes(step_re[:SUBLANES], step_im[:SUBLANES])
    pw_sub = _scan_planes(sub_re, sub_im)
    return bmat, cmat, abar, pw_block, pw_sub


def _power_table(re, im, n):
    while re.shape[0] < n:
        top_re, top_im = re[-1:], im[-1:]
        more_re, more_im = _cmul(re, im, top_re, top_im)
        re, im = jnp.concatenate([re, more_re]), jnp.concatenate([im, more_im])
    return re[:n], im[:n]


def _scan_planes(re, im):
    row = jnp.arange(SUBLANES)[:, None]
    planes_re = [jnp.where(row >= s, re[s - 1][None, :], 0.0) for s in (1, 2, 4)] + [re]
    planes_im = [jnp.where(row >= s, im[s - 1][None, :], 0.0) for s in (1, 2, 4)] + [im]
    return jnp.stack([jnp.stack(planes_re), jnp.stack(planes_im)])


def _time_on_sublanes(n_rows):
    steps = n_rows // SUBLANES
    p = np.zeros((n_rows, n_rows), np.float32)
    r = np.arange(n_rows)
    p[r, (r % SUBLANES) * steps + r // SUBLANES] = 1.0
    return jnp.asarray(p, BF16), jnp.asarray(p.T, BF16)


def _rotary_tables(pos):
    half = RET_HD // 2
    inv = ROPE_BASE ** (-np.arange(half, dtype=np.float64) / half)
    ang = np.asarray(pos, np.float64)[:, None] * inv[None, :]
    cos, sin = np.cos(ang), np.sin(ang)
    return (np.concatenate([cos, cos], axis=-1).astype(np.float32),
            np.concatenate([-sin, sin], axis=-1).astype(np.float32))


def _retention_consts(chunk, n_seq):
    lg = np.log1p(-np.exp2(-5.0 - np.arange(RET_HEADS, dtype=np.float64)))
    r = np.arange(chunk * n_seq)
    ti = (r % chunk).astype(np.float64)
    seq = r // chunk
    diff = ti[:, None] - ti[None, :]
    keep = (seq[:, None] == seq[None, :]) & (diff >= 0.0)
    mask = np.where(keep[None], np.exp(np.maximum(diff, 0.0)[None] * lg[:, None, None]), 0.0)
    q_dec = np.exp((ti + 1.0)[:, None] * lg[None, :])
    k_dec = np.exp((chunk - 1.0 - ti)[:, None] * lg[None, :])
    c_dec = np.exp(chunk * lg)
    expand = lambda d: np.repeat(d, RET_HD, axis=1).astype(np.float32)
    return (mask.astype(np.float32), expand(q_dec), expand(k_dec),
            np.broadcast_to(c_dec[:, None], (RET_HEADS, RET_HD)).astype(np.float32))


def _pool_weight(w_pool):
    eye = jnp.eye(len(POOL_WINDOWS), dtype=F32)
    return jnp.einsum("gcd,gk->gckd", w_pool, eye).reshape(D_POOL, D_POOL).astype(BF16)


def kernel(x_prompt, x_sample, state_ssm_re, state_ssm_im, state_ret, state_pool, state_ffn_conv, c_prompt, c_sample, w_ada, b_ada, norm1_g, w_in, ssm_a_re, ssm_a_im, ssm_log_dt, ssm_b_re, ssm_b_im, ssm_c_re, ssm_c_im, ssm_d, ssm_w_glu, ssm_b_glu, pool_w, pool_scale, w_out, norm2_g, ffn_w_up, ffn_conv_w, ffn_conv_b, ffn_w_down, final_norm_g):
    bp, lp, _ = x_prompt.shape
    bs, ls, _ = x_sample.shape
    tl_mix = 512
    seq_mix = 8

    n_all = bp + bs
    n_pad = -(-n_all // SUBLANES) * SUBLANES
    c_all = jnp.concatenate([c_sample, c_prompt, jnp.zeros((n_pad - n_all, D_MODEL), F32)], axis=0)
    mod = _ada_mod(c_all, w_ada, b_ada)

    cos_p, sin_p = _rotary_tables(np.arange(lp))
    cos_s, sin_s = _rotary_tables(PAST_LEN + np.arange(ls))
    cos_s, sin_s = np.tile(cos_s, (seq_mix, 1)), np.tile(sin_s, (seq_mix, 1))
    assert math.gcd(lp, RET_CHUNK) == RET_CHUNK and math.gcd(ls, RET_CHUNK) == ls == SUBLANES
    ret_p = _retention_consts(RET_CHUNK, 1)
    ret_s = _retention_consts(ls, seq_mix)
    perm = _time_on_sublanes(tl_mix)

    rows_total = bp * lp + bs * ls
    groups = (_Rows(bp, lp, ROW_TILE, bs, 0), _Rows(bs, ls, ROW_TILE, 0, bp * lp))
    groups_full = (_Rows(bp, lp, ROW_TILE_FULL, bs, 0), _Rows(bs, ls, ROW_TILE_FULL, 0, bp * lp))
    big_p, big_s = groups

    norm1 = norm1_g.reshape(DEPTH, 1, D_MODEL)
    norm2 = norm2_g.reshape(DEPTH, 1, D_MODEL)
    final_g = final_norm_g.reshape(1, 1, D_MODEL)
    conv_b = ffn_conv_b.reshape(DEPTH, 1, D_FF)
    sample_states = (state_ssm_re.reshape(DEPTH, bs, N_STATE), state_ssm_im.reshape(DEPTH, bs, N_STATE),
                     state_ret, state_pool)

    x_first = (x_prompt.reshape(bp * lp, D_MODEL), x_sample.reshape(bs * ls, D_MODEL))
    x_all = None

    def residual_rows(k, rows):
        return (x_first[k], 0) if x_all is None else (x_all, rows.first_tile)

    new_p = ([], [], [], [], [])
    s_mix = None
    s_conv = None
    for l in range(DEPTH):
        bmat, cmat, abar, pw_block, pw_sub = _ssm_consts(
            ssm_a_re[l], ssm_a_im[l], ssm_log_dt[l], ssm_b_re[l], ssm_b_im[l], ssm_c_re[l], ssm_c_im[l],
            tl_mix // SUBLANES)
        shared = (bmat, cmat, ssm_d[l].reshape(1, D_SSM), ssm_w_glu[l].astype(BF16),
                  ssm_b_glu[l].reshape(1, D_SSM))
        pool_c = (_pool_weight(pool_w[l]), pool_scale[l].reshape(1, D_POOL))
        pre = (mod, MOD_SHIFT1, MOD_SCALE1)

        h = None
        for k, rows in enumerate(groups):
            h = _norm(*residual_rows(k, rows), norm1, l, pre, rows, BF16, rows_total, h)
        proj = _matmul(h, w_in, l, groups, IN_PROJ_COLS, "in_proj")
        y_p, st_re, st_im, st_ret, st_pool = _mix_prompt(
            proj, bp, lp, (cos_p, sin_p) + perm + (abar, pw_sub) + shared + ret_p + pool_c, tl_mix)
        y_s, *s_mix = _mix_sample(proj, bp * lp, sample_states, l, s_mix, bs, ls,
                                  (cos_s, sin_s, pw_block) + shared + ret_s + pool_c, seq_mix)
        xh = None
        for k, (rows, y) in enumerate(zip(groups_full, (y_p, y_s))):
            xh = _out_proj(y, w_out, l, *residual_rows(k, rows), mod, norm2, rows, rows_total, xh)
        xn, h2 = xh
        act, tails, s_conv = _ffn_up(h2, ffn_w_up, ffn_conv_w, conv_b, l, state_ffn_conv, s_conv, big_p, big_s)
        x_all = _matmul(act, ffn_w_down, l, groups, FF_DOWN_COLS, "ffn_down", residual=(xn, mod, MOD_GATE2),
                        single_buffer_w=True)
        st_conv = tails[big_p.tiles_per_seq - 1::big_p.tiles_per_seq]
        for lst, st in zip(new_p, (st_re, st_im, st_ret, st_pool, st_conv)):
            lst.append(st)

    yp = _norm(x_all, big_p.first_tile, final_g, 0, None, big_p, F32)
    ys = _norm(x_all, big_s.first_tile, final_g, 0, None, big_s, F32)

    p_re, p_im, p_ret, p_pool, p_conv = [jnp.stack(st) for st in new_p]
    s_re, s_im, s_ret, s_pool = s_mix
    shape_p = (DEPTH, bp, SSM_GROUPS, SSM_STATE)
    shape_s = (DEPTH, bs, SSM_GROUPS, SSM_STATE)
    return (yp.reshape(bp, lp, D_MODEL), ys.reshape(bs, ls, D_MODEL),
            p_re.reshape(shape_p), p_im.reshape(shape_p), p_ret, p_pool, p_conv,
            s_re.reshape(shape_s), s_im.reshape(shape_s), s_ret, s_pool, s_conv)
```

```python
import functools
import math

import jax
import jax.numpy as jnp
import numpy as np
from jax import lax
from jax.experimental import pallas as pl
from jax.experimental.pallas import tpu as pltpu

F32 = jnp.float32
BF16 = jnp.bfloat16

D_MODEL = 2048
DEPTH = 2
PAST_LEN = 16384
D_SSM = 512
D_RET = 1024
D_POOL = 512
SSM_GROUP = 16
SSM_GROUPS = 32
SSM_STATE = 64
N_STATE = SSM_GROUPS * SSM_STATE
RET_HEADS = 8
RET_HD = 128
RET_CHUNK = 128
POOL_WINDOWS = (2, 4, 8, 16)
POOL_GD = 128
POOL_BUF = 15
D_FF = 5504
FF_TILE = 512
FF_LAST = D_FF % FF_TILE
CONV_W = 3
D_IN = D_SSM + 4 * D_RET + D_POOL
ROPE_BASE = 10000.0
EPS = 1e-6

SUBLANES = 8
LANES = 128
ROW_TILE_WIDE = 2048
ROW_TILE = 1024
ROW_TILE_FULL = 512
FF_SUB_ROWS = 512
IN_PROJ_COLS = 1024
FF_DOWN_COLS = 512
ADA_COLS = 1024
MOD_SHIFT1, MOD_SCALE1, MOD_GATE1, MOD_SHIFT2, MOD_SCALE2, MOD_GATE2 = range(6)
SCAN_LANES = 512
VMEM_LIMIT = 56 * 1024 * 1024

Q_OFF = D_SSM
K_OFF = D_SSM + D_RET
V_OFF = D_SSM + 2 * D_RET
G_OFF = D_SSM + 3 * D_RET
C_OFF = D_SSM + 4 * D_RET


def _params(*sem):
    return pltpu.CompilerParams(dimension_semantics=sem, vmem_limit_bytes=VMEM_LIMIT)


def _rms(x):
    return x * lax.rsqrt(jnp.mean(x * x, axis=-1, keepdims=True) + EPS)


def _ada_kernel(c_ref, w_ref, b_ref, o_ref):
    a = jax.nn.silu(c_ref[...]).astype(BF16)
    res = jnp.dot(a, w_ref[...].astype(BF16), preferred_element_type=F32) + b_ref[...]
    for r in range(res.shape[0]):
        o_ref[r] = res[r:r + 1, :]


def _ada_mod(c_all, w_ada, b_ada):
    nseq = c_all.shape[0]
    per_vec = D_MODEL // ADA_COLS
    return pl.pallas_call(
        _ada_kernel,
        grid=(DEPTH, 6 * per_vec),
        in_specs=[
            pl.BlockSpec((nseq, D_MODEL), lambda l, j: (0, 0)),
            pl.BlockSpec((None, D_MODEL, ADA_COLS), lambda l, j: (l, 0, j)),
            pl.BlockSpec((None, 1, ADA_COLS), lambda l, j: (l, 0, j)),
        ],
        out_specs=pl.BlockSpec((None, None, nseq, 1, ADA_COLS), lambda l, j: (l, j // per_vec, 0, 0, j % per_vec)),
        out_shape=jax.ShapeDtypeStruct((DEPTH, 6, nseq, 1, D_MODEL), F32),
        compiler_params=_params("arbitrary", "arbitrary"),
        name="ada_mod",
    )(c_all, w_ada, b_ada.reshape(DEPTH, 1, 6 * D_MODEL))


class _Rows:
    def __init__(self, n_seq, seq_len, tm, first_seq, first_row):
        if seq_len >= tm:
            assert seq_len % tm == 0
            self.bb, self.tl = 1, tm
            self.tiles_per_seq = seq_len // tm
        else:
            assert tm % seq_len == 0 and seq_len == SUBLANES
            self.bb, self.tl = tm // seq_len, seq_len
            self.tiles_per_seq = 1
        assert first_seq % self.bb == 0 and first_row % tm == 0
        self.tm = tm
        self.n_tiles = n_seq * seq_len // tm
        self.first_block = first_seq // self.bb
        self.first_tile = first_row // tm

    def mod_block(self, i):
        return self.first_block + i // self.tiles_per_seq

    def mod_spec(self, layer, vec, row_tile_of, cols=D_MODEL, col_block_of=lambda *ids: 0):
        return pl.BlockSpec(
            (None, None, self.bb, 1, cols),
            lambda *ids: (layer, vec, self.mod_block(row_tile_of(*ids)), 0, col_block_of(*ids)))


def _call_into(kernel_fn, prev, *, grid, in_specs, args, out_specs, out_shape, scratch_shapes=(), sem, name):
    n_in = len(args)
    in_specs, args = list(in_specs), list(args)
    aliases = {}
    for k, arr in sorted((prev or {}).items()):
        aliases[len(args)] = k
        in_specs.append(pl.BlockSpec(memory_space=pl.ANY))
        args.append(arr)
    n_args = len(args)

    def body(*refs):
        kernel_fn(*refs[:n_in], *refs[n_args:])

    return pl.pallas_call(
        body, grid=grid, in_specs=in_specs, out_specs=out_specs, out_shape=out_shape,
        scratch_shapes=list(scratch_shapes), input_output_aliases=aliases,
        compiler_params=_params(*sem), name=name,
    )(*args)


def _norm_kernel(*refs, bb, tl, modulated):
    if modulated:
        x_ref, g_ref, sh_ref, sc_ref, o_ref = refs
    else:
        x_ref, g_ref, o_ref = refs
    h = _rms(x_ref[...].reshape(bb, tl, D_MODEL)) * g_ref[...]
    if modulated:
        h = h * (1.0 + sc_ref[...]) + sh_ref[...]
    o_ref[...] = h.reshape(bb * tl, D_MODEL).astype(o_ref.dtype)


def _norm(x, x_tile0, g, layer, mods, rows, out_dtype, out_rows=None, prev=None):
    in_specs = [pl.BlockSpec((rows.tm, D_MODEL), lambda i: (x_tile0 + i, 0)),
                pl.BlockSpec((None, 1, D_MODEL), lambda i: (layer, 0, 0))]
    args = [x, g]
    if mods is not None:
        mod, shift_vec, scale_vec = mods
        in_specs += [rows.mod_spec(layer, shift_vec, lambda i: i), rows.mod_spec(layer, scale_vec, lambda i: i)]
        args += [mod, mod]
    out_tile0 = 0 if out_rows is None else rows.first_tile
    return _call_into(
        functools.partial(_norm_kernel, bb=rows.bb, tl=rows.tl, modulated=mods is not None),
        None if prev is None else {0: prev},
        grid=(rows.n_tiles,),
        in_specs=in_specs,
        args=args,
        out_specs=pl.BlockSpec((rows.tm, D_MODEL), lambda i: (out_tile0 + i, 0)),
        out_shape=jax.ShapeDtypeStruct((out_rows or rows.n_tiles * rows.tm, D_MODEL), out_dtype),
        sem=("arbitrary",),
        name="norm",
    )


def _matmul_kernel(*refs, groups, residual):
    if residual:
        a_ref, w_ref, x_ref, *gate_refs, o_ref, w_scr = refs
    else:
        a_ref, w_ref, o_ref, w_scr = refs
    i = pl.program_id(1)

    @pl.when(i == 0)
    def _():
        w_scr[...] = w_ref[...].astype(BF16)

    f = jnp.dot(a_ref[...], w_scr[...], preferred_element_type=F32)
    if not residual:
        o_ref[...] = f
        return
    tn = f.shape[1]
    for rows, g_ref in zip(groups, gate_refs):
        @pl.when((i >= rows.first_tile) & (i < rows.first_tile + rows.n_tiles))
        def _(rows=rows, g_ref=g_ref):
            xn = x_ref[...].reshape(rows.bb, rows.tl, tn) + g_ref[...] * f.reshape(rows.bb, rows.tl, tn)
            o_ref[...] = xn.reshape(rows.tm, tn)


def _matmul(a, w, layer, tm, tn, name, groups=(), residual=None, single_buffer_w=False):
    m, k = a.shape
    n = w.shape[2]
    n_tiles = pl.cdiv(m, tm)
    if residual is not None:
        assert m == sum(rows.n_tiles for rows in groups) * tm and all(rows.tm == tm for rows in groups)
    w_mode = dict(pipeline_mode=pl.Buffered(1)) if single_buffer_w else {}
    in_specs = [
        pl.BlockSpec((tm, k), lambda j, i: (i, 0)),
        pl.BlockSpec((None, k, tn), lambda j, i: (layer, 0, j), **w_mode),
    ]
    args = [a, w]
    if residual is not None:
        x, mod, gate_vec = residual
        in_specs.append(pl.BlockSpec((tm, tn), lambda j, i: (i, j)))
        args.append(x)
        for rows in groups:
            in_specs.append(rows.mod_spec(
                layer, gate_vec,
                lambda j, i, rows=rows: jnp.clip(i - rows.first_tile, 0, rows.n_tiles - 1),
                tn, lambda j, i: j))
            args.append(mod)
    return pl.pallas_call(
        functools.partial(_matmul_kernel, groups=groups, residual=residual is not None),
        grid=(n // tn, n_tiles),
        in_specs=in_specs,
        out_specs=pl.BlockSpec((tm, tn), lambda j, i: (i, j)),
        out_shape=jax.ShapeDtypeStruct((m, n), F32),
        scratch_shapes=[pltpu.VMEM((k, tn), BF16)],
        compiler_params=_params("arbitrary", "arbitrary"),
        name=name,
    )(*args)


def _ssm_drive(ub, bmat_ref, dr_scr, row0, n_rows):
    half_u = D_SSM // 2
    half_n = N_STATE // 2
    for hf in range(2):
        d = jnp.dot(ub[:, hf * half_u:(hf + 1) * half_u], bmat_ref[hf], preferred_element_type=F32)
        dr_scr[row0:row0 + n_rows, hf * half_n:(hf + 1) * half_n] = d[:, :half_n]
        dr_scr[row0:row0 + n_rows, N_STATE + hf * half_n:N_STATE + (hf + 1) * half_n] = d[:, half_n:]


def _cmul_add(a_re, a_im, x_re, x_im, y_re, y_im):
    return y_re + (a_re * x_re - a_im * x_im), y_im + (a_re * x_im + a_im * x_re)


def _scan_tile(dr_scr, abar_ref, pws_ref, carry_scr, n_steps):
    row = lax.broadcasted_iota(jnp.int32, (SUBLANES, SCAN_LANES), 0)
    for c in range(N_STATE // SCAN_LANES):
        lo = c * SCAN_LANES
        re_cols = slice(lo, lo + SCAN_LANES)
        im_cols = slice(N_STATE + lo, N_STATE + lo + SCAN_LANES)
        a_re = jnp.broadcast_to(abar_ref[0, :, re_cols], (SUBLANES, SCAN_LANES))
        a_im = jnp.broadcast_to(abar_ref[1, :, re_cols], (SUBLANES, SCAN_LANES))

        def local_step(t, h):
            r = pl.ds(pl.multiple_of(t * SUBLANES, SUBLANES), SUBLANES)
            h_re, h_im = _cmul_add(a_re, a_im, h[0], h[1], dr_scr[r, re_cols], dr_scr[r, im_cols])
            dr_scr[r, re_cols] = h_re
            dr_scr[r, im_cols] = h_im
            return h_re, h_im

        zero = jnp.zeros((SUBLANES, SCAN_LANES), F32)
        g_re, g_im = lax.fori_loop(0, n_steps, local_step, (zero, zero), unroll=2)
        for si, s in enumerate((1, 2, 4)):
            g_re, g_im = _cmul_add(pws_ref[0, si, :, re_cols], pws_ref[1, si, :, re_cols],
                                   pltpu.roll(g_re, s, 0), pltpu.roll(g_im, s, 0), g_re, g_im)
        c_re = carry_scr[SUBLANES - 1:SUBLANES, re_cols]
        c_im = carry_scr[SUBLANES - 1:SUBLANES, im_cols]
        e_re, e_im = _cmul_add(pws_ref[0, 3, :, re_cols], pws_ref[1, 3, :, re_cols], c_re, c_im, g_re, g_im)
        carry_scr[:, re_cols] = e_re
        carry_scr[:, im_cols] = e_im
        in_re = jnp.where(row == 0, c_re, pltpu.roll(e_re, 1, 0))
        in_im = jnp.where(row == 0, c_im, pltpu.roll(e_im, 1, 0))

        def fix_step(t, w):
            w_re, w_im = _cmul(a_re, a_im, w[0], w[1])
            r = pl.ds(pl.multiple_of(t * SUBLANES, SUBLANES), SUBLANES)
            dr_scr[r, re_cols] = dr_scr[r, re_cols] + w_re
            dr_scr[r, im_cols] = dr_scr[r, im_cols] + w_im
            return w_re, w_im

        lax.fori_loop(0, n_steps, fix_step, (in_re, in_im), unroll=2)


def _scan_block(dr_scr, pw_ref, b, carry_fn, block_end_fn=None):
    r = pl.multiple_of(b * SUBLANES + SUBLANES, SUBLANES)
    for c in range(N_STATE // SCAN_LANES):
        lo = c * SCAN_LANES
        re_cols = slice(lo, lo + SCAN_LANES)
        im_cols = slice(N_STATE + lo, N_STATE + lo + SCAN_LANES)
        d_re = dr_scr[pl.ds(r, SUBLANES), re_cols]
        d_im = dr_scr[pl.ds(r, SUBLANES), im_cols]
        for si, s in enumerate((1, 2, 4)):
            p_re = pw_ref[0, si, :, re_cols]
            p_im = pw_ref[1, si, :, re_cols]
            r_re = pltpu.roll(d_re, s, 0)
            r_im = pltpu.roll(d_im, s, 0)
            d_re, d_im = (d_re + (p_re * r_re - p_im * r_im),
                          d_im + (p_re * r_im + p_im * r_re))
        c_re, c_im = carry_fn(b, lo)
        a_re = pw_ref[0, 3, :, re_cols]
        a_im = pw_ref[1, 3, :, re_cols]
        h_re = d_re + (a_re * c_re - a_im * c_im)
        h_im = d_im + (a_re * c_im + a_im * c_re)
        dr_scr[pl.ds(r, SUBLANES), re_cols] = h_re
        dr_scr[pl.ds(r, SUBLANES), im_cols] = h_im
        if block_end_fn is not None:
            block_end_fn(b, lo, h_re[SUBLANES - 1:SUBLANES], h_im[SUBLANES - 1:SUBLANES])


def _ssm_scan(dr_scr, pw_ref, n_blocks, carry_fn, block_end_fn=None):
    def body(b, carry):
        _scan_block(dr_scr, pw_ref, b, carry_fn, block_end_fn)
        return carry

    lax.fori_loop(0, n_blocks, body, 0)


def _ssm_readout(dr_scr, cmat_ref, row0, n_rows):
    half_n = N_STATE // 2
    parts = []
    for hf in range(2):
        h_re = dr_scr[row0:row0 + n_rows, hf * half_n:(hf + 1) * half_n].astype(BF16)
        h_im = dr_scr[row0:row0 + n_rows, N_STATE + hf * half_n:N_STATE + (hf + 1) * half_n].astype(BF16)
        parts.append(jnp.dot(h_re, cmat_ref[hf, :half_n], preferred_element_type=F32)
                     + jnp.dot(h_im, cmat_ref[hf, half_n:], preferred_element_type=F32))
    return jnp.concatenate(parts, axis=-1)


def _ssm_gate(y, wglu_ref, bglu_ref):
    ya = jax.nn.gelu(y)
    gate = jnp.dot(ya.astype(BF16), wglu_ref[...], preferred_element_type=F32) + bglu_ref[...]
    return ya * jax.nn.sigmoid(gate)


def _unpermute_rows(perm_t_ref, x):
    hi = x.astype(BF16)
    rest = x - hi.astype(F32)
    mid = rest.astype(BF16)
    lo = (rest - mid.astype(F32)).astype(BF16)
    p = perm_t_ref[...]
    return (jnp.dot(p, hi, preferred_element_type=F32) + jnp.dot(p, mid, preferred_element_type=F32)
            + jnp.dot(p, lo, preferred_element_type=F32))


def _rotary(x, cosv, sinv):
    return x * cosv + pltpu.roll(x, RET_HD // 2, 1) * sinv


def _head_norm_gate(o, g):
    mu = jnp.mean(o, axis=-1, keepdims=True)
    var = jnp.mean(jnp.square(o - mu), axis=-1, keepdims=True)
    return (o - mu) * lax.rsqrt(var + EPS) * jax.nn.silu(g)


def _pool_counts(pos):
    return [jnp.minimum(pos + 1, w).astype(F32) for w in POOL_WINDOWS]


def _mix_prompt_kernel(proj_ref, cos_ref, sin_ref, perm_ref, permt_ref, abar_ref, pws_ref,
                       bmat_ref, cmat_ref, dskip_ref, wglu_ref, bglu_ref,
                       mask_ref, qdec_ref, kdec_ref, cdec_ref, wpool_ref, pscale_ref,
                       y_ref, hre_ref, him_ref, sret_ref, pbuf_ref,
                       dr_scr, carry_scr, pool_scr, *, tl):
    t = pl.program_id(1)

    @pl.when(t == 0)
    def _():
        carry_scr[...] = jnp.zeros((SUBLANES, 2 * N_STATE), F32)
        sret_ref[...] = jnp.zeros(sret_ref.shape, F32)
        pool_scr[0:16, :] = jnp.zeros((16, D_POOL), F32)

    u = proj_ref[:, 0:D_SSM]
    up = jnp.dot(perm_ref[...], u.astype(BF16), preferred_element_type=F32).astype(BF16)
    _ssm_drive(up, bmat_ref, dr_scr, 0, tl)
    _scan_tile(dr_scr, abar_ref, pws_ref, carry_scr, tl // SUBLANES)
    y = _unpermute_rows(permt_ref, _ssm_readout(dr_scr, cmat_ref, 0, tl)) + dskip_ref[...] * u
    y_ref[:, 0:D_SSM] = _ssm_gate(y, wglu_ref, bglu_ref).astype(BF16)
    hre_ref[0] = carry_scr[SUBLANES - 1:SUBLANES, 0:N_STATE]
    him_ref[0] = carry_scr[SUBLANES - 1:SUBLANES, N_STATE:2 * N_STATE]

    scale = RET_HD ** -0.5

    def chunk_body(ci, carry):
        rows = pl.ds(pl.multiple_of(ci * RET_CHUNK, RET_CHUNK), RET_CHUNK)
        cosv = cos_ref[rows, :]
        sinv = sin_ref[rows, :]
        for h in range(RET_HEADS):
            hs = slice(h * RET_HD, (h + 1) * RET_HD)
            q = proj_ref[rows, Q_OFF + h * RET_HD:Q_OFF + (h + 1) * RET_HD]
            k = proj_ref[rows, K_OFF + h * RET_HD:K_OFF + (h + 1) * RET_HD]
            v = proj_ref[rows, V_OFF + h * RET_HD:V_OFF + (h + 1) * RET_HD].astype(BF16)
            g = proj_ref[rows, G_OFF + h * RET_HD:G_OFF + (h + 1) * RET_HD]
            qr = _rotary(q, cosv, sinv)
            kr = _rotary(k, cosv, sinv) * scale
            s_old = sret_ref[0, h]
            sc = lax.dot_general(qr.astype(BF16), kr.astype(BF16), (((1,), (1,)), ((), ())),
                                 preferred_element_type=F32) * mask_ref[h]
            o = (jnp.dot(sc.astype(BF16), v, preferred_element_type=F32)
                 + jnp.dot((qr * qdec_ref[:, hs]).astype(BF16), s_old.astype(BF16),
                           preferred_element_type=F32))
            sret_ref[0, h] = (s_old * cdec_ref[h:h + 1, :]
                              + lax.dot_general((kr * kdec_ref[:, hs]).astype(BF16), v,
                                                (((0,), (0,)), ((), ())), preferred_element_type=F32))
            y_ref[rows, D_SSM + h * RET_HD:D_SSM + (h + 1) * RET_HD] = _head_norm_gate(o, g).astype(BF16)
        return carry

    lax.fori_loop(0, tl // RET_CHUNK, chunk_body, 0, unroll=True)

    uc = proj_ref[:, C_OFF:C_OFF + D_POOL]
    pool_scr[16:16 + tl, :] = uc
    pos = t * tl + lax.broadcasted_iota(jnp.int32, (tl, POOL_GD), 0)
    counts = _pool_counts(pos)
    parts = []
    for gi, w in enumerate(POOL_WINDOWS):
        lanes = slice(gi * POOL_GD, (gi + 1) * POOL_GD)
        acc = pool_scr[16:16 + tl, lanes]
        for j in range(1, w):
            acc = acc + pool_scr[16 - j:16 - j + tl, lanes]
        parts.append(acc / counts[gi] - uc[:, lanes])
    pooled = jnp.concatenate(parts, axis=-1).astype(BF16)
    yc = jnp.dot(pooled, wpool_ref[...], preferred_element_type=F32) * pscale_ref[...]
    y_ref[:, D_SSM + D_RET:] = yc.astype(BF16)
    pbuf_ref[0] = pool_scr[tl + 1:tl + 16, :]
    pool_scr[0:16, :] = pool_scr[tl:tl + 16, :]


def _mix_prompt(proj, n_seq, seq_len, consts, tl):
    cos_t, sin_t = consts[:2]
    whole = consts[2:]
    nt = seq_len // tl
    row_map = lambda b, t: (b * nt + t, 0)

    def const_spec(a):
        nd = a.ndim
        return pl.BlockSpec(a.shape, lambda b, t: (0,) * nd)

    in_specs = [
        pl.BlockSpec((tl, D_IN), row_map),
        pl.BlockSpec((tl, RET_HD), lambda b, t: (t, 0)),
        pl.BlockSpec((tl, RET_HD), lambda b, t: (t, 0)),
    ] + [const_spec(a) for a in whole]
    out_shape = (
        jax.ShapeDtypeStruct((n_seq * seq_len, D_MODEL), BF16),
        jax.ShapeDtypeStruct((n_seq, 1, N_STATE), F32),
        jax.ShapeDtypeStruct((n_seq, 1, N_STATE), F32),
        jax.ShapeDtypeStruct((n_seq, RET_HEADS, RET_HD, RET_HD), F32),
        jax.ShapeDtypeStruct((n_seq, POOL_BUF, D_POOL), F32),
    )
    out_specs = (
        pl.BlockSpec((tl, D_MODEL), row_map),
        pl.BlockSpec((1, 1, N_STATE), lambda b, t: (b, 0, 0)),
        pl.BlockSpec((1, 1, N_STATE), lambda b, t: (b, 0, 0)),
        pl.BlockSpec((1, RET_HEADS, RET_HD, RET_HD), lambda b, t: (b, 0, 0, 0)),
        pl.BlockSpec((1, POOL_BUF, D_POOL), lambda b, t: (b, 0, 0)),
    )
    return pl.pallas_call(
        functools.partial(_mix_prompt_kernel, tl=tl),
        grid=(n_seq, nt),
        in_specs=in_specs,
        out_specs=out_specs,
        out_shape=out_shape,
        scratch_shapes=[
            pltpu.VMEM((tl, 2 * N_STATE), F32),
            pltpu.VMEM((SUBLANES, 2 * N_STATE), F32),
            pltpu.VMEM((16 + tl, D_POOL), F32),
        ],
        compiler_params=_params("arbitrary", "arbitrary"),
        name="mix_prompt",
    )(proj, cos_t, sin_t, *whole)


def _mix_sample_kernel(proj_ref, cos_ref, sin_ref, pw_ref, bmat_ref, cmat_ref, dskip_ref, wglu_ref,
                       bglu_ref, mask_ref, qdec_ref, kdec_ref, cdec_ref, wpool_ref, pscale_ref,
                       h0re_ref, h0im_ref, s0_ref, pool0_ref,
                       y_ref, hre_ref, him_ref, sret_ref, pbuf_ref,
                       dr_scr, pool_scr, *, bs, seq_len, pos0):
    n_rows = bs * seq_len

    u = proj_ref[:, 0:D_SSM]
    _ssm_drive(u.astype(BF16), bmat_ref, dr_scr, SUBLANES, n_rows)

    def carry_fn(b, lo):
        return (h0re_ref[pl.ds(b, 1), lo:lo + SCAN_LANES], h0im_ref[pl.ds(b, 1), lo:lo + SCAN_LANES])

    def block_end_fn(b, lo, h_re, h_im):
        hre_ref[pl.ds(b, 1), lo:lo + SCAN_LANES] = h_re
        him_ref[pl.ds(b, 1), lo:lo + SCAN_LANES] = h_im

    _ssm_scan(dr_scr, pw_ref, bs, carry_fn, block_end_fn)
    y = _ssm_readout(dr_scr, cmat_ref, SUBLANES, n_rows) + dskip_ref[...] * u
    y_ref[:, 0:D_SSM] = _ssm_gate(y, wglu_ref, bglu_ref).astype(BF16)

    scale = RET_HD ** -0.5
    cosv = cos_ref[...]
    sinv = sin_ref[...]
    own = (lax.broadcasted_iota(jnp.int32, (n_rows, bs * RET_HD), 0) // seq_len
           == lax.broadcasted_iota(jnp.int32, (n_rows, bs * RET_HD), 1) // RET_HD)
    for h in range(RET_HEADS):
        hs = slice(h * RET_HD, (h + 1) * RET_HD)
        q = proj_ref[:, Q_OFF + h * RET_HD:Q_OFF + (h + 1) * RET_HD]
        k = proj_ref[:, K_OFF + h * RET_HD:K_OFF + (h + 1) * RET_HD]
        v = proj_ref[:, V_OFF + h * RET_HD:V_OFF + (h + 1) * RET_HD].astype(BF16)
        g = proj_ref[:, G_OFF + h * RET_HD:G_OFF + (h + 1) * RET_HD]
        qr = _rotary(q, cosv, sinv)
        kr = _rotary(k, cosv, sinv) * scale
        sc = lax.dot_general(qr.astype(BF16), kr.astype(BF16), (((1,), (1,)), ((), ())),
                             preferred_element_type=F32) * mask_ref[h]
        qd = jnp.where(own, jnp.concatenate([qr * qdec_ref[:, hs]] * bs, axis=1), 0.0).astype(BF16)
        kd = jnp.where(own, jnp.concatenate([kr * kdec_ref[:, hs]] * bs, axis=1), 0.0).astype(BF16)
        s_old = s0_ref[:, h].reshape(bs * RET_HD, RET_HD)
        o = (jnp.dot(sc.astype(BF16), v, preferred_element_type=F32)
             + jnp.dot(qd, s_old.astype(BF16), preferred_element_type=F32))
        s_new = (s_old * cdec_ref[h:h + 1, :]
                 + lax.dot_general(kd, v, (((0,), (0,)), ((), ())), preferred_element_type=F32))
        sret_ref[:, h] = s_new.reshape(bs, RET_HD, RET_HD)
        y_ref[:, D_SSM + h * RET_HD:D_SSM + (h + 1) * RET_HD] = _head_norm_gate(o, g).astype(BF16)

    uc = proj_ref[:, C_OFF:C_OFF + D_POOL].reshape(bs, seq_len, D_POOL)
    pool_scr[:, 1:16, :] = pool0_ref[...]
    pool_scr[:, 16:16 + seq_len, :] = uc
    pos = pos0 + lax.broadcasted_iota(jnp.int32, (1, seq_len, POOL_GD), 1)
    counts = _pool_counts(pos)
    parts = []
    for gi, w in enumerate(POOL_WINDOWS):
        lanes = slice(gi * POOL_GD, (gi + 1) * POOL_GD)
        acc = pool_scr[:, 16:16 + seq_len, lanes]
        for j in range(1, w):
            acc = acc + pool_scr[:, 16 - j:16 - j + seq_len, lanes]
        parts.append(acc / counts[gi] - uc[:, :, lanes])
    pooled = jnp.concatenate(parts, axis=-1).reshape(n_rows, D_POOL).astype(BF16)
    yc = jnp.dot(pooled, wpool_ref[...], preferred_element_type=F32) * pscale_ref[...]
    y_ref[:, D_SSM + D_RET:] = yc.astype(BF16)
    pbuf_ref[...] = pool_scr[:, seq_len + 1:seq_len + 16, :]


def _mix_sample(proj, proj_row0, states, layer, prev_out, n_seq, seq_len, consts, bs):
    n_rows = bs * seq_len

    def const_spec(a):
        nd = a.ndim
        return pl.BlockSpec(a.shape, lambda i: (0,) * nd)

    state_specs = [
        pl.BlockSpec((None, bs, N_STATE), lambda i: (layer, i, 0)),
        pl.BlockSpec((None, bs, N_STATE), lambda i: (layer, i, 0)),
        pl.BlockSpec((None, bs, RET_HEADS, RET_HD, RET_HD), lambda i: (layer, i, 0, 0, 0)),
        pl.BlockSpec((None, bs, POOL_BUF, D_POOL), lambda i: (layer, i, 0, 0)),
    ]
    assert proj_row0 % n_rows == 0
    in_specs = ([pl.BlockSpec((n_rows, D_IN), lambda i: (proj_row0 // n_rows + i, 0))]
                + [const_spec(a) for a in consts] + state_specs)
    args = [proj, *consts, *states]
    n_in = len(args)
    aliases = {}
    if prev_out is not None:
        in_specs += [pl.BlockSpec(memory_space=pl.ANY)] * len(prev_out)
        aliases = {n_in + k: 1 + k for k in range(len(prev_out))}
        args += list(prev_out)
    n_args = len(args)
    out_shape = (
        jax.ShapeDtypeStruct((n_seq * seq_len, D_MODEL), BF16),
        jax.ShapeDtypeStruct((DEPTH, n_seq, N_STATE), F32),
        jax.ShapeDtypeStruct((DEPTH, n_seq, N_STATE), F32),
        jax.ShapeDtypeStruct((DEPTH, n_seq, RET_HEADS, RET_HD, RET_HD), F32),
        jax.ShapeDtypeStruct((DEPTH, n_seq, POOL_BUF, D_POOL), F32),
    )
    out_specs = (pl.BlockSpec((n_rows, D_MODEL), lambda i: (i, 0)),) + tuple(state_specs)

    def body(*refs):
        _mix_sample_kernel(*refs[:n_in], *refs[n_args:], bs=bs, seq_len=seq_len, pos0=PAST_LEN)

    return pl.pallas_call(
        body,
        grid=(n_seq // bs,),
        in_specs=in_specs,
        out_specs=out_specs,
        out_shape=out_shape,
        scratch_shapes=[
            pltpu.VMEM((SUBLANES + n_rows, 2 * N_STATE), F32),
            pltpu.VMEM((bs, 16 + seq_len, D_POOL), F32),
        ],
        input_output_aliases=aliases,
        compiler_params=_params("arbitrary"),
        name="mix_sample",
    )(*args)


def _out_proj_kernel(y_ref, w_ref, x_ref, g1_ref, sh_ref, sc_ref, g_ref, xo_ref, h_ref, w_scr, *, bb, tl):
    @pl.when(pl.program_id(0) == 0)
    def _():
        w_scr[...] = w_ref[...].astype(BF16)

    f = jnp.dot(y_ref[...], w_scr[...], preferred_element_type=F32)
    xn = x_ref[...].reshape(bb, tl, D_MODEL) + g1_ref[...] * f.reshape(bb, tl, D_MODEL)
    xo_ref[...] = xn.reshape(bb * tl, D_MODEL)
    h = _rms(xn) * g_ref[...] * (1.0 + sc_ref[...]) + sh_ref[...]
    h_ref[...] = h.reshape(bb * tl, D_MODEL).astype(BF16)


def _out_proj(y, w, layer, x, x_tile0, mod, g, rows, out_rows, prev):
    out_spec = pl.BlockSpec((rows.tm, D_MODEL), lambda i: (rows.first_tile + i, 0))
    return _call_into(
        functools.partial(_out_proj_kernel, bb=rows.bb, tl=rows.tl),
        None if prev is None else {0: prev[0], 1: prev[1]},
        grid=(rows.n_tiles,),
        in_specs=[
            pl.BlockSpec((rows.tm, D_MODEL), lambda i: (i, 0)),
            pl.BlockSpec((None, D_MODEL, D_MODEL), lambda i: (layer, 0, 0), pipeline_mode=pl.Buffered(1)),
            pl.BlockSpec((rows.tm, D_MODEL), lambda i: (x_tile0 + i, 0)),
            rows.mod_spec(layer, MOD_GATE1, lambda i: i),
            rows.mod_spec(layer, MOD_SHIFT2, lambda i: i),
            rows.mod_spec(layer, MOD_SCALE2, lambda i: i),
            pl.BlockSpec((None, 1, D_MODEL), lambda i: (layer, 0, 0)),
        ],
        args=[y, w, x, mod, mod, mod, g],
        out_specs=(out_spec, out_spec),
        out_shape=(jax.ShapeDtypeStruct((out_rows, D_MODEL), F32),
                   jax.ShapeDtypeStruct((out_rows, D_MODEL), BF16)),
        scratch_shapes=[pltpu.VMEM((D_MODEL, D_MODEL), BF16)],
        sem=("arbitrary",),
        name="out_proj",
    )


def _conv_gate(a, b, am1, am2, cw, cb):
    conv = cb + am2 * cw[0:1]
    conv = conv + am1 * cw[1:2]
    conv = conv + a * cw[2:3]
    return jax.nn.silu(conv) * b


def _ffn_up_kernel(h_ref, wa_ref, wb_ref, cw_ref, cb_ref, st_ref, act_ref, tail_ref, ns_ref, w_scr, carry_scr,
                   *, prompt, n_seq, seq_len, n_chunks):
    c = pl.program_id(0)
    i = pl.program_id(1)

    @pl.when(i == 0)
    def _():
        w_scr[:, :FF_TILE] = wa_ref[...].astype(BF16)

    @pl.when((i == 0) & (c < n_chunks - 1))
    def _():
        w_scr[:, FF_TILE:] = wb_ref[0].astype(BF16)

    @pl.when((i == 0) & (c == n_chunks - 1))
    def _():
        w_scr[:, FF_TILE:FF_TILE + FF_LAST] = wb_ref[0, :, FF_TILE - FF_LAST:].astype(BF16)

    cw = cw_ref[...]
    cb = cb_ref[...]

    @pl.when(i >= prompt.n_tiles)
    def _():
        tl = seq_len
        sb = FF_SUB_ROWS // tl
        tok = lax.broadcasted_iota(jnp.int32, (sb, tl, FF_TILE), 1)
        for r in range(n_seq * seq_len // FF_SUB_ROWS):
            rows = slice(r * FF_SUB_ROWS, (r + 1) * FF_SUB_ROWS)
            sq = slice(r * sb, (r + 1) * sb)
            ab = jnp.dot(h_ref[rows, :], w_scr[...], preferred_element_type=F32)
            a = ab[:, :FF_TILE].reshape(sb, tl, FF_TILE)
            b = ab[:, FF_TILE:].reshape(sb, tl, FF_TILE)
            p0 = st_ref[sq, 0:1, :]
            p1 = st_ref[sq, 1:2, :]
            am1 = jnp.where(tok == 0, p1, pltpu.roll(a, 1, 1))
            am2 = jnp.where(tok == 0, p0, jnp.where(tok == 1, p1, pltpu.roll(a, 2, 1)))
            act = _conv_gate(a, b, am1, am2, cw, cb)
            act_ref[rows, :] = act.reshape(FF_SUB_ROWS, FF_TILE).astype(BF16)
            ns_ref[sq] = a[:, tl - 2:tl, :]

    @pl.when(i < prompt.n_tiles)
    def _():
        @pl.when(i % prompt.tiles_per_seq == 0)
        def _():
            carry_scr[...] = jnp.zeros((SUBLANES, FF_TILE), F32)

        tok = lax.broadcasted_iota(jnp.int32, (SUBLANES, FF_TILE), 0)
        for r in range(prompt.tm // FF_SUB_ROWS):
            rows = slice(r * FF_SUB_ROWS, (r + 1) * FF_SUB_ROWS)
            ab = jnp.dot(h_ref[rows, :], w_scr[...], preferred_element_type=F32)
            a = ab[:, :FF_TILE]
            b = ab[:, FF_TILE:]
            p0 = carry_scr[SUBLANES - 2:SUBLANES - 1, :]
            p1 = carry_scr[SUBLANES - 1:SUBLANES, :]
            r1 = pltpu.roll(a, 1, 0)
            r2 = pltpu.roll(a, 2, 0)
            head1 = jnp.where(tok == 0, p1, r1[:SUBLANES])
            head2 = jnp.where(tok == 0, p0, jnp.where(tok == 1, p1, r2[:SUBLANES]))
            am1 = jnp.concatenate([head1, r1[SUBLANES:]], axis=0)
            am2 = jnp.concatenate([head2, r2[SUBLANES:]], axis=0)
            act_ref[rows, :] = _conv_gate(a, b, am1, am2, cw, cb).astype(BF16)
            carry_scr[...] = a[FF_SUB_ROWS - SUBLANES:, :]
        tail_ref[0] = carry_scr[SUBLANES - 2:SUBLANES, :]


def _ffn_up(h, w_up, conv_w, conv_b, layer, state, prev_state, prompt, n_seq, seq_len):
    m = h.shape[0]
    tm = prompt.tm
    assert m == prompt.n_tiles * tm + n_seq * seq_len and n_seq * seq_len <= tm
    assert (n_seq * seq_len) % FF_SUB_ROWS == 0 and tm % FF_SUB_ROWS == 0
    n_tiles = prompt.n_tiles + 1
    n_chunks = pl.cdiv(D_FF, FF_TILE)
    in_specs = [
        pl.BlockSpec((tm, D_MODEL), lambda c, i: (i, 0)),
        pl.BlockSpec((None, D_MODEL, FF_TILE), lambda c, i: (layer, 0, c)),
        pl.BlockSpec((pl.Element(1), pl.Element(D_MODEL), pl.Element(FF_TILE)),
                     lambda c, i: (layer, 0, LANES * jnp.minimum((D_FF + c * FF_TILE) // LANES,
                                                                 (2 * D_FF - FF_TILE) // LANES))),
        pl.BlockSpec((None, CONV_W, FF_TILE), lambda c, i: (layer, 0, c)),
        pl.BlockSpec((None, 1, FF_TILE), lambda c, i: (layer, 0, c)),
        pl.BlockSpec((None, n_seq, CONV_W - 1, FF_TILE), lambda c, i: (layer, 0, 0, c)),
    ]
    return _call_into(
        functools.partial(_ffn_up_kernel, prompt=prompt, n_seq=n_seq, seq_len=seq_len, n_chunks=n_chunks),
        None if prev_state is None else {2: prev_state},
        grid=(n_chunks, n_tiles),
        in_specs=in_specs,
        args=[h, w_up, w_up, conv_w, conv_b, state],
        out_specs=(
            pl.BlockSpec((tm, FF_TILE), lambda c, i: (i, c)),
            pl.BlockSpec((1, CONV_W - 1, FF_TILE), lambda c, i: (jnp.minimum(i, prompt.n_tiles - 1), 0, c)),
            pl.BlockSpec((None, n_seq, CONV_W - 1, FF_TILE), lambda c, i: (layer, 0, 0, c)),
        ),
        out_shape=(jax.ShapeDtypeStruct((m, D_FF), BF16),
                   jax.ShapeDtypeStruct((prompt.n_tiles, CONV_W - 1, D_FF), F32),
                   jax.ShapeDtypeStruct((DEPTH, n_seq, CONV_W - 1, D_FF), F32)),
        scratch_shapes=[pltpu.VMEM((D_MODEL, 2 * FF_TILE), BF16), pltpu.VMEM((SUBLANES, FF_TILE), F32)],
        sem=("arbitrary", "arbitrary"),
        name="ffn_up",
    )


def _cmul(ar, ai, br, bi):
    return ar * br - ai * bi, ar * bi + ai * br


def _ssm_consts(a_re, a_im, log_dt, b_re, b_im, c_re, c_im, scan_steps):
    lam_re, lam_im = a_re.astype(F32), a_im.astype(F32)
    dt = jnp.exp(log_dt.astype(F32))[:, None]
    mag = jnp.exp(lam_re * dt)
    ab_re, ab_im = mag * jnp.cos(lam_im * dt), mag * jnp.sin(lam_im * dt)
    den = lam_re * lam_re + lam_im * lam_im
    f_re = ((ab_re - 1.0) * lam_re + ab_im * lam_im) / den
    f_im = (ab_im * lam_re - (ab_re - 1.0) * lam_im) / den
    br, bi = b_re.astype(F32), b_im.astype(F32)
    bb_re = f_re[..., None] * br - f_im[..., None] * bi
    bb_im = f_re[..., None] * bi + f_im[..., None] * br

    gh = SSM_GROUPS // 2
    eye = jnp.eye(gh, dtype=F32)

    def drive_half(m):
        return jnp.einsum("gph,gk->ghkp", m, eye).reshape(gh * SSM_GROUP, gh * SSM_STATE)

    def read_half(m):
        return jnp.einsum("ghp,gk->gpkh", m, eye).reshape(gh * SSM_STATE, gh * SSM_GROUP)

    bmat = jnp.stack([
        jnp.concatenate([drive_half(bb_re[s]), drive_half(bb_im[s])], axis=1)
        for s in (slice(0, gh), slice(gh, 2 * gh))]).astype(BF16)
    cre, cim = c_re.astype(F32), c_im.astype(F32)
    cmat = jnp.stack([
        jnp.concatenate([read_half(cre[s]), -read_half(cim[s])], axis=0)
        for s in (slice(0, gh), slice(gh, 2 * gh))]).astype(BF16)

    step_re, step_im = _power_table(ab_re.reshape(1, N_STATE), ab_im.reshape(1, N_STATE), scan_steps)
    sub_re, sub_im = _power_table(step_re[-1:], step_im[-1:], SUBLANES)
    abar = jnp.stack([step_re[:1], step_im[:1]])
    pw_block = _scan_planes(step_re[:SUBLANES], step_im[:SUBLANES])
    pw_sub = _scan_planes(sub_re, sub_im)
    return bmat, cmat, abar, pw_block, pw_sub


def _power_table(re, im, n):
    while re.shape[0] < n:
        top_re, top_im = re[-1:], im[-1:]
        more_re, more_im = _cmul(re, im, top_re, top_im)
        re, im = jnp.concatenate([re, more_re]), jnp.concatenate([im, more_im])
    return re[:n], im[:n]


def _scan_planes(re, im):
    row = jnp.arange(SUBLANES)[:, None]
    planes_re = [jnp.where(row >= s, re[s - 1][None, :], 0.0) for s in (1, 2, 4)] + [re]
    planes_im = [jnp.where(row >= s, im[s - 1][None, :], 0.0) for s in (1, 2, 4)] + [im]
    return jnp.stack([jnp.stack(planes_re), jnp.stack(planes_im)])


def _time_on_sublanes(n_rows):
    steps = n_rows // SUBLANES
    p = np.zeros((n_rows, n_rows), np.float32)
    r = np.arange(n_rows)
    p[r, (r % SUBLANES) * steps + r // SUBLANES] = 1.0
    return jnp.asarray(p, BF16), jnp.asarray(p.T, BF16)


def _rotary_tables(pos):
    half = RET_HD // 2
    inv = ROPE_BASE ** (-np.arange(half, dtype=np.float64) / half)
    ang = np.asarray(pos, np.float64)[:, None] * inv[None, :]
    cos, sin = np.cos(ang), np.sin(ang)
    return (np.concatenate([cos, cos], axis=-1).astype(np.float32),
            np.concatenate([-sin, sin], axis=-1).astype(np.float32))


def _retention_consts(chunk, n_seq):
    lg = np.log1p(-np.exp2(-5.0 - np.arange(RET_HEADS, dtype=np.float64)))
    r = np.arange(chunk * n_seq)
    ti = (r % chunk).astype(np.float64)
    seq = r // chunk
    diff = ti[:, None] - ti[None, :]
    keep = (seq[:, None] == seq[None, :]) & (diff >= 0.0)
    mask = np.where(keep[None], np.exp(np.maximum(diff, 0.0)[None] * lg[:, None, None]), 0.0)
    q_dec = np.exp((ti + 1.0)[:, None] * lg[None, :])
    k_dec = np.exp((chunk - 1.0 - ti)[:, None] * lg[None, :])
    c_dec = np.exp(chunk * lg)
    expand = lambda d: np.repeat(d, RET_HD, axis=1).astype(np.float32)
    return (mask.astype(np.float32), expand(q_dec), expand(k_dec),
            np.broadcast_to(c_dec[:, None], (RET_HEADS, RET_HD)).astype(np.float32))


def _pool_weight(w_pool):
    eye = jnp.eye(len(POOL_WINDOWS), dtype=F32)
    return jnp.einsum("gcd,gk->gckd", w_pool, eye).reshape(D_POOL, D_POOL).astype(BF16)


def kernel(x_prompt, x_sample, state_ssm_re, state_ssm_im, state_ret, state_pool, state_ffn_conv, c_prompt, c_sample, w_ada, b_ada, norm1_g, w_in, ssm_a_re, ssm_a_im, ssm_log_dt, ssm_b_re, ssm_b_im, ssm_c_re, ssm_c_im, ssm_d, ssm_w_glu, ssm_b_glu, pool_w, pool_scale, w_out, norm2_g, ffn_w_up, ffn_conv_w, ffn_conv_b, ffn_w_down, final_norm_g):
    bp, lp, _ = x_prompt.shape
    bs, ls, _ = x_sample.shape
    tl_mix = 512
    seq_mix = 8

    n_all = bp + bs
    n_pad = -(-n_all // SUBLANES) * SUBLANES
    c_all = jnp.concatenate([c_sample, c_prompt, jnp.zeros((n_pad - n_all, D_MODEL), F32)], axis=0)
    mod = _ada_mod(c_all, w_ada, b_ada)

    cos_p, sin_p = _rotary_tables(np.arange(lp))
    cos_s, sin_s = _rotary_tables(PAST_LEN + np.arange(ls))
    cos_s, sin_s = np.tile(cos_s, (seq_mix, 1)), np.tile(sin_s, (seq_mix, 1))
    assert math.gcd(lp, RET_CHUNK) == RET_CHUNK and math.gcd(ls, RET_CHUNK) == ls == SUBLANES
    ret_p = _retention_consts(RET_CHUNK, 1)
    ret_s = _retention_consts(ls, seq_mix)
    perm = _time_on_sublanes(tl_mix)

    rows_total = bp * lp + bs * ls
    groups = (_Rows(bp, lp, ROW_TILE, bs, 0), _Rows(bs, ls, ROW_TILE, 0, bp * lp))
    groups_full = (_Rows(bp, lp, ROW_TILE_FULL, bs, 0), _Rows(bs, ls, ROW_TILE_FULL, 0, bp * lp))
    big_p, big_s = groups
    wide_p = _Rows(bp, lp, ROW_TILE_WIDE, bs, 0)

    norm1 = norm1_g.reshape(DEPTH, 1, D_MODEL)
    norm2 = norm2_g.reshape(DEPTH, 1, D_MODEL)
    final_g = final_norm_g.reshape(1, 1, D_MODEL)
    conv_b = ffn_conv_b.reshape(DEPTH, 1, D_FF)
    sample_states = (state_ssm_re.reshape(DEPTH, bs, N_STATE), state_ssm_im.reshape(DEPTH, bs, N_STATE),
                     state_ret, state_pool)

    x_first = (x_prompt.reshape(bp * lp, D_MODEL), x_sample.reshape(bs * ls, D_MODEL))
    x_all = None

    def residual_rows(k, rows):
        return (x_first[k], 0) if x_all is None else (x_all, rows.first_tile)

    new_p = ([], [], [], [], [])
    s_mix = None
    s_conv = None
    for l in range(DEPTH):
        bmat, cmat, abar, pw_block, pw_sub = _ssm_consts(
            ssm_a_re[l], ssm_a_im[l], ssm_log_dt[l], ssm_b_re[l], ssm_b_im[l], ssm_c_re[l], ssm_c_im[l],
            tl_mix // SUBLANES)
        shared = (bmat, cmat, ssm_d[l].reshape(1, D_SSM), ssm_w_glu[l].astype(BF16),
                  ssm_b_glu[l].reshape(1, D_SSM))
        pool_c = (_pool_weight(pool_w[l]), pool_scale[l].reshape(1, D_POOL))
        pre = (mod, MOD_SHIFT1, MOD_SCALE1)

        h = None
        for k, rows in enumerate(groups):
            h = _norm(*residual_rows(k, rows), norm1, l, pre, rows, BF16, rows_total, h)
        proj = _matmul(h, w_in, l, ROW_TILE, IN_PROJ_COLS, "in_proj")
        y_p, st_re, st_im, st_ret, st_pool = _mix_prompt(
            proj, bp, lp, (cos_p, sin_p) + perm + (abar, pw_sub) + shared + ret_p + pool_c, tl_mix)
        y_s, *s_mix = _mix_sample(proj, bp * lp, sample_states, l, s_mix, bs, ls,
                                  (cos_s, sin_s, pw_block) + shared + ret_s + pool_c, seq_mix)
        xh = None
        for k, (rows, y) in enumerate(zip(groups_full, (y_p, y_s))):
            xh = _out_proj(y, w_out, l, *residual_rows(k, rows), mod, norm2, rows, rows_total, xh)
        xn, h2 = xh
        act, tails, s_conv = _ffn_up(h2, ffn_w_up, ffn_conv_w, conv_b, l, state_ffn_conv, s_conv, wide_p, bs, ls)
        x_all = _matmul(act, ffn_w_down, l, ROW_TILE, FF_DOWN_COLS, "ffn_down", groups=groups,
                        residual=(xn, mod, MOD_GATE2), single_buffer_w=True)
        st_conv = tails[wide_p.tiles_per_seq - 1::wide_p.tiles_per_seq]
        for lst, st in zip(new_p, (st_re, st_im, st_ret, st_pool, st_conv)):
            lst.append(st)

    yp = _norm(x_all, big_p.first_tile, final_g, 0, None, big_p, F32)
    ys = _norm(x_all, big_s.first_tile, final_g, 0, None, big_s, F32)

    p_re, p_im, p_ret, p_pool, p_conv = [jnp.stack(st) for st in new_p]
    s_re, s_im, s_ret, s_pool = s_mix
    shape_p = (DEPTH, bp, SSM_GROUPS, SSM_STATE)
    shape_s = (DEPTH, bs, SSM_GROUPS, SSM_STATE)
    return (yp.reshape(bp, lp, D_MODEL), ys.reshape(bs, ls, D_MODEL),
            p_re.reshape(shape_p), p_im.reshape(shape_p), p_ret, p_pool, p_conv,
            s_re.reshape(shape_s), s_im.reshape(shape_s), s_ret, s_pool, s_conv)
```

```python
import functools
import math

import jax
import jax.numpy as jnp
import numpy as np
from jax import lax
from jax.experimental import pallas as pl
from jax.experimental.pallas import tpu as pltpu

F32 = jnp.float32
BF16 = jnp.bfloat16

D_MODEL = 2048
DEPTH = 2
PAST_LEN = 16384
D_SSM = 512
D_RET = 1024
D_POOL = 512
SSM_GROUP = 16
SSM_GROUPS = 32
SSM_STATE = 64
N_STATE = SSM_GROUPS * SSM_STATE
RET_HEADS = 8
RET_HD = 128
RET_CHUNK = 128
POOL_WINDOWS = (2, 4, 8, 16)
POOL_GD = 128
POOL_BUF = 15
D_FF = 5504
FF_TILE = 512
FF_LAST = D_FF % FF_TILE
CONV_W = 3
D_IN = D_SSM + 4 * D_RET + D_POOL
ROPE_BASE = 10000.0
EPS = 1e-6

SUBLANES = 8
LANES = 128
ROW_TILE = 1024
ROW_TILE_FULL = 512
FF_SUB_ROWS = 512
IN_PROJ_COLS = 1024
FF_DOWN_COLS = 512
ADA_COLS = 1024
MOD_SHIFT1, MOD_SCALE1, MOD_GATE1, MOD_SHIFT2, MOD_SCALE2, MOD_GATE2 = range(6)
SCAN_LANES = 512
VMEM_LIMIT = 58 * 1024 * 1024

Q_OFF = D_SSM
K_OFF = D_SSM + D_RET
V_OFF = D_SSM + 2 * D_RET
G_OFF = D_SSM + 3 * D_RET
C_OFF = D_SSM + 4 * D_RET


def _params(*sem):
    return pltpu.CompilerParams(dimension_semantics=sem, vmem_limit_bytes=VMEM_LIMIT)


def _rms(x):
    return x * lax.rsqrt(jnp.mean(x * x, axis=-1, keepdims=True) + EPS)


def _ada_kernel(c_ref, w_ref, b_ref, o_ref):
    a = jax.nn.silu(c_ref[...]).astype(BF16)
    res = jnp.dot(a, w_ref[...].astype(BF16), preferred_element_type=F32) + b_ref[...]
    for r in range(res.shape[0]):
        o_ref[r] = res[r:r + 1, :]


def _ada_mod(c_all, w_ada, b_ada):
    nseq = c_all.shape[0]
    per_vec = D_MODEL // ADA_COLS
    return pl.pallas_call(
        _ada_kernel,
        grid=(DEPTH, 6 * per_vec),
        in_specs=[
            pl.BlockSpec((nseq, D_MODEL), lambda l, j: (0, 0)),
            pl.BlockSpec((None, D_MODEL, ADA_COLS), lambda l, j: (l, 0, j)),
            pl.BlockSpec((None, 1, ADA_COLS), lambda l, j: (l, 0, j)),
        ],
        out_specs=pl.BlockSpec((None, None, nseq, 1, ADA_COLS), lambda l, j: (l, j // per_vec, 0, 0, j % per_vec)),
        out_shape=jax.ShapeDtypeStruct((DEPTH, 6, nseq, 1, D_MODEL), F32),
        compiler_params=_params("arbitrary", "arbitrary"),
        name="ada_mod",
    )(c_all, w_ada, b_ada.reshape(DEPTH, 1, 6 * D_MODEL))


class _Rows:
    def __init__(self, n_seq, seq_len, tm, first_seq, first_row):
        if seq_len >= tm:
            assert seq_len % tm == 0
            self.bb, self.tl = 1, tm
            self.tiles_per_seq = seq_len // tm
        else:
            assert tm % seq_len == 0 and seq_len == SUBLANES
            self.bb, self.tl = tm // seq_len, seq_len
            self.tiles_per_seq = 1
        assert first_seq % self.bb == 0 and first_row % tm == 0
        self.tm = tm
        self.n_tiles = n_seq * seq_len // tm
        self.first_block = first_seq // self.bb
        self.first_tile = first_row // tm

    def mod_block(self, i):
        return self.first_block + i // self.tiles_per_seq

    def mod_spec(self, layer, vec, row_tile_of, cols=D_MODEL, col_block_of=lambda *ids: 0):
        return pl.BlockSpec(
            (None, None, self.bb, 1, cols),
            lambda *ids: (layer, vec, self.mod_block(row_tile_of(*ids)), 0, col_block_of(*ids)))


def _call_into(kernel_fn, prev, *, grid, in_specs, args, out_specs, out_shape, scratch_shapes=(), sem, name):
    n_in = len(args)
    in_specs, args = list(in_specs), list(args)
    aliases = {}
    for k, arr in sorted((prev or {}).items()):
        aliases[len(args)] = k
        in_specs.append(pl.BlockSpec(memory_space=pl.ANY))
        args.append(arr)
    n_args = len(args)

    def body(*refs):
        kernel_fn(*refs[:n_in], *refs[n_args:])

    return pl.pallas_call(
        body, grid=grid, in_specs=in_specs, out_specs=out_specs, out_shape=out_shape,
        scratch_shapes=list(scratch_shapes), input_output_aliases=aliases,
        compiler_params=_params(*sem), name=name,
    )(*args)


def _norm_kernel(*refs, bb, tl, modulated):
    if modulated:
        x_ref, g_ref, sh_ref, sc_ref, o_ref = refs
    else:
        x_ref, g_ref, o_ref = refs
    h = _rms(x_ref[...].reshape(bb, tl, D_MODEL)) * g_ref[...]
    if modulated:
        h = h * (1.0 + sc_ref[...]) + sh_ref[...]
    o_ref[...] = h.reshape(bb * tl, D_MODEL).astype(o_ref.dtype)


def _norm(x, x_tile0, g, layer, mods, rows, out_dtype, out_rows=None, prev=None):
    in_specs = [pl.BlockSpec((rows.tm, D_MODEL), lambda i: (x_tile0 + i, 0)),
                pl.BlockSpec((None, 1, D_MODEL), lambda i: (layer, 0, 0))]
    args = [x, g]
    if mods is not None:
        mod, shift_vec, scale_vec = mods
        in_specs += [rows.mod_spec(layer, shift_vec, lambda i: i), rows.mod_spec(layer, scale_vec, lambda i: i)]
        args += [mod, mod]
    out_tile0 = 0 if out_rows is None else rows.first_tile
    return _call_into(
        functools.partial(_norm_kernel, bb=rows.bb, tl=rows.tl, modulated=mods is not None),
        None if prev is None else {0: prev},
        grid=(rows.n_tiles,),
        in_specs=in_specs,
        args=args,
        out_specs=pl.BlockSpec((rows.tm, D_MODEL), lambda i: (out_tile0 + i, 0)),
        out_shape=jax.ShapeDtypeStruct((out_rows or rows.n_tiles * rows.tm, D_MODEL), out_dtype),
        sem=("arbitrary",),
        name="norm",
    )


def _matmul_kernel(*refs, groups, residual):
    if residual:
        a_ref, w_ref, x_ref, *gate_refs, o_ref, w_scr = refs
    else:
        a_ref, w_ref, o_ref, w_scr = refs
    i = pl.program_id(1)

    @pl.when(i == 0)
    def _():
        w_scr[...] = w_ref[...].astype(BF16)

    f = jnp.dot(a_ref[...], w_scr[...], preferred_element_type=F32)
    if not residual:
        o_ref[...] = f
        return
    tn = f.shape[1]
    for rows, g_ref in zip(groups, gate_refs):
        @pl.when((i >= rows.first_tile) & (i < rows.first_tile + rows.n_tiles))
        def _(rows=rows, g_ref=g_ref):
            xn = x_ref[...].reshape(rows.bb, rows.tl, tn) + g_ref[...] * f.reshape(rows.bb, rows.tl, tn)
            o_ref[...] = xn.reshape(rows.tm, tn)


def _matmul(a, w, layer, tm, tn, name, groups=(), residual=None, single_buffer_w=False):
    m, k = a.shape
    n = w.shape[2]
    n_tiles = pl.cdiv(m, tm)
    if residual is not None:
        assert m == sum(rows.n_tiles for rows in groups) * tm and all(rows.tm == tm for rows in groups)
    w_mode = dict(pipeline_mode=pl.Buffered(1)) if single_buffer_w else {}
    in_specs = [
        pl.BlockSpec((tm, k), lambda j, i: (i, 0)),
        pl.BlockSpec((None, k, tn), lambda j, i: (layer, 0, j), **w_mode),
    ]
    args = [a, w]
    if residual is not None:
        x, mod, gate_vec = residual
        in_specs.append(pl.BlockSpec((tm, tn), lambda j, i: (i, j)))
        args.append(x)
        for rows in groups:
            in_specs.append(rows.mod_spec(
                layer, gate_vec,
                lambda j, i, rows=rows: jnp.clip(i - rows.first_tile, 0, rows.n_tiles - 1),
                tn, lambda j, i: j))
            args.append(mod)
    return pl.pallas_call(
        functools.partial(_matmul_kernel, groups=groups, residual=residual is not None),
        grid=(n // tn, n_tiles),
        in_specs=in_specs,
        out_specs=pl.BlockSpec((tm, tn), lambda j, i: (i, j)),
        out_shape=jax.ShapeDtypeStruct((m, n), F32),
        scratch_shapes=[pltpu.VMEM((k, tn), BF16)],
        compiler_params=_params("arbitrary", "arbitrary"),
        name=name,
    )(*args)


def _ssm_drive(ub, bmat_ref, dr_scr, row0, n_rows):
    half_u = D_SSM // 2
    half_n = N_STATE // 2
    for hf in range(2):
        d = jnp.dot(ub[:, hf * half_u:(hf + 1) * half_u], bmat_ref[hf], preferred_element_type=F32)
        dr_scr[row0:row0 + n_rows, hf * half_n:(hf + 1) * half_n] = d[:, :half_n]
        dr_scr[row0:row0 + n_rows, N_STATE + hf * half_n:N_STATE + (hf + 1) * half_n] = d[:, half_n:]


def _cmul_add(a_re, a_im, x_re, x_im, y_re, y_im):
    return y_re + (a_re * x_re - a_im * x_im), y_im + (a_re * x_im + a_im * x_re)


def _scan_tile(dr_scr, abar_ref, pws_ref, carry_scr, n_steps):
    row = lax.broadcasted_iota(jnp.int32, (SUBLANES, SCAN_LANES), 0)
    for c in range(N_STATE // SCAN_LANES):
        lo = c * SCAN_LANES
        re_cols = slice(lo, lo + SCAN_LANES)
        im_cols = slice(N_STATE + lo, N_STATE + lo + SCAN_LANES)
        a_re = jnp.broadcast_to(abar_ref[0, :, re_cols], (SUBLANES, SCAN_LANES))
        a_im = jnp.broadcast_to(abar_ref[1, :, re_cols], (SUBLANES, SCAN_LANES))

        def local_step(t, h):
            r = pl.ds(pl.multiple_of(t * SUBLANES, SUBLANES), SUBLANES)
            h_re, h_im = _cmul_add(a_re, a_im, h[0], h[1], dr_scr[r, re_cols], dr_scr[r, im_cols])
            dr_scr[r, re_cols] = h_re
            dr_scr[r, im_cols] = h_im
            return h_re, h_im

        zero = jnp.zeros((SUBLANES, SCAN_LANES), F32)
        g_re, g_im = lax.fori_loop(0, n_steps, local_step, (zero, zero), unroll=2)
        for si, s in enumerate((1, 2, 4)):
            g_re, g_im = _cmul_add(pws_ref[0, si, :, re_cols], pws_ref[1, si, :, re_cols],
                                   pltpu.roll(g_re, s, 0), pltpu.roll(g_im, s, 0), g_re, g_im)
        c_re = carry_scr[SUBLANES - 1:SUBLANES, re_cols]
        c_im = carry_scr[SUBLANES - 1:SUBLANES, im_cols]
        e_re, e_im = _cmul_add(pws_ref[0, 3, :, re_cols], pws_ref[1, 3, :, re_cols], c_re, c_im, g_re, g_im)
        carry_scr[:, re_cols] = e_re
        carry_scr[:, im_cols] = e_im
        in_re = jnp.where(row == 0, c_re, pltpu.roll(e_re, 1, 0))
        in_im = jnp.where(row == 0, c_im, pltpu.roll(e_im, 1, 0))

        def fix_step(t, w):
            w_re, w_im = _cmul(a_re, a_im, w[0], w[1])
            r = pl.ds(pl.multiple_of(t * SUBLANES, SUBLANES), SUBLANES)
            dr_scr[r, re_cols] = dr_scr[r, re_cols] + w_re
            dr_scr[r, im_cols] = dr_scr[r, im_cols] + w_im
            return w_re, w_im

        lax.fori_loop(0, n_steps, fix_step, (in_re, in_im), unroll=2)


def _scan_block(dr_scr, pw_ref, b, carry_fn, block_end_fn=None):
    r = pl.multiple_of(b * SUBLANES + SUBLANES, SUBLANES)
    for c in range(N_STATE // SCAN_LANES):
        lo = c * SCAN_LANES
        re_cols = slice(lo, lo + SCAN_LANES)
        im_cols = slice(N_STATE + lo, N_STATE + lo + SCAN_LANES)
        d_re = dr_scr[pl.ds(r, SUBLANES), re_cols]
        d_im = dr_scr[pl.ds(r, SUBLANES), im_cols]
        for si, s in enumerate((1, 2, 4)):
            p_re = pw_ref[0, si, :, re_cols]
            p_im = pw_ref[1, si, :, re_cols]
            r_re = pltpu.roll(d_re, s, 0)
            r_im = pltpu.roll(d_im, s, 0)
            d_re, d_im = (d_re + (p_re * r_re - p_im * r_im),
                          d_im + (p_re * r_im + p_im * r_re))
        c_re, c_im = carry_fn(b, lo)
        a_re = pw_ref[0, 3, :, re_cols]
        a_im = pw_ref[1, 3, :, re_cols]
        h_re = d_re + (a_re * c_re - a_im * c_im)
        h_im = d_im + (a_re * c_im + a_im * c_re)
        dr_scr[pl.ds(r, SUBLANES), re_cols] = h_re
        dr_scr[pl.ds(r, SUBLANES), im_cols] = h_im
        if block_end_fn is not None:
            block_end_fn(b, lo, h_re[SUBLANES - 1:SUBLANES], h_im[SUBLANES - 1:SUBLANES])


def _ssm_scan(dr_scr, pw_ref, n_blocks, carry_fn, block_end_fn=None):
    def body(b, carry):
        _scan_block(dr_scr, pw_ref, b, carry_fn, block_end_fn)
        return carry

    lax.fori_loop(0, n_blocks, body, 0)


def _ssm_readout(dr_scr, cmat_ref, row0, n_rows):
    half_n = N_STATE // 2
    parts = []
    for hf in range(2):
        h_re = dr_scr[row0:row0 + n_rows, hf * half_n:(hf + 1) * half_n].astype(BF16)
        h_im = dr_scr[row0:row0 + n_rows, N_STATE + hf * half_n:N_STATE + (hf + 1) * half_n].astype(BF16)
        parts.append(jnp.dot(h_re, cmat_ref[hf, :half_n], preferred_element_type=F32)
                     + jnp.dot(h_im, cmat_ref[hf, half_n:], preferred_element_type=F32))
    return jnp.concatenate(parts, axis=-1)


def _ssm_gate(y, wglu_ref, bglu_ref):
    ya = jax.nn.gelu(y)
    gate = jnp.dot(ya.astype(BF16), wglu_ref[...], preferred_element_type=F32) + bglu_ref[...]
    return ya * jax.nn.sigmoid(gate)


def _unpermute_rows(perm_t_ref, x):
    hi = x.astype(BF16)
    rest = x - hi.astype(F32)
    mid = rest.astype(BF16)
    lo = (rest - mid.astype(F32)).astype(BF16)
    n = x.shape[1]
    out = jnp.dot(perm_t_ref[...], jnp.concatenate([hi, mid, lo], axis=1), preferred_element_type=F32)
    return out[:, :n] + out[:, n:2 * n] + out[:, 2 * n:]


def _rotary(x, cosv, sinv):
    return x * cosv + pltpu.roll(x, RET_HD // 2, 1) * sinv


def _head_norm_gate(o, g):
    mu = jnp.mean(o, axis=-1, keepdims=True)
    var = jnp.mean(jnp.square(o - mu), axis=-1, keepdims=True)
    return (o - mu) * lax.rsqrt(var + EPS) * jax.nn.silu(g)


def _pool_counts(pos):
    return [jnp.minimum(pos + 1, w).astype(F32) for w in POOL_WINDOWS]


def _mix_prompt_kernel(proj_ref, cos_ref, sin_ref, perm_ref, permt_ref, abar_ref, pws_ref,
                       bmat_ref, cmat_ref, dskip_ref, wglu_ref, bglu_ref,
                       mask_ref, qdec_ref, kdec_ref, cdec_ref, wpool_ref, pscale_ref,
                       y_ref, hre_ref, him_ref, sret_ref, pbuf_ref,
                       dr_scr, carry_scr, pool_scr, *, tl):
    t = pl.program_id(1)

    @pl.when(t == 0)
    def _():
        carry_scr[...] = jnp.zeros((SUBLANES, 2 * N_STATE), F32)
        sret_ref[...] = jnp.zeros(sret_ref.shape, F32)
        pool_scr[0:16, :] = jnp.zeros((16, D_POOL), F32)

    u = proj_ref[:, 0:D_SSM]
    up = jnp.dot(perm_ref[...], u.astype(BF16), preferred_element_type=F32).astype(BF16)
    _ssm_drive(up, bmat_ref, dr_scr, 0, tl)
    _scan_tile(dr_scr, abar_ref, pws_ref, carry_scr, tl // SUBLANES)
    y = _unpermute_rows(permt_ref, _ssm_readout(dr_scr, cmat_ref, 0, tl)) + dskip_ref[...] * u
    y_ref[:, 0:D_SSM] = _ssm_gate(y, wglu_ref, bglu_ref).astype(BF16)
    hre_ref[0] = carry_scr[SUBLANES - 1:SUBLANES, 0:N_STATE]
    him_ref[0] = carry_scr[SUBLANES - 1:SUBLANES, N_STATE:2 * N_STATE]

    scale = RET_HD ** -0.5

    def chunk_body(ci, carry):
        rows = pl.ds(pl.multiple_of(ci * RET_CHUNK, RET_CHUNK), RET_CHUNK)
        cosv = cos_ref[rows, :]
        sinv = sin_ref[rows, :]
        for h in range(RET_HEADS):
            hs = slice(h * RET_HD, (h + 1) * RET_HD)
            q = proj_ref[rows, Q_OFF + h * RET_HD:Q_OFF + (h + 1) * RET_HD]
            k = proj_ref[rows, K_OFF + h * RET_HD:K_OFF + (h + 1) * RET_HD]
            v = proj_ref[rows, V_OFF + h * RET_HD:V_OFF + (h + 1) * RET_HD].astype(BF16)
            g = proj_ref[rows, G_OFF + h * RET_HD:G_OFF + (h + 1) * RET_HD]
            qr = _rotary(q, cosv, sinv)
            kr = _rotary(k, cosv, sinv) * scale
            s_old = sret_ref[0, h]
            sc = lax.dot_general(qr.astype(BF16), kr.astype(BF16), (((1,), (1,)), ((), ())),
                                 preferred_element_type=F32) * mask_ref[h]
            o = (jnp.dot(sc.astype(BF16), v, preferred_element_type=F32)
                 + jnp.dot((qr * qdec_ref[:, hs]).astype(BF16), s_old.astype(BF16),
                           preferred_element_type=F32))
            sret_ref[0, h] = (s_old * cdec_ref[h:h + 1, :]
                              + lax.dot_general((kr * kdec_ref[:, hs]).astype(BF16), v,
                                                (((0,), (0,)), ((), ())), preferred_element_type=F32))
            y_ref[rows, D_SSM + h * RET_HD:D_SSM + (h + 1) * RET_HD] = _head_norm_gate(o, g).astype(BF16)
        return carry

    lax.fori_loop(0, tl // RET_CHUNK, chunk_body, 0, unroll=True)

    uc = proj_ref[:, C_OFF:C_OFF + D_POOL]
    pool_scr[16:16 + tl, :] = uc
    pos = t * tl + lax.broadcasted_iota(jnp.int32, (tl, POOL_GD), 0)
    counts = _pool_counts(pos)
    parts = []
    for gi, w in enumerate(POOL_WINDOWS):
        lanes = slice(gi * POOL_GD, (gi + 1) * POOL_GD)
        acc = pool_scr[16:16 + tl, lanes]
        for j in range(1, w):
            acc = acc + pool_scr[16 - j:16 - j + tl, lanes]
        parts.append(acc / counts[gi] - uc[:, lanes])
    pooled = jnp.concatenate(parts, axis=-1).astype(BF16)
    yc = jnp.dot(pooled, wpool_ref[...], preferred_element_type=F32) * pscale_ref[...]
    y_ref[:, D_SSM + D_RET:] = yc.astype(BF16)
    pbuf_ref[0] = pool_scr[tl + 1:tl + 16, :]
    pool_scr[0:16, :] = pool_scr[tl:tl + 16, :]


def _mix_prompt(proj, n_seq, seq_len, consts, tl):
    cos_t, sin_t = consts[:2]
    whole = consts[2:]
    nt = seq_len // tl
    row_map = lambda b, t: (b * nt + t, 0)

    def const_spec(a):
        nd = a.ndim
        return pl.BlockSpec(a.shape, lambda b, t: (0,) * nd)

    in_specs = [
        pl.BlockSpec((tl, D_IN), row_map),
        pl.BlockSpec((tl, RET_HD), lambda b, t: (t, 0)),
        pl.BlockSpec((tl, RET_HD), lambda b, t: (t, 0)),
    ] + [const_spec(a) for a in whole]
    out_shape = (
        jax.ShapeDtypeStruct((n_seq * seq_len, D_MODEL), BF16),
        jax.ShapeDtypeStruct((n_seq, 1, N_STATE), F32),
        jax.ShapeDtypeStruct((n_seq, 1, N_STATE), F32),
        jax.ShapeDtypeStruct((n_seq, RET_HEADS, RET_HD, RET_HD), F32),
        jax.ShapeDtypeStruct((n_seq, POOL_BUF, D_POOL), F32),
    )
    out_specs = (
        pl.BlockSpec((tl, D_MODEL), row_map),
        pl.BlockSpec((1, 1, N_STATE), lambda b, t: (b, 0, 0)),
        pl.BlockSpec((1, 1, N_STATE), lambda b, t: (b, 0, 0)),
        pl.BlockSpec((1, RET_HEADS, RET_HD, RET_HD), lambda b, t: (b, 0, 0, 0)),
        pl.BlockSpec((1, POOL_BUF, D_POOL), lambda b, t: (b, 0, 0)),
    )
    return pl.pallas_call(
        functools.partial(_mix_prompt_kernel, tl=tl),
        grid=(n_seq, nt),
        in_specs=in_specs,
        out_specs=out_specs,
        out_shape=out_shape,
        scratch_shapes=[
            pltpu.VMEM((tl, 2 * N_STATE), F32),
            pltpu.VMEM((SUBLANES, 2 * N_STATE), F32),
            pltpu.VMEM((16 + tl, D_POOL), F32),
        ],
        compiler_params=_params("arbitrary", "arbitrary"),
        name="mix_prompt",
    )(proj, cos_t, sin_t, *whole)


def _mix_sample_kernel(proj_ref, cos_ref, sin_ref, pw_ref, bmat_ref, cmat_ref, dskip_ref, wglu_ref,
                       bglu_ref, mask_ref, qdec_ref, kdec_ref, cdec_ref, wpool_ref, pscale_ref,
                       h0re_ref, h0im_ref, s0_ref, pool0_ref,
                       y_ref, hre_ref, him_ref, sret_ref, pbuf_ref,
                       dr_scr, pool_scr, *, bs, seq_len, pos0):
    n_rows = bs * seq_len

    u = proj_ref[:, 0:D_SSM]
    _ssm_drive(u.astype(BF16), bmat_ref, dr_scr, SUBLANES, n_rows)

    def carry_fn(b, lo):
        return (h0re_ref[pl.ds(b, 1), lo:lo + SCAN_LANES], h0im_ref[pl.ds(b, 1), lo:lo + SCAN_LANES])

    def block_end_fn(b, lo, h_re, h_im):
        hre_ref[pl.ds(b, 1), lo:lo + SCAN_LANES] = h_re
        him_ref[pl.ds(b, 1), lo:lo + SCAN_LANES] = h_im

    _ssm_scan(dr_scr, pw_ref, bs, carry_fn, block_end_fn)
    y = _ssm_readout(dr_scr, cmat_ref, SUBLANES, n_rows) + dskip_ref[...] * u
    y_ref[:, 0:D_SSM] = _ssm_gate(y, wglu_ref, bglu_ref).astype(BF16)

    scale = RET_HD ** -0.5
    cosv = cos_ref[...]
    sinv = sin_ref[...]
    own = (lax.broadcasted_iota(jnp.int32, (n_rows, bs * RET_HD), 0) // seq_len
           == lax.broadcasted_iota(jnp.int32, (n_rows, bs * RET_HD), 1) // RET_HD)
    for h in range(RET_HEADS):
        hs = slice(h * RET_HD, (h + 1) * RET_HD)
        q = proj_ref[:, Q_OFF + h * RET_HD:Q_OFF + (h + 1) * RET_HD]
        k = proj_ref[:, K_OFF + h * RET_HD:K_OFF + (h + 1) * RET_HD]
        v = proj_ref[:, V_OFF + h * RET_HD:V_OFF + (h + 1) * RET_HD].astype(BF16)
        g = proj_ref[:, G_OFF + h * RET_HD:G_OFF + (h + 1) * RET_HD]
        qr = _rotary(q, cosv, sinv)
        kr = _rotary(k, cosv, sinv) * scale
        sc = lax.dot_general(qr.astype(BF16), kr.astype(BF16), (((1,), (1,)), ((), ())),
                             preferred_element_type=F32) * mask_ref[h]
        qd = jnp.where(own, jnp.concatenate([qr * qdec_ref[:, hs]] * bs, axis=1), 0.0).astype(BF16)
        kd = jnp.where(own, jnp.concatenate([kr * kdec_ref[:, hs]] * bs, axis=1), 0.0).astype(BF16)
        s_old = s0_ref[:, h].reshape(bs * RET_HD, RET_HD)
        o = (jnp.dot(sc.astype(BF16), v, preferred_element_type=F32)
             + jnp.dot(qd, s_old.astype(BF16), preferred_element_type=F32))
        s_new = (s_old * cdec_ref[h:h + 1, :]
                 + lax.dot_general(kd, v, (((0,), (0,)), ((), ())), preferred_element_type=F32))
        sret_ref[:, h] = s_new.reshape(bs, RET_HD, RET_HD)
        y_ref[:, D_SSM + h * RET_HD:D_SSM + (h + 1) * RET_HD] = _head_norm_gate(o, g).astype(BF16)

    uc = proj_ref[:, C_OFF:C_OFF + D_POOL].reshape(bs, seq_len, D_POOL)
    pool_scr[:, 1:16, :] = pool0_ref[...]
    pool_scr[:, 16:16 + seq_len, :] = uc
    pos = pos0 + lax.broadcasted_iota(jnp.int32, (1, seq_len, POOL_GD), 1)
    counts = _pool_counts(pos)
    parts = []
    for gi, w in enumerate(POOL_WINDOWS):
        lanes = slice(gi * POOL_GD, (gi + 1) * POOL_GD)
        acc = pool_scr[:, 16:16 + seq_len, lanes]
        for j in range(1, w):
            acc = acc + pool_scr[:, 16 - j:16 - j + seq_len, lanes]
        parts.append(acc / counts[gi] - uc[:, :, lanes])
    pooled = jnp.concatenate(parts, axis=-1).reshape(n_rows, D_POOL).astype(BF16)
    yc = jnp.dot(pooled, wpool_ref[...], preferred_element_type=F32) * pscale_ref[...]
    y_ref[:, D_SSM + D_RET:] = yc.astype(BF16)
    pbuf_ref[...] = pool_scr[:, seq_len + 1:seq_len + 16, :]


def _mix_sample(proj, proj_row0, states, layer, prev_out, n_seq, seq_len, consts, bs):
    n_rows = bs * seq_len

    def const_spec(a):
        nd = a.ndim
        return pl.BlockSpec(a.shape, lambda i: (0,) * nd)

    state_specs = [
        pl.BlockSpec((None, bs, N_STATE), lambda i: (layer, i, 0)),
        pl.BlockSpec((None, bs, N_STATE), lambda i: (layer, i, 0)),
        pl.BlockSpec((None, bs, RET_HEADS, RET_HD, RET_HD), lambda i: (layer, i, 0, 0, 0)),
        pl.BlockSpec((None, bs, POOL_BUF, D_POOL), lambda i: (layer, i, 0, 0)),
    ]
    assert proj_row0 % n_rows == 0
    in_specs = ([pl.BlockSpec((n_rows, D_IN), lambda i: (proj_row0 // n_rows + i, 0))]
                + [const_spec(a) for a in consts] + state_specs)
    args = [proj, *consts, *states]
    n_in = len(args)
    aliases = {}
    if prev_out is not None:
        in_specs += [pl.BlockSpec(memory_space=pl.ANY)] * len(prev_out)
        aliases = {n_in + k: 1 + k for k in range(len(prev_out))}
        args += list(prev_out)
    n_args = len(args)
    out_shape = (
        jax.ShapeDtypeStruct((n_seq * seq_len, D_MODEL), BF16),
        jax.ShapeDtypeStruct((DEPTH, n_seq, N_STATE), F32),
        jax.ShapeDtypeStruct((DEPTH, n_seq, N_STATE), F32),
        jax.ShapeDtypeStruct((DEPTH, n_seq, RET_HEADS, RET_HD, RET_HD), F32),
        jax.ShapeDtypeStruct((DEPTH, n_seq, POOL_BUF, D_POOL), F32),
    )
    out_specs = (pl.BlockSpec((n_rows, D_MODEL), lambda i: (i, 0)),) + tuple(state_specs)

    def body(*refs):
        _mix_sample_kernel(*refs[:n_in], *refs[n_args:], bs=bs, seq_len=seq_len, pos0=PAST_LEN)

    return pl.pallas_call(
        body,
        grid=(n_seq // bs,),
        in_specs=in_specs,
        out_specs=out_specs,
        out_shape=out_shape,
        scratch_shapes=[
            pltpu.VMEM((SUBLANES + n_rows, 2 * N_STATE), F32),
            pltpu.VMEM((bs, 16 + seq_len, D_POOL), F32),
        ],
        input_output_aliases=aliases,
        compiler_params=_params("arbitrary"),
        name="mix_sample",
    )(*args)


def _out_proj_kernel(y_ref, w_ref, x_ref, g1_ref, sh_ref, sc_ref, g_ref, xo_ref, h_ref, w_scr, f_scr, *, bb, tl):
    i = pl.program_id(0)

    @pl.when(i == 0)
    def _():
        w_scr[...] = w_ref[...].astype(BF16)
        f_scr[1] = jnp.zeros(f_scr.shape[1:], F32)

    def step(slot):
        f_scr[slot] = jnp.dot(y_ref[...], w_scr[...], preferred_element_type=F32)
        f = f_scr[1 - slot]
        xn = x_ref[...].reshape(bb, tl, D_MODEL) + g1_ref[...] * f.reshape(bb, tl, D_MODEL)
        xo_ref[...] = xn.reshape(bb * tl, D_MODEL)
        h = _rms(xn) * g_ref[...] * (1.0 + sc_ref[...]) + sh_ref[...]
        h_ref[...] = h.reshape(bb * tl, D_MODEL).astype(BF16)

    for parity in range(2):
        pl.when(i % 2 == parity)(functools.partial(step, parity))


def _out_proj(y, w, layer, x, x_tile0, mod, g, rows, out_rows, prev):
    last = rows.n_tiles - 1
    done = lambda i: jnp.maximum(i - 1, 0)
    out_spec = pl.BlockSpec((rows.tm, D_MODEL), lambda i: (rows.first_tile + done(i), 0))
    return _call_into(
        functools.partial(_out_proj_kernel, bb=rows.bb, tl=rows.tl),
        None if prev is None else {0: prev[0], 1: prev[1]},
        grid=(rows.n_tiles + 1,),
        in_specs=[
            pl.BlockSpec((rows.tm, D_MODEL), lambda i: (jnp.minimum(i, last), 0)),
            pl.BlockSpec((None, D_MODEL, D_MODEL), lambda i: (layer, 0, 0), pipeline_mode=pl.Buffered(1)),
            pl.BlockSpec((rows.tm, D_MODEL), lambda i: (x_tile0 + done(i), 0)),
            rows.mod_spec(layer, MOD_GATE1, done),
            rows.mod_spec(layer, MOD_SHIFT2, done),
            rows.mod_spec(layer, MOD_SCALE2, done),
            pl.BlockSpec((None, 1, D_MODEL), lambda i: (layer, 0, 0)),
        ],
        args=[y, w, x, mod, mod, mod, g],
        out_specs=(out_spec, out_spec),
        out_shape=(jax.ShapeDtypeStruct((out_rows, D_MODEL), F32),
                   jax.ShapeDtypeStruct((out_rows, D_MODEL), BF16)),
        scratch_shapes=[pltpu.VMEM((D_MODEL, D_MODEL), BF16), pltpu.VMEM((2, rows.tm, D_MODEL), F32)],
        sem=("arbitrary",),
        name="out_proj",
    )


def _conv_gate(a, b, am1, am2, cw, cb):
    conv = cb + am2 * cw[0:1]
    conv = conv + am1 * cw[1:2]
    conv = conv + a * cw[2:3]
    return jax.nn.silu(conv) * b


def _ffn_up_kernel(h_ref, wa_ref, wb_ref, cw_ref, cb_ref, st_ref, act_ref, tail_ref, ns_ref, w_scr, carry_scr,
                   *, prompt, n_seq, seq_len, n_chunks):
    c = pl.program_id(0)
    i = pl.program_id(1)

    @pl.when(i == 0)
    def _():
        w_scr[:, :FF_TILE] = wa_ref[...].astype(BF16)

    @pl.when((i == 0) & (c < n_chunks - 1))
    def _():
        w_scr[:, FF_TILE:] = wb_ref[0].astype(BF16)

    @pl.when((i == 0) & (c == n_chunks - 1))
    def _():
        w_scr[:, FF_TILE:FF_TILE + FF_LAST] = wb_ref[0, :, FF_TILE - FF_LAST:].astype(BF16)

    cw = cw_ref[...]
    cb = cb_ref[...]

    @pl.when(i >= prompt.n_tiles)
    def _():
        tl = seq_len
        sb = FF_SUB_ROWS // tl
        tok = lax.broadcasted_iota(jnp.int32, (sb, tl, FF_TILE), 1)
        for r in range(n_seq * seq_len // FF_SUB_ROWS):
            rows = slice(r * FF_SUB_ROWS, (r + 1) * FF_SUB_ROWS)
            sq = slice(r * sb, (r + 1) * sb)
            ab = jnp.dot(h_ref[rows, :], w_scr[...], preferred_element_type=F32)
            a = ab[:, :FF_TILE].reshape(sb, tl, FF_TILE)
            b = ab[:, FF_TILE:].reshape(sb, tl, FF_TILE)
            p0 = st_ref[sq, 0:1, :]
            p1 = st_ref[sq, 1:2, :]
            am1 = jnp.where(tok == 0, p1, pltpu.roll(a, 1, 1))
            am2 = jnp.where(tok == 0, p0, jnp.where(tok == 1, p1, pltpu.roll(a, 2, 1)))
            act = _conv_gate(a, b, am1, am2, cw, cb)
            act_ref[rows, :] = act.reshape(FF_SUB_ROWS, FF_TILE).astype(BF16)
            ns_ref[sq] = a[:, tl - 2:tl, :]

    @pl.when(i < prompt.n_tiles)
    def _():
        @pl.when(i % prompt.tiles_per_seq == 0)
        def _():
            carry_scr[...] = jnp.zeros((SUBLANES, FF_TILE), F32)

        tok = lax.broadcasted_iota(jnp.int32, (SUBLANES, FF_TILE), 0)
        for r in range(prompt.tm // FF_SUB_ROWS):
            rows = slice(r * FF_SUB_ROWS, (r + 1) * FF_SUB_ROWS)
            ab = jnp.dot(h_ref[rows, :], w_scr[...], preferred_element_type=F32)
            a = ab[:, :FF_TILE]
            b = ab[:, FF_TILE:]
            p0 = carry_scr[SUBLANES - 2:SUBLANES - 1, :]
            p1 = carry_scr[SUBLANES - 1:SUBLANES, :]
            r1 = pltpu.roll(a, 1, 0)
            r2 = pltpu.roll(a, 2, 0)
            head1 = jnp.where(tok == 0, p1, r1[:SUBLANES])
            head2 = jnp.where(tok == 0, p0, jnp.where(tok == 1, p1, r2[:SUBLANES]))
            am1 = jnp.concatenate([head1, r1[SUBLANES:]], axis=0)
            am2 = jnp.concatenate([head2, r2[SUBLANES:]], axis=0)
            act_ref[rows, :] = _conv_gate(a, b, am1, am2, cw, cb).astype(BF16)
            carry_scr[...] = a[FF_SUB_ROWS - SUBLANES:, :]
        tail_ref[0] = carry_scr[SUBLANES - 2:SUBLANES, :]


def _ffn_up(h, w_up, conv_w, conv_b, layer, state, prev_state, prompt, n_seq, seq_len):
    m = h.shape[0]
    tm = prompt.tm
    assert m == prompt.n_tiles * tm + n_seq * seq_len and n_seq * seq_len <= tm
    assert (n_seq * seq_len) % FF_SUB_ROWS == 0 and tm % FF_SUB_ROWS == 0
    n_tiles = prompt.n_tiles + 1
    n_chunks = pl.cdiv(D_FF, FF_TILE)
    in_specs = [
        pl.BlockSpec((tm, D_MODEL), lambda c, i: (i, 0)),
        pl.BlockSpec((None, D_MODEL, FF_TILE), lambda c, i: (layer, 0, c)),
        pl.BlockSpec((pl.Element(1), pl.Element(D_MODEL), pl.Element(FF_TILE)),
                     lambda c, i: (layer, 0, LANES * jnp.minimum((D_FF + c * FF_TILE) // LANES,
                                                                 (2 * D_FF - FF_TILE) // LANES))),
        pl.BlockSpec((None, CONV_W, FF_TILE), lambda c, i: (layer, 0, c)),
        pl.BlockSpec((None, 1, FF_TILE), lambda c, i: (layer, 0, c)),
        pl.BlockSpec((None, n_seq, CONV_W - 1, FF_TILE), lambda c, i: (layer, 0, 0, c)),
    ]
    return _call_into(
        functools.partial(_ffn_up_kernel, prompt=prompt, n_seq=n_seq, seq_len=seq_len, n_chunks=n_chunks),
        None if prev_state is None else {2: prev_state},
        grid=(n_chunks, n_tiles),
        in_specs=in_specs,
        args=[h, w_up, w_up, conv_w, conv_b, state],
        out_specs=(
            pl.BlockSpec((tm, FF_TILE), lambda c, i: (i, c)),
            pl.BlockSpec((1, CONV_W - 1, FF_TILE), lambda c, i: (jnp.minimum(i, prompt.n_tiles - 1), 0, c)),
            pl.BlockSpec((None, n_seq, CONV_W - 1, FF_TILE), lambda c, i: (layer, 0, 0, c)),
        ),
        out_shape=(jax.ShapeDtypeStruct((m, D_FF), BF16),
                   jax.ShapeDtypeStruct((prompt.n_tiles, CONV_W - 1, D_FF), F32),
                   jax.ShapeDtypeStruct((DEPTH, n_seq, CONV_W - 1, D_FF), F32)),
        scratch_shapes=[pltpu.VMEM((D_MODEL, 2 * FF_TILE), BF16), pltpu.VMEM((SUBLANES, FF_TILE), F32)],
        sem=("arbitrary", "arbitrary"),
        name="ffn_up",
    )


def _cmul(ar, ai, br, bi):
    return ar * br - ai * bi, ar * bi + ai * br


def _ssm_consts(a_re, a_im, log_dt, b_re, b_im, c_re, c_im, scan_steps):
    lam_re, lam_im = a_re.astype(F32), a_im.astype(F32)
    dt = jnp.exp(log_dt.astype(F32))[:, None]
    mag = jnp.exp(lam_re * dt)
    ab_re, ab_im = mag * jnp.cos(lam_im * dt), mag * jnp.sin(lam_im * dt)
    den = lam_re * lam_re + lam_im * lam_im
    f_re = ((ab_re - 1.0) * lam_re + ab_im * lam_im) / den
    f_im = (ab_im * lam_re - (ab_re - 1.0) * lam_im) / den
    br, bi = b_re.astype(F32), b_im.astype(F32)
    bb_re = f_re[..., None] * br - f_im[..., None] * bi
    bb_im = f_re[..., None] * bi + f_im[..., None] * br

    gh = SSM_GROUPS // 2
    eye = jnp.eye(gh, dtype=F32)

    def drive_half(m):
        return jnp.einsum("gph,gk->ghkp", m, eye).reshape(gh * SSM_GROUP, gh * SSM_STATE)

    def read_half(m):
        return jnp.einsum("ghp,gk->gpkh", m, eye).reshape(gh * SSM_STATE, gh * SSM_GROUP)

    bmat = jnp.stack([
        jnp.concatenate([drive_half(bb_re[s]), drive_half(bb_im[s])], axis=1)
        for s in (slice(0, gh), slice(gh, 2 * gh))]).astype(BF16)
    cre, cim = c_re.astype(F32), c_im.astype(F32)
    cmat = jnp.stack([
        jnp.concatenate([read_half(cre[s]), -read_half(cim[s])], axis=0)
        for s in (slice(0, gh), slice(gh, 2 * gh))]).astype(BF16)

    step_re, step_im = _power_table(ab_re.reshape(1, N_STATE), ab_im.reshape(1, N_STATE), scan_steps)
    sub_re, sub_im = _power_table(step_re[-1:], step_im[-1:], SUBLANES)
    abar = jnp.stack([step_re[:1], step_im[:1]])
    pw_block = _scan_planes(step_re[:SUBLANES], step_im[:SUBLANES])
    pw_sub = _scan_planes(sub_re, sub_im)
    return bmat, cmat, abar, pw_block, pw_sub


def _power_table(re, im, n):
    while re.shape[0] < n:
        top_re, top_im = re[-1:], im[-1:]
        more_re, more_im = _cmul(re, im, top_re, top_im)
        re, im = jnp.concatenate([re, more_re]), jnp.concatenate([im, more_im])
    return re[:n], im[:n]


def _scan_planes(re, im):
    row = jnp.arange(SUBLANES)[:, None]
    planes_re = [jnp.where(row >= s, re[s - 1][None, :], 0.0) for s in (1, 2, 4)] + [re]
    planes_im = [jnp.where(row >= s, im[s - 1][None, :], 0.0) for s in (1, 2, 4)] + [im]
    return jnp.stack([jnp.stack(planes_re), jnp.stack(planes_im)])


def _time_on_sublanes(n_rows):
    steps = n_rows // SUBLANES
    p = np.zeros((n_rows, n_rows), np.float32)
    r = np.arange(n_rows)
    p[r, (r % SUBLANES) * steps + r // SUBLANES] = 1.0
    return jnp.asarray(p, BF16), jnp.asarray(p.T, BF16)


def _rotary_tables(pos):
    half = RET_HD // 2
    inv = ROPE_BASE ** (-np.arange(half, dtype=np.float64) / half)
    ang = np.asarray(pos, np.float64)[:, None] * inv[None, :]
    cos, sin = np.cos(ang), np.sin(ang)
    return (np.concatenate([cos, cos], axis=-1).astype(np.float32),
            np.concatenate([-sin, sin], axis=-1).astype(np.float32))


def _retention_consts(chunk, n_seq):
    lg = np.log1p(-np.exp2(-5.0 - np.arange(RET_HEADS, dtype=np.float64)))
    r = np.arange(chunk * n_seq)
    ti = (r % chunk).astype(np.float64)
    seq = r // chunk
    diff = ti[:, None] - ti[None, :]
    keep = (seq[:, None] == seq[None, :]) & (diff >= 0.0)
    mask = np.where(keep[None], np.exp(np.maximum(diff, 0.0)[None] * lg[:, None, None]), 0.0)
    q_dec = np.exp((ti + 1.0)[:, None] * lg[None, :])
    k_dec = np.exp((chunk - 1.0 - ti)[:, None] * lg[None, :])
    c_dec = np.exp(chunk * lg)
    expand = lambda d: np.repeat(d, RET_HD, axis=1).astype(np.float32)
    return (mask.astype(np.float32), expand(q_dec), expand(k_dec),
            np.broadcast_to(c_dec[:, None], (RET_HEADS, RET_HD)).astype(np.float32))


def _pool_weight(w_pool):
    eye = jnp.eye(len(POOL_WINDOWS), dtype=F32)
    return jnp.einsum("gcd,gk->gckd", w_pool, eye).reshape(D_POOL, D_POOL).astype(BF16)


def kernel(x_prompt, x_sample, state_ssm_re, state_ssm_im, state_ret, state_pool, state_ffn_conv, c_prompt, c_sample, w_ada, b_ada, norm1_g, w_in, ssm_a_re, ssm_a_im, ssm_log_dt, ssm_b_re, ssm_b_im, ssm_c_re, ssm_c_im, ssm_d, ssm_w_glu, ssm_b_glu, pool_w, pool_scale, w_out, norm2_g, ffn_w_up, ffn_conv_w, ffn_conv_b, ffn_w_down, final_norm_g):
    bp, lp, _ = x_prompt.shape
    bs, ls, _ = x_sample.shape
    tl_mix = 512
    seq_mix = 8

    n_all = bp + bs
    n_pad = -(-n_all // SUBLANES) * SUBLANES
    c_all = jnp.concatenate([c_sample, c_prompt, jnp.zeros((n_pad - n_all, D_MODEL), F32)], axis=0)
    mod = _ada_mod(c_all, w_ada, b_ada)

    cos_p, sin_p = _rotary_tables(np.arange(lp))
    cos_s, sin_s = _rotary_tables(PAST_LEN + np.arange(ls))
    cos_s, sin_s = np.tile(cos_s, (seq_mix, 1)), np.tile(sin_s, (seq_mix, 1))
    assert math.gcd(lp, RET_CHUNK) == RET_CHUNK and math.gcd(ls, RET_CHUNK) == ls == SUBLANES
    ret_p = _retention_consts(RET_CHUNK, 1)
    ret_s = _retention_consts(ls, seq_mix)
    perm = _time_on_sublanes(tl_mix)

    rows_total = bp * lp + bs * ls
    groups = (_Rows(bp, lp, ROW_TILE, bs, 0), _Rows(bs, ls, ROW_TILE, 0, bp * lp))
    groups_full = (_Rows(bp, lp, ROW_TILE_FULL, bs, 0), _Rows(bs, ls, ROW_TILE_FULL // 2, 0, bp * lp))
    big_p, big_s = groups

    norm1 = norm1_g.reshape(DEPTH, 1, D_MODEL)
    norm2 = norm2_g.reshape(DEPTH, 1, D_MODEL)
    final_g = final_norm_g.reshape(1, 1, D_MODEL)
    conv_b = ffn_conv_b.reshape(DEPTH, 1, D_FF)
    sample_states = (state_ssm_re.reshape(DEPTH, bs, N_STATE), state_ssm_im.reshape(DEPTH, bs, N_STATE),
                     state_ret, state_pool)

    x_first = (x_prompt.reshape(bp * lp, D_MODEL), x_sample.reshape(bs * ls, D_MODEL))
    x_all = None

    def residual_rows(k, rows):
        return (x_first[k], 0) if x_all is None else (x_all, rows.first_tile)

    new_p = ([], [], [], [], [])
    s_mix = None
    s_conv = None
    for l in range(DEPTH):
        bmat, cmat, abar, pw_block, pw_sub = _ssm_consts(
            ssm_a_re[l], ssm_a_im[l], ssm_log_dt[l], ssm_b_re[l], ssm_b_im[l], ssm_c_re[l], ssm_c_im[l],
            tl_mix // SUBLANES)
        shared = (bmat, cmat, ssm_d[l].reshape(1, D_SSM), ssm_w_glu[l].astype(BF16),
                  ssm_b_glu[l].reshape(1, D_SSM))
        pool_c = (_pool_weight(pool_w[l]), pool_scale[l].reshape(1, D_POOL))
        pre = (mod, MOD_SHIFT1, MOD_SCALE1)

        h = None
        for k, rows in enumerate(groups):
            h = _norm(*residual_rows(k, rows), norm1, l, pre, rows, BF16, rows_total, h)
        proj = _matmul(h, w_in, l, ROW_TILE, IN_PROJ_COLS, "in_proj")
        y_p, st_re, st_im, st_ret, st_pool = _mix_prompt(
            proj, bp, lp, (cos_p, sin_p) + perm + (abar, pw_sub) + shared + ret_p + pool_c, tl_mix)
        y_s, *s_mix = _mix_sample(proj, bp * lp, sample_states, l, s_mix, bs, ls,
                                  (cos_s, sin_s, pw_block) + shared + ret_s + pool_c, seq_mix)
        xh = None
        for k, (rows, y) in enumerate(zip(groups_full, (y_p, y_s))):
            xh = _out_proj(y, w_out, l, *residual_rows(k, rows), mod, norm2, rows, rows_total, xh)
        xn, h2 = xh
        act, tails, s_conv = _ffn_up(h2, ffn_w_up, ffn_conv_w, conv_b, l, state_ffn_conv, s_conv, big_p, bs, ls)
        x_all = _matmul(act, ffn_w_down, l, ROW_TILE, FF_DOWN_COLS, "ffn_down", groups=groups,
                        residual=(xn, mod, MOD_GATE2), single_buffer_w=True)
        st_conv = tails[big_p.tiles_per_seq - 1::big_p.tiles_per_seq]
        for lst, st in zip(new_p, (st_re, st_im, st_ret, st_pool, st_conv)):
            lst.append(st)

    yp = _norm(x_all, big_p.first_tile, final_g, 0, None, big_p, F32)
    ys = _norm(x_all, big_s.first_tile, final_g, 0, None, big_s, F32)

    p_re, p_im, p_ret, p_pool, p_conv = [jnp.stack(st) for st in new_p]
    s_re, s_im, s_ret, s_pool = s_mix
    shape_p = (DEPTH, bp, SSM_GROUPS, SSM_STATE)
    shape_s = (DEPTH, bs, SSM_GROUPS, SSM_STATE)
    return (yp.reshape(bp, lp, D_MODEL), ys.reshape(bs, ls, D_MODEL),
            p_re.reshape(shape_p), p_im.reshape(shape_p), p_ret, p_pool, p_conv,
            s_re.reshape(shape_s), s_im.reshape(shape_s), s_ret, s_pool, s_conv)
```

```python
import functools
import math

import jax
import jax.numpy as jnp
import numpy as np
from jax import lax
from jax.experimental import pallas as pl
from jax.experimental.pallas import tpu as pltpu

F32 = jnp.float32
BF16 = jnp.bfloat16

D_MODEL = 2048
DEPTH = 2
PAST_LEN = 16384
D_SSM = 512
D_RET = 1024
D_POOL = 512
SSM_GROUP = 16
SSM_GROUPS = 32
SSM_STATE = 64
N_STATE = SSM_GROUPS * SSM_STATE
RET_HEADS = 8
RET_HD = 128
RET_CHUNK = 128
POOL_WINDOWS = (2, 4, 8, 16)
POOL_GD = 128
POOL_BUF = 15
D_FF = 5504
FF_TILE = 512
FF_LAST = D_FF % FF_TILE
CONV_W = 3
D_IN = D_SSM + 4 * D_RET + D_POOL
ROPE_BASE = 10000.0
EPS = 1e-6

SUBLANES = 8
LANES = 128
ROW_TILE = 1024
ROW_TILE_FULL = 512
FF_SUB_ROWS = 512
IN_PROJ_COLS = 1024
FF_DOWN_COLS = 512
ADA_COLS = 1024
MOD_SHIFT1, MOD_SCALE1, MOD_GATE1, MOD_SHIFT2, MOD_SCALE2, MOD_GATE2 = range(6)
SCAN_LANES = 512
VMEM_LIMIT = 56 * 1024 * 1024

Q_OFF = D_SSM
K_OFF = D_SSM + D_RET
V_OFF = D_SSM + 2 * D_RET
G_OFF = D_SSM + 3 * D_RET
C_OFF = D_SSM + 4 * D_RET


def _params(*sem):
    return pltpu.CompilerParams(dimension_semantics=sem, vmem_limit_bytes=VMEM_LIMIT)


def _rms(x):
    return x * lax.rsqrt(jnp.mean(x * x, axis=-1, keepdims=True) + EPS)


def _ada_kernel(c_ref, w_ref, b_ref, o_ref):
    a = jax.nn.silu(c_ref[...]).astype(BF16)
    res = jnp.dot(a, w_ref[...].astype(BF16), preferred_element_type=F32) + b_ref[...]
    for r in range(res.shape[0]):
        o_ref[r] = res[r:r + 1, :]


def _ada_mod(c_all, w_ada, b_ada):
    nseq = c_all.shape[0]
    per_vec = D_MODEL // ADA_COLS
    return pl.pallas_call(
        _ada_kernel,
        grid=(DEPTH, 6 * per_vec),
        in_specs=[
            pl.BlockSpec((nseq, D_MODEL), lambda l, j: (0, 0)),
            pl.BlockSpec((None, D_MODEL, ADA_COLS), lambda l, j: (l, 0, j)),
            pl.BlockSpec((None, 1, ADA_COLS), lambda l, j: (l, 0, j)),
        ],
        out_specs=pl.BlockSpec((None, None, nseq, 1, ADA_COLS), lambda l, j: (l, j // per_vec, 0, 0, j % per_vec)),
        out_shape=jax.ShapeDtypeStruct((DEPTH, 6, nseq, 1, D_MODEL), F32),
        compiler_params=_params("arbitrary", "arbitrary"),
        name="ada_mod",
    )(c_all, w_ada, b_ada.reshape(DEPTH, 1, 6 * D_MODEL))


class _Rows:
    def __init__(self, n_seq, seq_len, tm, first_seq, first_row):
        if seq_len >= tm:
            assert seq_len % tm == 0
            self.bb, self.tl = 1, tm
            self.tiles_per_seq = seq_len // tm
        else:
            assert tm % seq_len == 0 and seq_len == SUBLANES
            self.bb, self.tl = tm // seq_len, seq_len
            self.tiles_per_seq = 1
        assert first_seq % self.bb == 0 and first_row % tm == 0
        self.tm = tm
        self.n_tiles = n_seq * seq_len // tm
        self.first_block = first_seq // self.bb
        self.first_tile = first_row // tm

    def mod_block(self, i):
        return self.first_block + i // self.tiles_per_seq

    def mod_spec(self, layer, vec, row_tile_of, cols=D_MODEL, col_block_of=lambda *ids: 0):
        return pl.BlockSpec(
            (None, None, self.bb, 1, cols),
            lambda *ids: (layer, vec, self.mod_block(row_tile_of(*ids)), 0, col_block_of(*ids)))


def _call_into(kernel_fn, prev, *, grid, in_specs, args, out_specs, out_shape, scratch_shapes=(), sem, name):
    n_in = len(args)
    in_specs, args = list(in_specs), list(args)
    aliases = {}
    for k, arr in sorted((prev or {}).items()):
        aliases[len(args)] = k
        in_specs.append(pl.BlockSpec(memory_space=pl.ANY))
        args.append(arr)
    n_args = len(args)

    def body(*refs):
        kernel_fn(*refs[:n_in], *refs[n_args:])

    return pl.pallas_call(
        body, grid=grid, in_specs=in_specs, out_specs=out_specs, out_shape=out_shape,
        scratch_shapes=list(scratch_shapes), input_output_aliases=aliases,
        compiler_params=_params(*sem), name=name,
    )(*args)


def _norm_kernel(*refs, bb, tl, modulated):
    if modulated:
        x_ref, g_ref, sh_ref, sc_ref, o_ref = refs
    else:
        x_ref, g_ref, o_ref = refs
    h = _rms(x_ref[...].reshape(bb, tl, D_MODEL)) * g_ref[...]
    if modulated:
        h = h * (1.0 + sc_ref[...]) + sh_ref[...]
    o_ref[...] = h.reshape(bb * tl, D_MODEL).astype(o_ref.dtype)


def _norm(x, x_tile0, g, layer, mods, rows, out_dtype, out_rows=None, prev=None):
    in_specs = [pl.BlockSpec((rows.tm, D_MODEL), lambda i: (x_tile0 + i, 0)),
                pl.BlockSpec((None, 1, D_MODEL), lambda i: (layer, 0, 0))]
    args = [x, g]
    if mods is not None:
        mod, shift_vec, scale_vec = mods
        in_specs += [rows.mod_spec(layer, shift_vec, lambda i: i), rows.mod_spec(layer, scale_vec, lambda i: i)]
        args += [mod, mod]
    out_tile0 = 0 if out_rows is None else rows.first_tile
    return _call_into(
        functools.partial(_norm_kernel, bb=rows.bb, tl=rows.tl, modulated=mods is not None),
        None if prev is None else {0: prev},
        grid=(rows.n_tiles,),
        in_specs=in_specs,
        args=args,
        out_specs=pl.BlockSpec((rows.tm, D_MODEL), lambda i: (out_tile0 + i, 0)),
        out_shape=jax.ShapeDtypeStruct((out_rows or rows.n_tiles * rows.tm, D_MODEL), out_dtype),
        sem=("arbitrary",),
        name="norm",
    )


def _matmul_kernel(*refs, groups, residual):
    if residual:
        a_ref, w_ref, x_ref, *gate_refs, o_ref, w_scr = refs
    else:
        a_ref, w_ref, o_ref, w_scr = refs
    i = pl.program_id(1)

    @pl.when(i == 0)
    def _():
        w_scr[...] = w_ref[...].astype(BF16)

    f = jnp.dot(a_ref[...], w_scr[...], preferred_element_type=F32)
    if not residual:
        o_ref[...] = f
        return
    tn = f.shape[1]
    for rows, g_ref in zip(groups, gate_refs):
        @pl.when((i >= rows.first_tile) & (i < rows.first_tile + rows.n_tiles))
        def _(rows=rows, g_ref=g_ref):
            xn = x_ref[...].reshape(rows.bb, rows.tl, tn) + g_ref[...] * f.reshape(rows.bb, rows.tl, tn)
            o_ref[...] = xn.reshape(rows.tm, tn)


def _matmul(a, w, layer, tm, tn, name, groups=(), residual=None, single_buffer_w=False):
    m, k = a.shape
    n = w.shape[2]
    n_tiles = pl.cdiv(m, tm)
    if residual is not None:
        assert m == sum(rows.n_tiles for rows in groups) * tm and all(rows.tm == tm for rows in groups)
    w_mode = dict(pipeline_mode=pl.Buffered(1)) if single_buffer_w else {}
    in_specs = [
        pl.BlockSpec((tm, k), lambda j, i: (i, 0)),
        pl.BlockSpec((None, k, tn), lambda j, i: (layer, 0, j), **w_mode),
    ]
    args = [a, w]
    if residual is not None:
        x, mod, gate_vec = residual
        in_specs.append(pl.BlockSpec((tm, tn), lambda j, i: (i, j)))
        args.append(x)
        for rows in groups:
            in_specs.append(rows.mod_spec(
                layer, gate_vec,
                lambda j, i, rows=rows: jnp.clip(i - rows.first_tile, 0, rows.n_tiles - 1),
                tn, lambda j, i: j))
            args.append(mod)
    return pl.pallas_call(
        functools.partial(_matmul_kernel, groups=groups, residual=residual is not None),
        grid=(n // tn, n_tiles),
        in_specs=in_specs,
        out_specs=pl.BlockSpec((tm, tn), lambda j, i: (i, j)),
        out_shape=jax.ShapeDtypeStruct((m, n), F32),
        scratch_shapes=[pltpu.VMEM((k, tn), BF16)],
        compiler_params=_params("arbitrary", "arbitrary"),
        name=name,
    )(*args)


def _ssm_drive(ub, bmat_ref, dr_scr, row0, n_rows):
    half_u = D_SSM // 2
    half_n = N_STATE // 2
    for hf in range(2):
        d = jnp.dot(ub[:, hf * half_u:(hf + 1) * half_u], bmat_ref[hf], preferred_element_type=F32)
        dr_scr[row0:row0 + n_rows, hf * half_n:(hf + 1) * half_n] = d[:, :half_n]
        dr_scr[row0:row0 + n_rows, N_STATE + hf * half_n:N_STATE + (hf + 1) * half_n] = d[:, half_n:]


def _cmul_add(a_re, a_im, x_re, x_im, y_re, y_im):
    return y_re + (a_re * x_re - a_im * x_im), y_im + (a_re * x_im + a_im * x_re)


def _scan_tile(dr_scr, abar_ref, pws_ref, carry_scr, n_steps):
    row = lax.broadcasted_iota(jnp.int32, (SUBLANES, SCAN_LANES), 0)
    for c in range(N_STATE // SCAN_LANES):
        lo = c * SCAN_LANES
        re_cols = slice(lo, lo + SCAN_LANES)
        im_cols = slice(N_STATE + lo, N_STATE + lo + SCAN_LANES)
        a_re = jnp.broadcast_to(abar_ref[0, :, re_cols], (SUBLANES, SCAN_LANES))
        a_im = jnp.broadcast_to(abar_ref[1, :, re_cols], (SUBLANES, SCAN_LANES))

        def local_step(t, h):
            r = pl.ds(pl.multiple_of(t * SUBLANES, SUBLANES), SUBLANES)
            h_re, h_im = _cmul_add(a_re, a_im, h[0], h[1], dr_scr[r, re_cols], dr_scr[r, im_cols])
            dr_scr[r, re_cols] = h_re
            dr_scr[r, im_cols] = h_im
            return h_re, h_im

        zero = jnp.zeros((SUBLANES, SCAN_LANES), F32)
        g_re, g_im = lax.fori_loop(0, n_steps, local_step, (zero, zero), unroll=2)
        for si, s in enumerate((1, 2, 4)):
            g_re, g_im = _cmul_add(pws_ref[0, si, :, re_cols], pws_ref[1, si, :, re_cols],
                                   pltpu.roll(g_re, s, 0), pltpu.roll(g_im, s, 0), g_re, g_im)
        c_re = carry_scr[SUBLANES - 1:SUBLANES, re_cols]
        c_im = carry_scr[SUBLANES - 1:SUBLANES, im_cols]
        e_re, e_im = _cmul_add(pws_ref[0, 3, :, re_cols], pws_ref[1, 3, :, re_cols], c_re, c_im, g_re, g_im)
        carry_scr[:, re_cols] = e_re
        carry_scr[:, im_cols] = e_im
        in_re = jnp.where(row == 0, c_re, pltpu.roll(e_re, 1, 0))
        in_im = jnp.where(row == 0, c_im, pltpu.roll(e_im, 1, 0))

        def fix_step(t, w):
            w_re, w_im = _cmul(a_re, a_im, w[0], w[1])
            r = pl.ds(pl.multiple_of(t * SUBLANES, SUBLANES), SUBLANES)
            dr_scr[r, re_cols] = dr_scr[r, re_cols] + w_re
            dr_scr[r, im_cols] = dr_scr[r, im_cols] + w_im
            return w_re, w_im

        lax.fori_loop(0, n_steps, fix_step, (in_re, in_im), unroll=2)


def _scan_block(dr_scr, pw_ref, b, carry_fn, block_end_fn=None):
    r = pl.multiple_of(b * SUBLANES + SUBLANES, SUBLANES)
    for c in range(N_STATE // SCAN_LANES):
        lo = c * SCAN_LANES
        re_cols = slice(lo, lo + SCAN_LANES)
        im_cols = slice(N_STATE + lo, N_STATE + lo + SCAN_LANES)
        d_re = dr_scr[pl.ds(r, SUBLANES), re_cols]
        d_im = dr_scr[pl.ds(r, SUBLANES), im_cols]
        for si, s in enumerate((1, 2, 4)):
            p_re = pw_ref[0, si, :, re_cols]
            p_im = pw_ref[1, si, :, re_cols]
            r_re = pltpu.roll(d_re, s, 0)
            r_im = pltpu.roll(d_im, s, 0)
            d_re, d_im = (d_re + (p_re * r_re - p_im * r_im),
                          d_im + (p_re * r_im + p_im * r_re))
        c_re, c_im = carry_fn(b, lo)
        a_re = pw_ref[0, 3, :, re_cols]
        a_im = pw_ref[1, 3, :, re_cols]
        h_re = d_re + (a_re * c_re - a_im * c_im)
        h_im = d_im + (a_re * c_im + a_im * c_re)
        dr_scr[pl.ds(r, SUBLANES), re_cols] = h_re
        dr_scr[pl.ds(r, SUBLANES), im_cols] = h_im
        if block_end_fn is not None:
            block_end_fn(b, lo, h_re[SUBLANES - 1:SUBLANES], h_im[SUBLANES - 1:SUBLANES])


def _ssm_scan(dr_scr, pw_ref, n_blocks, carry_fn, block_end_fn=None):
    def body(b, carry):
        _scan_block(dr_scr, pw_ref, b, carry_fn, block_end_fn)
        return carry

    lax.fori_loop(0, n_blocks, body, 0)


def _ssm_readout(dr_scr, cmat_ref, row0, n_rows):
    half_n = N_STATE // 2
    parts = []
    for hf in range(2):
        h_re = dr_scr[row0:row0 + n_rows, hf * half_n:(hf + 1) * half_n].astype(BF16)
        h_im = dr_scr[row0:row0 + n_rows, N_STATE + hf * half_n:N_STATE + (hf + 1) * half_n].astype(BF16)
        parts.append(jnp.dot(h_re, cmat_ref[hf, :half_n], preferred_element_type=F32)
                     + jnp.dot(h_im, cmat_ref[hf, half_n:], preferred_element_type=F32))
    return jnp.concatenate(parts, axis=-1)


def _ssm_gate(y, wglu_ref, bglu_ref):
    ya = jax.nn.gelu(y)
    gate = jnp.dot(ya.astype(BF16), wglu_ref[...], preferred_element_type=F32) + bglu_ref[...]
    return ya * jax.nn.sigmoid(gate)


def _unpermute_rows(perm_t_ref, x):
    hi = x.astype(BF16)
    rest = x - hi.astype(F32)
    mid = rest.astype(BF16)
    lo = (rest - mid.astype(F32)).astype(BF16)
    n = x.shape[1]
    out = jnp.dot(perm_t_ref[...], jnp.concatenate([hi, mid, lo], axis=1), preferred_element_type=F32)
    return out[:, :n] + out[:, n:2 * n] + out[:, 2 * n:]


def _rotary(x, cosv, sinv):
    return x * cosv + pltpu.roll(x, RET_HD // 2, 1) * sinv


def _head_norm_gate(o, g):
    mu = jnp.mean(o, axis=-1, keepdims=True)
    var = jnp.mean(jnp.square(o - mu), axis=-1, keepdims=True)
    return (o - mu) * lax.rsqrt(var + EPS) * jax.nn.silu(g)


def _pool_counts(pos):
    return [jnp.minimum(pos + 1, w).astype(F32) for w in POOL_WINDOWS]


def _mix_prompt_kernel(proj_ref, cos_ref, sin_ref, perm_ref, permt_ref, abar_ref, pws_ref,
                       bmat_ref, cmat_ref, dskip_ref, wglu_ref, bglu_ref,
                       mask_ref, qdec_ref, kdec_ref, cdec_ref, wpool_ref, pscale_ref,
                       y_ref, hre_ref, him_ref, sret_ref, pbuf_ref,
                       dr_scr, carry_scr, pool_scr, *, tl):
    t = pl.program_id(1)

    @pl.when(t == 0)
    def _():
        carry_scr[...] = jnp.zeros((SUBLANES, 2 * N_STATE), F32)
        sret_ref[...] = jnp.zeros(sret_ref.shape, F32)
        pool_scr[0:16, :] = jnp.zeros((16, D_POOL), F32)

    u = proj_ref[:, 0:D_SSM]
    up = jnp.dot(perm_ref[...], u.astype(BF16), preferred_element_type=F32).astype(BF16)
    _ssm_drive(up, bmat_ref, dr_scr, 0, tl)
    _scan_tile(dr_scr, abar_ref, pws_ref, carry_scr, tl // SUBLANES)
    y = _unpermute_rows(permt_ref, _ssm_readout(dr_scr, cmat_ref, 0, tl)) + dskip_ref[...] * u
    y_ref[:, 0:D_SSM] = _ssm_gate(y, wglu_ref, bglu_ref).astype(BF16)
    hre_ref[0] = carry_scr[SUBLANES - 1:SUBLANES, 0:N_STATE]
    him_ref[0] = carry_scr[SUBLANES - 1:SUBLANES, N_STATE:2 * N_STATE]

    scale = RET_HD ** -0.5

    def chunk_body(ci, carry):
        rows = pl.ds(pl.multiple_of(ci * RET_CHUNK, RET_CHUNK), RET_CHUNK)
        cosv = cos_ref[rows, :]
        sinv = sin_ref[rows, :]
        for h in range(RET_HEADS):
            hs = slice(h * RET_HD, (h + 1) * RET_HD)
            q = proj_ref[rows, Q_OFF + h * RET_HD:Q_OFF + (h + 1) * RET_HD]
            k = proj_ref[rows, K_OFF + h * RET_HD:K_OFF + (h + 1) * RET_HD]
            v = proj_ref[rows, V_OFF + h * RET_HD:V_OFF + (h + 1) * RET_HD].astype(BF16)
            g = proj_ref[rows, G_OFF + h * RET_HD:G_OFF + (h + 1) * RET_HD]
            qr = _rotary(q, cosv, sinv)
            kr = _rotary(k, cosv, sinv) * scale
            s_old = sret_ref[0, h]
            sc = lax.dot_general(qr.astype(BF16), kr.astype(BF16), (((1,), (1,)), ((), ())),
                                 preferred_element_type=F32) * mask_ref[h]
            o = (jnp.dot(sc.astype(BF16), v, preferred_element_type=F32)
                 + jnp.dot((qr * qdec_ref[:, hs]).astype(BF16), s_old.astype(BF16),
                           preferred_element_type=F32))
            sret_ref[0, h] = (s_old * cdec_ref[h:h + 1, :]
                              + lax.dot_general((kr * kdec_ref[:, hs]).astype(BF16), v,
                                                (((0,), (0,)), ((), ())), preferred_element_type=F32))
            y_ref[rows, D_SSM + h * RET_HD:D_SSM + (h + 1) * RET_HD] = _head_norm_gate(o, g).astype(BF16)
        return carry

    lax.fori_loop(0, tl // RET_CHUNK, chunk_body, 0, unroll=True)

    uc = proj_ref[:, C_OFF:C_OFF + D_POOL]
    pool_scr[16:16 + tl, :] = uc
    pos = t * tl + lax.broadcasted_iota(jnp.int32, (tl, POOL_GD), 0)
    counts = _pool_counts(pos)
    parts = []
    for gi, w in enumerate(POOL_WINDOWS):
        lanes = slice(gi * POOL_GD, (gi + 1) * POOL_GD)
        acc = pool_scr[:, lanes]
        shift = 1
        while shift < w:
            acc = acc + pltpu.roll(acc, shift, 0)
            shift *= 2
        parts.append(acc[16:] / counts[gi] - uc[:, lanes])
    pooled = jnp.concatenate(parts, axis=-1).astype(BF16)
    yc = jnp.dot(pooled, wpool_ref[...], preferred_element_type=F32) * pscale_ref[...]
    y_ref[:, D_SSM + D_RET:] = yc.astype(BF16)
    pbuf_ref[0] = pool_scr[tl + 1:tl + 16, :]
    pool_scr[0:16, :] = pool_scr[tl:tl + 16, :]


def _mix_prompt(proj, n_seq, seq_len, consts, tl):
    cos_t, sin_t = consts[:2]
    whole = consts[2:]
    nt = seq_len // tl
    row_map = lambda b, t: (b * nt + t, 0)

    def const_spec(a):
        nd = a.ndim
        return pl.BlockSpec(a.shape, lambda b, t: (0,) * nd)

    in_specs = [
        pl.BlockSpec((tl, D_IN), row_map),
        pl.BlockSpec((tl, RET_HD), lambda b, t: (t, 0)),
        pl.BlockSpec((tl, RET_HD), lambda b, t: (t, 0)),
    ] + [const_spec(a) for a in whole]
    out_shape = (
        jax.ShapeDtypeStruct((n_seq * seq_len, D_MODEL), BF16),
        jax.ShapeDtypeStruct((n_seq, 1, N_STATE), F32),
        jax.ShapeDtypeStruct((n_seq, 1, N_STATE), F32),
        jax.ShapeDtypeStruct((n_seq, RET_HEADS, RET_HD, RET_HD), F32),
        jax.ShapeDtypeStruct((n_seq, POOL_BUF, D_POOL), F32),
    )
    out_specs = (
        pl.BlockSpec((tl, D_MODEL), row_map),
        pl.BlockSpec((1, 1, N_STATE), lambda b, t: (b, 0, 0)),
        pl.BlockSpec((1, 1, N_STATE), lambda b, t: (b, 0, 0)),
        pl.BlockSpec((1, RET_HEADS, RET_HD, RET_HD), lambda b, t: (b, 0, 0, 0)),
        pl.BlockSpec((1, POOL_BUF, D_POOL), lambda b, t: (b, 0, 0)),
    )
    return pl.pallas_call(
        functools.partial(_mix_prompt_kernel, tl=tl),
        grid=(n_seq, nt),
        in_specs=in_specs,
        out_specs=out_specs,
        out_shape=out_shape,
        scratch_shapes=[
            pltpu.VMEM((tl, 2 * N_STATE), F32),
            pltpu.VMEM((SUBLANES, 2 * N_STATE), F32),
            pltpu.VMEM((16 + tl, D_POOL), F32),
        ],
        compiler_params=_params("arbitrary", "arbitrary"),
        name="mix_prompt",
    )(proj, cos_t, sin_t, *whole)


def _mix_sample_kernel(proj_ref, cos_ref, sin_ref, pw_ref, bmat_ref, cmat_ref, dskip_ref, wglu_ref,
                       bglu_ref, mask_ref, qdec_ref, kdec_ref, cdec_ref, wpool_ref, pscale_ref,
                       h0re_ref, h0im_ref, s0_ref, pool0_ref,
                       y_ref, hre_ref, him_ref, sret_ref, pbuf_ref,
                       dr_scr, pool_scr, *, bs, seq_len, pos0):
    n_rows = bs * seq_len

    u = proj_ref[:, 0:D_SSM]
    _ssm_drive(u.astype(BF16), bmat_ref, dr_scr, SUBLANES, n_rows)

    def carry_fn(b, lo):
        return (h0re_ref[pl.ds(b, 1), lo:lo + SCAN_LANES], h0im_ref[pl.ds(b, 1), lo:lo + SCAN_LANES])

    def block_end_fn(b, lo, h_re, h_im):
        hre_ref[pl.ds(b, 1), lo:lo + SCAN_LANES] = h_re
        him_ref[pl.ds(b, 1), lo:lo + SCAN_LANES] = h_im

    _ssm_scan(dr_scr, pw_ref, bs, carry_fn, block_end_fn)
    y = _ssm_readout(dr_scr, cmat_ref, SUBLANES, n_rows) + dskip_ref[...] * u
    y_ref[:, 0:D_SSM] = _ssm_gate(y, wglu_ref, bglu_ref).astype(BF16)

    scale = RET_HD ** -0.5
    cosv = cos_ref[...]
    sinv = sin_ref[...]
    own = (lax.broadcasted_iota(jnp.int32, (n_rows, bs * RET_HD), 0) // seq_len
           == lax.broadcasted_iota(jnp.int32, (n_rows, bs * RET_HD), 1) // RET_HD)
    for h in range(RET_HEADS):
        hs = slice(h * RET_HD, (h + 1) * RET_HD)
        q = proj_ref[:, Q_OFF + h * RET_HD:Q_OFF + (h + 1) * RET_HD]
        k = proj_ref[:, K_OFF + h * RET_HD:K_OFF + (h + 1) * RET_HD]
        v = proj_ref[:, V_OFF + h * RET_HD:V_OFF + (h + 1) * RET_HD].astype(BF16)
        g = proj_ref[:, G_OFF + h * RET_HD:G_OFF + (h + 1) * RET_HD]
        qr = _rotary(q, cosv, sinv)
        kr = _rotary(k, cosv, sinv) * scale
        sc = lax.dot_general(qr.astype(BF16), kr.astype(BF16), (((1,), (1,)), ((), ())),
                             preferred_element_type=F32) * mask_ref[h]
        qd = jnp.where(own, jnp.concatenate([qr * qdec_ref[:, hs]] * bs, axis=1), 0.0).astype(BF16)
        kd = jnp.where(own, jnp.concatenate([kr * kdec_ref[:, hs]] * bs, axis=1), 0.0).astype(BF16)
        s_old = s0_ref[:, h].reshape(bs * RET_HD, RET_HD)
        o = (jnp.dot(sc.astype(BF16), v, preferred_element_type=F32)
             + jnp.dot(qd, s_old.astype(BF16), preferred_element_type=F32))
        s_new = (s_old * cdec_ref[h:h + 1, :]
                 + lax.dot_general(kd, v, (((0,), (0,)), ((), ())), preferred_element_type=F32))
        sret_ref[:, h] = s_new.reshape(bs, RET_HD, RET_HD)
        y_ref[:, D_SSM + h * RET_HD:D_SSM + (h + 1) * RET_HD] = _head_norm_gate(o, g).astype(BF16)

    uc = proj_ref[:, C_OFF:C_OFF + D_POOL].reshape(bs, seq_len, D_POOL)
    pool_scr[:, 1:16, :] = pool0_ref[...]
    pool_scr[:, 16:16 + seq_len, :] = uc
    pos = pos0 + lax.broadcasted_iota(jnp.int32, (1, seq_len, POOL_GD), 1)
    counts = _pool_counts(pos)
    parts = []
    for gi, w in enumerate(POOL_WINDOWS):
        lanes = slice(gi * POOL_GD, (gi + 1) * POOL_GD)
        acc = pool_scr[:, 16:16 + seq_len, lanes]
        for j in range(1, w):
            acc = acc + pool_scr[:, 16 - j:16 - j + seq_len, lanes]
        parts.append(acc / counts[gi] - uc[:, :, lanes])
    pooled = jnp.concatenate(parts, axis=-1).reshape(n_rows, D_POOL).astype(BF16)
    yc = jnp.dot(pooled, wpool_ref[...], preferred_element_type=F32) * pscale_ref[...]
    y_ref[:, D_SSM + D_RET:] = yc.astype(BF16)
    pbuf_ref[...] = pool_scr[:, seq_len + 1:seq_len + 16, :]


def _mix_sample(proj, proj_row0, states, layer, prev_out, n_seq, seq_len, consts, bs):
    n_rows = bs * seq_len

    def const_spec(a):
        nd = a.ndim
        return pl.BlockSpec(a.shape, lambda i: (0,) * nd)

    state_specs = [
        pl.BlockSpec((None, bs, N_STATE), lambda i: (layer, i, 0)),
        pl.BlockSpec((None, bs, N_STATE), lambda i: (layer, i, 0)),
        pl.BlockSpec((None, bs, RET_HEADS, RET_HD, RET_HD), lambda i: (layer, i, 0, 0, 0)),
        pl.BlockSpec((None, bs, POOL_BUF, D_POOL), lambda i: (layer, i, 0, 0)),
    ]
    assert proj_row0 % n_rows == 0
    in_specs = ([pl.BlockSpec((n_rows, D_IN), lambda i: (proj_row0 // n_rows + i, 0))]
                + [const_spec(a) for a in consts] + state_specs)
    args = [proj, *consts, *states]
    n_in = len(args)
    aliases = {}
    if prev_out is not None:
        in_specs += [pl.BlockSpec(memory_space=pl.ANY)] * len(prev_out)
        aliases = {n_in + k: 1 + k for k in range(len(prev_out))}
        args += list(prev_out)
    n_args = len(args)
    out_shape = (
        jax.ShapeDtypeStruct((n_seq * seq_len, D_MODEL), BF16),
        jax.ShapeDtypeStruct((DEPTH, n_seq, N_STATE), F32),
        jax.ShapeDtypeStruct((DEPTH, n_seq, N_STATE), F32),
        jax.ShapeDtypeStruct((DEPTH, n_seq, RET_HEADS, RET_HD, RET_HD), F32),
        jax.ShapeDtypeStruct((DEPTH, n_seq, POOL_BUF, D_POOL), F32),
    )
    out_specs = (pl.BlockSpec((n_rows, D_MODEL), lambda i: (i, 0)),) + tuple(state_specs)

    def body(*refs):
        _mix_sample_kernel(*refs[:n_in], *refs[n_args:], bs=bs, seq_len=seq_len, pos0=PAST_LEN)

    return pl.pallas_call(
        body,
        grid=(n_seq // bs,),
        in_specs=in_specs,
        out_specs=out_specs,
        out_shape=out_shape,
        scratch_shapes=[
            pltpu.VMEM((SUBLANES + n_rows, 2 * N_STATE), F32),
            pltpu.VMEM((bs, 16 + seq_len, D_POOL), F32),
        ],
        input_output_aliases=aliases,
        compiler_params=_params("arbitrary"),
        name="mix_sample",
    )(*args)


def _out_proj_kernel(y_ref, w_ref, x_ref, g1_ref, sh_ref, sc_ref, g_ref, xo_ref, h_ref, w_scr, *, bb, tl):
    @pl.when(pl.program_id(0) == 0)
    def _():
        w_scr[...] = w_ref[...].astype(BF16)

    f = jnp.dot(y_ref[...], w_scr[...], preferred_element_type=F32)
    xn = x_ref[...].reshape(bb, tl, D_MODEL) + g1_ref[...] * f.reshape(bb, tl, D_MODEL)
    xo_ref[...] = xn.reshape(bb * tl, D_MODEL)
    h = _rms(xn) * g_ref[...] * (1.0 + sc_ref[...]) + sh_ref[...]
    h_ref[...] = h.reshape(bb * tl, D_MODEL).astype(BF16)


def _out_proj(y, w, layer, x, x_tile0, mod, g, rows, out_rows, prev):
    out_spec = pl.BlockSpec((rows.tm, D_MODEL), lambda i: (rows.first_tile + i, 0))
    return _call_into(
        functools.partial(_out_proj_kernel, bb=rows.bb, tl=rows.tl),
        None if prev is None else {0: prev[0], 1: prev[1]},
        grid=(rows.n_tiles,),
        in_specs=[
            pl.BlockSpec((rows.tm, D_MODEL), lambda i: (i, 0)),
            pl.BlockSpec((None, D_MODEL, D_MODEL), lambda i: (layer, 0, 0), pipeline_mode=pl.Buffered(1)),
            pl.BlockSpec((rows.tm, D_MODEL), lambda i: (x_tile0 + i, 0)),
            rows.mod_spec(layer, MOD_GATE1, lambda i: i),
            rows.mod_spec(layer, MOD_SHIFT2, lambda i: i),
            rows.mod_spec(layer, MOD_SCALE2, lambda i: i),
            pl.BlockSpec((None, 1, D_MODEL), lambda i: (layer, 0, 0)),
        ],
        args=[y, w, x, mod, mod, mod, g],
        out_specs=(out_spec, out_spec),
        out_shape=(jax.ShapeDtypeStruct((out_rows, D_MODEL), F32),
                   jax.ShapeDtypeStruct((out_rows, D_MODEL), BF16)),
        scratch_shapes=[pltpu.VMEM((D_MODEL, D_MODEL), BF16)],
        sem=("arbitrary",),
        name="out_proj",
    )


def _conv_gate(a, b, am1, am2, cw, cb):
    conv = cb + am2 * cw[0:1]
    conv = conv + am1 * cw[1:2]
    conv = conv + a * cw[2:3]
    return jax.nn.silu(conv) * b


def _ffn_up_kernel(h_ref, wa_ref, wb_ref, cw_ref, cb_ref, st_ref, act_ref, tail_ref, ns_ref, w_scr, carry_scr,
                   *, prompt, n_seq, seq_len, n_chunks):
    c = pl.program_id(0)
    i = pl.program_id(1)

    @pl.when(i == 0)
    def _():
        w_scr[:, :FF_TILE] = wa_ref[...].astype(BF16)

    @pl.when((i == 0) & (c < n_chunks - 1))
    def _():
        w_scr[:, FF_TILE:] = wb_ref[0].astype(BF16)

    @pl.when((i == 0) & (c == n_chunks - 1))
    def _():
        w_scr[:, FF_TILE:FF_TILE + FF_LAST] = wb_ref[0, :, FF_TILE - FF_LAST:].astype(BF16)

    cw = cw_ref[...]
    cb = cb_ref[...]

    @pl.when(i >= prompt.n_tiles)
    def _():
        tl = seq_len
        sb = FF_SUB_ROWS // tl
        tok = lax.broadcasted_iota(jnp.int32, (sb, tl, FF_TILE), 1)
        for r in range(n_seq * seq_len // FF_SUB_ROWS):
            rows = slice(r * FF_SUB_ROWS, (r + 1) * FF_SUB_ROWS)
            sq = slice(r * sb, (r + 1) * sb)
            ab = jnp.dot(h_ref[rows, :], w_scr[...], preferred_element_type=F32)
            a = ab[:, :FF_TILE].reshape(sb, tl, FF_TILE)
            b = ab[:, FF_TILE:].reshape(sb, tl, FF_TILE)
            p0 = st_ref[sq, 0:1, :]
            p1 = st_ref[sq, 1:2, :]
            am1 = jnp.where(tok == 0, p1, pltpu.roll(a, 1, 1))
            am2 = jnp.where(tok == 0, p0, jnp.where(tok == 1, p1, pltpu.roll(a, 2, 1)))
            act = _conv_gate(a, b, am1, am2, cw, cb)
            act_ref[rows, :] = act.reshape(FF_SUB_ROWS, FF_TILE).astype(BF16)
            ns_ref[sq] = a[:, tl - 2:tl, :]

    @pl.when(i < prompt.n_tiles)
    def _():
        @pl.when(i % prompt.tiles_per_seq == 0)
        def _():
            carry_scr[...] = jnp.zeros((SUBLANES, FF_TILE), F32)

        tok = lax.broadcasted_iota(jnp.int32, (SUBLANES, FF_TILE), 0)
        for r in range(prompt.tm // FF_SUB_ROWS):
            rows = slice(r * FF_SUB_ROWS, (r + 1) * FF_SUB_ROWS)
            ab = jnp.dot(h_ref[rows, :], w_scr[...], preferred_element_type=F32)
            a = ab[:, :FF_TILE]
            b = ab[:, FF_TILE:]
            p0 = carry_scr[SUBLANES - 2:SUBLANES - 1, :]
            p1 = carry_scr[SUBLANES - 1:SUBLANES, :]
            r1 = pltpu.roll(a, 1, 0)
            r2 = pltpu.roll(a, 2, 0)
            head1 = jnp.where(tok == 0, p1, r1[:SUBLANES])
            head2 = jnp.where(tok == 0, p0, jnp.where(tok == 1, p1, r2[:SUBLANES]))
            am1 = jnp.concatenate([head1, r1[SUBLANES:]], axis=0)
            am2 = jnp.concatenate([head2, r2[SUBLANES:]], axis=0)
            act_ref[rows, :] = _conv_gate(a, b, am1, am2, cw, cb).astype(BF16)
            carry_scr[...] = a[FF_SUB_ROWS - SUBLANES:, :]
        tail_ref[0] = carry_scr[SUBLANES - 2:SUBLANES, :]


def _ffn_up(h, w_up, conv_w, conv_b, layer, state, prev_state, prompt, n_seq, seq_len):
    m = h.shape[0]
    tm = prompt.tm
    assert m == prompt.n_tiles * tm + n_seq * seq_len and n_seq * seq_len <= tm
    assert (n_seq * seq_len) % FF_SUB_ROWS == 0 and tm % FF_SUB_ROWS == 0
    n_tiles = prompt.n_tiles + 1
    n_chunks = pl.cdiv(D_FF, FF_TILE)
    in_specs = [
        pl.BlockSpec((tm, D_MODEL), lambda c, i: (i, 0)),
        pl.BlockSpec((None, D_MODEL, FF_TILE), lambda c, i: (layer, 0, c)),
        pl.BlockSpec((pl.Element(1), pl.Element(D_MODEL), pl.Element(FF_TILE)),
                     lambda c, i: (layer, 0, LANES * jnp.minimum((D_FF + c * FF_TILE) // LANES,
                                                                 (2 * D_FF - FF_TILE) // LANES))),
        pl.BlockSpec((None, CONV_W, FF_TILE), lambda c, i: (layer, 0, c)),
        pl.BlockSpec((None, 1, FF_TILE), lambda c, i: (layer, 0, c)),
        pl.BlockSpec((None, n_seq, CONV_W - 1, FF_TILE), lambda c, i: (layer, 0, 0, c)),
    ]
    return _call_into(
        functools.partial(_ffn_up_kernel, prompt=prompt, n_seq=n_seq, seq_len=seq_len, n_chunks=n_chunks),
        None if prev_state is None else {2: prev_state},
        grid=(n_chunks, n_tiles),
        in_specs=in_specs,
        args=[h, w_up, w_up, conv_w, conv_b, state],
        out_specs=(
            pl.BlockSpec((tm, FF_TILE), lambda c, i: (i, c)),
            pl.BlockSpec((1, CONV_W - 1, FF_TILE), lambda c, i: (jnp.minimum(i, prompt.n_tiles - 1), 0, c)),
            pl.BlockSpec((None, n_seq, CONV_W - 1, FF_TILE), lambda c, i: (layer, 0, 0, c)),
        ),
        out_shape=(jax.ShapeDtypeStruct((m, D_FF), BF16),
                   jax.ShapeDtypeStruct((prompt.n_tiles, CONV_W - 1, D_FF), F32),
                   jax.ShapeDtypeStruct((DEPTH, n_seq, CONV_W - 1, D_FF), F32)),
        scratch_shapes=[pltpu.VMEM((D_MODEL, 2 * FF_TILE), BF16), pltpu.VMEM((SUBLANES, FF_TILE), F32)],
        sem=("arbitrary", "arbitrary"),
        name="ffn_up",
    )


def _cmul(ar, ai, br, bi):
    return ar * br - ai * bi, ar * bi + ai * br


def _ssm_consts(a_re, a_im, log_dt, b_re, b_im, c_re, c_im, scan_steps):
    lam_re, lam_im = a_re.astype(F32), a_im.astype(F32)
    dt = jnp.exp(log_dt.astype(F32))[:, None]
    mag = jnp.exp(lam_re * dt)
    ab_re, ab_im = mag * jnp.cos(lam_im * dt), mag * jnp.sin(lam_im * dt)
    den = lam_re * lam_re + lam_im * lam_im
    f_re = ((ab_re - 1.0) * lam_re + ab_im * lam_im) / den
    f_im = (ab_im * lam_re - (ab_re - 1.0) * lam_im) / den
    br, bi = b_re.astype(F32), b_im.astype(F32)
    bb_re = f_re[..., None] * br - f_im[..., None] * bi
    bb_im = f_re[..., None] * bi + f_im[..., None] * br

    gh = SSM_GROUPS // 2
    eye = jnp.eye(gh, dtype=F32)

    def drive_half(m):
        return jnp.einsum("gph,gk->ghkp", m, eye).reshape(gh * SSM_GROUP, gh * SSM_STATE)

    def read_half(m):
        return jnp.einsum("ghp,gk->gpkh", m, eye).reshape(gh * SSM_STATE, gh * SSM_GROUP)

    bmat = jnp.stack([
        jnp.concatenate([drive_half(bb_re[s]), drive_half(bb_im[s])], axis=1)
        for s in (slice(0, gh), slice(gh, 2 * gh))]).astype(BF16)
    cre, cim = c_re.astype(F32), c_im.astype(F32)
    cmat = jnp.stack([
        jnp.concatenate([read_half(cre[s]), -read_half(cim[s])], axis=0)
        for s in (slice(0, gh), slice(gh, 2 * gh))]).astype(BF16)

    step_re, step_im = _power_table(ab_re.reshape(1, N_STATE), ab_im.reshape(1, N_STATE), scan_steps)
    sub_re, sub_im = _power_table(step_re[-1:], step_im[-1:], SUBLANES)
    abar = jnp.stack([step_re[:1], step_im[:1]])
    pw_block = _scan_planes(step_re[:SUBLANES], step_im[:SUBLANES])
    pw_sub = _scan_planes(sub_re, sub_im)
    return bmat, cmat, abar, pw_block, pw_sub


def _power_table(re, im, n):
    while re.shape[0] < n:
        top_re, top_im = re[-1:], im[-1:]
        more_re, more_im = _cmul(re, im, top_re, top_im)
        re, im = jnp.concatenate([re, more_re]), jnp.concatenate([im, more_im])
    return re[:n], im[:n]


def _scan_planes(re, im):
    row = jnp.arange(SUBLANES)[:, None]
    planes_re = [jnp.where(row >= s, re[s - 1][None, :], 0.0) for s in (1, 2, 4)] + [re]
    planes_im = [jnp.where(row >= s, im[s - 1][None, :], 0.0) for s in (1, 2, 4)] + [im]
    return jnp.stack([jnp.stack(planes_re), jnp.stack(planes_im)])


def _time_on_sublanes(n_rows):
    steps = n_rows // SUBLANES
    p = np.zeros((n_rows, n_rows), np.float32)
    r = np.arange(n_rows)
    p[r, (r % SUBLANES) * steps + r // SUBLANES] = 1.0
    return jnp.asarray(p, BF16), jnp.asarray(p.T, BF16)


def _rotary_tables(pos):
    half = RET_HD // 2
    inv = ROPE_BASE ** (-np.arange(half, dtype=np.float64) / half)
    ang = np.asarray(pos, np.float64)[:, None] * inv[None, :]
    cos, sin = np.cos(ang), np.sin(ang)
    return (np.concatenate([cos, cos], axis=-1).astype(np.float32),
            np.concatenate([-sin, sin], axis=-1).astype(np.float32))


def _retention_consts(chunk, n_seq):
    lg = np.log1p(-np.exp2(-5.0 - np.arange(RET_HEADS, dtype=np.float64)))
    r = np.arange(chunk * n_seq)
    ti = (r % chunk).astype(np.float64)
    seq = r // chunk
    diff = ti[:, None] - ti[None, :]
    keep = (seq[:, None] == seq[None, :]) & (diff >= 0.0)
    mask = np.where(keep[None], np.exp(np.maximum(diff, 0.0)[None] * lg[:, None, None]), 0.0)
    q_dec = np.exp((ti + 1.0)[:, None] * lg[None, :])
    k_dec = np.exp((chunk - 1.0 - ti)[:, None] * lg[None, :])
    c_dec = np.exp(chunk * lg)
    expand = lambda d: np.repeat(d, RET_HD, axis=1).astype(np.float32)
    return (mask.astype(np.float32), expand(q_dec), expand(k_dec),
            np.broadcast_to(c_dec[:, None], (RET_HEADS, RET_HD)).astype(np.float32))


def _pool_weight(w_pool):
    eye = jnp.eye(len(POOL_WINDOWS), dtype=F32)
    return jnp.einsum("gcd,gk->gckd", w_pool, eye).reshape(D_POOL, D_POOL).astype(BF16)


def kernel(x_prompt, x_sample, state_ssm_re, state_ssm_im, state_ret, state_pool, state_ffn_conv, c_prompt, c_sample, w_ada, b_ada, norm1_g, w_in, ssm_a_re, ssm_a_im, ssm_log_dt, ssm_b_re, ssm_b_im, ssm_c_re, ssm_c_im, ssm_d, ssm_w_glu, ssm_b_glu, pool_w, pool_scale, w_out, norm2_g, ffn_w_up, ffn_conv_w, ffn_conv_b, ffn_w_down, final_norm_g):
    bp, lp, _ = x_prompt.shape
    bs, ls, _ = x_sample.shape
    tl_mix = 512
    seq_mix = 16

    n_all = bp + bs
    n_pad = -(-n_all // SUBLANES) * SUBLANES
    c_all = jnp.concatenate([c_sample, c_prompt, jnp.zeros((n_pad - n_all, D_MODEL), F32)], axis=0)
    mod = _ada_mod(c_all, w_ada, b_ada)

    cos_p, sin_p = _rotary_tables(np.arange(lp))
    cos_s, sin_s = _rotary_tables(PAST_LEN + np.arange(ls))
    cos_s, sin_s = np.tile(cos_s, (seq_mix, 1)), np.tile(sin_s, (seq_mix, 1))
    assert math.gcd(lp, RET_CHUNK) == RET_CHUNK and math.gcd(ls, RET_CHUNK) == ls == SUBLANES
    ret_p = _retention_consts(RET_CHUNK, 1)
    ret_s = _retention_consts(ls, seq_mix)
    perm = _time_on_sublanes(tl_mix)

    rows_total = bp * lp + bs * ls
    groups = (_Rows(bp, lp, ROW_TILE, bs, 0), _Rows(bs, ls, ROW_TILE, 0, bp * lp))
    groups_full = (_Rows(bp, lp, ROW_TILE_FULL, bs, 0), _Rows(bs, ls, ROW_TILE_FULL, 0, bp * lp))
    big_p, big_s = groups

    norm1 = norm1_g.reshape(DEPTH, 1, D_MODEL)
    norm2 = norm2_g.reshape(DEPTH, 1, D_MODEL)
    final_g = final_norm_g.reshape(1, 1, D_MODEL)
    conv_b = ffn_conv_b.reshape(DEPTH, 1, D_FF)
    sample_states = (state_ssm_re.reshape(DEPTH, bs, N_STATE), state_ssm_im.reshape(DEPTH, bs, N_STATE),
                     state_ret, state_pool)

    x_first = (x_prompt.reshape(bp * lp, D_MODEL), x_sample.reshape(bs * ls, D_MODEL))
    x_all = None

    def residual_rows(k, rows):
        return (x_first[k], 0) if x_all is None else (x_all, rows.first_tile)

    new_p = ([], [], [], [], [])
    s_mix = None
    s_conv = None
    for l in range(DEPTH):
        bmat, cmat, abar, pw_block, pw_sub = _ssm_consts(
            ssm_a_re[l], ssm_a_im[l], ssm_log_dt[l], ssm_b_re[l], ssm_b_im[l], ssm_c_re[l], ssm_c_im[l],
            tl_mix // SUBLANES)
        shared = (bmat, cmat, ssm_d[l].reshape(1, D_SSM), ssm_w_glu[l].astype(BF16),
                  ssm_b_glu[l].reshape(1, D_SSM))
        pool_c = (_pool_weight(pool_w[l]), pool_scale[l].reshape(1, D_POOL))
        pre = (mod, MOD_SHIFT1, MOD_SCALE1)

        h = None
        for k, rows in enumerate(groups):
            h = _norm(*residual_rows(k, rows), norm1, l, pre, rows, BF16, rows_total, h)
        proj = _matmul(h, w_in, l, ROW_TILE, IN_PROJ_COLS, "in_proj")
        y_p, st_re, st_im, st_ret, st_pool = _mix_prompt(
            proj, bp, lp, (cos_p, sin_p) + perm + (abar, pw_sub) + shared + ret_p + pool_c, tl_mix)
        y_s, *s_mix = _mix_sample(proj, bp * lp, sample_states, l, s_mix, bs, ls,
                                  (cos_s, sin_s, pw_block) + shared + ret_s + pool_c, seq_mix)
        xh = None
        for k, (rows, y) in enumerate(zip(groups_full, (y_p, y_s))):
            xh = _out_proj(y, w_out, l, *residual_rows(k, rows), mod, norm2, rows, rows_total, xh)
        xn, h2 = xh
        act, tails, s_conv = _ffn_up(h2, ffn_w_up, ffn_conv_w, conv_b, l, state_ffn_conv, s_conv, big_p, bs, ls)
        x_all = _matmul(act, ffn_w_down, l, ROW_TILE, FF_DOWN_COLS, "ffn_down", groups=groups,
                        residual=(xn, mod, MOD_GATE2), single_buffer_w=True)
        st_conv = tails[big_p.tiles_per_seq - 1::big_p.tiles_per_seq]
        for lst, st in zip(new_p, (st_re, st_im, st_ret, st_pool, st_conv)):
            lst.append(st)

    yp = _norm(x_all, big_p.first_tile, final_g, 0, None, big_p, F32)
    ys = _norm(x_all, big_s.first_tile, final_g, 0, None, big_s, F32)

    p_re, p_im, p_ret, p_pool, p_conv = [jnp.stack(st) for st in new_p]
    s_re, s_im, s_ret, s_pool = s_mix
    shape_p = (DEPTH, bp, SSM_GROUPS, SSM_STATE)
    shape_s = (DEPTH, bs, SSM_GROUPS, SSM_STATE)
    return (yp.reshape(bp, lp, D_MODEL), ys.reshape(bs, ls, D_MODEL),
            p_re.reshape(shape_p), p_im.reshape(shape_p), p_ret, p_pool, p_conv,
            s_re.reshape(shape_s), s_im.reshape(shape_s), s_ret, s_pool, s_conv)
```

```python
import functools
import math

import jax
import jax.numpy as jnp
import numpy as np
from jax import lax
from jax.experimental import pallas as pl
from jax.experimental.pallas import tpu as pltpu

F32 = jnp.float32
BF16 = jnp.bfloat16

D_MODEL = 2048
DEPTH = 2
PAST_LEN = 16384
D_SSM = 512
D_RET = 1024
D_POOL = 512
SSM_GROUP = 16
SSM_GROUPS = 32
SSM_STATE = 64
N_STATE = SSM_GROUPS * SSM_STATE
RET_HEADS = 8
RET_HD = 128
RET_CHUNK = 128
POOL_WINDOWS = (2, 4, 8, 16)
POOL_GD = 128
POOL_BUF = 15
D_FF = 5504
FF_TILE = 512
FF_LAST = D_FF % FF_TILE
CONV_W = 3
D_IN = D_SSM + 4 * D_RET + D_POOL
ROPE_BASE = 10000.0
EPS = 1e-6

SUBLANES = 8
LANES = 128
ROW_TILE = 1024
ROW_TILE_FULL = 512
FF_SUB_ROWS = 512
IN_PROJ_COLS = 1280
FF_DOWN_COLS = 512
ADA_COLS = 1024
MOD_SHIFT1, MOD_SCALE1, MOD_GATE1, MOD_SHIFT2, MOD_SCALE2, MOD_GATE2 = range(6)
SCAN_LANES = 512
MIX_TOKENS = 512
MIX_SEQS = 16
VMEM_LIMIT = 56 * 1024 * 1024

Q_OFF = D_SSM
K_OFF = D_SSM + D_RET
V_OFF = D_SSM + 2 * D_RET
G_OFF = D_SSM + 3 * D_RET
C_OFF = D_SSM + 4 * D_RET


def _params(*sem):
    return pltpu.CompilerParams(dimension_semantics=sem, vmem_limit_bytes=VMEM_LIMIT)


def _rms(x):
    return x * lax.rsqrt(jnp.mean(x * x, axis=-1, keepdims=True) + EPS)


def _ada_kernel(c_ref, w_ref, b_ref, o_ref):
    a = jax.nn.silu(c_ref[...]).astype(BF16)
    res = jnp.dot(a, w_ref[...].astype(BF16), preferred_element_type=F32) + b_ref[...]
    for r in range(res.shape[0]):
        o_ref[r] = res[r:r + 1, :]


def _ada_mod(c_all, w_ada, b_ada):
    nseq = c_all.shape[0]
    per_vec = D_MODEL // ADA_COLS
    return pl.pallas_call(
        _ada_kernel,
        grid=(DEPTH, 6 * per_vec),
        in_specs=[
            pl.BlockSpec((nseq, D_MODEL), lambda l, j: (0, 0)),
            pl.BlockSpec((None, D_MODEL, ADA_COLS), lambda l, j: (l, 0, j)),
            pl.BlockSpec((None, 1, ADA_COLS), lambda l, j: (l, 0, j)),
        ],
        out_specs=pl.BlockSpec((None, None, nseq, 1, ADA_COLS), lambda l, j: (l, j // per_vec, 0, 0, j % per_vec)),
        out_shape=jax.ShapeDtypeStruct((DEPTH, 6, nseq, 1, D_MODEL), F32),
        compiler_params=_params("arbitrary", "arbitrary"),
        name="ada_mod",
    )(c_all, w_ada, b_ada.reshape(DEPTH, 1, 6 * D_MODEL))


class _Rows:
    def __init__(self, n_seq, seq_len, tm, first_seq, first_row):
        if seq_len >= tm:
            assert seq_len % tm == 0
            self.bb, self.tl = 1, tm
            self.tiles_per_seq = seq_len // tm
        else:
            assert tm % seq_len == 0 and seq_len == SUBLANES
            self.bb, self.tl = tm // seq_len, seq_len
            self.tiles_per_seq = 1
        assert first_seq % self.bb == 0 and first_row % tm == 0
        self.tm = tm
        self.n_tiles = n_seq * seq_len // tm
        self.first_block = first_seq // self.bb
        self.first_tile = first_row // tm

    def mod_block(self, i):
        return self.first_block + i // self.tiles_per_seq

    def mod_spec(self, layer, vec, row_tile_of, cols=D_MODEL, col_block_of=lambda *ids: 0):
        return pl.BlockSpec(
            (None, None, self.bb, 1, cols),
            lambda *ids: (layer, vec, self.mod_block(row_tile_of(*ids)), 0, col_block_of(*ids)))


def _call_into(kernel_fn, prev, *, grid, in_specs, args, out_specs, out_shape, scratch_shapes=(), sem, name):
    n_in = len(args)
    in_specs, args = list(in_specs), list(args)
    aliases = {}
    for k, arr in sorted((prev or {}).items()):
        aliases[len(args)] = k
        in_specs.append(pl.BlockSpec(memory_space=pl.ANY))
        args.append(arr)
    n_args = len(args)

    def body(*refs):
        kernel_fn(*refs[:n_in], *refs[n_args:])

    return pl.pallas_call(
        body, grid=grid, in_specs=in_specs, out_specs=out_specs, out_shape=out_shape,
        scratch_shapes=list(scratch_shapes), input_output_aliases=aliases,
        compiler_params=_params(*sem), name=name,
    )(*args)


def _norm_kernel(*refs, bb, tl, modulated):
    if modulated:
        x_ref, g_ref, sh_ref, sc_ref, o_ref = refs
    else:
        x_ref, g_ref, o_ref = refs
    h = _rms(x_ref[...].reshape(bb, tl, D_MODEL)) * g_ref[...]
    if modulated:
        h = h * (1.0 + sc_ref[...]) + sh_ref[...]
    o_ref[...] = h.reshape(bb * tl, D_MODEL).astype(o_ref.dtype)


def _norm(x, x_tile0, g, layer, mods, rows, out_dtype, out_rows=None, prev=None):
    in_specs = [pl.BlockSpec((rows.tm, D_MODEL), lambda i: (x_tile0 + i, 0)),
                pl.BlockSpec((None, 1, D_MODEL), lambda i: (layer, 0, 0))]
    args = [x, g]
    if mods is not None:
        mod, shift_vec, scale_vec = mods
        in_specs += [rows.mod_spec(layer, shift_vec, lambda i: i), rows.mod_spec(layer, scale_vec, lambda i: i)]
        args += [mod, mod]
    out_tile0 = 0 if out_rows is None else rows.first_tile
    return _call_into(
        functools.partial(_norm_kernel, bb=rows.bb, tl=rows.tl, modulated=mods is not None),
        None if prev is None else {0: prev},
        grid=(rows.n_tiles,),
        in_specs=in_specs,
        args=args,
        out_specs=pl.BlockSpec((rows.tm, D_MODEL), lambda i: (out_tile0 + i, 0)),
        out_shape=jax.ShapeDtypeStruct((out_rows or rows.n_tiles * rows.tm, D_MODEL), out_dtype),
        sem=("arbitrary",),
        name="norm",
    )


def _matmul_kernel(*refs, groups, residual):
    if residual:
        a_ref, w_ref, x_ref, *gate_refs, o_ref, w_scr = refs
    else:
        a_ref, w_ref, o_ref, w_scr = refs
    i = pl.program_id(1)

    @pl.when(i == 0)
    def _():
        w_scr[...] = w_ref[...].astype(BF16)

    f = jnp.dot(a_ref[...], w_scr[...], preferred_element_type=F32)
    if not residual:
        o_ref[...] = f
        return
    tn = f.shape[1]
    for rows, g_ref in zip(groups, gate_refs):
        @pl.when((i >= rows.first_tile) & (i < rows.first_tile + rows.n_tiles))
        def _(rows=rows, g_ref=g_ref):
            xn = x_ref[...].reshape(rows.bb, rows.tl, tn) + g_ref[...] * f.reshape(rows.bb, rows.tl, tn)
            o_ref[...] = xn.reshape(rows.tm, tn)


def _matmul(a, w, layer, tm, tn, name, groups=(), residual=None, single_buffer_w=False):
    m, k = a.shape
    n = w.shape[2]
    n_tiles = pl.cdiv(m, tm)
    if residual is not None:
        assert m == sum(rows.n_tiles for rows in groups) * tm and all(rows.tm == tm for rows in groups)
    w_mode = dict(pipeline_mode=pl.Buffered(1)) if single_buffer_w else {}
    in_specs = [
        pl.BlockSpec((tm, k), lambda j, i: (i, 0)),
        pl.BlockSpec((None, k, tn), lambda j, i: (layer, 0, j), **w_mode),
    ]
    args = [a, w]
    if residual is not None:
        x, mod, gate_vec = residual
        in_specs.append(pl.BlockSpec((tm, tn), lambda j, i: (i, j)))
        args.append(x)
        for rows in groups:
            in_specs.append(rows.mod_spec(
                layer, gate_vec,
                lambda j, i, rows=rows: jnp.clip(i - rows.first_tile, 0, rows.n_tiles - 1),
                tn, lambda j, i: j))
            args.append(mod)
    return pl.pallas_call(
        functools.partial(_matmul_kernel, groups=groups, residual=residual is not None),
        grid=(n // tn, n_tiles),
        in_specs=in_specs,
        out_specs=pl.BlockSpec((tm, tn), lambda j, i: (i, j)),
        out_shape=jax.ShapeDtypeStruct((m, n), F32),
        scratch_shapes=[pltpu.VMEM((k, tn), BF16)],
        compiler_params=_params("arbitrary", "arbitrary"),
        name=name,
    )(*args)


def _ssm_drive(ub, bmat_ref, dr_scr, row0, n_rows):
    half_u = D_SSM // 2
    half_n = N_STATE // 2
    for hf in range(2):
        d = jnp.dot(ub[:, hf * half_u:(hf + 1) * half_u], bmat_ref[hf], preferred_element_type=F32)
        dr_scr[row0:row0 + n_rows, hf * half_n:(hf + 1) * half_n] = d[:, :half_n]
        dr_scr[row0:row0 + n_rows, N_STATE + hf * half_n:N_STATE + (hf + 1) * half_n] = d[:, half_n:]


def _cmul_add(a_re, a_im, x_re, x_im, y_re, y_im):
    return y_re + (a_re * x_re - a_im * x_im), y_im + (a_re * x_im + a_im * x_re)


def _scan_tile(dr_scr, abar_ref, pws_ref, carry_scr, n_steps):
    row = lax.broadcasted_iota(jnp.int32, (SUBLANES, SCAN_LANES), 0)
    for c in range(N_STATE // SCAN_LANES):
        lo = c * SCAN_LANES
        re_cols = slice(lo, lo + SCAN_LANES)
        im_cols = slice(N_STATE + lo, N_STATE + lo + SCAN_LANES)
        a_re = jnp.broadcast_to(abar_ref[0, :, re_cols], (SUBLANES, SCAN_LANES))
        a_im = jnp.broadcast_to(abar_ref[1, :, re_cols], (SUBLANES, SCAN_LANES))

        def local_step(t, h):
            r = pl.ds(pl.multiple_of(t * SUBLANES, SUBLANES), SUBLANES)
            h_re, h_im = _cmul_add(a_re, a_im, h[0], h[1], dr_scr[r, re_cols], dr_scr[r, im_cols])
            dr_scr[r, re_cols] = h_re
            dr_scr[r, im_cols] = h_im
            return h_re, h_im

        zero = jnp.zeros((SUBLANES, SCAN_LANES), F32)
        g_re, g_im = lax.fori_loop(0, n_steps, local_step, (zero, zero), unroll=2)
        for si, s in enumerate((1, 2, 4)):
            g_re, g_im = _cmul_add(pws_ref[0, si, :, re_cols], pws_ref[1, si, :, re_cols],
                                   pltpu.roll(g_re, s, 0), pltpu.roll(g_im, s, 0), g_re, g_im)
        c_re = carry_scr[SUBLANES - 1:SUBLANES, re_cols]
        c_im = carry_scr[SUBLANES - 1:SUBLANES, im_cols]
        e_re, e_im = _cmul_add(pws_ref[0, 3, :, re_cols], pws_ref[1, 3, :, re_cols], c_re, c_im, g_re, g_im)
        carry_scr[:, re_cols] = e_re
        carry_scr[:, im_cols] = e_im
        in_re = jnp.where(row == 0, c_re, pltpu.roll(e_re, 1, 0))
        in_im = jnp.where(row == 0, c_im, pltpu.roll(e_im, 1, 0))

        def fix_step(t, w):
            w_re, w_im = _cmul(a_re, a_im, w[0], w[1])
            r = pl.ds(pl.multiple_of(t * SUBLANES, SUBLANES), SUBLANES)
            dr_scr[r, re_cols] = dr_scr[r, re_cols] + w_re
            dr_scr[r, im_cols] = dr_scr[r, im_cols] + w_im
            return w_re, w_im

        lax.fori_loop(0, n_steps, fix_step, (in_re, in_im), unroll=2)


def _scan_block(dr_scr, pw_ref, b, carry_fn, block_end_fn=None):
    r = pl.multiple_of(b * SUBLANES + SUBLANES, SUBLANES)
    for c in range(N_STATE // SCAN_LANES):
        lo = c * SCAN_LANES
        re_cols = slice(lo, lo + SCAN_LANES)
        im_cols = slice(N_STATE + lo, N_STATE + lo + SCAN_LANES)
        d_re = dr_scr[pl.ds(r, SUBLANES), re_cols]
        d_im = dr_scr[pl.ds(r, SUBLANES), im_cols]
        for si, s in enumerate((1, 2, 4)):
            p_re = pw_ref[0, si, :, re_cols]
            p_im = pw_ref[1, si, :, re_cols]
            r_re = pltpu.roll(d_re, s, 0)
            r_im = pltpu.roll(d_im, s, 0)
            d_re, d_im = (d_re + (p_re * r_re - p_im * r_im),
                          d_im + (p_re * r_im + p_im * r_re))
        c_re, c_im = carry_fn(b, lo)
        a_re = pw_ref[0, 3, :, re_cols]
        a_im = pw_ref[1, 3, :, re_cols]
        h_re = d_re + (a_re * c_re - a_im * c_im)
        h_im = d_im + (a_re * c_im + a_im * c_re)
        dr_scr[pl.ds(r, SUBLANES), re_cols] = h_re
        dr_scr[pl.ds(r, SUBLANES), im_cols] = h_im
        if block_end_fn is not None:
            block_end_fn(b, lo, h_re[SUBLANES - 1:SUBLANES], h_im[SUBLANES - 1:SUBLANES])


def _ssm_scan(dr_scr, pw_ref, n_blocks, carry_fn, block_end_fn=None):
    def body(b, carry):
        _scan_block(dr_scr, pw_ref, b, carry_fn, block_end_fn)
        return carry

    lax.fori_loop(0, n_blocks, body, 0)


def _ssm_readout(dr_scr, cmat_ref, row0, n_rows):
    half_n = N_STATE // 2
    parts = []
    for hf in range(2):
        h_re = dr_scr[row0:row0 + n_rows, hf * half_n:(hf + 1) * half_n].astype(BF16)
        h_im = dr_scr[row0:row0 + n_rows, N_STATE + hf * half_n:N_STATE + (hf + 1) * half_n].astype(BF16)
        parts.append(jnp.dot(h_re, cmat_ref[hf, :half_n], preferred_element_type=F32)
                     + jnp.dot(h_im, cmat_ref[hf, half_n:], preferred_element_type=F32))
    return jnp.concatenate(parts, axis=-1)


def _ssm_gate(y, wglu_ref, bglu_ref):
    ya = jax.nn.gelu(y)
    gate = jnp.dot(ya.astype(BF16), wglu_ref[...], preferred_element_type=F32) + bglu_ref[...]
    return ya * jax.nn.sigmoid(gate)


def _unpermute_rows(perm_t_ref, x):
    hi = x.astype(BF16)
    rest = x - hi.astype(F32)
    mid = rest.astype(BF16)
    lo = (rest - mid.astype(F32)).astype(BF16)
    n = x.shape[1]
    out = jnp.dot(perm_t_ref[...], jnp.concatenate([hi, mid, lo], axis=1), preferred_element_type=F32)
    return out[:, :n] + out[:, n:2 * n] + out[:, 2 * n:]


def _rotary(x, cosv, sinv):
    return x * cosv + pltpu.roll(x, RET_HD // 2, 1) * sinv


def _head_norm_gate(o, g):
    mu = jnp.mean(o, axis=-1, keepdims=True)
    var = jnp.mean(jnp.square(o - mu), axis=-1, keepdims=True)
    return (o - mu) * lax.rsqrt(var + EPS) * jax.nn.silu(g)


def _pool_counts(pos):
    return [jnp.minimum(pos + 1, w).astype(F32) for w in POOL_WINDOWS]


def _mix_prompt_kernel(proj_ref, cos_ref, sin_ref, perm_ref, permt_ref, abar_ref, pws_ref,
                       bmat_ref, cmat_ref, dskip_ref, wglu_ref, bglu_ref,
                       mask_ref, qdec_ref, kdec_ref, cdec_ref, wpool_ref, pscale_ref,
                       y_ref, hre_ref, him_ref, sret_ref, pbuf_ref,
                       dr_scr, carry_scr, pool_scr, *, tl):
    t = pl.program_id(1)

    @pl.when(t == 0)
    def _():
        carry_scr[...] = jnp.zeros((SUBLANES, 2 * N_STATE), F32)
        sret_ref[...] = jnp.zeros(sret_ref.shape, F32)
        pool_scr[0:16, :] = jnp.zeros((16, D_POOL), F32)

    u = proj_ref[:, 0:D_SSM]
    up = jnp.dot(perm_ref[...], u.astype(BF16), preferred_element_type=F32).astype(BF16)
    _ssm_drive(up, bmat_ref, dr_scr, 0, tl)
    _scan_tile(dr_scr, abar_ref, pws_ref, carry_scr, tl // SUBLANES)
    y = _unpermute_rows(permt_ref, _ssm_readout(dr_scr, cmat_ref, 0, tl)) + dskip_ref[...] * u
    y_ref[:, 0:D_SSM] = _ssm_gate(y, wglu_ref, bglu_ref).astype(BF16)
    hre_ref[0] = carry_scr[SUBLANES - 1:SUBLANES, 0:N_STATE]
    him_ref[0] = carry_scr[SUBLANES - 1:SUBLANES, N_STATE:2 * N_STATE]

    scale = RET_HD ** -0.5

    def chunk_body(ci, carry):
        rows = pl.ds(pl.multiple_of(ci * RET_CHUNK, RET_CHUNK), RET_CHUNK)
        cosv = cos_ref[rows, :]
        sinv = sin_ref[rows, :]
        for h in range(RET_HEADS):
            hs = slice(h * RET_HD, (h + 1) * RET_HD)
            q = proj_ref[rows, Q_OFF + h * RET_HD:Q_OFF + (h + 1) * RET_HD]
            k = proj_ref[rows, K_OFF + h * RET_HD:K_OFF + (h + 1) * RET_HD]
            v = proj_ref[rows, V_OFF + h * RET_HD:V_OFF + (h + 1) * RET_HD].astype(BF16)
            g = proj_ref[rows, G_OFF + h * RET_HD:G_OFF + (h + 1) * RET_HD]
            qr = _rotary(q, cosv, sinv)
            kr = _rotary(k, cosv, sinv) * scale
            s_old = sret_ref[0, h]
            sc = lax.dot_general(qr.astype(BF16), kr.astype(BF16), (((1,), (1,)), ((), ())),
                                 preferred_element_type=F32) * mask_ref[h]
            o = (jnp.dot(sc.astype(BF16), v, preferred_element_type=F32)
                 + jnp.dot((qr * qdec_ref[:, hs]).astype(BF16), s_old.astype(BF16),
                           preferred_element_type=F32))
            sret_ref[0, h] = (s_old * cdec_ref[h:h + 1, :]
                              + lax.dot_general((kr * kdec_ref[:, hs]).astype(BF16), v,
                                                (((0,), (0,)), ((), ())), preferred_element_type=F32))
            y_ref[rows, D_SSM + h * RET_HD:D_SSM + (h + 1) * RET_HD] = _head_norm_gate(o, g).astype(BF16)
        return carry

    lax.fori_loop(0, tl // RET_CHUNK, chunk_body, 0, unroll=True)

    uc = proj_ref[:, C_OFF:C_OFF + D_POOL]
    pool_scr[16:16 + tl, :] = uc
    pos = t * tl + lax.broadcasted_iota(jnp.int32, (tl, POOL_GD), 0)
    counts = _pool_counts(pos)
    parts = []
    for gi, w in enumerate(POOL_WINDOWS):
        lanes = slice(gi * POOL_GD, (gi + 1) * POOL_GD)
        acc = pool_scr[:, lanes]
        shift = 1
        while shift < w:
            acc = acc + pltpu.roll(acc, shift, 0)
            shift *= 2
        parts.append(acc[16:] / counts[gi] - uc[:, lanes])
    pooled = jnp.concatenate(parts, axis=-1).astype(BF16)
    yc = jnp.dot(pooled, wpool_ref[...], preferred_element_type=F32) * pscale_ref[...]
    y_ref[:, D_SSM + D_RET:] = yc.astype(BF16)
    pbuf_ref[0] = pool_scr[tl + 1:tl + 16, :]
    pool_scr[0:16, :] = pool_scr[tl:tl + 16, :]


def _mix_prompt(proj, n_seq, seq_len, consts, tl):
    cos_t, sin_t = consts[:2]
    whole = consts[2:]
    nt = seq_len // tl
    row_map = lambda b, t: (b * nt + t, 0)

    def const_spec(a):
        nd = a.ndim
        return pl.BlockSpec(a.shape, lambda b, t: (0,) * nd)

    in_specs = [
        pl.BlockSpec((tl, D_IN), row_map),
        pl.BlockSpec((tl, RET_HD), lambda b, t: (t, 0)),
        pl.BlockSpec((tl, RET_HD), lambda b, t: (t, 0)),
    ] + [const_spec(a) for a in whole]
    out_shape = (
        jax.ShapeDtypeStruct((n_seq * seq_len, D_MODEL), BF16),
        jax.ShapeDtypeStruct((n_seq, 1, N_STATE), F32),
        jax.ShapeDtypeStruct((n_seq, 1, N_STATE), F32),
        jax.ShapeDtypeStruct((n_seq, RET_HEADS, RET_HD, RET_HD), F32),
        jax.ShapeDtypeStruct((n_seq, POOL_BUF, D_POOL), F32),
    )
    out_specs = (
        pl.BlockSpec((tl, D_MODEL), row_map),
        pl.BlockSpec((1, 1, N_STATE), lambda b, t: (b, 0, 0)),
        pl.BlockSpec((1, 1, N_STATE), lambda b, t: (b, 0, 0)),
        pl.BlockSpec((1, RET_HEADS, RET_HD, RET_HD), lambda b, t: (b, 0, 0, 0)),
        pl.BlockSpec((1, POOL_BUF, D_POOL), lambda b, t: (b, 0, 0)),
    )
    return pl.pallas_call(
        functools.partial(_mix_prompt_kernel, tl=tl),
        grid=(n_seq, nt),
        in_specs=in_specs,
        out_specs=out_specs,
        out_shape=out_shape,
        scratch_shapes=[
            pltpu.VMEM((tl, 2 * N_STATE), F32),
            pltpu.VMEM((SUBLANES, 2 * N_STATE), F32),
            pltpu.VMEM((16 + tl, D_POOL), F32),
        ],
        compiler_params=_params("arbitrary", "arbitrary"),
        name="mix_prompt",
    )(proj, cos_t, sin_t, *whole)


def _mix_sample_kernel(proj_ref, cos_ref, sin_ref, pw_ref, bmat_ref, cmat_ref, dskip_ref, wglu_ref,
                       bglu_ref, mask_ref, qdec_ref, kdec_ref, cdec_ref, wpool_ref, pscale_ref,
                       h0re_ref, h0im_ref, s0_ref, pool0_ref,
                       y_ref, hre_ref, him_ref, sret_ref, pbuf_ref,
                       dr_scr, pool_scr, *, bs, seq_len, pos0):
    n_rows = bs * seq_len

    u = proj_ref[:, 0:D_SSM]
    _ssm_drive(u.astype(BF16), bmat_ref, dr_scr, SUBLANES, n_rows)

    def carry_fn(b, lo):
        return (h0re_ref[pl.ds(b, 1), lo:lo + SCAN_LANES], h0im_ref[pl.ds(b, 1), lo:lo + SCAN_LANES])

    def block_end_fn(b, lo, h_re, h_im):
        hre_ref[pl.ds(b, 1), lo:lo + SCAN_LANES] = h_re
        him_ref[pl.ds(b, 1), lo:lo + SCAN_LANES] = h_im

    _ssm_scan(dr_scr, pw_ref, bs, carry_fn, block_end_fn)
    y = _ssm_readout(dr_scr, cmat_ref, SUBLANES, n_rows) + dskip_ref[...] * u
    y_ref[:, 0:D_SSM] = _ssm_gate(y, wglu_ref, bglu_ref).astype(BF16)

    scale = RET_HD ** -0.5
    cosv = cos_ref[...]
    sinv = sin_ref[...]
    own = (lax.broadcasted_iota(jnp.int32, (n_rows, bs * RET_HD), 0) // seq_len
           == lax.broadcasted_iota(jnp.int32, (n_rows, bs * RET_HD), 1) // RET_HD)
    for h in range(RET_HEADS):
        hs = slice(h * RET_HD, (h + 1) * RET_HD)
        q = proj_ref[:, Q_OFF + h * RET_HD:Q_OFF + (h + 1) * RET_HD]
        k = proj_ref[:, K_OFF + h * RET_HD:K_OFF + (h + 1) * RET_HD]
        v = proj_ref[:, V_OFF + h * RET_HD:V_OFF + (h + 1) * RET_HD].astype(BF16)
        g = proj_ref[:, G_OFF + h * RET_HD:G_OFF + (h + 1) * RET_HD]
        qr = _rotary(q, cosv, sinv)
        kr = _rotary(k, cosv, sinv) * scale
        sc = lax.dot_general(qr.astype(BF16), kr.astype(BF16), (((1,), (1,)), ((), ())),
                             preferred_element_type=F32) * mask_ref[h]
        qd = jnp.where(own, jnp.concatenate([qr * qdec_ref[:, hs]] * bs, axis=1), 0.0).astype(BF16)
        kd = jnp.where(own, jnp.concatenate([kr * kdec_ref[:, hs]] * bs, axis=1), 0.0).astype(BF16)
        s_old = s0_ref[:, h].reshape(bs * RET_HD, RET_HD)
        o = (jnp.dot(sc.astype(BF16), v, preferred_element_type=F32)
             + jnp.dot(qd, s_old.astype(BF16), preferred_element_type=F32))
        s_new = (s_old * cdec_ref[h:h + 1, :]
                 + lax.dot_general(kd, v, (((0,), (0,)), ((), ())), preferred_element_type=F32))
        sret_ref[:, h] = s_new.reshape(bs, RET_HD, RET_HD)
        y_ref[:, D_SSM + h * RET_HD:D_SSM + (h + 1) * RET_HD] = _head_norm_gate(o, g).astype(BF16)

    uc = proj_ref[:, C_OFF:C_OFF + D_POOL].reshape(bs, seq_len, D_POOL)
    pool_scr[:, 1:16, :] = pool0_ref[...]
    pool_scr[:, 16:16 + seq_len, :] = uc
    pos = pos0 + lax.broadcasted_iota(jnp.int32, (1, seq_len, POOL_GD), 1)
    counts = _pool_counts(pos)
    parts = []
    for gi, w in enumerate(POOL_WINDOWS):
        lanes = slice(gi * POOL_GD, (gi + 1) * POOL_GD)
        acc = pool_scr[:, 16:16 + seq_len, lanes]
        for j in range(1, w):
            acc = acc + pool_scr[:, 16 - j:16 - j + seq_len, lanes]
        parts.append(acc / counts[gi] - uc[:, :, lanes])
    pooled = jnp.concatenate(parts, axis=-1).reshape(n_rows, D_POOL).astype(BF16)
    yc = jnp.dot(pooled, wpool_ref[...], preferred_element_type=F32) * pscale_ref[...]
    y_ref[:, D_SSM + D_RET:] = yc.astype(BF16)
    pbuf_ref[...] = pool_scr[:, seq_len + 1:seq_len + 16, :]


def _mix_sample(proj, proj_row0, states, layer, prev_out, n_seq, seq_len, consts, bs):
    n_rows = bs * seq_len

    def const_spec(a):
        nd = a.ndim
        return pl.BlockSpec(a.shape, lambda i: (0,) * nd)

    state_specs = [
        pl.BlockSpec((None, bs, N_STATE), lambda i: (layer, i, 0)),
        pl.BlockSpec((None, bs, N_STATE), lambda i: (layer, i, 0)),
        pl.BlockSpec((None, bs, RET_HEADS, RET_HD, RET_HD), lambda i: (layer, i, 0, 0, 0)),
        pl.BlockSpec((None, bs, POOL_BUF, D_POOL), lambda i: (layer, i, 0, 0)),
    ]
    assert proj_row0 % n_rows == 0
    in_specs = ([pl.BlockSpec((n_rows, D_IN), lambda i: (proj_row0 // n_rows + i, 0))]
                + [const_spec(a) for a in consts] + state_specs)
    args = [proj, *consts, *states]
    n_in = len(args)
    aliases = {}
    if prev_out is not None:
        in_specs += [pl.BlockSpec(memory_space=pl.ANY)] * len(prev_out)
        aliases = {n_in + k: 1 + k for k in range(len(prev_out))}
        args += list(prev_out)
    n_args = len(args)
    out_shape = (
        jax.ShapeDtypeStruct((n_seq * seq_len, D_MODEL), BF16),
        jax.ShapeDtypeStruct((DEPTH, n_seq, N_STATE), F32),
        jax.ShapeDtypeStruct((DEPTH, n_seq, N_STATE), F32),
        jax.ShapeDtypeStruct((DEPTH, n_seq, RET_HEADS, RET_HD, RET_HD), F32),
        jax.ShapeDtypeStruct((DEPTH, n_seq, POOL_BUF, D_POOL), F32),
    )
    out_specs = (pl.BlockSpec((n_rows, D_MODEL), lambda i: (i, 0)),) + tuple(state_specs)

    def body(*refs):
        _mix_sample_kernel(*refs[:n_in], *refs[n_args:], bs=bs, seq_len=seq_len, pos0=PAST_LEN)

    return pl.pallas_call(
        body,
        grid=(n_seq // bs,),
        in_specs=in_specs,
        out_specs=out_specs,
        out_shape=out_shape,
        scratch_shapes=[
            pltpu.VMEM((SUBLANES + n_rows, 2 * N_STATE), F32),
            pltpu.VMEM((bs, 16 + seq_len, D_POOL), F32),
        ],
        input_output_aliases=aliases,
        compiler_params=_params("arbitrary"),
        name="mix_sample",
    )(*args)


def _out_proj_kernel(y_ref, w_ref, x_ref, g1_ref, sh_ref, sc_ref, g_ref, xo_ref, h_ref, w_scr, *, bb, tl):
    @pl.when(pl.program_id(0) == 0)
    def _():
        w_scr[...] = w_ref[...].astype(BF16)

    f = jnp.dot(y_ref[...], w_scr[...], preferred_element_type=F32)
    xn = x_ref[...].reshape(bb, tl, D_MODEL) + g1_ref[...] * f.reshape(bb, tl, D_MODEL)
    xo_ref[...] = xn.reshape(bb * tl, D_MODEL)
    h = _rms(xn) * g_ref[...] * (1.0 + sc_ref[...]) + sh_ref[...]
    h_ref[...] = h.reshape(bb * tl, D_MODEL).astype(BF16)


def _out_proj(y, w, layer, x, x_tile0, mod, g, rows, out_rows, prev):
    out_spec = pl.BlockSpec((rows.tm, D_MODEL), lambda i: (rows.first_tile + i, 0))
    return _call_into(
        functools.partial(_out_proj_kernel, bb=rows.bb, tl=rows.tl),
        None if prev is None else {0: prev[0], 1: prev[1]},
        grid=(rows.n_tiles,),
        in_specs=[
            pl.BlockSpec((rows.tm, D_MODEL), lambda i: (i, 0)),
            pl.BlockSpec((None, D_MODEL, D_MODEL), lambda i: (layer, 0, 0), pipeline_mode=pl.Buffered(1)),
            pl.BlockSpec((rows.tm, D_MODEL), lambda i: (x_tile0 + i, 0)),
            rows.mod_spec(layer, MOD_GATE1, lambda i: i),
            rows.mod_spec(layer, MOD_SHIFT2, lambda i: i),
            rows.mod_spec(layer, MOD_SCALE2, lambda i: i),
            pl.BlockSpec((None, 1, D_MODEL), lambda i: (layer, 0, 0)),
        ],
        args=[y, w, x, mod, mod, mod, g],
        out_specs=(out_spec, out_spec),
        out_shape=(jax.ShapeDtypeStruct((out_rows, D_MODEL), F32),
                   jax.ShapeDtypeStruct((out_rows, D_MODEL), BF16)),
        scratch_shapes=[pltpu.VMEM((D_MODEL, D_MODEL), BF16)],
        sem=("arbitrary",),
        name="out_proj",
    )


def _conv_gate(a, b, am1, am2, cw, cb):
    conv = cb + am2 * cw[0:1]
    conv = conv + am1 * cw[1:2]
    conv = conv + a * cw[2:3]
    return jax.nn.silu(conv) * b


def _ffn_up_kernel(h_ref, wa_ref, wb_ref, cw_ref, cb_ref, st_ref, act_ref, tail_ref, ns_ref, w_scr, carry_scr,
                   *, prompt, n_seq, seq_len, n_chunks):
    c = pl.program_id(0)
    i = pl.program_id(1)

    @pl.when(i == 0)
    def _():
        w_scr[:, :FF_TILE] = wa_ref[...].astype(BF16)

    @pl.when((i == 0) & (c < n_chunks - 1))
    def _():
        w_scr[:, FF_TILE:] = wb_ref[0].astype(BF16)

    @pl.when((i == 0) & (c == n_chunks - 1))
    def _():
        w_scr[:, FF_TILE:FF_TILE + FF_LAST] = wb_ref[0, :, FF_TILE - FF_LAST:].astype(BF16)

    cw = cw_ref[...]
    cb = cb_ref[...]

    @pl.when(i >= prompt.n_tiles)
    def _():
        tl = seq_len
        sb = FF_SUB_ROWS // tl
        tok = lax.broadcasted_iota(jnp.int32, (sb, tl, FF_TILE), 1)
        for r in range(n_seq * seq_len // FF_SUB_ROWS):
            rows = slice(r * FF_SUB_ROWS, (r + 1) * FF_SUB_ROWS)
            sq = slice(r * sb, (r + 1) * sb)
            ab = jnp.dot(h_ref[rows, :], w_scr[...], preferred_element_type=F32)
            a = ab[:, :FF_TILE].reshape(sb, tl, FF_TILE)
            b = ab[:, FF_TILE:].reshape(sb, tl, FF_TILE)
            p0 = st_ref[sq, 0:1, :]
            p1 = st_ref[sq, 1:2, :]
            am1 = jnp.where(tok == 0, p1, pltpu.roll(a, 1, 1))
            am2 = jnp.where(tok == 0, p0, jnp.where(tok == 1, p1, pltpu.roll(a, 2, 1)))
            act = _conv_gate(a, b, am1, am2, cw, cb)
            act_ref[rows, :] = act.reshape(FF_SUB_ROWS, FF_TILE).astype(BF16)
            ns_ref[sq] = a[:, tl - 2:tl, :]

    @pl.when(i < prompt.n_tiles)
    def _():
        @pl.when(i % prompt.tiles_per_seq == 0)
        def _():
            carry_scr[...] = jnp.zeros((SUBLANES, FF_TILE), F32)

        tok = lax.broadcasted_iota(jnp.int32, (SUBLANES, FF_TILE), 0)
        for r in range(prompt.tm // FF_SUB_ROWS):
            rows = slice(r * FF_SUB_ROWS, (r + 1) * FF_SUB_ROWS)
            ab = jnp.dot(h_ref[rows, :], w_scr[...], preferred_element_type=F32)
            a = ab[:, :FF_TILE]
            b = ab[:, FF_TILE:]
            p0 = carry_scr[SUBLANES - 2:SUBLANES - 1, :]
            p1 = carry_scr[SUBLANES - 1:SUBLANES, :]
            r1 = pltpu.roll(a, 1, 0)
            r2 = pltpu.roll(a, 2, 0)
            head1 = jnp.where(tok == 0, p1, r1[:SUBLANES])
            head2 = jnp.where(tok == 0, p0, jnp.where(tok == 1, p1, r2[:SUBLANES]))
            am1 = jnp.concatenate([head1, r1[SUBLANES:]], axis=0)
            am2 = jnp.concatenate([head2, r2[SUBLANES:]], axis=0)
            act_ref[rows, :] = _conv_gate(a, b, am1, am2, cw, cb).astype(BF16)
            carry_scr[...] = a[FF_SUB_ROWS - SUBLANES:, :]
        tail_ref[0] = carry_scr[SUBLANES - 2:SUBLANES, :]


def _ffn_up(h, w_up, conv_w, conv_b, layer, state, prev_state, prompt, n_seq, seq_len):
    m = h.shape[0]
    tm = prompt.tm
    assert m == prompt.n_tiles * tm + n_seq * seq_len and n_seq * seq_len <= tm
    assert (n_seq * seq_len) % FF_SUB_ROWS == 0 and tm % FF_SUB_ROWS == 0
    n_tiles = prompt.n_tiles + 1
    n_chunks = pl.cdiv(D_FF, FF_TILE)
    in_specs = [
        pl.BlockSpec((tm, D_MODEL), lambda c, i: (i, 0)),
        pl.BlockSpec((None, D_MODEL, FF_TILE), lambda c, i: (layer, 0, c)),
        pl.BlockSpec((pl.Element(1), pl.Element(D_MODEL), pl.Element(FF_TILE)),
                     lambda c, i: (layer, 0, LANES * jnp.minimum((D_FF + c * FF_TILE) // LANES,
                                                                 (2 * D_FF - FF_TILE) // LANES))),
        pl.BlockSpec((None, CONV_W, FF_TILE), lambda c, i: (layer, 0, c)),
        pl.BlockSpec((None, 1, FF_TILE), lambda c, i: (layer, 0, c)),
        pl.BlockSpec((None, n_seq, CONV_W - 1, FF_TILE), lambda c, i: (layer, 0, 0, c)),
    ]
    return _call_into(
        functools.partial(_ffn_up_kernel, prompt=prompt, n_seq=n_seq, seq_len=seq_len, n_chunks=n_chunks),
        None if prev_state is None else {2: prev_state},
        grid=(n_chunks, n_tiles),
        in_specs=in_specs,
        args=[h, w_up, w_up, conv_w, conv_b, state],
        out_specs=(
            pl.BlockSpec((tm, FF_TILE), lambda c, i: (i, c)),
            pl.BlockSpec((1, CONV_W - 1, FF_TILE), lambda c, i: (jnp.minimum(i, prompt.n_tiles - 1), 0, c)),
            pl.BlockSpec((None, n_seq, CONV_W - 1, FF_TILE), lambda c, i: (layer, 0, 0, c)),
        ),
        out_shape=(jax.ShapeDtypeStruct((m, D_FF), BF16),
                   jax.ShapeDtypeStruct((prompt.n_tiles, CONV_W - 1, D_FF), F32),
                   jax.ShapeDtypeStruct((DEPTH, n_seq, CONV_W - 1, D_FF), F32)),
        scratch_shapes=[pltpu.VMEM((D_MODEL, 2 * FF_TILE), BF16), pltpu.VMEM((SUBLANES, FF_TILE), F32)],
        sem=("arbitrary", "arbitrary"),
        name="ffn_up",
    )


def _cmul(ar, ai, br, bi):
    return ar * br - ai * bi, ar * bi + ai * br


def _ssm_consts(a_re, a_im, log_dt, b_re, b_im, c_re, c_im, scan_steps):
    lam_re, lam_im = a_re.astype(F32), a_im.astype(F32)
    dt = jnp.exp(log_dt.astype(F32))[:, None]
    mag = jnp.exp(lam_re * dt)
    ab_re, ab_im = mag * jnp.cos(lam_im * dt), mag * jnp.sin(lam_im * dt)
    den = lam_re * lam_re + lam_im * lam_im
    f_re = ((ab_re - 1.0) * lam_re + ab_im * lam_im) / den
    f_im = (ab_im * lam_re - (ab_re - 1.0) * lam_im) / den
    br, bi = b_re.astype(F32), b_im.astype(F32)
    bb_re = f_re[..., None] * br - f_im[..., None] * bi
    bb_im = f_re[..., None] * bi + f_im[..., None] * br

    gh = SSM_GROUPS // 2
    eye = jnp.eye(gh, dtype=F32)

    def drive_half(m):
        return jnp.einsum("gph,gk->ghkp", m, eye).reshape(gh * SSM_GROUP, gh * SSM_STATE)

    def read_half(m):
        return jnp.einsum("ghp,gk->gpkh", m, eye).reshape(gh * SSM_STATE, gh * SSM_GROUP)

    bmat = jnp.stack([
        jnp.concatenate([drive_half(bb_re[s]), drive_half(bb_im[s])], axis=1)
        for s in (slice(0, gh), slice(gh, 2 * gh))]).astype(BF16)
    cre, cim = c_re.astype(F32), c_im.astype(F32)
    cmat = jnp.stack([
        jnp.concatenate([read_half(cre[s]), -read_half(cim[s])], axis=0)
        for s in (slice(0, gh), slice(gh, 2 * gh))]).astype(BF16)

    step_re, step_im = _power_table(ab_re.reshape(1, N_STATE), ab_im.reshape(1, N_STATE), scan_steps)
    sub_re, sub_im = _power_table(step_re[-1:], step_im[-1:], SUBLANES)
    abar = jnp.stack([step_re[:1], step_im[:1]])
    pw_block = _scan_planes(step_re[:SUBLANES], step_im[:SUBLANES])
    pw_sub = _scan_planes(sub_re, sub_im)
    return bmat, cmat, abar, pw_block, pw_sub


def _power_table(re, im, n):
    while re.shape[0] < n:
        top_re, top_im = re[-1:], im[-1:]
        more_re, more_im = _cmul(re, im, top_re, top_im)
        re, im = jnp.concatenate([re, more_re]), jnp.concatenate([im, more_im])
    return re[:n], im[:n]


def _scan_planes(re, im):
    row = jnp.arange(SUBLANES)[:, None]
    planes_re = [jnp.where(row >= s, re[s - 1][None, :], 0.0) for s in (1, 2, 4)] + [re]
    planes_im = [jnp.where(row >= s, im[s - 1][None, :], 0.0) for s in (1, 2, 4)] + [im]
    return jnp.stack([jnp.stack(planes_re), jnp.stack(planes_im)])


def _time_on_sublanes(n_rows):
    steps = n_rows // SUBLANES
    p = np.zeros((n_rows, n_rows), np.float32)
    r = np.arange(n_rows)
    p[r, (r % SUBLANES) * steps + r // SUBLANES] = 1.0
    return jnp.asarray(p, BF16), jnp.asarray(p.T, BF16)


def _rotary_tables(pos):
    half = RET_HD // 2
    inv = ROPE_BASE ** (-np.arange(half, dtype=np.float64) / half)
    ang = np.asarray(pos, np.float64)[:, None] * inv[None, :]
    cos, sin = np.cos(ang), np.sin(ang)
    return (np.concatenate([cos, cos], axis=-1).astype(np.float32),
            np.concatenate([-sin, sin], axis=-1).astype(np.float32))


def _retention_consts(chunk, n_seq):
    lg = np.log1p(-np.exp2(-5.0 - np.arange(RET_HEADS, dtype=np.float64)))
    r = np.arange(chunk * n_seq)
    ti = (r % chunk).astype(np.float64)
    seq = r // chunk
    diff = ti[:, None] - ti[None, :]
    keep = (seq[:, None] == seq[None, :]) & (diff >= 0.0)
    mask = np.where(keep[None], np.exp(np.maximum(diff, 0.0)[None] * lg[:, None, None]), 0.0)
    q_dec = np.exp((ti + 1.0)[:, None] * lg[None, :])
    k_dec = np.exp((chunk - 1.0 - ti)[:, None] * lg[None, :])
    c_dec = np.exp(chunk * lg)
    expand = lambda d: np.repeat(d, RET_HD, axis=1).astype(np.float32)
    return (mask.astype(np.float32), expand(q_dec), expand(k_dec),
            np.broadcast_to(c_dec[:, None], (RET_HEADS, RET_HD)).astype(np.float32))


def _pool_weight(w_pool):
    eye = jnp.eye(len(POOL_WINDOWS), dtype=F32)
    return jnp.einsum("gcd,gk->gckd", w_pool, eye).reshape(D_POOL, D_POOL).astype(BF16)


def kernel(x_prompt, x_sample, state_ssm_re, state_ssm_im, state_ret, state_pool, state_ffn_conv, c_prompt, c_sample, w_ada, b_ada, norm1_g, w_in, ssm_a_re, ssm_a_im, ssm_log_dt, ssm_b_re, ssm_b_im, ssm_c_re, ssm_c_im, ssm_d, ssm_w_glu, ssm_b_glu, pool_w, pool_scale, w_out, norm2_g, ffn_w_up, ffn_conv_w, ffn_conv_b, ffn_w_down, final_norm_g):
    bp, lp, _ = x_prompt.shape
    bs, ls, _ = x_sample.shape
    tl_mix, seq_mix = MIX_TOKENS, MIX_SEQS

    n_all = bp + bs
    n_pad = -(-n_all // SUBLANES) * SUBLANES
    c_all = jnp.concatenate([c_sample, c_prompt, jnp.zeros((n_pad - n_all, D_MODEL), F32)], axis=0)
    mod = _ada_mod(c_all, w_ada, b_ada)

    cos_p, sin_p = _rotary_tables(np.arange(lp))
    cos_s, sin_s = _rotary_tables(PAST_LEN + np.arange(ls))
    cos_s, sin_s = np.tile(cos_s, (seq_mix, 1)), np.tile(sin_s, (seq_mix, 1))
    assert math.gcd(lp, RET_CHUNK) == RET_CHUNK and math.gcd(ls, RET_CHUNK) == ls == SUBLANES
    ret_p = _retention_consts(RET_CHUNK, 1)
    ret_s = _retention_consts(ls, seq_mix)
    perm = _time_on_sublanes(tl_mix)

    rows_total = bp * lp + bs * ls
    groups = (_Rows(bp, lp, ROW_TILE, bs, 0), _Rows(bs, ls, ROW_TILE, 0, bp * lp))
    groups_full = (_Rows(bp, lp, ROW_TILE_FULL, bs, 0), _Rows(bs, ls, ROW_TILE_FULL, 0, bp * lp))
    big_p, big_s = groups

    norm1 = norm1_g.reshape(DEPTH, 1, D_MODEL)
    norm2 = norm2_g.reshape(DEPTH, 1, D_MODEL)
    final_g = final_norm_g.reshape(1, 1, D_MODEL)
    conv_b = ffn_conv_b.reshape(DEPTH, 1, D_FF)
    sample_states = (state_ssm_re.reshape(DEPTH, bs, N_STATE), state_ssm_im.reshape(DEPTH, bs, N_STATE),
                     state_ret, state_pool)

    x_first = (x_prompt.reshape(bp * lp, D_MODEL), x_sample.reshape(bs * ls, D_MODEL))
    x_all = None

    def residual_rows(k, rows):
        return (x_first[k], 0) if x_all is None else (x_all, rows.first_tile)

    new_p = ([], [], [], [], [])
    s_mix = None
    s_conv = None
    for l in range(DEPTH):
        bmat, cmat, abar, pw_block, pw_sub = _ssm_consts(
            ssm_a_re[l], ssm_a_im[l], ssm_log_dt[l], ssm_b_re[l], ssm_b_im[l], ssm_c_re[l], ssm_c_im[l],
            tl_mix // SUBLANES)
        shared = (bmat, cmat, ssm_d[l].reshape(1, D_SSM), ssm_w_glu[l].astype(BF16),
                  ssm_b_glu[l].reshape(1, D_SSM))
        pool_c = (_pool_weight(pool_w[l]), pool_scale[l].reshape(1, D_POOL))
        pre = (mod, MOD_SHIFT1, MOD_SCALE1)

        h = None
        for k, rows in enumerate(groups):
            h = _norm(*residual_rows(k, rows), norm1, l, pre, rows, BF16, rows_total, h)
        proj = _matmul(h, w_in, l, ROW_TILE, IN_PROJ_COLS, "in_proj")
        y_p, st_re, st_im, st_ret, st_pool = _mix_prompt(
            proj, bp, lp, (cos_p, sin_p) + perm + (abar, pw_sub) + shared + ret_p + pool_c, tl_mix)
        y_s, *s_mix = _mix_sample(proj, bp * lp, sample_states, l, s_mix, bs, ls,
                                  (cos_s, sin_s, pw_block) + shared + ret_s + pool_c, seq_mix)
        xh = None
        for k, (rows, y) in enumerate(zip(groups_full, (y_p, y_s))):
            xh = _out_proj(y, w_out, l, *residual_rows(k, rows), mod, norm2, rows, rows_total, xh)
        xn, h2 = xh
        act, tails, s_conv = _ffn_up(h2, ffn_w_up, ffn_conv_w, conv_b, l, state_ffn_conv, s_conv, big_p, bs, ls)
        x_all = _matmul(act, ffn_w_down, l, ROW_TILE, FF_DOWN_COLS, "ffn_down", groups=groups,
                        residual=(xn, mod, MOD_GATE2), single_buffer_w=True)
        st_conv = tails[big_p.tiles_per_seq - 1::big_p.tiles_per_seq]
        for lst, st in zip(new_p, (st_re, st_im, st_ret, st_pool, st_conv)):
            lst.append(st)

    yp = _norm(x_all, big_p.first_tile, final_g, 0, None, big_p, F32)
    ys = _norm(x_all, big_s.first_tile, final_g, 0, None, big_s, F32)

    p_re, p_im, p_ret, p_pool, p_conv = [jnp.stack(st) for st in new_p]
    s_re, s_im, s_ret, s_pool = s_mix
    shape_p = (DEPTH, bp, SSM_GROUPS, SSM_STATE)
    shape_s = (DEPTH, bs, SSM_GROUPS, SSM_STATE)
    return (yp.reshape(bp, lp, D_MODEL), ys.reshape(bs, ls, D_MODEL),
            p_re.reshape(shape_p), p_im.reshape(shape_p), p_ret, p_pool, p_conv,
            s_re.reshape(shape_s), s_im.reshape(shape_s), s_ret, s_pool, s_conv)
```

```python
import functools
import math

import jax
import jax.numpy as jnp
import numpy as np
from jax import lax
from jax.experimental import pallas as pl
from jax.experimental.pallas import tpu as pltpu

F32 = jnp.float32
BF16 = jnp.bfloat16

D_MODEL = 2048
DEPTH = 2
PAST_LEN = 16384
D_SSM = 512
D_RET = 1024
D_POOL = 512
SSM_GROUP = 16
SSM_GROUPS = 32
SSM_STATE = 64
N_STATE = SSM_GROUPS * SSM_STATE
RET_HEADS = 8
RET_HD = 128
RET_CHUNK = 128
POOL_WINDOWS = (2, 4, 8, 16)
POOL_GD = 128
POOL_BUF = 15
D_FF = 5504
FF_TILE = 512
FF_LAST = D_FF % FF_TILE
CONV_W = 3
D_IN = D_SSM + 4 * D_RET + D_POOL
ROPE_BASE = 10000.0
EPS = 1e-6

SUBLANES = 8
LANES = 128
ROW_TILE = 1024
ROW_TILE_FULL = 512
FF_SUB_ROWS = 512
IN_PROJ_COLS = 1024
FF_DOWN_COLS = 512
ADA_COLS = 1024
MOD_SHIFT1, MOD_SCALE1, MOD_GATE1, MOD_SHIFT2, MOD_SCALE2, MOD_GATE2 = range(6)
SCAN_LANES = 512
MIX_TOKENS = 512
MIX_SEQS = 16
VMEM_LIMIT = 56 * 1024 * 1024

Q_OFF = D_SSM
K_OFF = D_SSM + D_RET
V_OFF = D_SSM + 2 * D_RET
G_OFF = D_SSM + 3 * D_RET
C_OFF = D_SSM + 4 * D_RET


def _params(*sem):
    return pltpu.CompilerParams(dimension_semantics=sem, vmem_limit_bytes=VMEM_LIMIT)


def _rms(x):
    return x * lax.rsqrt(jnp.mean(x * x, axis=-1, keepdims=True) + EPS)


def _ada_kernel(c_ref, w_ref, b_ref, o_ref):
    a = jax.nn.silu(c_ref[...]).astype(BF16)
    res = jnp.dot(a, w_ref[...].astype(BF16), preferred_element_type=F32) + b_ref[...]
    for r in range(res.shape[0]):
        o_ref[r] = res[r:r + 1, :]


def _ada_mod(c_all, w_ada, b_ada):
    nseq = c_all.shape[0]
    per_vec = D_MODEL // ADA_COLS
    return pl.pallas_call(
        _ada_kernel,
        grid=(DEPTH, 6 * per_vec),
        in_specs=[
            pl.BlockSpec((nseq, D_MODEL), lambda l, j: (0, 0)),
            pl.BlockSpec((None, D_MODEL, ADA_COLS), lambda l, j: (l, 0, j)),
            pl.BlockSpec((None, 1, ADA_COLS), lambda l, j: (l, 0, j)),
        ],
        out_specs=pl.BlockSpec((None, None, nseq, 1, ADA_COLS), lambda l, j: (l, j // per_vec, 0, 0, j % per_vec)),
        out_shape=jax.ShapeDtypeStruct((DEPTH, 6, nseq, 1, D_MODEL), F32),
        compiler_params=_params("arbitrary", "arbitrary"),
        name="ada_mod",
    )(c_all, w_ada, b_ada.reshape(DEPTH, 1, 6 * D_MODEL))


class _Rows:
    def __init__(self, n_seq, seq_len, tm, first_seq, first_row):
        if seq_len >= tm:
            assert seq_len % tm == 0
            self.bb, self.tl = 1, tm
            self.tiles_per_seq = seq_len // tm
        else:
            assert tm % seq_len == 0 and seq_len == SUBLANES
            self.bb, self.tl = tm // seq_len, seq_len
            self.tiles_per_seq = 1
        assert first_seq % self.bb == 0 and first_row % tm == 0
        self.tm = tm
        self.n_tiles = n_seq * seq_len // tm
        self.first_block = first_seq // self.bb
        self.first_tile = first_row // tm

    def mod_block(self, i):
        return self.first_block + i // self.tiles_per_seq

    def mod_spec(self, layer, vec, row_tile_of, cols=D_MODEL, col_block_of=lambda *ids: 0):
        return pl.BlockSpec(
            (None, None, self.bb, 1, cols),
            lambda *ids: (layer, vec, self.mod_block(row_tile_of(*ids)), 0, col_block_of(*ids)))


def _call_into(kernel_fn, prev, *, grid, in_specs, args, out_specs, out_shape, scratch_shapes=(), sem, name):
    n_in = len(args)
    in_specs, args = list(in_specs), list(args)
    aliases = {}
    for k, arr in sorted((prev or {}).items()):
        aliases[len(args)] = k
        in_specs.append(pl.BlockSpec(memory_space=pl.ANY))
        args.append(arr)
    n_args = len(args)

    def body(*refs):
        kernel_fn(*refs[:n_in], *refs[n_args:])

    return pl.pallas_call(
        body, grid=grid, in_specs=in_specs, out_specs=out_specs, out_shape=out_shape,
        scratch_shapes=list(scratch_shapes), input_output_aliases=aliases,
        compiler_params=_params(*sem), name=name,
    )(*args)


def _norm_kernel(*refs, bb, tl, modulated):
    if modulated:
        x_ref, g_ref, sh_ref, sc_ref, o_ref = refs
    else:
        x_ref, g_ref, o_ref = refs
    h = _rms(x_ref[...].reshape(bb, tl, D_MODEL)) * g_ref[...]
    if modulated:
        h = h * (1.0 + sc_ref[...]) + sh_ref[...]
    o_ref[...] = h.reshape(bb * tl, D_MODEL).astype(o_ref.dtype)


def _norm(x, x_tile0, g, layer, mods, rows, out_dtype, out_rows=None, prev=None):
    in_specs = [pl.BlockSpec((rows.tm, D_MODEL), lambda i: (x_tile0 + i, 0)),
                pl.BlockSpec((None, 1, D_MODEL), lambda i: (layer, 0, 0))]
    args = [x, g]
    if mods is not None:
        mod, shift_vec, scale_vec = mods
        in_specs += [rows.mod_spec(layer, shift_vec, lambda i: i), rows.mod_spec(layer, scale_vec, lambda i: i)]
        args += [mod, mod]
    out_tile0 = 0 if out_rows is None else rows.first_tile
    return _call_into(
        functools.partial(_norm_kernel, bb=rows.bb, tl=rows.tl, modulated=mods is not None),
        None if prev is None else {0: prev},
        grid=(rows.n_tiles,),
        in_specs=in_specs,
        args=args,
        out_specs=pl.BlockSpec((rows.tm, D_MODEL), lambda i: (out_tile0 + i, 0)),
        out_shape=jax.ShapeDtypeStruct((out_rows or rows.n_tiles * rows.tm, D_MODEL), out_dtype),
        sem=("arbitrary",),
        name="norm",
    )


def _matmul_kernel(*refs, groups, residual):
    if residual:
        a_ref, w_ref, x_ref, *gate_refs, o_ref, w_scr = refs
    else:
        a_ref, w_ref, o_ref, w_scr = refs
    i = pl.program_id(1)

    @pl.when(i == 0)
    def _():
        w_scr[...] = w_ref[...].astype(BF16)

    f = jnp.dot(a_ref[...], w_scr[...], preferred_element_type=F32)
    if not residual:
        o_ref[...] = f
        return
    tn = f.shape[1]
    for rows, g_ref in zip(groups, gate_refs):
        @pl.when((i >= rows.first_tile) & (i < rows.first_tile + rows.n_tiles))
        def _(rows=rows, g_ref=g_ref):
            xn = x_ref[...].reshape(rows.bb, rows.tl, tn) + g_ref[...] * f.reshape(rows.bb, rows.tl, tn)
            o_ref[...] = xn.reshape(rows.tm, tn)


def _matmul(a, w, layer, tm, tn, name, groups=(), residual=None, single_buffer_w=False):
    m, k = a.shape
    n = w.shape[2]
    n_tiles = pl.cdiv(m, tm)
    if residual is not None:
        assert m == sum(rows.n_tiles for rows in groups) * tm and all(rows.tm == tm for rows in groups)
    w_mode = dict(pipeline_mode=pl.Buffered(1)) if single_buffer_w else {}
    in_specs = [
        pl.BlockSpec((tm, k), lambda j, i: (i, 0)),
        pl.BlockSpec((None, k, tn), lambda j, i: (layer, 0, j), **w_mode),
    ]
    args = [a, w]
    if residual is not None:
        x, mod, gate_vec = residual
        in_specs.append(pl.BlockSpec((tm, tn), lambda j, i: (i, j)))
        args.append(x)
        for rows in groups:
            in_specs.append(rows.mod_spec(
                layer, gate_vec,
                lambda j, i, rows=rows: jnp.clip(i - rows.first_tile, 0, rows.n_tiles - 1),
                tn, lambda j, i: j))
            args.append(mod)
    return pl.pallas_call(
        functools.partial(_matmul_kernel, groups=groups, residual=residual is not None),
        grid=(n // tn, n_tiles),
        in_specs=in_specs,
        out_specs=pl.BlockSpec((tm, tn), lambda j, i: (i, j)),
        out_shape=jax.ShapeDtypeStruct((m, n), F32),
        scratch_shapes=[pltpu.VMEM((k, tn), BF16)],
        compiler_params=_params("arbitrary", "arbitrary"),
        name=name,
    )(*args)


def _ssm_drive(ub, bmat_ref, dr_scr, row0, n_rows):
    half_u = D_SSM // 2
    half_n = N_STATE // 2
    for hf in range(2):
        d = jnp.dot(ub[:, hf * half_u:(hf + 1) * half_u], bmat_ref[hf], preferred_element_type=F32)
        dr_scr[row0:row0 + n_rows, hf * half_n:(hf + 1) * half_n] = d[:, :half_n]
        dr_scr[row0:row0 + n_rows, N_STATE + hf * half_n:N_STATE + (hf + 1) * half_n] = d[:, half_n:]


def _cmul_add(a_re, a_im, x_re, x_im, y_re, y_im):
    return y_re + (a_re * x_re - a_im * x_im), y_im + (a_re * x_im + a_im * x_re)


def _scan_tile(dr_scr, abar_ref, pws_ref, carry_scr, n_steps):
    row = lax.broadcasted_iota(jnp.int32, (SUBLANES, SCAN_LANES), 0)
    for c in range(N_STATE // SCAN_LANES):
        lo = c * SCAN_LANES
        re_cols = slice(lo, lo + SCAN_LANES)
        im_cols = slice(N_STATE + lo, N_STATE + lo + SCAN_LANES)
        a_re = jnp.broadcast_to(abar_ref[0, :, re_cols], (SUBLANES, SCAN_LANES))
        a_im = jnp.broadcast_to(abar_ref[1, :, re_cols], (SUBLANES, SCAN_LANES))

        def local_step(t, h):
            r = pl.ds(pl.multiple_of(t * SUBLANES, SUBLANES), SUBLANES)
            h_re, h_im = _cmul_add(a_re, a_im, h[0], h[1], dr_scr[r, re_cols], dr_scr[r, im_cols])
            dr_scr[r, re_cols] = h_re
            dr_scr[r, im_cols] = h_im
            return h_re, h_im

        zero = jnp.zeros((SUBLANES, SCAN_LANES), F32)
        g_re, g_im = lax.fori_loop(0, n_steps, local_step, (zero, zero), unroll=2)
        for si, s in enumerate((1, 2, 4)):
            g_re, g_im = _cmul_add(pws_ref[0, si, :, re_cols], pws_ref[1, si, :, re_cols],
                                   pltpu.roll(g_re, s, 0), pltpu.roll(g_im, s, 0), g_re, g_im)
        c_re = carry_scr[SUBLANES - 1:SUBLANES, re_cols]
        c_im = carry_scr[SUBLANES - 1:SUBLANES, im_cols]
        e_re, e_im = _cmul_add(pws_ref[0, 3, :, re_cols], pws_ref[1, 3, :, re_cols], c_re, c_im, g_re, g_im)
        carry_scr[:, re_cols] = e_re
        carry_scr[:, im_cols] = e_im
        in_re = jnp.where(row == 0, c_re, pltpu.roll(e_re, 1, 0))
        in_im = jnp.where(row == 0, c_im, pltpu.roll(e_im, 1, 0))

        def fix_step(t, w):
            w_re, w_im = _cmul(a_re, a_im, w[0], w[1])
            r = pl.ds(pl.multiple_of(t * SUBLANES, SUBLANES), SUBLANES)
            dr_scr[r, re_cols] = dr_scr[r, re_cols] + w_re
            dr_scr[r, im_cols] = dr_scr[r, im_cols] + w_im
            return w_re, w_im

        lax.fori_loop(0, n_steps, fix_step, (in_re, in_im), unroll=2)


def _scan_block(dr_scr, pw_ref, b, carry_fn, block_end_fn=None):
    r = pl.multiple_of(b * SUBLANES + SUBLANES, SUBLANES)
    for c in range(N_STATE // SCAN_LANES):
        lo = c * SCAN_LANES
        re_cols = slice(lo, lo + SCAN_LANES)
        im_cols = slice(N_STATE + lo, N_STATE + lo + SCAN_LANES)
        d_re = dr_scr[pl.ds(r, SUBLANES), re_cols]
        d_im = dr_scr[pl.ds(r, SUBLANES), im_cols]
        for si, s in enumerate((1, 2, 4)):
            p_re = pw_ref[0, si, :, re_cols]
            p_im = pw_ref[1, si, :, re_cols]
            r_re = pltpu.roll(d_re, s, 0)
            r_im = pltpu.roll(d_im, s, 0)
            d_re, d_im = (d_re + (p_re * r_re - p_im * r_im),
                          d_im + (p_re * r_im + p_im * r_re))
        c_re, c_im = carry_fn(b, lo)
        a_re = pw_ref[0, 3, :, re_cols]
        a_im = pw_ref[1, 3, :, re_cols]
        h_re = d_re + (a_re * c_re - a_im * c_im)
        h_im = d_im + (a_re * c_im + a_im * c_re)
        dr_scr[pl.ds(r, SUBLANES), re_cols] = h_re
        dr_scr[pl.ds(r, SUBLANES), im_cols] = h_im
        if block_end_fn is not None:
            block_end_fn(b, lo, h_re[SUBLANES - 1:SUBLANES], h_im[SUBLANES - 1:SUBLANES])


def _ssm_scan(dr_scr, pw_ref, n_blocks, carry_fn, block_end_fn=None):
    def body(b, carry):
        _scan_block(dr_scr, pw_ref, b, carry_fn, block_end_fn)
        return carry

    lax.fori_loop(0, n_blocks, body, 0)


def _ssm_readout(dr_scr, cmat_ref, row0, n_rows):
    half_n = N_STATE // 2
    parts = []
    for hf in range(2):
        h_re = dr_scr[row0:row0 + n_rows, hf * half_n:(hf + 1) * half_n].astype(BF16)
        h_im = dr_scr[row0:row0 + n_rows, N_STATE + hf * half_n:N_STATE + (hf + 1) * half_n].astype(BF16)
        parts.append(jnp.dot(h_re, cmat_ref[hf, :half_n], preferred_element_type=F32)
                     + jnp.dot(h_im, cmat_ref[hf, half_n:], preferred_element_type=F32))
    return jnp.concatenate(parts, axis=-1)


def _ssm_gate(y, wglu_ref, bglu_ref):
    ya = jax.nn.gelu(y)
    gate = jnp.dot(ya.astype(BF16), wglu_ref[...], preferred_element_type=F32) + bglu_ref[...]
    return ya * jax.nn.sigmoid(gate)


def _unpermute_rows(perm_t_ref, x):
    hi = x.astype(BF16)
    rest = x - hi.astype(F32)
    mid = rest.astype(BF16)
    lo = (rest - mid.astype(F32)).astype(BF16)
    n = x.shape[1]
    out = jnp.dot(perm_t_ref[...], jnp.concatenate([hi, mid, lo], axis=1), preferred_element_type=F32)
    return out[:, :n] + out[:, n:2 * n] + out[:, 2 * n:]


def _rotary(x, cosv, sinv):
    return x * cosv + pltpu.roll(x, RET_HD // 2, 1) * sinv


def _head_norm_gate(o, g):
    mu = jnp.mean(o, axis=-1, keepdims=True)
    var = jnp.mean(jnp.square(o - mu), axis=-1, keepdims=True)
    return (o - mu) * lax.rsqrt(var + EPS) * jax.nn.silu(g)


def _pool_counts(pos):
    return [jnp.minimum(pos + 1, w).astype(F32) for w in POOL_WINDOWS]


def _mix_prompt_kernel(proj_ref, cos_ref, sin_ref, perm_ref, permt_ref, abar_ref, pws_ref,
                       bmat_ref, cmat_ref, dskip_ref, wglu_ref, bglu_ref,
                       mask_ref, qdec_ref, kdec_ref, cdec_ref, wpool_ref, pscale_ref,
                       y_ref, hre_ref, him_ref, sret_ref, pbuf_ref,
                       dr_scr, carry_scr, pool_scr, *, tl):
    t = pl.program_id(1)

    @pl.when(t == 0)
    def _():
        carry_scr[...] = jnp.zeros((SUBLANES, 2 * N_STATE), F32)
        sret_ref[...] = jnp.zeros(sret_ref.shape, F32)
        pool_scr[0:16, :] = jnp.zeros((16, D_POOL), F32)

    u = proj_ref[:, 0:D_SSM]
    up = jnp.dot(perm_ref[...], u.astype(BF16), preferred_element_type=F32).astype(BF16)
    _ssm_drive(up, bmat_ref, dr_scr, 0, tl)
    _scan_tile(dr_scr, abar_ref, pws_ref, carry_scr, tl // SUBLANES)
    y = _unpermute_rows(permt_ref, _ssm_readout(dr_scr, cmat_ref, 0, tl)) + dskip_ref[...] * u
    y_ref[:, 0:D_SSM] = _ssm_gate(y, wglu_ref, bglu_ref).astype(BF16)
    hre_ref[0] = carry_scr[SUBLANES - 1:SUBLANES, 0:N_STATE]
    him_ref[0] = carry_scr[SUBLANES - 1:SUBLANES, N_STATE:2 * N_STATE]

    scale = RET_HD ** -0.5

    def chunk_body(ci, carry):
        rows = pl.ds(pl.multiple_of(ci * RET_CHUNK, RET_CHUNK), RET_CHUNK)
        cosv = cos_ref[rows, :]
        sinv = sin_ref[rows, :]
        for h in range(RET_HEADS):
            hs = slice(h * RET_HD, (h + 1) * RET_HD)
            q = proj_ref[rows, Q_OFF + h * RET_HD:Q_OFF + (h + 1) * RET_HD]
            k = proj_ref[rows, K_OFF + h * RET_HD:K_OFF + (h + 1) * RET_HD]
            v = proj_ref[rows, V_OFF + h * RET_HD:V_OFF + (h + 1) * RET_HD].astype(BF16)
            g = proj_ref[rows, G_OFF + h * RET_HD:G_OFF + (h + 1) * RET_HD]
            qr = _rotary(q, cosv, sinv)
            kr = _rotary(k, cosv, sinv) * scale
            s_old = sret_ref[0, h]
            sc = lax.dot_general(qr.astype(BF16), kr.astype(BF16), (((1,), (1,)), ((), ())),
                                 preferred_element_type=F32) * mask_ref[h]
            o = (jnp.dot(sc.astype(BF16), v, preferred_element_type=F32)
                 + jnp.dot((qr * qdec_ref[:, hs]).astype(BF16), s_old.astype(BF16),
                           preferred_element_type=F32))
            sret_ref[0, h] = (s_old * cdec_ref[h:h + 1, :]
                              + lax.dot_general((kr * kdec_ref[:, hs]).astype(BF16), v,
                                                (((0,), (0,)), ((), ())), preferred_element_type=F32))
            y_ref[rows, D_SSM + h * RET_HD:D_SSM + (h + 1) * RET_HD] = _head_norm_gate(o, g).astype(BF16)
        return carry

    lax.fori_loop(0, tl // RET_CHUNK, chunk_body, 0, unroll=True)

    uc = proj_ref[:, C_OFF:C_OFF + D_POOL]
    pool_scr[16:16 + tl, :] = uc
    pos = t * tl + lax.broadcasted_iota(jnp.int32, (tl, POOL_GD), 0)
    counts = _pool_counts(pos)
    parts = []
    for gi, w in enumerate(POOL_WINDOWS):
        lanes = slice(gi * POOL_GD, (gi + 1) * POOL_GD)
        acc = pool_scr[:, lanes]
        shift = 1
        while shift < w:
            acc = acc + pltpu.roll(acc, shift, 0)
            shift *= 2
        parts.append(acc[16:] / counts[gi] - uc[:, lanes])
    pooled = jnp.concatenate(parts, axis=-1).astype(BF16)
    yc = jnp.dot(pooled, wpool_ref[...], preferred_element_type=F32) * pscale_ref[...]
    y_ref[:, D_SSM + D_RET:] = yc.astype(BF16)
    pbuf_ref[0] = pool_scr[tl + 1:tl + 16, :]
    pool_scr[0:16, :] = pool_scr[tl:tl + 16, :]


def _mix_prompt(proj, n_seq, seq_len, consts, tl):
    cos_t, sin_t = consts[:2]
    whole = consts[2:]
    nt = seq_len // tl
    row_map = lambda b, t: (b * nt + t, 0)

    def const_spec(a):
        nd = a.ndim
        return pl.BlockSpec(a.shape, lambda b, t: (0,) * nd)

    in_specs = [
        pl.BlockSpec((tl, D_IN), row_map),
        pl.BlockSpec((tl, RET_HD), lambda b, t: (t, 0)),
        pl.BlockSpec((tl, RET_HD), lambda b, t: (t, 0)),
    ] + [const_spec(a) for a in whole]
    out_shape = (
        jax.ShapeDtypeStruct((n_seq * seq_len, D_MODEL), BF16),
        jax.ShapeDtypeStruct((n_seq, 1, N_STATE), F32),
        jax.ShapeDtypeStruct((n_seq, 1, N_STATE), F32),
        jax.ShapeDtypeStruct((n_seq, RET_HEADS, RET_HD, RET_HD), F32),
        jax.ShapeDtypeStruct((n_seq, POOL_BUF, D_POOL), F32),
    )
    out_specs = (
        pl.BlockSpec((tl, D_MODEL), row_map),
        pl.BlockSpec((1, 1, N_STATE), lambda b, t: (b, 0, 0)),
        pl.BlockSpec((1, 1, N_STATE), lambda b, t: (b, 0, 0)),
        pl.BlockSpec((1, RET_HEADS, RET_HD, RET_HD), lambda b, t: (b, 0, 0, 0)),
        pl.BlockSpec((1, POOL_BUF, D_POOL), lambda b, t: (b, 0, 0)),
    )
    return pl.pallas_call(
        functools.partial(_mix_prompt_kernel, tl=tl),
        grid=(n_seq, nt),
        in_specs=in_specs,
        out_specs=out_specs,
        out_shape=out_shape,
        scratch_shapes=[
            pltpu.VMEM((tl, 2 * N_STATE), F32),
            pltpu.VMEM((SUBLANES, 2 * N_STATE), F32),
            pltpu.VMEM((16 + tl, D_POOL), F32),
        ],
        compiler_params=_params("arbitrary", "arbitrary"),
        name="mix_prompt",
    )(proj, cos_t, sin_t, *whole)


def _mix_sample_kernel(proj_ref, cos_ref, sin_ref, pw_ref, bmat_ref, cmat_ref, dskip_ref, wglu_ref,
                       bglu_ref, mask_ref, qdec_ref, kdec_ref, cdec_ref, wpool_ref, pscale_ref,
                       h0re_ref, h0im_ref, s0_ref, pool0_ref,
                       y_ref, hre_ref, him_ref, sret_ref, pbuf_ref,
                       dr_scr, pool_scr, *, bs, seq_len, pos0):
    n_rows = bs * seq_len

    u = proj_ref[:, 0:D_SSM]
    _ssm_drive(u.astype(BF16), bmat_ref, dr_scr, SUBLANES, n_rows)

    def carry_fn(b, lo):
        return (h0re_ref[pl.ds(b, 1), lo:lo + SCAN_LANES], h0im_ref[pl.ds(b, 1), lo:lo + SCAN_LANES])

    def block_end_fn(b, lo, h_re, h_im):
        hre_ref[pl.ds(b, 1), lo:lo + SCAN_LANES] = h_re
        him_ref[pl.ds(b, 1), lo:lo + SCAN_LANES] = h_im

    _ssm_scan(dr_scr, pw_ref, bs, carry_fn, block_end_fn)
    y = _ssm_readout(dr_scr, cmat_ref, SUBLANES, n_rows) + dskip_ref[...] * u
    y_ref[:, 0:D_SSM] = _ssm_gate(y, wglu_ref, bglu_ref).astype(BF16)

    scale = RET_HD ** -0.5
    cosv = cos_ref[...]
    sinv = sin_ref[...]
    own = (lax.broadcasted_iota(jnp.int32, (n_rows, bs * RET_HD), 0) // seq_len
           == lax.broadcasted_iota(jnp.int32, (n_rows, bs * RET_HD), 1) // RET_HD)
    for h in range(RET_HEADS):
        hs = slice(h * RET_HD, (h + 1) * RET_HD)
        q = proj_ref[:, Q_OFF + h * RET_HD:Q_OFF + (h + 1) * RET_HD]
        k = proj_ref[:, K_OFF + h * RET_HD:K_OFF + (h + 1) * RET_HD]
        v = proj_ref[:, V_OFF + h * RET_HD:V_OFF + (h + 1) * RET_HD].astype(BF16)
        g = proj_ref[:, G_OFF + h * RET_HD:G_OFF + (h + 1) * RET_HD]
        qr = _rotary(q, cosv, sinv)
        kr = _rotary(k, cosv, sinv) * scale
        sc = lax.dot_general(qr.astype(BF16), kr.astype(BF16), (((1,), (1,)), ((), ())),
                             preferred_element_type=F32) * mask_ref[h]
        qd = jnp.where(own, jnp.concatenate([qr * qdec_ref[:, hs]] * bs, axis=1), 0.0).astype(BF16)
        kd = jnp.where(own, jnp.concatenate([kr * kdec_ref[:, hs]] * bs, axis=1), 0.0).astype(BF16)
        s_old = s0_ref[:, h].reshape(bs * RET_HD, RET_HD)
        o = (jnp.dot(sc.astype(BF16), v, preferred_element_type=F32)
             + jnp.dot(qd, s_old.astype(BF16), preferred_element_type=F32))
        s_new = (s_old * cdec_ref[h:h + 1, :]
                 + lax.dot_general(kd, v, (((0,), (0,)), ((), ())), preferred_element_type=F32))
        sret_ref[:, h] = s_new.reshape(bs, RET_HD, RET_HD)
        y_ref[:, D_SSM + h * RET_HD:D_SSM + (h + 1) * RET_HD] = _head_norm_gate(o, g).astype(BF16)

    uc = proj_ref[:, C_OFF:C_OFF + D_POOL].reshape(bs, seq_len, D_POOL)
    pool_scr[:, 1:16, :] = pool0_ref[...]
    pool_scr[:, 16:16 + seq_len, :] = uc
    pos = pos0 + lax.broadcasted_iota(jnp.int32, (1, seq_len, POOL_GD), 1)
    counts = _pool_counts(pos)
    parts = []
    for gi, w in enumerate(POOL_WINDOWS):
        lanes = slice(gi * POOL_GD, (gi + 1) * POOL_GD)
        acc = pool_scr[:, 16:16 + seq_len, lanes]
        for j in range(1, w):
            acc = acc + pool_scr[:, 16 - j:16 - j + seq_len, lanes]
        parts.append(acc / counts[gi] - uc[:, :, lanes])
    pooled = jnp.concatenate(parts, axis=-1).reshape(n_rows, D_POOL).astype(BF16)
    yc = jnp.dot(pooled, wpool_ref[...], preferred_element_type=F32) * pscale_ref[...]
    y_ref[:, D_SSM + D_RET:] = yc.astype(BF16)
    pbuf_ref[...] = pool_scr[:, seq_len + 1:seq_len + 16, :]


def _mix_sample(proj, proj_row0, states, layer, prev_out, n_seq, seq_len, consts, bs):
    n_rows = bs * seq_len

    def const_spec(a):
        nd = a.ndim
        return pl.BlockSpec(a.shape, lambda i: (0,) * nd)

    state_specs = [
        pl.BlockSpec((None, bs, N_STATE), lambda i: (layer, i, 0)),
        pl.BlockSpec((None, bs, N_STATE), lambda i: (layer, i, 0)),
        pl.BlockSpec((None, bs, RET_HEADS, RET_HD, RET_HD), lambda i: (layer, i, 0, 0, 0)),
        pl.BlockSpec((None, bs, POOL_BUF, D_POOL), lambda i: (layer, i, 0, 0)),
    ]
    assert proj_row0 % n_rows == 0
    in_specs = ([pl.BlockSpec((n_rows, D_IN), lambda i: (proj_row0 // n_rows + i, 0))]
                + [const_spec(a) for a in consts] + state_specs)
    args = [proj, *consts, *states]
    n_in = len(args)
    aliases = {}
    if prev_out is not None:
        in_specs += [pl.BlockSpec(memory_space=pl.ANY)] * len(prev_out)
        aliases = {n_in + k: 1 + k for k in range(len(prev_out))}
        args += list(prev_out)
    n_args = len(args)
    out_shape = (
        jax.ShapeDtypeStruct((n_seq * seq_len, D_MODEL), BF16),
        jax.ShapeDtypeStruct((DEPTH, n_seq, N_STATE), F32),
        jax.ShapeDtypeStruct((DEPTH, n_seq, N_STATE), F32),
        jax.ShapeDtypeStruct((DEPTH, n_seq, RET_HEADS, RET_HD, RET_HD), F32),
        jax.ShapeDtypeStruct((DEPTH, n_seq, POOL_BUF, D_POOL), F32),
    )
    out_specs = (pl.BlockSpec((n_rows, D_MODEL), lambda i: (i, 0)),) + tuple(state_specs)

    def body(*refs):
        _mix_sample_kernel(*refs[:n_in], *refs[n_args:], bs=bs, seq_len=seq_len, pos0=PAST_LEN)

    return pl.pallas_call(
        body,
        grid=(n_seq // bs,),
        in_specs=in_specs,
        out_specs=out_specs,
        out_shape=out_shape,
        scratch_shapes=[
            pltpu.VMEM((SUBLANES + n_rows, 2 * N_STATE), F32),
            pltpu.VMEM((bs, 16 + seq_len, D_POOL), F32),
        ],
        input_output_aliases=aliases,
        compiler_params=_params("arbitrary"),
        name="mix_sample",
    )(*args)


def _out_proj_kernel(y_ref, w_ref, x_ref, g1_ref, sh_ref, sc_ref, g_ref, xo_ref, h_ref, w_scr, *, bb, tl):
    @pl.when(pl.program_id(0) == 0)
    def _():
        w_scr[...] = w_ref[...].astype(BF16)

    f = jnp.dot(y_ref[...], w_scr[...], preferred_element_type=F32)
    xn = x_ref[...].reshape(bb, tl, D_MODEL) + g1_ref[...] * f.reshape(bb, tl, D_MODEL)
    xo_ref[...] = xn.reshape(bb * tl, D_MODEL)
    h = _rms(xn) * g_ref[...] * (1.0 + sc_ref[...]) + sh_ref[...]
    h_ref[...] = h.reshape(bb * tl, D_MODEL).astype(BF16)


def _out_proj(y, w, layer, x, x_tile0, mod, g, rows, out_rows, prev):
    out_spec = pl.BlockSpec((rows.tm, D_MODEL), lambda i: (rows.first_tile + i, 0))
    return _call_into(
        functools.partial(_out_proj_kernel, bb=rows.bb, tl=rows.tl),
        None if prev is None else {0: prev[0], 1: prev[1]},
        grid=(rows.n_tiles,),
        in_specs=[
            pl.BlockSpec((rows.tm, D_MODEL), lambda i: (i, 0)),
            pl.BlockSpec((None, D_MODEL, D_MODEL), lambda i: (layer, 0, 0), pipeline_mode=pl.Buffered(1)),
            pl.BlockSpec((rows.tm, D_MODEL), lambda i: (x_tile0 + i, 0)),
            rows.mod_spec(layer, MOD_GATE1, lambda i: i),
            rows.mod_spec(layer, MOD_SHIFT2, lambda i: i),
            rows.mod_spec(layer, MOD_SCALE2, lambda i: i),
            pl.BlockSpec((None, 1, D_MODEL), lambda i: (layer, 0, 0)),
        ],
        args=[y, w, x, mod, mod, mod, g],
        out_specs=(out_spec, out_spec),
        out_shape=(jax.ShapeDtypeStruct((out_rows, D_MODEL), F32),
                   jax.ShapeDtypeStruct((out_rows, D_MODEL), BF16)),
        scratch_shapes=[pltpu.VMEM((D_MODEL, D_MODEL), BF16)],
        sem=("arbitrary",),
        name="out_proj",
    )


def _conv_gate(a, b, am1, am2, cw, cb):
    conv = cb + am2 * cw[0:1]
    conv = conv + am1 * cw[1:2]
    conv = conv + a * cw[2:3]
    return jax.nn.silu(conv) * b


def _ffn_up_kernel(h_ref, wa_ref, wb_ref, cw_ref, cb_ref, st_ref, act_ref, tail_ref, ns_ref, w_scr, carry_scr,
                   *, prompt, n_seq, seq_len, n_chunks):
    c = pl.program_id(0)
    i = pl.program_id(1)

    @pl.when(i == 0)
    def _():
        w_scr[:, :FF_TILE] = wa_ref[...].astype(BF16)

    @pl.when((i == 0) & (c < n_chunks - 1))
    def _():
        w_scr[:, FF_TILE:] = wb_ref[0].astype(BF16)

    @pl.when((i == 0) & (c == n_chunks - 1))
    def _():
        w_scr[:, FF_TILE:FF_TILE + FF_LAST] = wb_ref[0, :, FF_TILE - FF_LAST:].astype(BF16)

    cw = cw_ref[...]
    cb = cb_ref[...]

    @pl.when(i >= prompt.n_tiles)
    def _():
        tl = seq_len
        sb = FF_SUB_ROWS // tl
        tok = lax.broadcasted_iota(jnp.int32, (sb, tl, FF_TILE), 1)
        for r in range(n_seq * seq_len // FF_SUB_ROWS):
            rows = slice(r * FF_SUB_ROWS, (r + 1) * FF_SUB_ROWS)
            sq = slice(r * sb, (r + 1) * sb)
            ab = jnp.dot(h_ref[rows, :], w_scr[...], preferred_element_type=F32)
            a = ab[:, :FF_TILE].reshape(sb, tl, FF_TILE)
            b = ab[:, FF_TILE:].reshape(sb, tl, FF_TILE)
            p0 = st_ref[sq, 0:1, :]
            p1 = st_ref[sq, 1:2, :]
            am1 = jnp.where(tok == 0, p1, pltpu.roll(a, 1, 1))
            am2 = jnp.where(tok == 0, p0, jnp.where(tok == 1, p1, pltpu.roll(a, 2, 1)))
            act = _conv_gate(a, b, am1, am2, cw, cb)
            act_ref[rows, :] = act.reshape(FF_SUB_ROWS, FF_TILE).astype(BF16)
            ns_ref[sq] = a[:, tl - 2:tl, :]

    @pl.when(i < prompt.n_tiles)
    def _():
        @pl.when(i % prompt.tiles_per_seq == 0)
        def _():
            carry_scr[...] = jnp.zeros((SUBLANES, FF_TILE), F32)

        tok = lax.broadcasted_iota(jnp.int32, (SUBLANES, FF_TILE), 0)
        for r in range(prompt.tm // FF_SUB_ROWS):
            rows = slice(r * FF_SUB_ROWS, (r + 1) * FF_SUB_ROWS)
            ab = jnp.dot(h_ref[rows, :], w_scr[...], preferred_element_type=F32)
            a = ab[:, :FF_TILE]
            b = ab[:, FF_TILE:]
            p0 = carry_scr[SUBLANES - 2:SUBLANES - 1, :]
            p1 = carry_scr[SUBLANES - 1:SUBLANES, :]
            r1 = pltpu.roll(a, 1, 0)
            r2 = pltpu.roll(a, 2, 0)
            head1 = jnp.where(tok == 0, p1, r1[:SUBLANES])
            head2 = jnp.where(tok == 0, p0, jnp.where(tok == 1, p1, r2[:SUBLANES]))
            am1 = jnp.concatenate([head1, r1[SUBLANES:]], axis=0)
            am2 = jnp.concatenate([head2, r2[SUBLANES:]], axis=0)
            act_ref[rows, :] = _conv_gate(a, b, am1, am2, cw, cb).astype(BF16)
            carry_scr[...] = a[FF_SUB_ROWS - SUBLANES:, :]
        tail_ref[0] = carry_scr[SUBLANES - 2:SUBLANES, :]


def _ffn_up(h, w_up, conv_w, conv_b, layer, state, prev_state, prompt, n_seq, seq_len):
    m = h.shape[0]
    tm = prompt.tm
    assert m == prompt.n_tiles * tm + n_seq * seq_len and n_seq * seq_len <= tm
    assert (n_seq * seq_len) % FF_SUB_ROWS == 0 and tm % FF_SUB_ROWS == 0
    n_tiles = prompt.n_tiles + 1
    n_chunks = pl.cdiv(D_FF, FF_TILE)
    in_specs = [
        pl.BlockSpec((tm, D_MODEL), lambda c, i: (i, 0)),
        pl.BlockSpec((None, D_MODEL, FF_TILE), lambda c, i: (layer, 0, c)),
        pl.BlockSpec((pl.Element(1), pl.Element(D_MODEL), pl.Element(FF_TILE)),
                     lambda c, i: (layer, 0, LANES * jnp.minimum((D_FF + c * FF_TILE) // LANES,
                                                                 (2 * D_FF - FF_TILE) // LANES))),
        pl.BlockSpec((None, CONV_W, FF_TILE), lambda c, i: (layer, 0, c)),
        pl.BlockSpec((None, 1, FF_TILE), lambda c, i: (layer, 0, c)),
        pl.BlockSpec((None, n_seq, CONV_W - 1, FF_TILE), lambda c, i: (layer, 0, 0, c)),
    ]
    return _call_into(
        functools.partial(_ffn_up_kernel, prompt=prompt, n_seq=n_seq, seq_len=seq_len, n_chunks=n_chunks),
        None if prev_state is None else {2: prev_state},
        grid=(n_chunks, n_tiles),
        in_specs=in_specs,
        args=[h, w_up, w_up, conv_w, conv_b, state],
        out_specs=(
            pl.BlockSpec((tm, FF_TILE), lambda c, i: (i, c)),
            pl.BlockSpec((1, CONV_W - 1, FF_TILE), lambda c, i: (jnp.minimum(i, prompt.n_tiles - 1), 0, c)),
            pl.BlockSpec((None, n_seq, CONV_W - 1, FF_TILE), lambda c, i: (layer, 0, 0, c)),
        ),
        out_shape=(jax.ShapeDtypeStruct((m, D_FF), BF16),
                   jax.ShapeDtypeStruct((prompt.n_tiles, CONV_W - 1, D_FF), F32),
                   jax.ShapeDtypeStruct((DEPTH, n_seq, CONV_W - 1, D_FF), F32)),
        scratch_shapes=[pltpu.VMEM((D_MODEL, 2 * FF_TILE), BF16), pltpu.VMEM((SUBLANES, FF_TILE), F32)],
        sem=("arbitrary", "arbitrary"),
        name="ffn_up",
    )


def _cmul(ar, ai, br, bi):
    return ar * br - ai * bi, ar * bi + ai * br


def _ssm_consts(a_re, a_im, log_dt, b_re, b_im, c_re, c_im, scan_steps):
    lam_re, lam_im = a_re.astype(F32), a_im.astype(F32)
    dt = jnp.exp(log_dt.astype(F32))[:, None]
    mag = jnp.exp(lam_re * dt)
    ab_re, ab_im = mag * jnp.cos(lam_im * dt), mag * jnp.sin(lam_im * dt)
    den = lam_re * lam_re + lam_im * lam_im
    f_re = ((ab_re - 1.0) * lam_re + ab_im * lam_im) / den
    f_im = (ab_im * lam_re - (ab_re - 1.0) * lam_im) / den
    br, bi = b_re.astype(F32), b_im.astype(F32)
    bb_re = f_re[..., None] * br - f_im[..., None] * bi
    bb_im = f_re[..., None] * bi + f_im[..., None] * br

    gh = SSM_GROUPS // 2
    eye = jnp.eye(gh, dtype=F32)

    def drive_half(m):
        return jnp.einsum("gph,gk->ghkp", m, eye).reshape(gh * SSM_GROUP, gh * SSM_STATE)

    def read_half(m):
        return jnp.einsum("ghp,gk->gpkh", m, eye).reshape(gh * SSM_STATE, gh * SSM_GROUP)

    bmat = jnp.stack([
        jnp.concatenate([drive_half(bb_re[s]), drive_half(bb_im[s])], axis=1)
        for s in (slice(0, gh), slice(gh, 2 * gh))]).astype(BF16)
    cre, cim = c_re.astype(F32), c_im.astype(F32)
    cmat = jnp.stack([
        jnp.concatenate([read_half(cre[s]), -read_half(cim[s])], axis=0)
        for s in (slice(0, gh), slice(gh, 2 * gh))]).astype(BF16)

    step_re, step_im = _power_table(ab_re.reshape(1, N_STATE), ab_im.reshape(1, N_STATE), scan_steps)
    sub_re, sub_im = _power_table(step_re[-1:], step_im[-1:], SUBLANES)
    abar = jnp.stack([step_re[:1], step_im[:1]])
    pw_block = _scan_planes(step_re[:SUBLANES], step_im[:SUBLANES])
    pw_sub = _scan_planes(sub_re, sub_im)
    return bmat, cmat, abar, pw_block, pw_sub


def _power_table(re, im, n):
    while re.shape[0] < n:
        top_re, top_im = re[-1:], im[-1:]
        more_re, more_im = _cmul(re, im, top_re, top_im)
        re, im = jnp.concatenate([re, more_re]), jnp.concatenate([im, more_im])
    return re[:n], im[:n]


def _scan_planes(re, im):
    row = jnp.arange(SUBLANES)[:, None]
    planes_re = [jnp.where(row >= s, re[s - 1][None, :], 0.0) for s in (1, 2, 4)] + [re]
    planes_im = [jnp.where(row >= s, im[s - 1][None, :], 0.0) for s in (1, 2, 4)] + [im]
    return jnp.stack([jnp.stack(planes_re), jnp.stack(planes_im)])


def _time_on_sublanes(n_rows):
    steps = n_rows // SUBLANES
    p = np.zeros((n_rows, n_rows), np.float32)
    r = np.arange(n_rows)
    p[r, (r % SUBLANES) * steps + r // SUBLANES] = 1.0
    return jnp.asarray(p, BF16), jnp.asarray(p.T, BF16)


def _rotary_tables(pos):
    half = RET_HD // 2
    inv = ROPE_BASE ** (-np.arange(half, dtype=np.float64) / half)
    ang = np.asarray(pos, np.float64)[:, None] * inv[None, :]
    cos, sin = np.cos(ang), np.sin(ang)
    return (np.concatenate([cos, cos], axis=-1).astype(np.float32),
            np.concatenate([-sin, sin], axis=-1).astype(np.float32))


def _retention_consts(chunk, n_seq):
    lg = np.log1p(-np.exp2(-5.0 - np.arange(RET_HEADS, dtype=np.float64)))
    r = np.arange(chunk * n_seq)
    ti = (r % chunk).astype(np.float64)
    seq = r // chunk
    diff = ti[:, None] - ti[None, :]
    keep = (seq[:, None] == seq[None, :]) & (diff >= 0.0)
    mask = np.where(keep[None], np.exp(np.maximum(diff, 0.0)[None] * lg[:, None, None]), 0.0)
    q_dec = np.exp((ti + 1.0)[:, None] * lg[None, :])
    k_dec = np.exp((chunk - 1.0 - ti)[:, None] * lg[None, :])
    c_dec = np.exp(chunk * lg)
    expand = lambda d: np.repeat(d, RET_HD, axis=1).astype(np.float32)
    return (mask.astype(np.float32), expand(q_dec), expand(k_dec),
            np.broadcast_to(c_dec[:, None], (RET_HEADS, RET_HD)).astype(np.float32))


def _pool_weight(w_pool):
    eye = jnp.eye(len(POOL_WINDOWS), dtype=F32)
    return jnp.einsum("gcd,gk->gckd", w_pool, eye).reshape(D_POOL, D_POOL).astype(BF16)


def kernel(x_prompt, x_sample, state_ssm_re, state_ssm_im, state_ret, state_pool, state_ffn_conv, c_prompt, c_sample, w_ada, b_ada, norm1_g, w_in, ssm_a_re, ssm_a_im, ssm_log_dt, ssm_b_re, ssm_b_im, ssm_c_re, ssm_c_im, ssm_d, ssm_w_glu, ssm_b_glu, pool_w, pool_scale, w_out, norm2_g, ffn_w_up, ffn_conv_w, ffn_conv_b, ffn_w_down, final_norm_g):
    bp, lp, _ = x_prompt.shape
    bs, ls, _ = x_sample.shape
    tl_mix, seq_mix = MIX_TOKENS, MIX_SEQS

    n_all = bp + bs
    n_pad = -(-n_all // SUBLANES) * SUBLANES
    c_all = jnp.concatenate([c_sample, c_prompt, jnp.zeros((n_pad - n_all, D_MODEL), F32)], axis=0)
    mod = _ada_mod(c_all, w_ada, b_ada)

    cos_p, sin_p = _rotary_tables(np.arange(lp))
    cos_s, sin_s = _rotary_tables(PAST_LEN + np.arange(ls))
    cos_s, sin_s = np.tile(cos_s, (seq_mix, 1)), np.tile(sin_s, (seq_mix, 1))
    assert math.gcd(lp, RET_CHUNK) == RET_CHUNK and math.gcd(ls, RET_CHUNK) == ls == SUBLANES
    ret_p = _retention_consts(RET_CHUNK, 1)
    ret_s = _retention_consts(ls, seq_mix)
    perm = _time_on_sublanes(tl_mix)

    rows_total = bp * lp + bs * ls
    groups = (_Rows(bp, lp, ROW_TILE, bs, 0), _Rows(bs, ls, ROW_TILE, 0, bp * lp))
    groups_full = (_Rows(bp, lp, ROW_TILE_FULL, bs, 0), _Rows(bs, ls, ROW_TILE_FULL, 0, bp * lp))
    big_p, big_s = groups

    norm1 = norm1_g.reshape(DEPTH, 1, D_MODEL)
    norm2 = norm2_g.reshape(DEPTH, 1, D_MODEL)
    final_g = final_norm_g.reshape(1, 1, D_MODEL)
    conv_b = ffn_conv_b.reshape(DEPTH, 1, D_FF)
    sample_states = (state_ssm_re.reshape(DEPTH, bs, N_STATE), state_ssm_im.reshape(DEPTH, bs, N_STATE),
                     state_ret, state_pool)

    x_first = (x_prompt.reshape(bp * lp, D_MODEL), x_sample.reshape(bs * ls, D_MODEL))
    x_all = None

    def residual_rows(k, rows):
        return (x_first[k], 0) if x_all is None else (x_all, rows.first_tile)

    new_p = ([], [], [], [], [])
    s_mix = None
    s_conv = None
    for l in range(DEPTH):
        bmat, cmat, abar, pw_block, pw_sub = _ssm_consts(
            ssm_a_re[l], ssm_a_im[l], ssm_log_dt[l], ssm_b_re[l], ssm_b_im[l], ssm_c_re[l], ssm_c_im[l],
            tl_mix // SUBLANES)
        shared = (bmat, cmat, ssm_d[l].reshape(1, D_SSM), ssm_w_glu[l].astype(BF16),
                  ssm_b_glu[l].reshape(1, D_SSM))
        pool_c = (_pool_weight(pool_w[l]), pool_scale[l].reshape(1, D_POOL))
        pre = (mod, MOD_SHIFT1, MOD_SCALE1)

        h = None
        for k, rows in enumerate(groups):
            h = _norm(*residual_rows(k, rows), norm1, l, pre, rows, BF16, rows_total, h)
        proj = _matmul(h, w_in, l, ROW_TILE, IN_PROJ_COLS, "in_proj")
        y_p, st_re, st_im, st_ret, st_pool = _mix_prompt(
            proj, bp, lp, (cos_p, sin_p) + perm + (abar, pw_sub) + shared + ret_p + pool_c, tl_mix)
        y_s, *s_mix = _mix_sample(proj, bp * lp, sample_states, l, s_mix, bs, ls,
                                  (cos_s, sin_s, pw_block) + shared + ret_s + pool_c, seq_mix)
        xh = None
        for k, (rows, y) in enumerate(zip(groups_full, (y_p, y_s))):
            xh = _out_proj(y, w_out, l, *residual_rows(k, rows), mod, norm2, rows, rows_total, xh)
        xn, h2 = xh
        act, tails, s_conv = _ffn_up(h2, ffn_w_up, ffn_conv_w, conv_b, l, state_ffn_conv, s_conv, big_p, bs, ls)
        x_all = _matmul(act, ffn_w_down, l, ROW_TILE, FF_DOWN_COLS, "ffn_down", groups=groups,
                        residual=(xn, mod, MOD_GATE2), single_buffer_w=True)
        st_conv = tails[big_p.tiles_per_seq - 1::big_p.tiles_per_seq]
        for lst, st in zip(new_p, (st_re, st_im, st_ret, st_pool, st_conv)):
            lst.append(st)

    yp = _norm(x_all, big_p.first_tile, final_g, 0, None, big_p, F32)
    ys = _norm(x_all, big_s.first_tile, final_g, 0, None, big_s, F32)

    p_re, p_im, p_ret, p_pool, p_conv = [jnp.stack(st) for st in new_p]
    s_re, s_im, s_ret, s_pool = s_mix
    shape_p = (DEPTH, bp, SSM_GROUPS, SSM_STATE)
    shape_s = (DEPTH, bs, SSM_GROUPS, SSM_STATE)
    return (yp.reshape(bp, lp, D_MODEL), ys.reshape(bs, ls, D_MODEL),
            p_re.reshape(shape_p), p_im.reshape(shape_p), p_ret, p_pool, p_conv,
            s_re.reshape(shape_s), s_im.reshape(shape_s), s_ret, s_pool, s_conv)
```

```python
import functools
import math

import jax
import jax.numpy as jnp
import numpy as np
from jax import lax
from jax.experimental import pallas as pl
from jax.experimental.pallas import tpu as pltpu

F32 = jnp.float32
BF16 = jnp.bfloat16

D_MODEL = 2048
DEPTH = 2
PAST_LEN = 16384
D_SSM = 512
D_RET = 1024
D_POOL = 512
SSM_GROUP = 16
SSM_GROUPS = 32
SSM_STATE = 64
N_STATE = SSM_GROUPS * SSM_STATE
RET_HEADS = 8
RET_HD = 128
RET_CHUNK = 128
POOL_WINDOWS = (2, 4, 8, 16)
POOL_GD = 128
POOL_BUF = 15
D_FF = 5504
FF_TILE = 512
FF_LAST = D_FF % FF_TILE
CONV_W = 3
D_IN = D_SSM + 4 * D_RET + D_POOL
ROPE_BASE = 10000.0
EPS = 1e-6

SUBLANES = 8
LANES = 128
ROW_TILE = 1024
ROW_TILE_FULL = 512
FF_SUB_ROWS = 512
IN_PROJ_COLS = 1024
FF_DOWN_COLS = 512
ADA_COLS = 1024
MOD_SHIFT1, MOD_SCALE1, MOD_GATE1, MOD_SHIFT2, MOD_SCALE2, MOD_GATE2 = range(6)
SCAN_LANES = 512
MIX_TOKENS = 512
MIX_SEQS = 16
VMEM_LIMIT = 56 * 1024 * 1024

Q_OFF = D_SSM
K_OFF = D_SSM + D_RET
V_OFF = D_SSM + 2 * D_RET
G_OFF = D_SSM + 3 * D_RET
C_OFF = D_SSM + 4 * D_RET


def _params(*sem):
    return pltpu.CompilerParams(dimension_semantics=sem, vmem_limit_bytes=VMEM_LIMIT)


def _rms(x):
    return x * lax.rsqrt(jnp.mean(x * x, axis=-1, keepdims=True) + EPS)


def _ada_kernel(c_ref, w_ref, b_ref, o_ref):
    a = jax.nn.silu(c_ref[...]).astype(BF16)
    res = jnp.dot(a, w_ref[...].astype(BF16), preferred_element_type=F32) + b_ref[...]
    for r in range(res.shape[0]):
        o_ref[r] = res[r:r + 1, :]


def _ada_mod(c_all, w_ada, b_ada):
    nseq = c_all.shape[0]
    per_vec = D_MODEL // ADA_COLS
    return pl.pallas_call(
        _ada_kernel,
        grid=(DEPTH, 6 * per_vec),
        in_specs=[
            pl.BlockSpec((nseq, D_MODEL), lambda l, j: (0, 0)),
            pl.BlockSpec((None, D_MODEL, ADA_COLS), lambda l, j: (l, 0, j)),
            pl.BlockSpec((None, 1, ADA_COLS), lambda l, j: (l, 0, j)),
        ],
        out_specs=pl.BlockSpec((None, None, nseq, 1, ADA_COLS), lambda l, j: (l, j // per_vec, 0, 0, j % per_vec)),
        out_shape=jax.ShapeDtypeStruct((DEPTH, 6, nseq, 1, D_MODEL), F32),
        compiler_params=_params("arbitrary", "arbitrary"),
        name="ada_mod",
    )(c_all, w_ada, b_ada.reshape(DEPTH, 1, 6 * D_MODEL))


class _Rows:
    def __init__(self, n_seq, seq_len, tm, first_seq, first_row):
        if seq_len >= tm:
            assert seq_len % tm == 0
            self.bb, self.tl = 1, tm
            self.tiles_per_seq = seq_len // tm
        else:
            assert tm % seq_len == 0 and seq_len == SUBLANES
            self.bb, self.tl = tm // seq_len, seq_len
            self.tiles_per_seq = 1
        assert first_seq % self.bb == 0 and first_row % tm == 0
        self.tm = tm
        self.n_tiles = n_seq * seq_len // tm
        self.first_block = first_seq // self.bb
        self.first_tile = first_row // tm

    def mod_block(self, i):
        return self.first_block + i // self.tiles_per_seq

    def mod_spec(self, layer, vec, row_tile_of, cols=D_MODEL, col_block_of=lambda *ids: 0):
        return pl.BlockSpec(
            (None, None, self.bb, 1, cols),
            lambda *ids: (layer, vec, self.mod_block(row_tile_of(*ids)), 0, col_block_of(*ids)))


def _call_into(kernel_fn, prev, *, grid, in_specs, args, out_specs, out_shape, scratch_shapes=(), sem, name):
    n_in = len(args)
    in_specs, args = list(in_specs), list(args)
    aliases = {}
    for k, arr in sorted((prev or {}).items()):
        aliases[len(args)] = k
        in_specs.append(pl.BlockSpec(memory_space=pl.ANY))
        args.append(arr)
    n_args = len(args)

    def body(*refs):
        kernel_fn(*refs[:n_in], *refs[n_args:])

    return pl.pallas_call(
        body, grid=grid, in_specs=in_specs, out_specs=out_specs, out_shape=out_shape,
        scratch_shapes=list(scratch_shapes), input_output_aliases=aliases,
        compiler_params=_params(*sem), name=name,
    )(*args)


def _norm_kernel(*refs, bb, tl, modulated):
    if modulated:
        x_ref, g_ref, sh_ref, sc_ref, o_ref = refs
    else:
        x_ref, g_ref, o_ref = refs
    h = _rms(x_ref[...].reshape(bb, tl, D_MODEL)) * g_ref[...]
    if modulated:
        h = h * (1.0 + sc_ref[...]) + sh_ref[...]
    o_ref[...] = h.reshape(bb * tl, D_MODEL).astype(o_ref.dtype)


def _norm(x, x_tile0, g, layer, mods, rows, out_dtype, out_rows=None, prev=None):
    in_specs = [pl.BlockSpec((rows.tm, D_MODEL), lambda i: (x_tile0 + i, 0)),
                pl.BlockSpec((None, 1, D_MODEL), lambda i: (layer, 0, 0))]
    args = [x, g]
    if mods is not None:
        mod, shift_vec, scale_vec = mods
        in_specs += [rows.mod_spec(layer, shift_vec, lambda i: i), rows.mod_spec(layer, scale_vec, lambda i: i)]
        args += [mod, mod]
    out_tile0 = 0 if out_rows is None else rows.first_tile
    return _call_into(
        functools.partial(_norm_kernel, bb=rows.bb, tl=rows.tl, modulated=mods is not None),
        None if prev is None else {0: prev},
        grid=(rows.n_tiles,),
        in_specs=in_specs,
        args=args,
        out_specs=pl.BlockSpec((rows.tm, D_MODEL), lambda i: (out_tile0 + i, 0)),
        out_shape=jax.ShapeDtypeStruct((out_rows or rows.n_tiles * rows.tm, D_MODEL), out_dtype),
        sem=("arbitrary",),
        name="norm",
    )


def _matmul_kernel(*refs, groups, residual):
    if residual:
        a_ref, w_ref, x_ref, *gate_refs, o_ref, w_scr = refs
    else:
        a_ref, w_ref, o_ref, w_scr = refs
    i = pl.program_id(1)

    @pl.when(i == 0)
    def _():
        w_scr[...] = w_ref[...].astype(BF16)

    f = jnp.dot(a_ref[...], w_scr[...], preferred_element_type=F32)
    if not residual:
        o_ref[...] = f
        return
    tn = f.shape[1]
    for rows, g_ref in zip(groups, gate_refs):
        @pl.when((i >= rows.first_tile) & (i < rows.first_tile + rows.n_tiles))
        def _(rows=rows, g_ref=g_ref):
            xn = x_ref[...].reshape(rows.bb, rows.tl, tn) + g_ref[...] * f.reshape(rows.bb, rows.tl, tn)
            o_ref[...] = xn.reshape(rows.tm, tn)


def _matmul(a, w, layer, tm, tn, name, groups=(), residual=None, single_buffer_w=False):
    m, k = a.shape
    n = w.shape[2]
    n_tiles = pl.cdiv(m, tm)
    if residual is not None:
        assert m == sum(rows.n_tiles for rows in groups) * tm and all(rows.tm == tm for rows in groups)
    w_mode = dict(pipeline_mode=pl.Buffered(1)) if single_buffer_w else {}
    in_specs = [
        pl.BlockSpec((tm, k), lambda j, i: (i, 0)),
        pl.BlockSpec((None, k, tn), lambda j, i: (layer, 0, j), **w_mode),
    ]
    args = [a, w]
    if residual is not None:
        x, mod, gate_vec = residual
        in_specs.append(pl.BlockSpec((tm, tn), lambda j, i: (i, j)))
        args.append(x)
        for rows in groups:
            in_specs.append(rows.mod_spec(
                layer, gate_vec,
                lambda j, i, rows=rows: jnp.clip(i - rows.first_tile, 0, rows.n_tiles - 1),
                tn, lambda j, i: j))
            args.append(mod)
    return pl.pallas_call(
        functools.partial(_matmul_kernel, groups=groups, residual=residual is not None),
        grid=(n // tn, n_tiles),
        in_specs=in_specs,
        out_specs=pl.BlockSpec((tm, tn), lambda j, i: (i, j)),
        out_shape=jax.ShapeDtypeStruct((m, n), F32),
        scratch_shapes=[pltpu.VMEM((k, tn), BF16)],
        compiler_params=_params("arbitrary", "arbitrary"),
        name=name,
    )(*args)


def _ssm_drive(ub, bmat_ref, dr_scr, row0, n_rows):
    half_u = D_SSM // 2
    half_n = N_STATE // 2
    for hf in range(2):
        d = jnp.dot(ub[:, hf * half_u:(hf + 1) * half_u], bmat_ref[hf], preferred_element_type=F32)
        dr_scr[row0:row0 + n_rows, hf * half_n:(hf + 1) * half_n] = d[:, :half_n]
        dr_scr[row0:row0 + n_rows, N_STATE + hf * half_n:N_STATE + (hf + 1) * half_n] = d[:, half_n:]


def _cmul_add(a_re, a_im, x_re, x_im, y_re, y_im):
    return y_re + (a_re * x_re - a_im * x_im), y_im + (a_re * x_im + a_im * x_re)


def _scan_tile(dr_scr, abar_ref, pws_ref, carry_scr, n_steps):
    row = lax.broadcasted_iota(jnp.int32, (SUBLANES, SCAN_LANES), 0)
    for c in range(N_STATE // SCAN_LANES):
        lo = c * SCAN_LANES
        re_cols = slice(lo, lo + SCAN_LANES)
        im_cols = slice(N_STATE + lo, N_STATE + lo + SCAN_LANES)
        a_re = jnp.broadcast_to(abar_ref[0, :, re_cols], (SUBLANES, SCAN_LANES))
        a_im = jnp.broadcast_to(abar_ref[1, :, re_cols], (SUBLANES, SCAN_LANES))

        def local_step(t, h):
            r = pl.ds(pl.multiple_of(t * SUBLANES, SUBLANES), SUBLANES)
            h_re, h_im = _cmul_add(a_re, a_im, h[0], h[1], dr_scr[r, re_cols], dr_scr[r, im_cols])
            dr_scr[r, re_cols] = h_re
            dr_scr[r, im_cols] = h_im
            return h_re, h_im

        zero = jnp.zeros((SUBLANES, SCAN_LANES), F32)
        g_re, g_im = lax.fori_loop(0, n_steps, local_step, (zero, zero), unroll=2)
        for si, s in enumerate((1, 2, 4)):
            g_re, g_im = _cmul_add(pws_ref[0, si, :, re_cols], pws_ref[1, si, :, re_cols],
                                   pltpu.roll(g_re, s, 0), pltpu.roll(g_im, s, 0), g_re, g_im)
        c_re = carry_scr[SUBLANES - 1:SUBLANES, re_cols]
        c_im = carry_scr[SUBLANES - 1:SUBLANES, im_cols]
        e_re, e_im = _cmul_add(pws_ref[0, 3, :, re_cols], pws_ref[1, 3, :, re_cols], c_re, c_im, g_re, g_im)
        carry_scr[:, re_cols] = e_re
        carry_scr[:, im_cols] = e_im
        in_re = jnp.where(row == 0, c_re, pltpu.roll(e_re, 1, 0))
        in_im = jnp.where(row == 0, c_im, pltpu.roll(e_im, 1, 0))

        def fix_step(t, w):
            w_re, w_im = _cmul(a_re, a_im, w[0], w[1])
            r = pl.ds(pl.multiple_of(t * SUBLANES, SUBLANES), SUBLANES)
            dr_scr[r, re_cols] = dr_scr[r, re_cols] + w_re
            dr_scr[r, im_cols] = dr_scr[r, im_cols] + w_im
            return w_re, w_im

        lax.fori_loop(0, n_steps, fix_step, (in_re, in_im), unroll=2)


def _scan_block(dr_scr, pw_ref, b, carry_fn, block_end_fn=None):
    r = pl.multiple_of(b * SUBLANES + SUBLANES, SUBLANES)
    for c in range(N_STATE // SCAN_LANES):
        lo = c * SCAN_LANES
        re_cols = slice(lo, lo + SCAN_LANES)
        im_cols = slice(N_STATE + lo, N_STATE + lo + SCAN_LANES)
        d_re = dr_scr[pl.ds(r, SUBLANES), re_cols]
        d_im = dr_scr[pl.ds(r, SUBLANES), im_cols]
        for si, s in enumerate((1, 2, 4)):
            p_re = pw_ref[0, si, :, re_cols]
            p_im = pw_ref[1, si, :, re_cols]
            r_re = pltpu.roll(d_re, s, 0)
            r_im = pltpu.roll(d_im, s, 0)
            d_re, d_im = (d_re + (p_re * r_re - p_im * r_im),
                          d_im + (p_re * r_im + p_im * r_re))
        c_re, c_im = carry_fn(b, lo)
        a_re = pw_ref[0, 3, :, re_cols]
        a_im = pw_ref[1, 3, :, re_cols]
        h_re = d_re + (a_re * c_re - a_im * c_im)
        h_im = d_im + (a_re * c_im + a_im * c_re)
        dr_scr[pl.ds(r, SUBLANES), re_cols] = h_re
        dr_scr[pl.ds(r, SUBLANES), im_cols] = h_im
        if block_end_fn is not None:
            block_end_fn(b, lo, h_re[SUBLANES - 1:SUBLANES], h_im[SUBLANES - 1:SUBLANES])


def _ssm_scan(dr_scr, pw_ref, n_blocks, carry_fn, block_end_fn=None):
    def body(b, carry):
        _scan_block(dr_scr, pw_ref, b, carry_fn, block_end_fn)
        return carry

    lax.fori_loop(0, n_blocks, body, 0)


def _ssm_readout(dr_scr, cmat_ref, row0, n_rows):
    half_n = N_STATE // 2
    parts = []
    for hf in range(2):
        h_re = dr_scr[row0:row0 + n_rows, hf * half_n:(hf + 1) * half_n].astype(BF16)
        h_im = dr_scr[row0:row0 + n_rows, N_STATE + hf * half_n:N_STATE + (hf + 1) * half_n].astype(BF16)
        parts.append(jnp.dot(h_re, cmat_ref[hf, :half_n], preferred_element_type=F32)
                     + jnp.dot(h_im, cmat_ref[hf, half_n:], preferred_element_type=F32))
    return jnp.concatenate(parts, axis=-1)


def _ssm_gate(y, wglu_ref, bglu_ref):
    ya = jax.nn.gelu(y)
    gate = jnp.dot(ya.astype(BF16), wglu_ref[...], preferred_element_type=F32) + bglu_ref[...]
    return ya * jax.nn.sigmoid(gate)


def _unpermute_rows(perm_t_ref, x):
    hi = x.astype(BF16)
    rest = x - hi.astype(F32)
    mid = rest.astype(BF16)
    lo = (rest - mid.astype(F32)).astype(BF16)
    n = x.shape[1]
    out = jnp.dot(perm_t_ref[...], jnp.concatenate([hi, mid, lo], axis=1), preferred_element_type=F32)
    return out[:, :n] + out[:, n:2 * n] + out[:, 2 * n:]


def _rotary(x, cosv, sinv):
    return x * cosv + pltpu.roll(x, RET_HD // 2, 1) * sinv


def _head_norm_gate(o, g):
    mu = jnp.mean(o, axis=-1, keepdims=True)
    var = jnp.mean(jnp.square(o - mu), axis=-1, keepdims=True)
    return (o - mu) * lax.rsqrt(var + EPS) * jax.nn.silu(g)


def _pool_counts(pos):
    return [jnp.minimum(pos + 1, w).astype(F32) for w in POOL_WINDOWS]


def _mix_prompt_kernel(proj_ref, cos_ref, sin_ref, perm_ref, permt_ref, abar_ref, pws_ref,
                       bmat_ref, cmat_ref, dskip_ref, wglu_ref, bglu_ref,
                       mask_ref, qdec_ref, kdec_ref, cdec_ref, wpool_ref, pscale_ref,
                       y_ref, hre_ref, him_ref, sret_ref, pbuf_ref,
                       dr_scr, carry_scr, pool_scr, *, tl):
    t = pl.program_id(1)

    @pl.when(t == 0)
    def _():
        carry_scr[...] = jnp.zeros((SUBLANES, 2 * N_STATE), F32)
        sret_ref[...] = jnp.zeros(sret_ref.shape, F32)
        pool_scr[0:16, :] = jnp.zeros((16, D_POOL), F32)

    u = proj_ref[:, 0:D_SSM]
    up = jnp.dot(perm_ref[...], u.astype(BF16), preferred_element_type=F32).astype(BF16)
    _ssm_drive(up, bmat_ref, dr_scr, 0, tl)
    _scan_tile(dr_scr, abar_ref, pws_ref, carry_scr, tl // SUBLANES)
    y = _unpermute_rows(permt_ref, _ssm_readout(dr_scr, cmat_ref, 0, tl)) + dskip_ref[...] * u
    y_ref[:, 0:D_SSM] = _ssm_gate(y, wglu_ref, bglu_ref).astype(BF16)
    hre_ref[0] = carry_scr[SUBLANES - 1:SUBLANES, 0:N_STATE]
    him_ref[0] = carry_scr[SUBLANES - 1:SUBLANES, N_STATE:2 * N_STATE]

    scale = RET_HD ** -0.5

    def chunk_body(ci, carry):
        rows = pl.ds(pl.multiple_of(ci * RET_CHUNK, RET_CHUNK), RET_CHUNK)
        cosv = cos_ref[rows, :]
        sinv = sin_ref[rows, :]
        for h in range(RET_HEADS):
            hs = slice(h * RET_HD, (h + 1) * RET_HD)
            q = proj_ref[rows, Q_OFF + h * RET_HD:Q_OFF + (h + 1) * RET_HD]
            k = proj_ref[rows, K_OFF + h * RET_HD:K_OFF + (h + 1) * RET_HD]
            v = proj_ref[rows, V_OFF + h * RET_HD:V_OFF + (h + 1) * RET_HD].astype(BF16)
            g = proj_ref[rows, G_OFF + h * RET_HD:G_OFF + (h + 1) * RET_HD]
            qr = _rotary(q, cosv, sinv)
            kr = _rotary(k, cosv, sinv) * scale
            s_old = sret_ref[0, h]
            sc = lax.dot_general(qr.astype(BF16), kr.astype(BF16), (((1,), (1,)), ((), ())),
                                 preferred_element_type=F32) * mask_ref[h]
            o = (jnp.dot(sc.astype(BF16), v, preferred_element_type=F32)
                 + jnp.dot((qr * qdec_ref[:, hs]).astype(BF16), s_old.astype(BF16),
                           preferred_element_type=F32))
            sret_ref[0, h] = (s_old * cdec_ref[h:h + 1, :]
                              + lax.dot_general((kr * kdec_ref[:, hs]).astype(BF16), v,
                                                (((0,), (0,)), ((), ())), preferred_element_type=F32))
            y_ref[rows, D_SSM + h * RET_HD:D_SSM + (h + 1) * RET_HD] = _head_norm_gate(o, g).astype(BF16)
        return carry

    lax.fori_loop(0, tl // RET_CHUNK, chunk_body, 0, unroll=True)

    uc = proj_ref[:, C_OFF:C_OFF + D_POOL]
    pool_scr[16:16 + tl, :] = uc
    pos = t * tl + lax.broadcasted_iota(jnp.int32, (tl, POOL_GD), 0)
    counts = _pool_counts(pos)
    parts = []
    for gi, w in enumerate(POOL_WINDOWS):
        lanes = slice(gi * POOL_GD, (gi + 1) * POOL_GD)
        acc = pool_scr[:, lanes]
        shift = 1
        while shift < w:
            acc = acc + pltpu.roll(acc, shift, 0)
            shift *= 2
        parts.append(acc[16:] / counts[gi] - uc[:, lanes])
    pooled = jnp.concatenate(parts, axis=-1).astype(BF16)
    yc = jnp.dot(pooled, wpool_ref[...], preferred_element_type=F32) * pscale_ref[...]
    y_ref[:, D_SSM + D_RET:] = yc.astype(BF16)
    pbuf_ref[0] = pool_scr[tl + 1:tl + 16, :]
    pool_scr[0:16, :] = pool_scr[tl:tl + 16, :]


def _mix_prompt(proj, n_seq, seq_len, consts, tl, layer, prev_states):
    cos_t, sin_t = consts[:2]
    whole = consts[2:]
    nt = seq_len // tl
    row_map = lambda b, t: (b * nt + t, 0)

    def const_spec(a):
        nd = a.ndim
        return pl.BlockSpec(a.shape, lambda b, t: (0,) * nd)

    in_specs = [
        pl.BlockSpec((tl, D_IN), row_map),
        pl.BlockSpec((tl, RET_HD), lambda b, t: (t, 0)),
        pl.BlockSpec((tl, RET_HD), lambda b, t: (t, 0)),
    ] + [const_spec(a) for a in whole]
    out_shape = (
        jax.ShapeDtypeStruct((n_seq * seq_len, D_MODEL), BF16),
        jax.ShapeDtypeStruct((DEPTH, n_seq, 1, N_STATE), F32),
        jax.ShapeDtypeStruct((DEPTH, n_seq, 1, N_STATE), F32),
        jax.ShapeDtypeStruct((DEPTH, n_seq, RET_HEADS, RET_HD, RET_HD), F32),
        jax.ShapeDtypeStruct((DEPTH, n_seq, POOL_BUF, D_POOL), F32),
    )
    out_specs = (
        pl.BlockSpec((tl, D_MODEL), row_map),
        pl.BlockSpec((None, 1, 1, N_STATE), lambda b, t: (layer, b, 0, 0)),
        pl.BlockSpec((None, 1, 1, N_STATE), lambda b, t: (layer, b, 0, 0)),
        pl.BlockSpec((None, 1, RET_HEADS, RET_HD, RET_HD), lambda b, t: (layer, b, 0, 0, 0)),
        pl.BlockSpec((None, 1, POOL_BUF, D_POOL), lambda b, t: (layer, b, 0, 0)),
    )
    return _call_into(
        functools.partial(_mix_prompt_kernel, tl=tl),
        None if prev_states is None else {1 + k: arr for k, arr in enumerate(prev_states)},
        grid=(n_seq, nt),
        in_specs=in_specs,
        args=[proj, cos_t, sin_t, *whole],
        out_specs=out_specs,
        out_shape=out_shape,
        scratch_shapes=[
            pltpu.VMEM((tl, 2 * N_STATE), F32),
            pltpu.VMEM((SUBLANES, 2 * N_STATE), F32),
            pltpu.VMEM((16 + tl, D_POOL), F32),
        ],
        sem=("arbitrary", "arbitrary"),
        name="mix_prompt",
    )


def _mix_sample_kernel(proj_ref, cos_ref, sin_ref, pw_ref, bmat_ref, cmat_ref, dskip_ref, wglu_ref,
                       bglu_ref, mask_ref, qdec_ref, kdec_ref, cdec_ref, wpool_ref, pscale_ref,
                       h0re_ref, h0im_ref, s0_ref, pool0_ref,
                       y_ref, hre_ref, him_ref, sret_ref, pbuf_ref,
                       dr_scr, pool_scr, *, bs, seq_len, pos0):
    n_rows = bs * seq_len

    u = proj_ref[:, 0:D_SSM]
    _ssm_drive(u.astype(BF16), bmat_ref, dr_scr, SUBLANES, n_rows)

    def carry_fn(b, lo):
        return (h0re_ref[pl.ds(b, 1), lo:lo + SCAN_LANES], h0im_ref[pl.ds(b, 1), lo:lo + SCAN_LANES])

    def block_end_fn(b, lo, h_re, h_im):
        hre_ref[pl.ds(b, 1), lo:lo + SCAN_LANES] = h_re
        him_ref[pl.ds(b, 1), lo:lo + SCAN_LANES] = h_im

    _ssm_scan(dr_scr, pw_ref, bs, carry_fn, block_end_fn)
    y = _ssm_readout(dr_scr, cmat_ref, SUBLANES, n_rows) + dskip_ref[...] * u
    y_ref[:, 0:D_SSM] = _ssm_gate(y, wglu_ref, bglu_ref).astype(BF16)

    scale = RET_HD ** -0.5
    cosv = cos_ref[...]
    sinv = sin_ref[...]
    own = (lax.broadcasted_iota(jnp.int32, (n_rows, bs * RET_HD), 0) // seq_len
           == lax.broadcasted_iota(jnp.int32, (n_rows, bs * RET_HD), 1) // RET_HD)
    for h in range(RET_HEADS):
        hs = slice(h * RET_HD, (h + 1) * RET_HD)
        q = proj_ref[:, Q_OFF + h * RET_HD:Q_OFF + (h + 1) * RET_HD]
        k = proj_ref[:, K_OFF + h * RET_HD:K_OFF + (h + 1) * RET_HD]
        v = proj_ref[:, V_OFF + h * RET_HD:V_OFF + (h + 1) * RET_HD].astype(BF16)
        g = proj_ref[:, G_OFF + h * RET_HD:G_OFF + (h + 1) * RET_HD]
        qr = _rotary(q, cosv, sinv)
        kr = _rotary(k, cosv, sinv) * scale
        sc = lax.dot_general(qr.astype(BF16), kr.astype(BF16), (((1,), (1,)), ((), ())),
                             preferred_element_type=F32) * mask_ref[h]
        qd = jnp.where(own, jnp.concatenate([qr * qdec_ref[:, hs]] * bs, axis=1), 0.0).astype(BF16)
        kd = jnp.where(own, jnp.concatenate([kr * kdec_ref[:, hs]] * bs, axis=1), 0.0).astype(BF16)
        s_old = s0_ref[:, h].reshape(bs * RET_HD, RET_HD)
        o = (jnp.dot(sc.astype(BF16), v, preferred_element_type=F32)
             + jnp.dot(qd, s_old.astype(BF16), preferred_element_type=F32))
        s_new = (s_old * cdec_ref[h:h + 1, :]
                 + lax.dot_general(kd, v, (((0,), (0,)), ((), ())), preferred_element_type=F32))
        sret_ref[:, h] = s_new.reshape(bs, RET_HD, RET_HD)
        y_ref[:, D_SSM + h * RET_HD:D_SSM + (h + 1) * RET_HD] = _head_norm_gate(o, g).astype(BF16)

    uc = proj_ref[:, C_OFF:C_OFF + D_POOL].reshape(bs, seq_len, D_POOL)
    pool_scr[:, 1:16, :] = pool0_ref[...]
    pool_scr[:, 16:16 + seq_len, :] = uc
    pos = pos0 + lax.broadcasted_iota(jnp.int32, (1, seq_len, POOL_GD), 1)
    counts = _pool_counts(pos)
    parts = []
    for gi, w in enumerate(POOL_WINDOWS):
        lanes = slice(gi * POOL_GD, (gi + 1) * POOL_GD)
        acc = pool_scr[:, 16:16 + seq_len, lanes]
        for j in range(1, w):
            acc = acc + pool_scr[:, 16 - j:16 - j + seq_len, lanes]
        parts.append(acc / counts[gi] - uc[:, :, lanes])
    pooled = jnp.concatenate(parts, axis=-1).reshape(n_rows, D_POOL).astype(BF16)
    yc = jnp.dot(pooled, wpool_ref[...], preferred_element_type=F32) * pscale_ref[...]
    y_ref[:, D_SSM + D_RET:] = yc.astype(BF16)
    pbuf_ref[...] = pool_scr[:, seq_len + 1:seq_len + 16, :]


def _mix_sample(proj, proj_row0, states, layer, prev_out, n_seq, seq_len, consts, bs):
    n_rows = bs * seq_len

    def const_spec(a):
        nd = a.ndim
        return pl.BlockSpec(a.shape, lambda i: (0,) * nd)

    state_specs = [
        pl.BlockSpec((None, bs, N_STATE), lambda i: (layer, i, 0)),
        pl.BlockSpec((None, bs, N_STATE), lambda i: (layer, i, 0)),
        pl.BlockSpec((None, bs, RET_HEADS, RET_HD, RET_HD), lambda i: (layer, i, 0, 0, 0)),
        pl.BlockSpec((None, bs, POOL_BUF, D_POOL), lambda i: (layer, i, 0, 0)),
    ]
    assert proj_row0 % n_rows == 0
    in_specs = ([pl.BlockSpec((n_rows, D_IN), lambda i: (proj_row0 // n_rows + i, 0))]
                + [const_spec(a) for a in consts] + state_specs)
    args = [proj, *consts, *states]
    n_in = len(args)
    aliases = {}
    if prev_out is not None:
        in_specs += [pl.BlockSpec(memory_space=pl.ANY)] * len(prev_out)
        aliases = {n_in + k: 1 + k for k in range(len(prev_out))}
        args += list(prev_out)
    n_args = len(args)
    out_shape = (
        jax.ShapeDtypeStruct((n_seq * seq_len, D_MODEL), BF16),
        jax.ShapeDtypeStruct((DEPTH, n_seq, N_STATE), F32),
        jax.ShapeDtypeStruct((DEPTH, n_seq, N_STATE), F32),
        jax.ShapeDtypeStruct((DEPTH, n_seq, RET_HEADS, RET_HD, RET_HD), F32),
        jax.ShapeDtypeStruct((DEPTH, n_seq, POOL_BUF, D_POOL), F32),
    )
    out_specs = (pl.BlockSpec((n_rows, D_MODEL), lambda i: (i, 0)),) + tuple(state_specs)

    def body(*refs):
        _mix_sample_kernel(*refs[:n_in], *refs[n_args:], bs=bs, seq_len=seq_len, pos0=PAST_LEN)

    return pl.pallas_call(
        body,
        grid=(n_seq // bs,),
        in_specs=in_specs,
        out_specs=out_specs,
        out_shape=out_shape,
        scratch_shapes=[
            pltpu.VMEM((SUBLANES + n_rows, 2 * N_STATE), F32),
            pltpu.VMEM((bs, 16 + seq_len, D_POOL), F32),
        ],
        input_output_aliases=aliases,
        compiler_params=_params("arbitrary"),
        name="mix_sample",
    )(*args)


def _out_proj_kernel(y_ref, w_ref, x_ref, g1_ref, sh_ref, sc_ref, g_ref, xo_ref, h_ref, w_scr, *, bb, tl):
    @pl.when(pl.program_id(0) == 0)
    def _():
        w_scr[...] = w_ref[...].astype(BF16)

    f = jnp.dot(y_ref[...], w_scr[...], preferred_element_type=F32)
    xn = x_ref[...].reshape(bb, tl, D_MODEL) + g1_ref[...] * f.reshape(bb, tl, D_MODEL)
    xo_ref[...] = xn.reshape(bb * tl, D_MODEL)
    h = _rms(xn) * g_ref[...] * (1.0 + sc_ref[...]) + sh_ref[...]
    h_ref[...] = h.reshape(bb * tl, D_MODEL).astype(BF16)


def _out_proj(y, w, layer, x, x_tile0, mod, g, rows, out_rows, prev):
    out_spec = pl.BlockSpec((rows.tm, D_MODEL), lambda i: (rows.first_tile + i, 0))
    return _call_into(
        functools.partial(_out_proj_kernel, bb=rows.bb, tl=rows.tl),
        None if prev is None else {0: prev[0], 1: prev[1]},
        grid=(rows.n_tiles,),
        in_specs=[
            pl.BlockSpec((rows.tm, D_MODEL), lambda i: (i, 0)),
            pl.BlockSpec((None, D_MODEL, D_MODEL), lambda i: (layer, 0, 0), pipeline_mode=pl.Buffered(1)),
            pl.BlockSpec((rows.tm, D_MODEL), lambda i: (x_tile0 + i, 0)),
            rows.mod_spec(layer, MOD_GATE1, lambda i: i),
            rows.mod_spec(layer, MOD_SHIFT2, lambda i: i),
            rows.mod_spec(layer, MOD_SCALE2, lambda i: i),
            pl.BlockSpec((None, 1, D_MODEL), lambda i: (layer, 0, 0)),
        ],
        args=[y, w, x, mod, mod, mod, g],
        out_specs=(out_spec, out_spec),
        out_shape=(jax.ShapeDtypeStruct((out_rows, D_MODEL), F32),
                   jax.ShapeDtypeStruct((out_rows, D_MODEL), BF16)),
        scratch_shapes=[pltpu.VMEM((D_MODEL, D_MODEL), BF16)],
        sem=("arbitrary",),
        name="out_proj",
    )


def _conv_gate(a, b, am1, am2, cw, cb):
    conv = cb + am2 * cw[0:1]
    conv = conv + am1 * cw[1:2]
    conv = conv + a * cw[2:3]
    return jax.nn.silu(conv) * b


def _ffn_up_kernel(h_ref, wa_ref, wb_ref, cw_ref, cb_ref, st_ref, act_ref, tail_ref, ns_ref, w_scr, carry_scr,
                   *, prompt, n_seq, seq_len, n_chunks):
    c = pl.program_id(0)
    i = pl.program_id(1)

    @pl.when(i == 0)
    def _():
        w_scr[:, :FF_TILE] = wa_ref[...].astype(BF16)

    @pl.when((i == 0) & (c < n_chunks - 1))
    def _():
        w_scr[:, FF_TILE:] = wb_ref[0].astype(BF16)

    @pl.when((i == 0) & (c == n_chunks - 1))
    def _():
        w_scr[:, FF_TILE:FF_TILE + FF_LAST] = wb_ref[0, :, FF_TILE - FF_LAST:].astype(BF16)

    cw = cw_ref[...]
    cb = cb_ref[...]

    @pl.when(i >= prompt.n_tiles)
    def _():
        tl = seq_len
        sb = FF_SUB_ROWS // tl
        tok = lax.broadcasted_iota(jnp.int32, (sb, tl, FF_TILE), 1)
        for r in range(n_seq * seq_len // FF_SUB_ROWS):
            rows = slice(r * FF_SUB_ROWS, (r + 1) * FF_SUB_ROWS)
            sq = slice(r * sb, (r + 1) * sb)
            ab = jnp.dot(h_ref[rows, :], w_scr[...], preferred_element_type=F32)
            a = ab[:, :FF_TILE].reshape(sb, tl, FF_TILE)
            b = ab[:, FF_TILE:].reshape(sb, tl, FF_TILE)
            p0 = st_ref[sq, 0:1, :]
            p1 = st_ref[sq, 1:2, :]
            am1 = jnp.where(tok == 0, p1, pltpu.roll(a, 1, 1))
            am2 = jnp.where(tok == 0, p0, jnp.where(tok == 1, p1, pltpu.roll(a, 2, 1)))
            act = _conv_gate(a, b, am1, am2, cw, cb)
            act_ref[rows, :] = act.reshape(FF_SUB_ROWS, FF_TILE).astype(BF16)
            ns_ref[sq] = a[:, tl - 2:tl, :]

    @pl.when(i < prompt.n_tiles)
    def _():
        @pl.when(i % prompt.tiles_per_seq == 0)
        def _():
            carry_scr[...] = jnp.zeros((SUBLANES, FF_TILE), F32)

        tok = lax.broadcasted_iota(jnp.int32, (SUBLANES, FF_TILE), 0)
        for r in range(prompt.tm // FF_SUB_ROWS):
            rows = slice(r * FF_SUB_ROWS, (r + 1) * FF_SUB_ROWS)
            ab = jnp.dot(h_ref[rows, :], w_scr[...], preferred_element_type=F32)
            a = ab[:, :FF_TILE]
            b = ab[:, FF_TILE:]
            p0 = carry_scr[SUBLANES - 2:SUBLANES - 1, :]
            p1 = carry_scr[SUBLANES - 1:SUBLANES, :]
            r1 = pltpu.roll(a, 1, 0)
            r2 = pltpu.roll(a, 2, 0)
            head1 = jnp.where(tok == 0, p1, r1[:SUBLANES])
            head2 = jnp.where(tok == 0, p0, jnp.where(tok == 1, p1, r2[:SUBLANES]))
            am1 = jnp.concatenate([head1, r1[SUBLANES:]], axis=0)
            am2 = jnp.concatenate([head2, r2[SUBLANES:]], axis=0)
            act_ref[rows, :] = _conv_gate(a, b, am1, am2, cw, cb).astype(BF16)
            carry_scr[...] = a[FF_SUB_ROWS - SUBLANES:, :]
        tail_ref[0] = carry_scr[SUBLANES - 2:SUBLANES, :]


def _ffn_up(h, w_up, conv_w, conv_b, layer, state, prev_state, prompt, n_seq, seq_len):
    m = h.shape[0]
    tm = prompt.tm
    assert m == prompt.n_tiles * tm + n_seq * seq_len and n_seq * seq_len <= tm
    assert (n_seq * seq_len) % FF_SUB_ROWS == 0 and tm % FF_SUB_ROWS == 0
    n_tiles = prompt.n_tiles + 1
    n_chunks = pl.cdiv(D_FF, FF_TILE)
    in_specs = [
        pl.BlockSpec((tm, D_MODEL), lambda c, i: (i, 0)),
        pl.BlockSpec((None, D_MODEL, FF_TILE), lambda c, i: (layer, 0, c)),
        pl.BlockSpec((pl.Element(1), pl.Element(D_MODEL), pl.Element(FF_TILE)),
                     lambda c, i: (layer, 0, LANES * jnp.minimum((D_FF + c * FF_TILE) // LANES,
                                                                 (2 * D_FF - FF_TILE) // LANES))),
        pl.BlockSpec((None, CONV_W, FF_TILE), lambda c, i: (layer, 0, c)),
        pl.BlockSpec((None, 1, FF_TILE), lambda c, i: (layer, 0, c)),
        pl.BlockSpec((None, n_seq, CONV_W - 1, FF_TILE), lambda c, i: (layer, 0, 0, c)),
    ]
    return _call_into(
        functools.partial(_ffn_up_kernel, prompt=prompt, n_seq=n_seq, seq_len=seq_len, n_chunks=n_chunks),
        None if prev_state is None else {2: prev_state},
        grid=(n_chunks, n_tiles),
        in_specs=in_specs,
        args=[h, w_up, w_up, conv_w, conv_b, state],
        out_specs=(
            pl.BlockSpec((tm, FF_TILE), lambda c, i: (i, c)),
            pl.BlockSpec((1, CONV_W - 1, FF_TILE), lambda c, i: (jnp.minimum(i, prompt.n_tiles - 1), 0, c)),
            pl.BlockSpec((None, n_seq, CONV_W - 1, FF_TILE), lambda c, i: (layer, 0, 0, c)),
        ),
        out_shape=(jax.ShapeDtypeStruct((m, D_FF), BF16),
                   jax.ShapeDtypeStruct((prompt.n_tiles, CONV_W - 1, D_FF), F32),
                   jax.ShapeDtypeStruct((DEPTH, n_seq, CONV_W - 1, D_FF), F32)),
        scratch_shapes=[pltpu.VMEM((D_MODEL, 2 * FF_TILE), BF16), pltpu.VMEM((SUBLANES, FF_TILE), F32)],
        sem=("arbitrary", "arbitrary"),
        name="ffn_up",
    )


def _cmul(ar, ai, br, bi):
    return ar * br - ai * bi, ar * bi + ai * br


def _ssm_consts(a_re, a_im, log_dt, b_re, b_im, c_re, c_im, scan_steps):
    lam_re, lam_im = a_re.astype(F32), a_im.astype(F32)
    dt = jnp.exp(log_dt.astype(F32))[:, None]
    mag = jnp.exp(lam_re * dt)
    ab_re, ab_im = mag * jnp.cos(lam_im * dt), mag * jnp.sin(lam_im * dt)
    den = lam_re * lam_re + lam_im * lam_im
    f_re = ((ab_re - 1.0) * lam_re + ab_im * lam_im) / den
    f_im = (ab_im * lam_re - (ab_re - 1.0) * lam_im) / den
    br, bi = b_re.astype(F32), b_im.astype(F32)
    bb_re = f_re[..., None] * br - f_im[..., None] * bi
    bb_im = f_re[..., None] * bi + f_im[..., None] * br

    gh = SSM_GROUPS // 2
    eye = jnp.eye(gh, dtype=F32)

    def drive_half(m):
        return jnp.einsum("gph,gk->ghkp", m, eye).reshape(gh * SSM_GROUP, gh * SSM_STATE)

    def read_half(m):
        return jnp.einsum("ghp,gk->gpkh", m, eye).reshape(gh * SSM_STATE, gh * SSM_GROUP)

    bmat = jnp.stack([
        jnp.concatenate([drive_half(bb_re[s]), drive_half(bb_im[s])], axis=1)
        for s in (slice(0, gh), slice(gh, 2 * gh))]).astype(BF16)
    cre, cim = c_re.astype(F32), c_im.astype(F32)
    cmat = jnp.stack([
        jnp.concatenate([read_half(cre[s]), -read_half(cim[s])], axis=0)
        for s in (slice(0, gh), slice(gh, 2 * gh))]).astype(BF16)

    step_re, step_im = _power_table(ab_re.reshape(1, N_STATE), ab_im.reshape(1, N_STATE), scan_steps)
    sub_re, sub_im = _power_table(step_re[-1:], step_im[-1:], SUBLANES)
    abar = jnp.stack([step_re[:1], step_im[:1]])
    pw_block = _scan_planes(step_re[:SUBLANES], step_im[:SUBLANES])
    pw_sub = _scan_planes(sub_re, sub_im)
    return bmat, cmat, abar, pw_block, pw_sub


def _power_table(re, im, n):
    while re.shape[0] < n:
        top_re, top_im = re[-1:], im[-1:]
        more_re, more_im = _cmul(re, im, top_re, top_im)
        re, im = jnp.concatenate([re, more_re]), jnp.concatenate([im, more_im])
    return re[:n], im[:n]


def _scan_planes(re, im):
    row = jnp.arange(SUBLANES)[:, None]
    planes_re = [jnp.where(row >= s, re[s - 1][None, :], 0.0) for s in (1, 2, 4)] + [re]
    planes_im = [jnp.where(row >= s, im[s - 1][None, :], 0.0) for s in (1, 2, 4)] + [im]
    return jnp.stack([jnp.stack(planes_re), jnp.stack(planes_im)])


def _time_on_sublanes(n_rows):
    steps = n_rows // SUBLANES
    p = np.zeros((n_rows, n_rows), np.float32)
    r = np.arange(n_rows)
    p[r, (r % SUBLANES) * steps + r // SUBLANES] = 1.0
    return jnp.asarray(p, BF16), jnp.asarray(p.T, BF16)


def _rotary_tables(pos):
    half = RET_HD // 2
    inv = ROPE_BASE ** (-np.arange(half, dtype=np.float64) / half)
    ang = np.asarray(pos, np.float64)[:, None] * inv[None, :]
    cos, sin = np.cos(ang), np.sin(ang)
    return (np.concatenate([cos, cos], axis=-1).astype(np.float32),
            np.concatenate([-sin, sin], axis=-1).astype(np.float32))


def _retention_consts(chunk, n_seq):
    lg = np.log1p(-np.exp2(-5.0 - np.arange(RET_HEADS, dtype=np.float64)))
    r = np.arange(chunk * n_seq)
    ti = (r % chunk).astype(np.float64)
    seq = r // chunk
    diff = ti[:, None] - ti[None, :]
    keep = (seq[:, None] == seq[None, :]) & (diff >= 0.0)
    mask = np.where(keep[None], np.exp(np.maximum(diff, 0.0)[None] * lg[:, None, None]), 0.0)
    q_dec = np.exp((ti + 1.0)[:, None] * lg[None, :])
    k_dec = np.exp((chunk - 1.0 - ti)[:, None] * lg[None, :])
    c_dec = np.exp(chunk * lg)
    expand = lambda d: np.repeat(d, RET_HD, axis=1).astype(np.float32)
    return (mask.astype(np.float32), expand(q_dec), expand(k_dec),
            np.broadcast_to(c_dec[:, None], (RET_HEADS, RET_HD)).astype(np.float32))


def _pool_weight(w_pool):
    eye = jnp.eye(len(POOL_WINDOWS), dtype=F32)
    return jnp.einsum("gcd,gk->gckd", w_pool, eye).reshape(D_POOL, D_POOL).astype(BF16)


def kernel(x_prompt, x_sample, state_ssm_re, state_ssm_im, state_ret, state_pool, state_ffn_conv, c_prompt, c_sample, w_ada, b_ada, norm1_g, w_in, ssm_a_re, ssm_a_im, ssm_log_dt, ssm_b_re, ssm_b_im, ssm_c_re, ssm_c_im, ssm_d, ssm_w_glu, ssm_b_glu, pool_w, pool_scale, w_out, norm2_g, ffn_w_up, ffn_conv_w, ffn_conv_b, ffn_w_down, final_norm_g):
    bp, lp, _ = x_prompt.shape
    bs, ls, _ = x_sample.shape
    tl_mix, seq_mix = MIX_TOKENS, MIX_SEQS

    n_all = bp + bs
    n_pad = -(-n_all // SUBLANES) * SUBLANES
    c_all = jnp.concatenate([c_sample, c_prompt, jnp.zeros((n_pad - n_all, D_MODEL), F32)], axis=0)
    mod = _ada_mod(c_all, w_ada, b_ada)

    cos_p, sin_p = _rotary_tables(np.arange(lp))
    cos_s, sin_s = _rotary_tables(PAST_LEN + np.arange(ls))
    cos_s, sin_s = np.tile(cos_s, (seq_mix, 1)), np.tile(sin_s, (seq_mix, 1))
    assert math.gcd(lp, RET_CHUNK) == RET_CHUNK and math.gcd(ls, RET_CHUNK) == ls == SUBLANES
    ret_p = _retention_consts(RET_CHUNK, 1)
    ret_s = _retention_consts(ls, seq_mix)
    perm = _time_on_sublanes(tl_mix)

    rows_total = bp * lp + bs * ls
    groups = (_Rows(bp, lp, ROW_TILE, bs, 0), _Rows(bs, ls, ROW_TILE, 0, bp * lp))
    groups_full = (_Rows(bp, lp, ROW_TILE_FULL, bs, 0), _Rows(bs, ls, ROW_TILE_FULL, 0, bp * lp))
    big_p, big_s = groups

    norm1 = norm1_g.reshape(DEPTH, 1, D_MODEL)
    norm2 = norm2_g.reshape(DEPTH, 1, D_MODEL)
    final_g = final_norm_g.reshape(1, 1, D_MODEL)
    conv_b = ffn_conv_b.reshape(DEPTH, 1, D_FF)
    sample_states = (state_ssm_re.reshape(DEPTH, bs, N_STATE), state_ssm_im.reshape(DEPTH, bs, N_STATE),
                     state_ret, state_pool)

    x_first = (x_prompt.reshape(bp * lp, D_MODEL), x_sample.reshape(bs * ls, D_MODEL))
    x_all = None

    def residual_rows(k, rows):
        return (x_first[k], 0) if x_all is None else (x_all, rows.first_tile)

    p_mix = s_mix = None
    p_conv = []
    s_conv = None
    for l in range(DEPTH):
        bmat, cmat, abar, pw_block, pw_sub = _ssm_consts(
            ssm_a_re[l], ssm_a_im[l], ssm_log_dt[l], ssm_b_re[l], ssm_b_im[l], ssm_c_re[l], ssm_c_im[l],
            tl_mix // SUBLANES)
        shared = (bmat, cmat, ssm_d[l].reshape(1, D_SSM), ssm_w_glu[l].astype(BF16),
                  ssm_b_glu[l].reshape(1, D_SSM))
        pool_c = (_pool_weight(pool_w[l]), pool_scale[l].reshape(1, D_POOL))
        pre = (mod, MOD_SHIFT1, MOD_SCALE1)

        h = None
        for k, rows in enumerate(groups):
            h = _norm(*residual_rows(k, rows), norm1, l, pre, rows, BF16, rows_total, h)
        proj = _matmul(h, w_in, l, ROW_TILE, IN_PROJ_COLS, "in_proj")
        y_p, *p_mix = _mix_prompt(
            proj, bp, lp, (cos_p, sin_p) + perm + (abar, pw_sub) + shared + ret_p + pool_c, tl_mix, l, p_mix)
        y_s, *s_mix = _mix_sample(proj, bp * lp, sample_states, l, s_mix, bs, ls,
                                  (cos_s, sin_s, pw_block) + shared + ret_s + pool_c, seq_mix)
        xh = None
        for k, (rows, y) in enumerate(zip(groups_full, (y_p, y_s))):
            xh = _out_proj(y, w_out, l, *residual_rows(k, rows), mod, norm2, rows, rows_total, xh)
        xn, h2 = xh
        act, tails, s_conv = _ffn_up(h2, ffn_w_up, ffn_conv_w, conv_b, l, state_ffn_conv, s_conv, big_p, bs, ls)
        x_all = _matmul(act, ffn_w_down, l, ROW_TILE, FF_DOWN_COLS, "ffn_down", groups=groups,
                        residual=(xn, mod, MOD_GATE2), single_buffer_w=True)
        p_conv.append(tails[big_p.tiles_per_seq - 1::big_p.tiles_per_seq])

    yp = _norm(x_all, big_p.first_tile, final_g, 0, None, big_p, F32)
    ys = _norm(x_all, big_s.first_tile, final_g, 0, None, big_s, F32)

    p_re, p_im, p_ret, p_pool = p_mix
    p_conv = jnp.stack(p_conv)
    s_re, s_im, s_ret, s_pool = s_mix
    shape_p = (DEPTH, bp, SSM_GROUPS, SSM_STATE)
    shape_s = (DEPTH, bs, SSM_GROUPS, SSM_STATE)
    return (yp.reshape(bp, lp, D_MODEL), ys.reshape(bs, ls, D_MODEL),
            p_re.reshape(shape_p), p_im.reshape(shape_p), p_ret, p_pool, p_conv,
            s_re.reshape(shape_s), s_im.reshape(shape_s), s_ret, s_pool, s_conv)
```

```python
import functools
import math

import jax
import jax.numpy as jnp
import numpy as np
from jax import lax
from jax.experimental import pallas as pl
from jax.experimental.pallas import tpu as pltpu

F32 = jnp.float32
BF16 = jnp.bfloat16

D_MODEL = 2048
DEPTH = 2
PAST_LEN = 16384
D_SSM = 512
D_RET = 1024
D_POOL = 512
SSM_GROUP = 16
SSM_GROUPS = 32
SSM_STATE = 64
N_STATE = SSM_GROUPS * SSM_STATE
RET_HEADS = 8
RET_HD = 128
RET_CHUNK = 128
POOL_WINDOWS = (2, 4, 8, 16)
POOL_GD = 128
POOL_BUF = 15
D_FF = 5504
FF_TILE = 512
FF_LAST = D_FF % FF_TILE
CONV_W = 3
D_IN = D_SSM + 4 * D_RET + D_POOL
ROPE_BASE = 10000.0
EPS = 1e-6

SUBLANES = 8
LANES = 128
ROW_TILE = 1024
ROW_TILE_FULL = 512
FF_SUB_ROWS = 512
IN_PROJ_COLS = 1024
FF_DOWN_COLS = 512
ADA_COLS = 1024
MOD_SHIFT1, MOD_SCALE1, MOD_GATE1, MOD_SHIFT2, MOD_SCALE2, MOD_GATE2 = range(6)
SCAN_LANES = 512
MIX_TOKENS = 512
MIX_SEQS = 16
VMEM_LIMIT = 56 * 1024 * 1024

Q_OFF = D_SSM
K_OFF = D_SSM + D_RET
V_OFF = D_SSM + 2 * D_RET
G_OFF = D_SSM + 3 * D_RET
C_OFF = D_SSM + 4 * D_RET


def _params(*sem):
    return pltpu.CompilerParams(dimension_semantics=sem, vmem_limit_bytes=VMEM_LIMIT)


def _rms(x):
    return x * lax.rsqrt(jnp.mean(x * x, axis=-1, keepdims=True) + EPS)


def _ada_kernel(c_ref, w_ref, b_ref, o_ref):
    a = jax.nn.silu(c_ref[...]).astype(BF16)
    res = jnp.dot(a, w_ref[...].astype(BF16), preferred_element_type=F32) + b_ref[...]
    for r in range(res.shape[0]):
        o_ref[r] = res[r:r + 1, :]


def _ada_mod(c_all, w_ada, b_ada):
    nseq = c_all.shape[0]
    per_vec = D_MODEL // ADA_COLS
    return pl.pallas_call(
        _ada_kernel,
        grid=(DEPTH, 6 * per_vec),
        in_specs=[
            pl.BlockSpec((nseq, D_MODEL), lambda l, j: (0, 0)),
            pl.BlockSpec((None, D_MODEL, ADA_COLS), lambda l, j: (l, 0, j)),
            pl.BlockSpec((None, 1, ADA_COLS), lambda l, j: (l, 0, j)),
        ],
        out_specs=pl.BlockSpec((None, None, nseq, 1, ADA_COLS), lambda l, j: (l, j // per_vec, 0, 0, j % per_vec)),
        out_shape=jax.ShapeDtypeStruct((DEPTH, 6, nseq, 1, D_MODEL), F32),
        compiler_params=_params("arbitrary", "arbitrary"),
        name="ada_mod",
    )(c_all, w_ada, b_ada.reshape(DEPTH, 1, 6 * D_MODEL))


class _Rows:
    def __init__(self, n_seq, seq_len, tm, first_seq, first_row):
        if seq_len >= tm:
            assert seq_len % tm == 0
            self.bb, self.tl = 1, tm
            self.tiles_per_seq = seq_len // tm
        else:
            assert tm % seq_len == 0 and seq_len == SUBLANES
            self.bb, self.tl = tm // seq_len, seq_len
            self.tiles_per_seq = 1
        assert first_seq % self.bb == 0 and first_row % tm == 0
        self.tm = tm
        self.n_tiles = n_seq * seq_len // tm
        self.first_block = first_seq // self.bb
        self.first_tile = first_row // tm

    def mod_block(self, i):
        return self.first_block + i // self.tiles_per_seq

    def mod_spec(self, layer, vec, row_tile_of, cols=D_MODEL, col_block_of=lambda *ids: 0):
        return pl.BlockSpec(
            (None, None, self.bb, 1, cols),
            lambda *ids: (layer, vec, self.mod_block(row_tile_of(*ids)), 0, col_block_of(*ids)))


def _call_into(kernel_fn, prev, *, grid, in_specs, args, out_specs, out_shape, scratch_shapes=(), sem, name):
    n_in = len(args)
    in_specs, args = list(in_specs), list(args)
    aliases = {}
    for k, arr in sorted((prev or {}).items()):
        aliases[len(args)] = k
        in_specs.append(pl.BlockSpec(memory_space=pl.ANY))
        args.append(arr)
    n_args = len(args)

    def body(*refs):
        kernel_fn(*refs[:n_in], *refs[n_args:])

    return pl.pallas_call(
        body, grid=grid, in_specs=in_specs, out_specs=out_specs, out_shape=out_shape,
        scratch_shapes=list(scratch_shapes), input_output_aliases=aliases,
        compiler_params=_params(*sem), name=name,
    )(*args)


def _norm_kernel(*refs, bb, tl, modulated):
    if modulated:
        x_ref, g_ref, sh_ref, sc_ref, o_ref = refs
    else:
        x_ref, g_ref, o_ref = refs
    h = _rms(x_ref[...].reshape(bb, tl, D_MODEL)) * g_ref[...]
    if modulated:
        h = h * (1.0 + sc_ref[...]) + sh_ref[...]
    o_ref[...] = h.reshape(bb * tl, D_MODEL).astype(o_ref.dtype)


def _norm(x, x_tile0, g, layer, mods, rows, out_dtype, out_rows=None, prev=None):
    in_specs = [pl.BlockSpec((rows.tm, D_MODEL), lambda i: (x_tile0 + i, 0)),
                pl.BlockSpec((None, 1, D_MODEL), lambda i: (layer, 0, 0))]
    args = [x, g]
    if mods is not None:
        mod, shift_vec, scale_vec = mods
        in_specs += [rows.mod_spec(layer, shift_vec, lambda i: i), rows.mod_spec(layer, scale_vec, lambda i: i)]
        args += [mod, mod]
    out_tile0 = 0 if out_rows is None else rows.first_tile
    return _call_into(
        functools.partial(_norm_kernel, bb=rows.bb, tl=rows.tl, modulated=mods is not None),
        None if prev is None else {0: prev},
        grid=(rows.n_tiles,),
        in_specs=in_specs,
        args=args,
        out_specs=pl.BlockSpec((rows.tm, D_MODEL), lambda i: (out_tile0 + i, 0)),
        out_shape=jax.ShapeDtypeStruct((out_rows or rows.n_tiles * rows.tm, D_MODEL), out_dtype),
        sem=("arbitrary",),
        name="norm",
    )


def _matmul_kernel(*refs, groups, residual, prefetch):
    if prefetch:
        *refs, w_stage, w_sem = refs
    if residual:
        a_ref, w_ref, x_ref, *gate_refs, o_ref, w_scr = refs
    else:
        a_ref, w_ref, o_ref, w_scr = refs
    i = pl.program_id(1)

    if prefetch:
        layer, tn_w = prefetch
        j = pl.program_id(0)

        def chunk_copy(c):
            cols = pl.ds(pl.multiple_of(c * tn_w, tn_w), tn_w)
            return pltpu.make_async_copy(w_ref.at[layer, :, cols], w_stage, w_sem)

        @pl.when(i == 0)
        def _():
            @pl.when(j == 0)
            def _():
                chunk_copy(0).start()

            chunk_copy(j).wait()
            w_scr[...] = w_stage[...].astype(BF16)

            @pl.when(j + 1 < pl.num_programs(0))
            def _():
                chunk_copy(j + 1).start()
    else:
        @pl.when(i == 0)
        def _():
            w_scr[...] = w_ref[...].astype(BF16)

    f = jnp.dot(a_ref[...], w_scr[...], preferred_element_type=F32)
    if not residual:
        o_ref[...] = f
        return
    tn = f.shape[1]
    for rows, g_ref in zip(groups, gate_refs):
        @pl.when((i >= rows.first_tile) & (i < rows.first_tile + rows.n_tiles))
        def _(rows=rows, g_ref=g_ref):
            xn = x_ref[...].reshape(rows.bb, rows.tl, tn) + g_ref[...] * f.reshape(rows.bb, rows.tl, tn)
            o_ref[...] = xn.reshape(rows.tm, tn)


def _matmul(a, w, layer, tm, tn, name, groups=(), residual=None, single_buffer_w=False):
    m, k = a.shape
    n = w.shape[2]
    n_tiles = pl.cdiv(m, tm)
    if residual is not None:
        assert m == sum(rows.n_tiles for rows in groups) * tm and all(rows.tm == tm for rows in groups)
    w_spec = (pl.BlockSpec(memory_space=pl.ANY) if single_buffer_w
              else pl.BlockSpec((None, k, tn), lambda j, i: (layer, 0, j)))
    in_specs = [pl.BlockSpec((tm, k), lambda j, i: (i, 0)), w_spec]
    scratch = [pltpu.VMEM((k, tn), BF16)]
    if single_buffer_w:
        scratch += [pltpu.VMEM((k, tn), F32), pltpu.SemaphoreType.DMA(())]
    args = [a, w]
    if residual is not None:
        x, mod, gate_vec = residual
        in_specs.append(pl.BlockSpec((tm, tn), lambda j, i: (i, j)))
        args.append(x)
        for rows in groups:
            in_specs.append(rows.mod_spec(
                layer, gate_vec,
                lambda j, i, rows=rows: jnp.clip(i - rows.first_tile, 0, rows.n_tiles - 1),
                tn, lambda j, i: j))
            args.append(mod)
    return pl.pallas_call(
        functools.partial(_matmul_kernel, groups=groups, residual=residual is not None,
                          prefetch=(layer, tn) if single_buffer_w else None),
        grid=(n // tn, n_tiles),
        in_specs=in_specs,
        out_specs=pl.BlockSpec((tm, tn), lambda j, i: (i, j)),
        out_shape=jax.ShapeDtypeStruct((m, n), F32),
        scratch_shapes=scratch,
        compiler_params=_params("arbitrary", "arbitrary"),
        name=name,
    )(*args)


def _ssm_drive(ub, bmat_ref, dr_scr, row0, n_rows):
    half_u = D_SSM // 2
    half_n = N_STATE // 2
    for hf in range(2):
        d = jnp.dot(ub[:, hf * half_u:(hf + 1) * half_u], bmat_ref[hf], preferred_element_type=F32)
        dr_scr[row0:row0 + n_rows, hf * half_n:(hf + 1) * half_n] = d[:, :half_n]
        dr_scr[row0:row0 + n_rows, N_STATE + hf * half_n:N_STATE + (hf + 1) * half_n] = d[:, half_n:]


def _cmul_add(a_re, a_im, x_re, x_im, y_re, y_im):
    return y_re + (a_re * x_re - a_im * x_im), y_im + (a_re * x_im + a_im * x_re)


def _scan_tile(dr_scr, abar_ref, pws_ref, carry_scr, n_steps):
    row = lax.broadcasted_iota(jnp.int32, (SUBLANES, SCAN_LANES), 0)
    for c in range(N_STATE // SCAN_LANES):
        lo = c * SCAN_LANES
        re_cols = slice(lo, lo + SCAN_LANES)
        im_cols = slice(N_STATE + lo, N_STATE + lo + SCAN_LANES)
        a_re = jnp.broadcast_to(abar_ref[0, :, re_cols], (SUBLANES, SCAN_LANES))
        a_im = jnp.broadcast_to(abar_ref[1, :, re_cols], (SUBLANES, SCAN_LANES))

        def local_step(t, h):
            r = pl.ds(pl.multiple_of(t * SUBLANES, SUBLANES), SUBLANES)
            h_re, h_im = _cmul_add(a_re, a_im, h[0], h[1], dr_scr[r, re_cols], dr_scr[r, im_cols])
            dr_scr[r, re_cols] = h_re
            dr_scr[r, im_cols] = h_im
            return h_re, h_im

        zero = jnp.zeros((SUBLANES, SCAN_LANES), F32)
        g_re, g_im = lax.fori_loop(0, n_steps, local_step, (zero, zero), unroll=2)
        for si, s in enumerate((1, 2, 4)):
            g_re, g_im = _cmul_add(pws_ref[0, si, :, re_cols], pws_ref[1, si, :, re_cols],
                                   pltpu.roll(g_re, s, 0), pltpu.roll(g_im, s, 0), g_re, g_im)
        c_re = carry_scr[SUBLANES - 1:SUBLANES, re_cols]
        c_im = carry_scr[SUBLANES - 1:SUBLANES, im_cols]
        e_re, e_im = _cmul_add(pws_ref[0, 3, :, re_cols], pws_ref[1, 3, :, re_cols], c_re, c_im, g_re, g_im)
        carry_scr[:, re_cols] = e_re
        carry_scr[:, im_cols] = e_im
        in_re = jnp.where(row == 0, c_re, pltpu.roll(e_re, 1, 0))
        in_im = jnp.where(row == 0, c_im, pltpu.roll(e_im, 1, 0))

        def fix_step(t, w):
            w_re, w_im = _cmul(a_re, a_im, w[0], w[1])
            r = pl.ds(pl.multiple_of(t * SUBLANES, SUBLANES), SUBLANES)
            dr_scr[r, re_cols] = dr_scr[r, re_cols] + w_re
            dr_scr[r, im_cols] = dr_scr[r, im_cols] + w_im
            return w_re, w_im

        lax.fori_loop(0, n_steps, fix_step, (in_re, in_im), unroll=2)


def _scan_block(dr_scr, pw_ref, b, carry_fn, block_end_fn=None):
    r = pl.multiple_of(b * SUBLANES + SUBLANES, SUBLANES)
    for c in range(N_STATE // SCAN_LANES):
        lo = c * SCAN_LANES
        re_cols = slice(lo, lo + SCAN_LANES)
        im_cols = slice(N_STATE + lo, N_STATE + lo + SCAN_LANES)
        d_re = dr_scr[pl.ds(r, SUBLANES), re_cols]
        d_im = dr_scr[pl.ds(r, SUBLANES), im_cols]
        for si, s in enumerate((1, 2, 4)):
            p_re = pw_ref[0, si, :, re_cols]
            p_im = pw_ref[1, si, :, re_cols]
            r_re = pltpu.roll(d_re, s, 0)
            r_im = pltpu.roll(d_im, s, 0)
            d_re, d_im = (d_re + (p_re * r_re - p_im * r_im),
                          d_im + (p_re * r_im + p_im * r_re))
        c_re, c_im = carry_fn(b, lo)
        a_re = pw_ref[0, 3, :, re_cols]
        a_im = pw_ref[1, 3, :, re_cols]
        h_re = d_re + (a_re * c_re - a_im * c_im)
        h_im = d_im + (a_re * c_im + a_im * c_re)
        dr_scr[pl.ds(r, SUBLANES), re_cols] = h_re
        dr_scr[pl.ds(r, SUBLANES), im_cols] = h_im
        if block_end_fn is not None:
            block_end_fn(b, lo, h_re[SUBLANES - 1:SUBLANES], h_im[SUBLANES - 1:SUBLANES])


def _ssm_scan(dr_scr, pw_ref, n_blocks, carry_fn, block_end_fn=None):
    def body(b, carry):
        _scan_block(dr_scr, pw_ref, b, carry_fn, block_end_fn)
        return carry

    lax.fori_loop(0, n_blocks, body, 0)


def _ssm_readout(dr_scr, cmat_ref, row0, n_rows):
    half_n = N_STATE // 2
    parts = []
    for hf in range(2):
        h_re = dr_scr[row0:row0 + n_rows, hf * half_n:(hf + 1) * half_n].astype(BF16)
        h_im = dr_scr[row0:row0 + n_rows, N_STATE + hf * half_n:N_STATE + (hf + 1) * half_n].astype(BF16)
        parts.append(jnp.dot(h_re, cmat_ref[hf, :half_n], preferred_element_type=F32)
                     + jnp.dot(h_im, cmat_ref[hf, half_n:], preferred_element_type=F32))
    return jnp.concatenate(parts, axis=-1)


def _ssm_gate(y, wglu_ref, bglu_ref):
    ya = jax.nn.gelu(y)
    gate = jnp.dot(ya.astype(BF16), wglu_ref[...], preferred_element_type=F32) + bglu_ref[...]
    return ya * jax.nn.sigmoid(gate)


def _unpermute_rows(perm_t_ref, x):
    hi = x.astype(BF16)
    rest = x - hi.astype(F32)
    mid = rest.astype(BF16)
    lo = (rest - mid.astype(F32)).astype(BF16)
    n = x.shape[1]
    out = jnp.dot(perm_t_ref[...], jnp.concatenate([hi, mid, lo], axis=1), preferred_element_type=F32)
    return out[:, :n] + out[:, n:2 * n] + out[:, 2 * n:]


def _rotary(x, cosv, sinv):
    return x * cosv + pltpu.roll(x, RET_HD // 2, 1) * sinv


def _head_norm_gate(o, g):
    mu = jnp.mean(o, axis=-1, keepdims=True)
    var = jnp.mean(jnp.square(o - mu), axis=-1, keepdims=True)
    return (o - mu) * lax.rsqrt(var + EPS) * jax.nn.silu(g)


def _pool_counts(pos):
    return [jnp.minimum(pos + 1, w).astype(F32) for w in POOL_WINDOWS]


def _mix_prompt_kernel(proj_ref, cos_ref, sin_ref, perm_ref, permt_ref, abar_ref, pws_ref,
                       bmat_ref, cmat_ref, dskip_ref, wglu_ref, bglu_ref,
                       mask_ref, qdec_ref, kdec_ref, cdec_ref, wpool_ref, pscale_ref,
                       y_ref, hre_ref, him_ref, sret_ref, pbuf_ref,
                       dr_scr, carry_scr, pool_scr, *, tl):
    t = pl.program_id(1)

    @pl.when(t == 0)
    def _():
        carry_scr[...] = jnp.zeros((SUBLANES, 2 * N_STATE), F32)
        sret_ref[...] = jnp.zeros(sret_ref.shape, F32)
        pool_scr[0:16, :] = jnp.zeros((16, D_POOL), F32)

    u = proj_ref[:, 0:D_SSM]
    up = jnp.dot(perm_ref[...], u.astype(BF16), preferred_element_type=F32).astype(BF16)
    _ssm_drive(up, bmat_ref, dr_scr, 0, tl)
    _scan_tile(dr_scr, abar_ref, pws_ref, carry_scr, tl // SUBLANES)
    y = _unpermute_rows(permt_ref, _ssm_readout(dr_scr, cmat_ref, 0, tl)) + dskip_ref[...] * u
    y_ref[:, 0:D_SSM] = _ssm_gate(y, wglu_ref, bglu_ref).astype(BF16)
    hre_ref[0] = carry_scr[SUBLANES - 1:SUBLANES, 0:N_STATE]
    him_ref[0] = carry_scr[SUBLANES - 1:SUBLANES, N_STATE:2 * N_STATE]

    scale = RET_HD ** -0.5

    def chunk_body(ci, carry):
        rows = pl.ds(pl.multiple_of(ci * RET_CHUNK, RET_CHUNK), RET_CHUNK)
        cosv = cos_ref[rows, :]
        sinv = sin_ref[rows, :]
        for h in range(RET_HEADS):
            hs = slice(h * RET_HD, (h + 1) * RET_HD)
            q = proj_ref[rows, Q_OFF + h * RET_HD:Q_OFF + (h + 1) * RET_HD]
            k = proj_ref[rows, K_OFF + h * RET_HD:K_OFF + (h + 1) * RET_HD]
            v = proj_ref[rows, V_OFF + h * RET_HD:V_OFF + (h + 1) * RET_HD].astype(BF16)
            g = proj_ref[rows, G_OFF + h * RET_HD:G_OFF + (h + 1) * RET_HD]
            qr = _rotary(q, cosv, sinv)
            kr = _rotary(k, cosv, sinv) * scale
            s_old = sret_ref[0, h]
            sc = lax.dot_general(qr.astype(BF16), kr.astype(BF16), (((1,), (1,)), ((), ())),
                                 preferred_element_type=F32) * mask_ref[h]
            o = (jnp.dot(sc.astype(BF16), v, preferred_element_type=F32)
                 + jnp.dot((qr * qdec_ref[:, hs]).astype(BF16), s_old.astype(BF16),
                           preferred_element_type=F32))
            sret_ref[0, h] = (s_old * cdec_ref[h:h + 1, :]
                              + lax.dot_general((kr * kdec_ref[:, hs]).astype(BF16), v,
                                                (((0,), (0,)), ((), ())), preferred_element_type=F32))
            y_ref[rows, D_SSM + h * RET_HD:D_SSM + (h + 1) * RET_HD] = _head_norm_gate(o, g).astype(BF16)
        return carry

    lax.fori_loop(0, tl // RET_CHUNK, chunk_body, 0, unroll=True)

    uc = proj_ref[:, C_OFF:C_OFF + D_POOL]
    pool_scr[16:16 + tl, :] = uc
    pos = t * tl + lax.broadcasted_iota(jnp.int32, (tl, POOL_GD), 0)
    counts = _pool_counts(pos)
    parts = []
    for gi, w in enumerate(POOL_WINDOWS):
        lanes = slice(gi * POOL_GD, (gi + 1) * POOL_GD)
        acc = pool_scr[:, lanes]
        shift = 1
        while shift < w:
            acc = acc + pltpu.roll(acc, shift, 0)
            shift *= 2
        parts.append(acc[16:] / counts[gi] - uc[:, lanes])
    pooled = jnp.concatenate(parts, axis=-1).astype(BF16)
    yc = jnp.dot(pooled, wpool_ref[...], preferred_element_type=F32) * pscale_ref[...]
    y_ref[:, D_SSM + D_RET:] = yc.astype(BF16)
    pbuf_ref[0] = pool_scr[tl + 1:tl + 16, :]
    pool_scr[0:16, :] = pool_scr[tl:tl + 16, :]


def _mix_prompt(proj, n_seq, seq_len, consts, tl):
    cos_t, sin_t = consts[:2]
    whole = consts[2:]
    nt = seq_len // tl
    row_map = lambda b, t: (b * nt + t, 0)

    def const_spec(a):
        nd = a.ndim
        return pl.BlockSpec(a.shape, lambda b, t: (0,) * nd)

    in_specs = [
        pl.BlockSpec((tl, D_IN), row_map),
        pl.BlockSpec((tl, RET_HD), lambda b, t: (t, 0)),
        pl.BlockSpec((tl, RET_HD), lambda b, t: (t, 0)),
    ] + [const_spec(a) for a in whole]
    out_shape = (
        jax.ShapeDtypeStruct((n_seq * seq_len, D_MODEL), BF16),
        jax.ShapeDtypeStruct((n_seq, 1, N_STATE), F32),
        jax.ShapeDtypeStruct((n_seq, 1, N_STATE), F32),
        jax.ShapeDtypeStruct((n_seq, RET_HEADS, RET_HD, RET_HD), F32),
        jax.ShapeDtypeStruct((n_seq, POOL_BUF, D_POOL), F32),
    )
    out_specs = (
        pl.BlockSpec((tl, D_MODEL), row_map),
        pl.BlockSpec((1, 1, N_STATE), lambda b, t: (b, 0, 0)),
        pl.BlockSpec((1, 1, N_STATE), lambda b, t: (b, 0, 0)),
        pl.BlockSpec((1, RET_HEADS, RET_HD, RET_HD), lambda b, t: (b, 0, 0, 0)),
        pl.BlockSpec((1, POOL_BUF, D_POOL), lambda b, t: (b, 0, 0)),
    )
    return pl.pallas_call(
        functools.partial(_mix_prompt_kernel, tl=tl),
        grid=(n_seq, nt),
        in_specs=in_specs,
        out_specs=out_specs,
        out_shape=out_shape,
        scratch_shapes=[
            pltpu.VMEM((tl, 2 * N_STATE), F32),
            pltpu.VMEM((SUBLANES, 2 * N_STATE), F32),
            pltpu.VMEM((16 + tl, D_POOL), F32),
        ],
        compiler_params=_params("arbitrary", "arbitrary"),
        name="mix_prompt",
    )(proj, cos_t, sin_t, *whole)


def _mix_sample_kernel(proj_ref, cos_ref, sin_ref, pw_ref, bmat_ref, cmat_ref, dskip_ref, wglu_ref,
                       bglu_ref, mask_ref, qdec_ref, kdec_ref, cdec_ref, wpool_ref, pscale_ref,
                       h0re_ref, h0im_ref, s0_ref, pool0_ref,
                       y_ref, hre_ref, him_ref, sret_ref, pbuf_ref,
                       dr_scr, pool_scr, *, bs, seq_len, pos0):
    n_rows = bs * seq_len

    u = proj_ref[:, 0:D_SSM]
    _ssm_drive(u.astype(BF16), bmat_ref, dr_scr, SUBLANES, n_rows)

    def carry_fn(b, lo):
        return (h0re_ref[pl.ds(b, 1), lo:lo + SCAN_LANES], h0im_ref[pl.ds(b, 1), lo:lo + SCAN_LANES])

    def block_end_fn(b, lo, h_re, h_im):
        hre_ref[pl.ds(b, 1), lo:lo + SCAN_LANES] = h_re
        him_ref[pl.ds(b, 1), lo:lo + SCAN_LANES] = h_im

    _ssm_scan(dr_scr, pw_ref, bs, carry_fn, block_end_fn)
    y = _ssm_readout(dr_scr, cmat_ref, SUBLANES, n_rows) + dskip_ref[...] * u
    y_ref[:, 0:D_SSM] = _ssm_gate(y, wglu_ref, bglu_ref).astype(BF16)

    scale = RET_HD ** -0.5
    cosv = cos_ref[...]
    sinv = sin_ref[...]
    own = (lax.broadcasted_iota(jnp.int32, (n_rows, bs * RET_HD), 0) // seq_len
           == lax.broadcasted_iota(jnp.int32, (n_rows, bs * RET_HD), 1) // RET_HD)
    for h in range(RET_HEADS):
        hs = slice(h * RET_HD, (h + 1) * RET_HD)
        q = proj_ref[:, Q_OFF + h * RET_HD:Q_OFF + (h + 1) * RET_HD]
        k = proj_ref[:, K_OFF + h * RET_HD:K_OFF + (h + 1) * RET_HD]
        v = proj_ref[:, V_OFF + h * RET_HD:V_OFF + (h + 1) * RET_HD].astype(BF16)
        g = proj_ref[:, G_OFF + h * RET_HD:G_OFF + (h + 1) * RET_HD]
        qr = _rotary(q, cosv, sinv)
        kr = _rotary(k, cosv, sinv) * scale
        sc = lax.dot_general(qr.astype(BF16), kr.astype(BF16), (((1,), (1,)), ((), ())),
                             preferred_element_type=F32) * mask_ref[h]
        qd = jnp.where(own, jnp.concatenate([qr * qdec_ref[:, hs]] * bs, axis=1), 0.0).astype(BF16)
        kd = jnp.where(own, jnp.concatenate([kr * kdec_ref[:, hs]] * bs, axis=1), 0.0).astype(BF16)
        s_old = s0_ref[:, h].reshape(bs * RET_HD, RET_HD)
        o = (jnp.dot(sc.astype(BF16), v, preferred_element_type=F32)
             + jnp.dot(qd, s_old.astype(BF16), preferred_element_type=F32))
        s_new = (s_old * cdec_ref[h:h + 1, :]
                 + lax.dot_general(kd, v, (((0,), (0,)), ((), ())), preferred_element_type=F32))
        sret_ref[:, h] = s_new.reshape(bs, RET_HD, RET_HD)
        y_ref[:, D_SSM + h * RET_HD:D_SSM + (h + 1) * RET_HD] = _head_norm_gate(o, g).astype(BF16)

    uc = proj_ref[:, C_OFF:C_OFF + D_POOL].reshape(bs, seq_len, D_POOL)
    pool_scr[:, 1:16, :] = pool0_ref[...]
    pool_scr[:, 16:16 + seq_len, :] = uc
    pos = pos0 + lax.broadcasted_iota(jnp.int32, (1, seq_len, POOL_GD), 1)
    counts = _pool_counts(pos)
    parts = []
    for gi, w in enumerate(POOL_WINDOWS):
        lanes = slice(gi * POOL_GD, (gi + 1) * POOL_GD)
        acc = pool_scr[:, 16:16 + seq_len, lanes]
        for j in range(1, w):
            acc = acc + pool_scr[:, 16 - j:16 - j + seq_len, lanes]
        parts.append(acc / counts[gi] - uc[:, :, lanes])
    pooled = jnp.concatenate(parts, axis=-1).reshape(n_rows, D_POOL).astype(BF16)
    yc = jnp.dot(pooled, wpool_ref[...], preferred_element_type=F32) * pscale_ref[...]
    y_ref[:, D_SSM + D_RET:] = yc.astype(BF16)
    pbuf_ref[...] = pool_scr[:, seq_len + 1:seq_len + 16, :]


def _mix_sample(proj, proj_row0, states, layer, prev_out, n_seq, seq_len, consts, bs):
    n_rows = bs * seq_len

    def const_spec(a):
        nd = a.ndim
        return pl.BlockSpec(a.shape, lambda i: (0,) * nd)

    state_specs = [
        pl.BlockSpec((None, bs, N_STATE), lambda i: (layer, i, 0)),
        pl.BlockSpec((None, bs, N_STATE), lambda i: (layer, i, 0)),
        pl.BlockSpec((None, bs, RET_HEADS, RET_HD, RET_HD), lambda i: (layer, i, 0, 0, 0)),
        pl.BlockSpec((None, bs, POOL_BUF, D_POOL), lambda i: (layer, i, 0, 0)),
    ]
    assert proj_row0 % n_rows == 0
    in_specs = ([pl.BlockSpec((n_rows, D_IN), lambda i: (proj_row0 // n_rows + i, 0))]
                + [const_spec(a) for a in consts] + state_specs)
    args = [proj, *consts, *states]
    n_in = len(args)
    aliases = {}
    if prev_out is not None:
        in_specs += [pl.BlockSpec(memory_space=pl.ANY)] * len(prev_out)
        aliases = {n_in + k: 1 + k for k in range(len(prev_out))}
        args += list(prev_out)
    n_args = len(args)
    out_shape = (
        jax.ShapeDtypeStruct((n_seq * seq_len, D_MODEL), BF16),
        jax.ShapeDtypeStruct((DEPTH, n_seq, N_STATE), F32),
        jax.ShapeDtypeStruct((DEPTH, n_seq, N_STATE), F32),
        jax.ShapeDtypeStruct((DEPTH, n_seq, RET_HEADS, RET_HD, RET_HD), F32),
        jax.ShapeDtypeStruct((DEPTH, n_seq, POOL_BUF, D_POOL), F32),
    )
    out_specs = (pl.BlockSpec((n_rows, D_MODEL), lambda i: (i, 0)),) + tuple(state_specs)

    def body(*refs):
        _mix_sample_kernel(*refs[:n_in], *refs[n_args:], bs=bs, seq_len=seq_len, pos0=PAST_LEN)

    return pl.pallas_call(
        body,
        grid=(n_seq // bs,),
        in_specs=in_specs,
        out_specs=out_specs,
        out_shape=out_shape,
        scratch_shapes=[
            pltpu.VMEM((SUBLANES + n_rows, 2 * N_STATE), F32),
            pltpu.VMEM((bs, 16 + seq_len, D_POOL), F32),
        ],
        input_output_aliases=aliases,
        compiler_params=_params("arbitrary"),
        name="mix_sample",
    )(*args)


def _out_proj_kernel(y_ref, w_ref, x_ref, g1_ref, sh_ref, sc_ref, g_ref, xo_ref, h_ref, w_scr, *, bb, tl):
    @pl.when(pl.program_id(0) == 0)
    def _():
        w_scr[...] = w_ref[...].astype(BF16)

    f = jnp.dot(y_ref[...], w_scr[...], preferred_element_type=F32)
    xn = x_ref[...].reshape(bb, tl, D_MODEL) + g1_ref[...] * f.reshape(bb, tl, D_MODEL)
    xo_ref[...] = xn.reshape(bb * tl, D_MODEL)
    h = _rms(xn) * g_ref[...] * (1.0 + sc_ref[...]) + sh_ref[...]
    h_ref[...] = h.reshape(bb * tl, D_MODEL).astype(BF16)


def _out_proj(y, w, layer, x, x_tile0, mod, g, rows, out_rows, prev):
    out_spec = pl.BlockSpec((rows.tm, D_MODEL), lambda i: (rows.first_tile + i, 0))
    return _call_into(
        functools.partial(_out_proj_kernel, bb=rows.bb, tl=rows.tl),
        None if prev is None else {0: prev[0], 1: prev[1]},
        grid=(rows.n_tiles,),
        in_specs=[
            pl.BlockSpec((rows.tm, D_MODEL), lambda i: (i, 0)),
            pl.BlockSpec((None, D_MODEL, D_MODEL), lambda i: (layer, 0, 0), pipeline_mode=pl.Buffered(1)),
            pl.BlockSpec((rows.tm, D_MODEL), lambda i: (x_tile0 + i, 0)),
            rows.mod_spec(layer, MOD_GATE1, lambda i: i),
            rows.mod_spec(layer, MOD_SHIFT2, lambda i: i),
            rows.mod_spec(layer, MOD_SCALE2, lambda i: i),
            pl.BlockSpec((None, 1, D_MODEL), lambda i: (layer, 0, 0)),
        ],
        args=[y, w, x, mod, mod, mod, g],
        out_specs=(out_spec, out_spec),
        out_shape=(jax.ShapeDtypeStruct((out_rows, D_MODEL), F32),
                   jax.ShapeDtypeStruct((out_rows, D_MODEL), BF16)),
        scratch_shapes=[pltpu.VMEM((D_MODEL, D_MODEL), BF16)],
        sem=("arbitrary",),
        name="out_proj",
    )


def _conv_gate(a, b, am1, am2, cw, cb):
    conv = cb + am2 * cw[0:1]
    conv = conv + am1 * cw[1:2]
    conv = conv + a * cw[2:3]
    return jax.nn.silu(conv) * b


def _ffn_up_kernel(h_ref, wa_ref, wb_ref, cw_ref, cb_ref, st_ref, act_ref, tail_ref, ns_ref, w_scr, carry_scr,
                   *, prompt, n_seq, seq_len, n_chunks):
    c = pl.program_id(0)
    i = pl.program_id(1)

    @pl.when(i == 0)
    def _():
        w_scr[:, :FF_TILE] = wa_ref[...].astype(BF16)

    @pl.when((i == 0) & (c < n_chunks - 1))
    def _():
        w_scr[:, FF_TILE:] = wb_ref[0].astype(BF16)

    @pl.when((i == 0) & (c == n_chunks - 1))
    def _():
        w_scr[:, FF_TILE:FF_TILE + FF_LAST] = wb_ref[0, :, FF_TILE - FF_LAST:].astype(BF16)

    cw = cw_ref[...]
    cb = cb_ref[...]

    @pl.when(i >= prompt.n_tiles)
    def _():
        tl = seq_len
        sb = FF_SUB_ROWS // tl
        tok = lax.broadcasted_iota(jnp.int32, (sb, tl, FF_TILE), 1)
        for r in range(n_seq * seq_len // FF_SUB_ROWS):
            rows = slice(r * FF_SUB_ROWS, (r + 1) * FF_SUB_ROWS)
            sq = slice(r * sb, (r + 1) * sb)
            ab = jnp.dot(h_ref[rows, :], w_scr[...], preferred_element_type=F32)
            a = ab[:, :FF_TILE].reshape(sb, tl, FF_TILE)
            b = ab[:, FF_TILE:].reshape(sb, tl, FF_TILE)
            p0 = st_ref[sq, 0:1, :]
            p1 = st_ref[sq, 1:2, :]
            am1 = jnp.where(tok == 0, p1, pltpu.roll(a, 1, 1))
            am2 = jnp.where(tok == 0, p0, jnp.where(tok == 1, p1, pltpu.roll(a, 2, 1)))
            act = _conv_gate(a, b, am1, am2, cw, cb)
            act_ref[rows, :] = act.reshape(FF_SUB_ROWS, FF_TILE).astype(BF16)
            ns_ref[sq] = a[:, tl - 2:tl, :]

    @pl.when(i < prompt.n_tiles)
    def _():
        @pl.when(i % prompt.tiles_per_seq == 0)
        def _():
            carry_scr[...] = jnp.zeros((SUBLANES, FF_TILE), F32)

        tok = lax.broadcasted_iota(jnp.int32, (SUBLANES, FF_TILE), 0)
        for r in range(prompt.tm // FF_SUB_ROWS):
            rows = slice(r * FF_SUB_ROWS, (r + 1) * FF_SUB_ROWS)
            ab = jnp.dot(h_ref[rows, :], w_scr[...], preferred_element_type=F32)
            a = ab[:, :FF_TILE]
            b = ab[:, FF_TILE:]
            p0 = carry_scr[SUBLANES - 2:SUBLANES - 1, :]
            p1 = carry_scr[SUBLANES - 1:SUBLANES, :]
            r1 = pltpu.roll(a, 1, 0)
            r2 = pltpu.roll(a, 2, 0)
            head1 = jnp.where(tok == 0, p1, r1[:SUBLANES])
            head2 = jnp.where(tok == 0, p0, jnp.where(tok == 1, p1, r2[:SUBLANES]))
            am1 = jnp.concatenate([head1, r1[SUBLANES:]], axis=0)
            am2 = jnp.concatenate([head2, r2[SUBLANES:]], axis=0)
            act_ref[rows, :] = _conv_gate(a, b, am1, am2, cw, cb).astype(BF16)
            carry_scr[...] = a[FF_SUB_ROWS - SUBLANES:, :]
        tail_ref[0] = carry_scr[SUBLANES - 2:SUBLANES, :]


def _ffn_up(h, w_up, conv_w, conv_b, layer, state, prev_state, prompt, n_seq, seq_len):
    m = h.shape[0]
    tm = prompt.tm
    assert m == prompt.n_tiles * tm + n_seq * seq_len and n_seq * seq_len <= tm
    assert (n_seq * seq_len) % FF_SUB_ROWS == 0 and tm % FF_SUB_ROWS == 0
    n_tiles = prompt.n_tiles + 1
    n_chunks = pl.cdiv(D_FF, FF_TILE)
    in_specs = [
        pl.BlockSpec((tm, D_MODEL), lambda c, i: (i, 0)),
        pl.BlockSpec((None, D_MODEL, FF_TILE), lambda c, i: (layer, 0, c)),
        pl.BlockSpec((pl.Element(1), pl.Element(D_MODEL), pl.Element(FF_TILE)),
                     lambda c, i: (layer, 0, LANES * jnp.minimum((D_FF + c * FF_TILE) // LANES,
                                                                 (2 * D_FF - FF_TILE) // LANES))),
        pl.BlockSpec((None, CONV_W, FF_TILE), lambda c, i: (layer, 0, c)),
        pl.BlockSpec((None, 1, FF_TILE), lambda c, i: (layer, 0, c)),
        pl.BlockSpec((None, n_seq, CONV_W - 1, FF_TILE), lambda c, i: (layer, 0, 0, c)),
    ]
    return _call_into(
        functools.partial(_ffn_up_kernel, prompt=prompt, n_seq=n_seq, seq_len=seq_len, n_chunks=n_chunks),
        None if prev_state is None else {2: prev_state},
        grid=(n_chunks, n_tiles),
        in_specs=in_specs,
        args=[h, w_up, w_up, conv_w, conv_b, state],
        out_specs=(
            pl.BlockSpec((tm, FF_TILE), lambda c, i: (i, c)),
            pl.BlockSpec((1, CONV_W - 1, FF_TILE), lambda c, i: (jnp.minimum(i, prompt.n_tiles - 1), 0, c)),
            pl.BlockSpec((None, n_seq, CONV_W - 1, FF_TILE), lambda c, i: (layer, 0, 0, c)),
        ),
        out_shape=(jax.ShapeDtypeStruct((m, D_FF), BF16),
                   jax.ShapeDtypeStruct((prompt.n_tiles, CONV_W - 1, D_FF), F32),
                   jax.ShapeDtypeStruct((DEPTH, n_seq, CONV_W - 1, D_FF), F32)),
        scratch_shapes=[pltpu.VMEM((D_MODEL, 2 * FF_TILE), BF16), pltpu.VMEM((SUBLANES, FF_TILE), F32)],
        sem=("arbitrary", "arbitrary"),
        name="ffn_up",
    )


def _cmul(ar, ai, br, bi):
    return ar * br - ai * bi, ar * bi + ai * br


def _ssm_consts(a_re, a_im, log_dt, b_re, b_im, c_re, c_im, scan_steps):
    lam_re, lam_im = a_re.astype(F32), a_im.astype(F32)
    dt = jnp.exp(log_dt.astype(F32))[:, None]
    mag = jnp.exp(lam_re * dt)
    ab_re, ab_im = mag * jnp.cos(lam_im * dt), mag * jnp.sin(lam_im * dt)
    den = lam_re * lam_re + lam_im * lam_im
    f_re = ((ab_re - 1.0) * lam_re + ab_im * lam_im) / den
    f_im = (ab_im * lam_re - (ab_re - 1.0) * lam_im) / den
    br, bi = b_re.astype(F32), b_im.astype(F32)
    bb_re = f_re[..., None] * br - f_im[..., None] * bi
    bb_im = f_re[..., None] * bi + f_im[..., None] * br

    gh = SSM_GROUPS // 2
    eye = jnp.eye(gh, dtype=F32)

    def drive_half(m):
        return jnp.einsum("gph,gk->ghkp", m, eye).reshape(gh * SSM_GROUP, gh * SSM_STATE)

    def read_half(m):
        return jnp.einsum("ghp,gk->gpkh", m, eye).reshape(gh * SSM_STATE, gh * SSM_GROUP)

    bmat = jnp.stack([
        jnp.concatenate([drive_half(bb_re[s]), drive_half(bb_im[s])], axis=1)
        for s in (slice(0, gh), slice(gh, 2 * gh))]).astype(BF16)
    cre, cim = c_re.astype(F32), c_im.astype(F32)
    cmat = jnp.stack([
        jnp.concatenate([read_half(cre[s]), -read_half(cim[s])], axis=0)
        for s in (slice(0, gh), slice(gh, 2 * gh))]).astype(BF16)

    step_re, step_im = _power_table(ab_re.reshape(1, N_STATE), ab_im.reshape(1, N_STATE), scan_steps)
    sub_re, sub_im = _power_table(step_re[-1:], step_im[-1:], SUBLANES)
    abar = jnp.stack([step_re[:1], step_im[:1]])
    pw_block = _scan_planes(step_re[:SUBLANES], step_im[:SUBLANES])
    pw_sub = _scan_planes(sub_re, sub_im)
    return bmat, cmat, abar, pw_block, pw_sub


def _power_table(re, im, n):
    while re.shape[0] < n:
        top_re, top_im = re[-1:], im[-1:]
        more_re, more_im = _cmul(re, im, top_re, top_im)
        re, im = jnp.concatenate([re, more_re]), jnp.concatenate([im, more_im])
    return re[:n], im[:n]


def _scan_planes(re, im):
    row = jnp.arange(SUBLANES)[:, None]
    planes_re = [jnp.where(row >= s, re[s - 1][None, :], 0.0) for s in (1, 2, 4)] + [re]
    planes_im = [jnp.where(row >= s, im[s - 1][None, :], 0.0) for s in (1, 2, 4)] + [im]
    return jnp.stack([jnp.stack(planes_re), jnp.stack(planes_im)])


def _time_on_sublanes(n_rows):
    steps = n_rows // SUBLANES
    p = np.zeros((n_rows, n_rows), np.float32)
    r = np.arange(n_rows)
    p[r, (r % SUBLANES) * steps + r // SUBLANES] = 1.0
    return jnp.asarray(p, BF16), jnp.asarray(p.T, BF16)


def _rotary_tables(pos):
    half = RET_HD // 2
    inv = ROPE_BASE ** (-np.arange(half, dtype=np.float64) / half)
    ang = np.asarray(pos, np.float64)[:, None] * inv[None, :]
    cos, sin = np.cos(ang), np.sin(ang)
    return (np.concatenate([cos, cos], axis=-1).astype(np.float32),
            np.concatenate([-sin, sin], axis=-1).astype(np.float32))


def _retention_consts(chunk, n_seq):
    lg = np.log1p(-np.exp2(-5.0 - np.arange(RET_HEADS, dtype=np.float64)))
    r = np.arange(chunk * n_seq)
    ti = (r % chunk).astype(np.float64)
    seq = r // chunk
    diff = ti[:, None] - ti[None, :]
    keep = (seq[:, None] == seq[None, :]) & (diff >= 0.0)
    mask = np.where(keep[None], np.exp(np.maximum(diff, 0.0)[None] * lg[:, None, None]), 0.0)
    q_dec = np.exp((ti + 1.0)[:, None] * lg[None, :])
    k_dec = np.exp((chunk - 1.0 - ti)[:, None] * lg[None, :])
    c_dec = np.exp(chunk * lg)
    expand = lambda d: np.repeat(d, RET_HD, axis=1).astype(np.float32)
    return (mask.astype(np.float32), expand(q_dec), expand(k_dec),
            np.broadcast_to(c_dec[:, None], (RET_HEADS, RET_HD)).astype(np.float32))


def _pool_weight(w_pool):
    eye = jnp.eye(len(POOL_WINDOWS), dtype=F32)
    return jnp.einsum("gcd,gk->gckd", w_pool, eye).reshape(D_POOL, D_POOL).astype(BF16)


def kernel(x_prompt, x_sample, state_ssm_re, state_ssm_im, state_ret, state_pool, state_ffn_conv, c_prompt, c_sample, w_ada, b_ada, norm1_g, w_in, ssm_a_re, ssm_a_im, ssm_log_dt, ssm_b_re, ssm_b_im, ssm_c_re, ssm_c_im, ssm_d, ssm_w_glu, ssm_b_glu, pool_w, pool_scale, w_out, norm2_g, ffn_w_up, ffn_conv_w, ffn_conv_b, ffn_w_down, final_norm_g):
    bp, lp, _ = x_prompt.shape
    bs, ls, _ = x_sample.shape
    tl_mix, seq_mix = MIX_TOKENS, MIX_SEQS

    n_all = bp + bs
    n_pad = -(-n_all // SUBLANES) * SUBLANES
    c_all = jnp.concatenate([c_sample, c_prompt, jnp.zeros((n_pad - n_all, D_MODEL), F32)], axis=0)
    mod = _ada_mod(c_all, w_ada, b_ada)

    cos_p, sin_p = _rotary_tables(np.arange(lp))
    cos_s, sin_s = _rotary_tables(PAST_LEN + np.arange(ls))
    cos_s, sin_s = np.tile(cos_s, (seq_mix, 1)), np.tile(sin_s, (seq_mix, 1))
    assert math.gcd(lp, RET_CHUNK) == RET_CHUNK and math.gcd(ls, RET_CHUNK) == ls == SUBLANES
    ret_p = _retention_consts(RET_CHUNK, 1)
    ret_s = _retention_consts(ls, seq_mix)
    perm = _time_on_sublanes(tl_mix)

    rows_total = bp * lp + bs * ls
    groups = (_Rows(bp, lp, ROW_TILE, bs, 0), _Rows(bs, ls, ROW_TILE, 0, bp * lp))
    groups_full = (_Rows(bp, lp, ROW_TILE_FULL, bs, 0), _Rows(bs, ls, ROW_TILE_FULL, 0, bp * lp))
    big_p, big_s = groups

    norm1 = norm1_g.reshape(DEPTH, 1, D_MODEL)
    norm2 = norm2_g.reshape(DEPTH, 1, D_MODEL)
    final_g = final_norm_g.reshape(1, 1, D_MODEL)
    conv_b = ffn_conv_b.reshape(DEPTH, 1, D_FF)
    sample_states = (state_ssm_re.reshape(DEPTH, bs, N_STATE), state_ssm_im.reshape(DEPTH, bs, N_STATE),
                     state_ret, state_pool)

    x_first = (x_prompt.reshape(bp * lp, D_MODEL), x_sample.reshape(bs * ls, D_MODEL))
    x_all = None

    def residual_rows(k, rows):
        return (x_first[k], 0) if x_all is None else (x_all, rows.first_tile)

    new_p = ([], [], [], [], [])
    s_mix = None
    s_conv = None
    for l in range(DEPTH):
        bmat, cmat, abar, pw_block, pw_sub = _ssm_consts(
            ssm_a_re[l], ssm_a_im[l], ssm_log_dt[l], ssm_b_re[l], ssm_b_im[l], ssm_c_re[l], ssm_c_im[l],
            tl_mix // SUBLANES)
        shared = (bmat, cmat, ssm_d[l].reshape(1, D_SSM), ssm_w_glu[l].astype(BF16),
                  ssm_b_glu[l].reshape(1, D_SSM))
        pool_c = (_pool_weight(pool_w[l]), pool_scale[l].reshape(1, D_POOL))
        pre = (mod, MOD_SHIFT1, MOD_SCALE1)

        h = None
        for k, rows in enumerate(groups):
            h = _norm(*residual_rows(k, rows), norm1, l, pre, rows, BF16, rows_total, h)
        proj = _matmul(h, w_in, l, ROW_TILE, IN_PROJ_COLS, "in_proj")
        y_p, st_re, st_im, st_ret, st_pool = _mix_prompt(
            proj, bp, lp, (cos_p, sin_p) + perm + (abar, pw_sub) + shared + ret_p + pool_c, tl_mix)
        y_s, *s_mix = _mix_sample(proj, bp * lp, sample_states, l, s_mix, bs, ls,
                                  (cos_s, sin_s, pw_block) + shared + ret_s + pool_c, seq_mix)
        xh = None
        for k, (rows, y) in enumerate(zip(groups_full, (y_p, y_s))):
            xh = _out_proj(y, w_out, l, *residual_rows(k, rows), mod, norm2, rows, rows_total, xh)
        xn, h2 = xh
        act, tails, s_conv = _ffn_up(h2, ffn_w_up, ffn_conv_w, conv_b, l, state_ffn_conv, s_conv, big_p, bs, ls)
        x_all = _matmul(act, ffn_w_down, l, ROW_TILE, FF_DOWN_COLS, "ffn_down", groups=groups,
                        residual=(xn, mod, MOD_GATE2), single_buffer_w=True)
        st_conv = tails[big_p.tiles_per_seq - 1::big_p.tiles_per_seq]
        for lst, st in zip(new_p, (st_re, st_im, st_ret, st_pool, st_conv)):
            lst.append(st)

    yp = _norm(x_all, big_p.first_tile, final_g, 0, None, big_p, F32)
    ys = _norm(x_all, big_s.first_tile, final_g, 0, None, big_s, F32)

    p_re, p_im, p_ret, p_pool, p_conv = [jnp.stack(st) for st in new_p]
    s_re, s_im, s_ret, s_pool = s_mix
    shape_p = (DEPTH, bp, SSM_GROUPS, SSM_STATE)
    shape_s = (DEPTH, bs, SSM_GROUPS, SSM_STATE)
    return (yp.reshape(bp, lp, D_MODEL), ys.reshape(bs, ls, D_MODEL),
            p_re.reshape(shape_p), p_im.reshape(shape_p), p_ret, p_pool, p_conv,
            s_re.reshape(shape_s), s_im.reshape(shape_s), s_ret, s_pool, s_conv)
```
